```python
import jax, jax.numpy as jnp
from jax import lax
import numpy as np

D_MODEL = 1024
BATCH = 8
SEQ = 8192
DEPTH = 4

EXPAND = 2
D_BRANCH = EXPAND * D_MODEL
HEAD_DIM = 64
N_HEADS = D_BRANCH // HEAD_DIM
N_MIXERS = 3
N_LAYERS_A = (DEPTH + 2) // 3
N_LAYERS_B = (DEPTH + 1) // 3
N_LAYERS_C = DEPTH // 3
RMS_EPS = 1e-5
BLOCK = 128

POOL_WINDOWS = (2, 4, 8, 16)
N_POOL_GROUPS = len(POOL_WINDOWS)
POOL_GROUP_DIM = D_BRANCH // N_POOL_GROUPS
A_IN_WIDTH = 2 * D_BRANCH

SWA_WINDOW = 128
SWA_KV_HEADS = N_HEADS // 8
SWA_GROUP = N_HEADS // SWA_KV_HEADS
KV_WIDTH = SWA_KV_HEADS * HEAD_DIM
B_IN_WIDTH = 2 * D_BRANCH + 2 * KV_WIDTH

DILATED_PAIRS = ((128, 1), (512, 4), (2048, 16))
N_DIL_GROUPS = len(DILATED_PAIRS)
C_IN_WIDTH = (3 * N_DIL_GROUPS + 1) * D_BRANCH

kernel_name = "hybrid_pool_swa_dilated_gated_trunk"


def _rmsnorm(x, g):
    x32 = x.astype(jnp.float32)
    y = x32 * lax.rsqrt(jnp.mean(x32 * x32, axis=-1, keepdims=True) + RMS_EPS)
    return (y * g.astype(jnp.float32)).astype(x.dtype)


def _banded_attention(q, k, v, max_dist, sink):
    r_n, n, seq_len, hk, grp, hd = q.shape
    n_blk = -(-seq_len // BLOCK)
    pad_end = n_blk * BLOCK - seq_len
    qp = jnp.pad(q, ((0, 0), (0, 0), (0, pad_end), (0, 0), (0, 0), (0, 0)))
    kv_pad = ((0, 0), (0, 0), (BLOCK, pad_end), (0, 0), (0, 0))
    kp = jnp.pad(k, kv_pad)
    vp = jnp.pad(v, kv_pad)
    scale = HEAD_DIM ** -0.5

    def block(idx):
        r = idx // n_blk
        start = (idx % n_blk) * BLOCK
        qr = lax.dynamic_index_in_dim(qp, r, 0, keepdims=False)
        kr = lax.dynamic_index_in_dim(kp, r, 0, keepdims=False)
        vr = lax.dynamic_index_in_dim(vp, r, 0, keepdims=False)
        qb = lax.dynamic_slice_in_dim(qr, start, BLOCK, axis=1).astype(jnp.float32)
        kb = lax.dynamic_slice_in_dim(kr, start, 2 * BLOCK, axis=1).astype(jnp.float32)
        vb = lax.dynamic_slice_in_dim(vr, start, 2 * BLOCK, axis=1).astype(jnp.float32)
        s = jnp.einsum('nqkgd,nskd->nkgqs', qb, kb) * scale
        q_pos = start + jnp.arange(BLOCK)
        k_pos = start - BLOCK + jnp.arange(2 * BLOCK)
        dist = q_pos[:, None] - k_pos[None, :]
        valid = (dist >= 0) & (dist <= max_dist) & (k_pos >= 0)[None, :]
        s = jnp.where(valid, s, -jnp.inf)
        lse = jax.nn.logsumexp(s, axis=-1)
        if sink is not None:
            lse = jnp.logaddexp(lse, sink.astype(jnp.float32)[None, :, :, None])
        p = jnp.exp(s - lse[..., None])
        o = jnp.einsum('nkgqs,nskd->nqkgd', p, vb)
        return o, lse

    o, lse = lax.map(block, jnp.arange(r_n * n_blk))
    o = o.reshape(r_n, n_blk, n, BLOCK, hk, grp, hd)
    o = jnp.transpose(o, (0, 2, 1, 3, 4, 5, 6)).reshape(r_n, n, n_blk * BLOCK, hk, grp, hd)[:, :, :seq_len]
    lse = lse.reshape(r_n, n_blk, n, hk, grp, BLOCK)
    lse = jnp.transpose(lse, (0, 2, 1, 5, 3, 4)).reshape(r_n, n, n_blk * BLOCK, hk, grp)[:, :, :seq_len]
    return o, lse


def _pool_mixer(u, w_group, scale):
    bsz, seq, _ = u.shape
    ug = u.reshape(bsz, seq, N_POOL_GROUPS, POOL_GROUP_DIM)
    c = jnp.cumsum(ug.astype(jnp.float32), axis=1)
    t = jnp.arange(seq)
    pooled = []
    for gi, w in enumerate(POOL_WINDOWS):
        cg = c[:, :, gi]
        shifted = jnp.pad(cg, ((0, 0), (w, 0), (0, 0)))[:, :seq]
        count = jnp.minimum(t + 1, w).astype(jnp.float32)
        pooled.append((cg - shifted) / count[None, :, None])
    d = (jnp.stack(pooled, axis=2) - ug.astype(jnp.float32)).astype(u.dtype)
    y = jnp.einsum('bsgc,gcd->bsgd', d, w_group)
    return y.reshape(bsz, seq, D_BRANCH) * scale


def _swa_mixer(p, sinks):
    bsz, seq, _ = p.shape
    q, k, v, gate = jnp.split(p, [D_BRANCH, D_BRANCH + KV_WIDTH, D_BRANCH + 2 * KV_WIDTH], axis=-1)
    q = q.reshape(1, bsz, seq, SWA_KV_HEADS, SWA_GROUP, HEAD_DIM)
    k = k.reshape(1, bsz, seq, SWA_KV_HEADS, HEAD_DIM)
    v = v.reshape(1, bsz, seq, SWA_KV_HEADS, HEAD_DIM)
    o, _ = _banded_attention(q, k, v, SWA_WINDOW - 1, sinks.reshape(SWA_KV_HEADS, SWA_GROUP))
    return o[0].reshape(bsz, seq, D_BRANCH).astype(p.dtype), gate


def _to_residues(a, d):
    bsz, seq = a.shape[:2]
    a = a.reshape((bsz, seq // d, d) + a.shape[2:])
    return jnp.moveaxis(a, 2, 0)


def _from_residues(a):
    d, bsz, l = a.shape[:3]
    a = jnp.moveaxis(a, 0, 2)
    return a.reshape((bsz, l * d) + a.shape[3:])


def _dilated_mixer(h, w_in):
    bsz, seq, _ = h.shape

    def proj(c):
        return jnp.einsum('bsd,de->bse', h, w_in[:, c * D_BRANCH:(c + 1) * D_BRANCH])

    outs, lses = [], []
    for gi, (window, dil) in enumerate(DILATED_PAIRS):
        q, k, v = (proj(3 * gi + j).reshape(bsz, seq, N_HEADS, HEAD_DIM) for j in range(3))
        o, lse = _banded_attention(_to_residues(q, dil)[..., None, :], _to_residues(k, dil),
                                   _to_residues(v, dil), window // dil, None)
        outs.append(_from_residues(o[..., 0, :]))
        lses.append(_from_residues(lse[..., 0]))
    wts = jax.nn.softmax(jnp.stack(lses, 0), axis=0)
    o = jnp.einsum('gbsh,gbshd->bshd', wts, jnp.stack(outs, 0))
    return o.reshape(bsz, seq, D_BRANCH).astype(h.dtype), proj(3 * N_DIL_GROUPS)


def _fwd_setup_inputs(seed: int = 0) -> dict:
    key = jax.random.key(seed)
    ks = jax.random.split(key, 11)
    f32 = jnp.float32
    nrm = jax.random.normal
    return {
        "x": nrm(ks[0], (BATCH, SEQ, D_MODEL), f32),
        "norm_g": 1.0 + 0.05 * nrm(ks[1], (DEPTH, D_MODEL), f32),
        "final_g": 1.0 + 0.05 * nrm(ks[2], (D_MODEL,), f32),
        "w_out": nrm(ks[3], (DEPTH, D_BRANCH, D_MODEL), f32) * D_BRANCH ** -0.5,
        "a_w_in": nrm(ks[4], (N_LAYERS_A, D_MODEL, A_IN_WIDTH), f32) * D_MODEL ** -0.5,
        "a_w_group": nrm(ks[5], (N_LAYERS_A, N_POOL_GROUPS, POOL_GROUP_DIM, POOL_GROUP_DIM), f32) * POOL_GROUP_DIM ** -0.5,
        "a_scale": 1.0 + 0.1 * nrm(ks[6], (N_LAYERS_A, D_BRANCH), f32),
        "b_w_in": nrm(ks[7], (N_LAYERS_B, D_MODEL, B_IN_WIDTH), f32) * D_MODEL ** -0.5,
        "b_sinks": 0.5 * nrm(ks[8], (N_LAYERS_B, N_HEADS), f32),
        "c_w_in": nrm(ks[9], (N_LAYERS_C, D_MODEL, C_IN_WIDTH), f32) * D_MODEL ** -0.5,
    }


def _fwd_reference(x, norm_g, final_g, w_out, a_w_in, a_w_group, a_scale, b_w_in, b_sinks, c_w_in):
    for i in range(DEPTH):
        h = _rmsnorm(x, norm_g[i])
        kind, j = i % N_MIXERS, i // N_MIXERS
        if kind == 0:
            p = jnp.einsum('bsd,de->bse', h, a_w_in[j])
            u, gate = jnp.split(p, 2, axis=-1)
            y = _pool_mixer(u, a_w_group[j], a_scale[j])
        elif kind == 1:
            y, gate = _swa_mixer(jnp.einsum('bsd,de->bse', h, b_w_in[j]), b_sinks[j])
        else:
            y, gate = _dilated_mixer(h, c_w_in[j])
        x = x + jnp.einsum('bse,ed->bsd', y * jax.nn.silu(gate), w_out[i])
    return _rmsnorm(x, final_g)


import jax as _jax
import jax.numpy as _jnp

TWIN_FORMAT = 'train_step'
FWD_PARAMS = ['x', 'norm_g', 'final_g', 'w_out', 'a_w_in', 'a_w_group', 'a_scale', 'b_w_in', 'b_sinks', 'c_w_in']
TWIN_WEIGHTS = ['norm_g', 'final_g', 'w_out', 'a_w_in', 'a_w_group', 'a_scale', 'b_w_in', 'b_sinks', 'c_w_in']
TWIN_DIFF_INPUT = 'x'
TWIN_INPUTS = ['x', 'norm_g', 'final_g', 'w_out', 'a_w_in', 'a_w_group', 'a_scale', 'b_w_in', 'b_sinks', 'c_w_in', 'loss_target', 'm_norm_g', 'm_final_g', 'm_w_out', 'm_a_w_in', 'm_a_w_group', 'm_a_scale', 'm_b_w_in', 'm_b_sinks', 'm_c_w_in', 'v_norm_g', 'v_final_g', 'v_w_out', 'v_a_w_in', 'v_a_w_group', 'v_a_scale', 'v_b_w_in', 'v_b_sinks', 'v_c_w_in']
TWIN_OUTPUTS = ['loss', 'grad_x', 'grad_norm_g', 'grad_final_g', 'grad_w_out', 'grad_a_w_in', 'grad_a_w_group', 'grad_a_scale', 'grad_b_w_in', 'grad_b_sinks', 'grad_c_w_in', 'delta_norm_g', 'delta_final_g', 'delta_w_out', 'delta_a_w_in', 'delta_a_w_group', 'delta_a_scale', 'delta_b_w_in', 'delta_b_sinks', 'delta_c_w_in', 'new_m_norm_g', 'new_m_final_g', 'new_m_w_out', 'new_m_a_w_in', 'new_m_a_w_group', 'new_m_a_scale', 'new_m_b_w_in', 'new_m_b_sinks', 'new_m_c_w_in', 'new_v_norm_g', 'new_v_final_g', 'new_v_w_out', 'new_v_a_w_in', 'new_v_a_w_group', 'new_v_a_scale', 'new_v_b_w_in', 'new_v_b_sinks', 'new_v_c_w_in']
TWIN_LEAF_KINDS = {'loss': 'loss', 'grad_x': 'grad_x', 'grad_norm_g': 'grad_w', 'grad_final_g': 'grad_w', 'grad_w_out': 'grad_w', 'grad_a_w_in': 'grad_w', 'grad_a_w_group': 'grad_w', 'grad_a_scale': 'grad_w', 'grad_b_w_in': 'grad_w', 'grad_b_sinks': 'grad_w', 'grad_c_w_in': 'grad_w', 'delta_norm_g': 'delta_w', 'delta_final_g': 'delta_w', 'delta_w_out': 'delta_w', 'delta_a_w_in': 'delta_w', 'delta_a_w_group': 'delta_w', 'delta_a_scale': 'delta_w', 'delta_b_w_in': 'delta_w', 'delta_b_sinks': 'delta_w', 'delta_c_w_in': 'delta_w', 'new_m_norm_g': 'new_m', 'new_m_final_g': 'new_m', 'new_m_w_out': 'new_m', 'new_m_a_w_in': 'new_m', 'new_m_a_w_group': 'new_m', 'new_m_a_scale': 'new_m', 'new_m_b_w_in': 'new_m', 'new_m_b_sinks': 'new_m', 'new_m_c_w_in': 'new_m', 'new_v_norm_g': 'new_v', 'new_v_final_g': 'new_v', 'new_v_w_out': 'new_v', 'new_v_a_w_in': 'new_v', 'new_v_a_w_group': 'new_v', 'new_v_a_scale': 'new_v', 'new_v_b_w_in': 'new_v', 'new_v_b_sinks': 'new_v', 'new_v_c_w_in': 'new_v'}


def _forward(args):
    return _fwd_reference(*[args[k] for k in FWD_PARAMS])


def _output_shape():
    out = _jax.eval_shape(lambda: _forward(_fwd_setup_inputs(0)))
    return out.shape, out.dtype

N_MICROBATCH = 1
ADAM_LR = 0.001
ADAM_B1 = 0.9
ADAM_B2 = 0.999
ADAM_EPS = 1e-08
ADAM_WD = 0.01
ADAM_STEP = 10
PER_EXAMPLE_BATCH_AXIS = {'x': 0, 'loss_target': 0}
SHARED_INPUTS = []
_WEIGHT_DTYPES = {'norm_g': _jnp.float32, 'final_g': _jnp.float32, 'w_out': _jnp.float32, 'a_w_in': _jnp.float32, 'a_w_group': _jnp.float32, 'a_scale': _jnp.float32, 'b_w_in': _jnp.float32, 'b_sinks': _jnp.float32, 'c_w_in': _jnp.float32}
MOMENT_SCALE = {'norm_g': 1.252787e-01, 'final_g': 6.390163e+01, 'w_out': 8.613222e-02, 'a_w_in': 8.488763e-02, 'a_w_group': 8.364909e-02, 'a_scale': 8.242806e-02, 'b_w_in': 2.755822e-02, 'b_sinks': 2.025115e-02, 'c_w_in': 1.153193e-02}


def _to_microbatches(a, axis):
    t = _jnp.moveaxis(a, axis, 0)
    t = t.reshape((N_MICROBATCH, t.shape[0] // N_MICROBATCH) + t.shape[1:])
    return _jnp.moveaxis(t, 1, axis + 1)


def setup_inputs(seed: int = 0) -> dict:
    inp = _fwd_setup_inputs(seed)
    key = _jax.random.fold_in(_jax.random.key(seed), 7919)
    shape, _ = _output_shape()
    out = dict(inp)
    out["loss_target"] = _jax.random.normal(_jax.random.fold_in(key, 0), shape, _jnp.float32)
    for i, name in enumerate(TWIN_WEIGHTS):
        w = inp[name].astype(_jnp.float32)
        if MOMENT_SCALE is None:
            s = _jnp.sqrt(_jnp.mean(_jnp.square(w)) + 1e-30)
        else:
            s = MOMENT_SCALE[name]
        km, kv = _jax.random.split(_jax.random.fold_in(key, i + 1))
        out[name] = w
        out["m_" + name] = s * _jax.random.normal(km, w.shape, _jnp.float32)
        out["v_" + name] = (s * s) * _jax.random.uniform(kv, w.shape, _jnp.float32, 0.5, 1.5)
    if N_MICROBATCH > 1:
        for name, axis in PER_EXAMPLE_BATCH_AXIS.items():
            out[name] = _to_microbatches(out[name], axis)
    return {'x': out['x'], 'norm_g': out['norm_g'], 'final_g': out['final_g'], 'w_out': out['w_out'], 'a_w_in': out['a_w_in'], 'a_w_group': out['a_w_group'], 'a_scale': out['a_scale'], 'b_w_in': out['b_w_in'], 'b_sinks': out['b_sinks'], 'c_w_in': out['c_w_in'], 'loss_target': out['loss_target'], 'm_norm_g': out['m_norm_g'], 'm_final_g': out['m_final_g'], 'm_w_out': out['m_w_out'], 'm_a_w_in': out['m_a_w_in'], 'm_a_w_group': out['m_a_w_group'], 'm_a_scale': out['m_a_scale'], 'm_b_w_in': out['m_b_w_in'], 'm_b_sinks': out['m_b_sinks'], 'm_c_w_in': out['m_c_w_in'], 'v_norm_g': out['v_norm_g'], 'v_final_g': out['v_final_g'], 'v_w_out': out['v_w_out'], 'v_a_w_in': out['v_a_w_in'], 'v_a_w_group': out['v_a_w_group'], 'v_a_scale': out['v_a_scale'], 'v_b_w_in': out['v_b_w_in'], 'v_b_sinks': out['v_b_sinks'], 'v_c_w_in': out['v_c_w_in']}


def _loss(weights, diff, rest, loss_target):
    with _jax.named_scope("forward"):
        args = {**rest, TWIN_DIFF_INPUT: diff, **{k: w.astype(_WEIGHT_DTYPES[k]) for k, w in weights.items()}}
        y = _forward(args)
    with _jax.named_scope("loss_head"):
        err = _jnp.square(y.astype(_jnp.float32) - loss_target)
        return 0.5 * _jnp.sum(_jnp.mean(err, axis=-1)) if err.ndim else 0.5 * err


def _adamw(w, g, m, v):
    m = ADAM_B1 * m + (1.0 - ADAM_B1) * g
    v = ADAM_B2 * v + (1.0 - ADAM_B2) * _jnp.square(g)
    m_hat = m / (1.0 - ADAM_B1 ** ADAM_STEP)
    v_hat = v / (1.0 - ADAM_B2 ** ADAM_STEP)
    delta = -ADAM_LR * (m_hat / (_jnp.sqrt(v_hat) + ADAM_EPS) + ADAM_WD * w)
    return delta, m, v


def reference(x, norm_g, final_g, w_out, a_w_in, a_w_group, a_scale, b_w_in, b_sinks, c_w_in, loss_target, m_norm_g, m_final_g, m_w_out, m_a_w_in, m_a_w_group, m_a_scale, m_b_w_in, m_b_sinks, m_c_w_in, v_norm_g, v_final_g, v_w_out, v_a_w_in, v_a_w_group, v_a_scale, v_b_w_in, v_b_sinks, v_c_w_in):
    given = dict(x=x, norm_g=norm_g, final_g=final_g, w_out=w_out, a_w_in=a_w_in, a_w_group=a_w_group, a_scale=a_scale, b_w_in=b_w_in, b_sinks=b_sinks, c_w_in=c_w_in, loss_target=loss_target, m_norm_g=m_norm_g, m_final_g=m_final_g, m_w_out=m_w_out, m_a_w_in=m_a_w_in, m_a_w_group=m_a_w_group, m_a_scale=m_a_scale, m_b_w_in=m_b_w_in, m_b_sinks=m_b_sinks, m_c_w_in=m_c_w_in, v_norm_g=v_norm_g, v_final_g=v_final_g, v_w_out=v_w_out, v_a_w_in=v_a_w_in, v_a_w_group=v_a_w_group, v_a_scale=v_a_scale, v_b_w_in=v_b_w_in, v_b_sinks=v_b_sinks, v_c_w_in=v_c_w_in)
    weights = {n: given[n] for n in TWIN_WEIGHTS}
    shared = {n: given[n] for n in SHARED_INPUTS}
    per_example = {n: given[n] for n in ['x']}
    grad_fn = _jax.value_and_grad(_loss, argnums=(0, 1))

    def one_microbatch(ex, loss_target):
        ex = dict(ex)
        diff = ex.pop(TWIN_DIFF_INPUT)
        return grad_fn(weights, diff, {**shared, **ex}, loss_target)

    if N_MICROBATCH == 1:
        loss, (grad_w, grad_x) = one_microbatch(per_example, given["loss_target"])
    else:
        def body(carry, xs):
            loss_sum, grad_sum = carry
            l_k, (gw_k, gx_k) = one_microbatch(xs[0], xs[1])
            with _jax.named_scope("update"):
                return (loss_sum + l_k, _jax.tree.map(_jnp.add, grad_sum, gw_k)), gx_k

        init = (_jnp.zeros((), _jnp.float32), _jax.tree.map(_jnp.zeros_like, weights))
        (loss, grad_w), grad_x = _jax.lax.scan(body, init, (per_example, given["loss_target"]))
    with _jax.named_scope("update"):
        delta_w, new_m, new_v = {}, {}, {}
        for n in TWIN_WEIGHTS:
            delta_w[n], new_m[n], new_v[n] = _adamw(weights[n], grad_w[n], given["m_" + n], given["v_" + n])
    return (loss, grad_x, *[grad_w[n] for n in TWIN_WEIGHTS], *[delta_w[n] for n in TWIN_WEIGHTS],
            *[new_m[n] for n in TWIN_WEIGHTS], *[new_v[n] for n in TWIN_WEIGHTS])
```

```python
import functools

import jax
import jax.numpy as jnp
from jax import lax
from jax.experimental import pallas as pl
from jax.experimental.pallas import tpu as pltpu

F32 = jnp.float32
BF16 = jnp.bfloat16

N_DEV = 8
HEAD_DIM = 64
LANES = 128
BLOCK = 128
Q_PER_KV = 8
POOL_WINDOWS = (2, 4, 8, 16)
POOL_HALO = 16
DILATED_PAIRS = ((128, 1), (512, 4), (2048, 16))
SWA_MAX_DIST = 127
RMS_EPS = 1e-5
PACK_W = 1024
NEG = -1e30

ADAM_LR = 0.001
ADAM_B1 = 0.9
ADAM_B2 = 0.999
ADAM_EPS = 1e-08
ADAM_WD = 0.01
ADAM_STEP = 10

VMEM_LIMIT = 48 * 1024 * 1024


def _params(*sem):
    return pltpu.CompilerParams(dimension_semantics=sem if sem else None, vmem_limit_bytes=VMEM_LIMIT)


def _pick(dim, target, mult=LANES):
    if dim <= target:
        return dim
    t = target - target % mult
    while dim % t:
        t -= mult
    return t


def _sigmoid(x):
    return 1.0 / (1.0 + jnp.exp(-x))


def _exchange(payloads, name):
    n_pay = len(payloads)
    bcast = [b for _, b in payloads]
    out_shapes = []
    for a, b in payloads:
        shp = a.shape if b else a.shape[1:]
        out_shapes.append(jax.ShapeDtypeStruct((N_DEV,) + tuple(shp), a.dtype))

    def body(*refs):
        send = refs[:n_pay]
        recv = refs[n_pay:2 * n_pay]
        send_sems, recv_sems, local_sems = refs[2 * n_pay:]
        x, y, c = lax.axis_index("x"), lax.axis_index("y"), lax.axis_index("c")
        me = 4 * x + 2 * y + c
        locals_ = []
        for k in range(n_pay):
            src = send[k] if bcast[k] else send[k].at[me]
            cp = pltpu.make_async_copy(src, recv[k].at[me], local_sems.at[k])
            cp.start()
            locals_.append(cp)
        sends = []
        for r in range(1, N_DEV):
            px, py, pc = x ^ ((r >> 2) & 1), y ^ ((r >> 1) & 1), c ^ (r & 1)
            peer = 4 * px + 2 * py + pc
            for k in range(n_pay):
                src = send[k] if bcast[k] else send[k].at[peer]
                cp = pltpu.make_async_remote_copy(
                    src_ref=src, dst_ref=recv[k].at[me],
                    send_sem=send_sems.at[k, r - 1], recv_sem=recv_sems.at[k, r - 1],
                    device_id=(px, py, pc), device_id_type=pl.DeviceIdType.MESH)
                cp.start()
                sends.append(cp)
        for r in range(1, N_DEV):
            px, py, pc = x ^ ((r >> 2) & 1), y ^ ((r >> 1) & 1), c ^ (r & 1)
            peer = 4 * px + 2 * py + pc
            for k in range(n_pay):
                src = send[k] if bcast[k] else send[k].at[peer]
                pltpu.make_async_remote_copy(
                    src_ref=src, dst_ref=recv[k].at[peer],
                    send_sem=send_sems.at[k, r - 1], recv_sem=recv_sems.at[k, r - 1],
                    device_id=(px, py, pc), device_id_type=pl.DeviceIdType.MESH).wait_recv()
        for cp in sends:
            cp.wait_send()
        for cp in locals_:
            cp.wait()

    any_spec = pl.BlockSpec(memory_space=pl.ANY)
    return pl.pallas_call(
        body, name=name,
        out_shape=tuple(out_shapes),
        in_specs=[any_spec] * n_pay,
        out_specs=tuple([any_spec] * n_pay),
        scratch_shapes=[pltpu.SemaphoreType.DMA((n_pay, N_DEV - 1)),
                        pltpu.SemaphoreType.DMA((n_pay, N_DEV - 1)),
                        pltpu.SemaphoreType.DMA((n_pay,))],
    )(*[a for a, _ in payloads])


def _matmul(a, b, out_dtype, name, tm=1024, tn=1024, tk=1024):
    m, kdim = a.shape
    n = b.shape[1]
    tm, tn, tk = _pick(m, tm), _pick(n, tn), _pick(kdim, tk)
    nk = kdim // tk

    if nk == 1:
        def body(a_ref, b_ref, o_ref):
            o_ref[...] = jnp.dot(a_ref[...], b_ref[...], preferred_element_type=F32).astype(o_ref.dtype)
        scratch = []
    else:
        def body(a_ref, b_ref, o_ref, acc_ref):
            kk = pl.program_id(2)

            @pl.when(kk == 0)
            def _():
                acc_ref[...] = jnp.zeros_like(acc_ref)

            acc_ref[...] += jnp.dot(a_ref[...], b_ref[...], preferred_element_type=F32)

            @pl.when(kk == nk - 1)
            def _():
                o_ref[...] = acc_ref[...].astype(o_ref.dtype)
        scratch = [pltpu.VMEM((tm, tn), F32)]

    return pl.pallas_call(
        body, name=name,
        out_shape=jax.ShapeDtypeStruct((m, n), out_dtype),
        grid=(m // tm, n // tn, nk),
        in_specs=[pl.BlockSpec((tm, tk), lambda i, j, k: (i, k)),
                  pl.BlockSpec((tk, tn), lambda i, j, k: (k, j))],
        out_specs=pl.BlockSpec((tm, tn), lambda i, j, k: (i, j)),
        scratch_shapes=scratch,
        compiler_params=_params("parallel", "parallel", "arbitrary"),
    )(a, b)


def _grouped_matmul(a, w, name, tm=1024):
    s, e = a.shape
    ng, g, _ = w.shape
    tm = _pick(s, tm)

    def body(a_ref, w_ref, o_ref):
        o_ref[...] = jnp.dot(a_ref[...], w_ref[...], preferred_element_type=F32)

    return pl.pallas_call(
        body, name=name,
        out_shape=jax.ShapeDtypeStruct((s, e), F32),
        grid=(s // tm, ng),
        in_specs=[pl.BlockSpec((tm, g), lambda i, j: (i, j)),
                  pl.BlockSpec((None, g, g), lambda i, j: (j, 0, 0))],
        out_specs=pl.BlockSpec((tm, g), lambda i, j: (i, j)),
        compiler_params=_params("parallel", "parallel"),
    )(a, w)


def _grouped_weight_grad(at, b, ng, name, tk=1024):
    e, s = at.shape
    g = e // ng
    tk = _pick(s, tk)
    nk = s // tk

    def body(a_ref, b_ref, o_ref):
        kk = pl.program_id(1)

        @pl.when(kk == 0)
        def _():
            o_ref[...] = jnp.zeros_like(o_ref)

        o_ref[...] += jnp.dot(a_ref[...], b_ref[...], preferred_element_type=F32)

    return pl.pallas_call(
        body, name=name,
        out_shape=jax.ShapeDtypeStruct((ng, g, g), F32),
        grid=(ng, nk),
        in_specs=[pl.BlockSpec((g, tk), lambda j, k: (j, k)),
                  pl.BlockSpec((tk, g), lambda j, k: (k, j))],
        out_specs=pl.BlockSpec((None, g, g), lambda j, k: (j, 0, 0)),
        compiler_params=_params("parallel", "arbitrary"),
    )(at, b)


def _rms(x):
    r = lax.rsqrt(jnp.mean(x * x, axis=1, keepdims=True) + RMS_EPS)
    return x * r, r


def _rmsnorm_fwd(x, g, name, ts=256):
    s, d = x.shape
    ts = _pick(s, ts, 8)

    def body(x_ref, g_ref, h_ref):
        xhat, _ = _rms(x_ref[...])
        h_ref[...] = (xhat * g_ref[...]).astype(BF16)

    return pl.pallas_call(
        body, name=name,
        out_shape=jax.ShapeDtypeStruct((s, d), BF16),
        grid=(s // ts,),
        in_specs=[pl.BlockSpec((ts, d), lambda i: (i, 0)), pl.BlockSpec((1, d), lambda i: (0, 0))],
        out_specs=pl.BlockSpec((ts, d), lambda i: (i, 0)),
        compiler_params=_params("parallel"),
    )(x, g)


def _outproj_norm(z, w, x, g, name, tm=512):
    s, e = z.shape
    d = w.shape[1]
    tm = _pick(s, tm)

    def body(z_ref, w_ref, x_ref, g_ref, xo_ref, h_ref):
        xn = x_ref[...] + jnp.dot(z_ref[...], w_ref[...], preferred_element_type=F32)
        xo_ref[...] = xn
        xhat, _ = _rms(xn)
        h_ref[...] = (xhat * g_ref[...]).astype(BF16)

    return pl.pallas_call(
        body, name=name,
        out_shape=(jax.ShapeDtypeStruct((s, d), F32), jax.ShapeDtypeStruct((s, d), BF16)),
        grid=(s // tm,),
        in_specs=[pl.BlockSpec((tm, e), lambda i: (i, 0)), pl.BlockSpec((e, d), lambda i: (0, 0)),
                  pl.BlockSpec((tm, d), lambda i: (i, 0)), pl.BlockSpec((1, d), lambda i: (0, 0))],
        out_specs=(pl.BlockSpec((tm, d), lambda i: (i, 0)), pl.BlockSpec((tm, d), lambda i: (i, 0))),
        compiler_params=_params("parallel"),
    )(z, w, x, g)


def _outproj_loss(z, w, x, g, target, name, tm=512):
    s, e = z.shape
    d = w.shape[1]
    tm = _pick(s, tm)

    def body(z_ref, w_ref, x_ref, g_ref, t_ref, dx_ref, dxb_ref, dg_ref, loss_ref):
        i = pl.program_id(0)
        xn = x_ref[...] + jnp.dot(z_ref[...], w_ref[...], preferred_element_type=F32)
        xhat, r = _rms(xn)
        gain = g_ref[...]
        diff = xhat * gain - t_ref[...]
        dout = diff * (1.0 / d)
        dxhat = dout * gain
        dx = r * (dxhat - xhat * jnp.mean(dxhat * xhat, axis=1, keepdims=True))
        dx_ref[...] = dx
        dxb_ref[...] = dx.astype(BF16)

        @pl.when(i == 0)
        def _():
            dg_ref[...] = jnp.zeros_like(dg_ref)
            loss_ref[...] = jnp.zeros_like(loss_ref)

        dg_ref[...] += jnp.sum(dout * xhat, axis=0, keepdims=True)
        loss_ref[...] += jnp.sum(diff * diff, axis=0, keepdims=True)

    row = lambda i: (i, 0)
    fixed = lambda i: (0, 0)
    return pl.pallas_call(
        body, name=name,
        out_shape=(jax.ShapeDtypeStruct((s, d), F32), jax.ShapeDtypeStruct((s, d), BF16),
                   jax.ShapeDtypeStruct((1, d), F32), jax.ShapeDtypeStruct((1, d), F32)),
        grid=(s // tm,),
        in_specs=[pl.BlockSpec((tm, e), row), pl.BlockSpec((e, d), fixed), pl.BlockSpec((tm, d), row),
                  pl.BlockSpec((1, d), fixed), pl.BlockSpec((tm, d), row)],
        out_specs=(pl.BlockSpec((tm, d), row), pl.BlockSpec((tm, d), row),
                   pl.BlockSpec((1, d), fixed), pl.BlockSpec((1, d), fixed)),
        compiler_params=_params("arbitrary"),
    )(z, w, x, g, target)


def _rmsnorm_bwd(dh, x, g, dx_next, name, ts=256):
    s, d = x.shape
    ts = _pick(s, ts, 8)

    def body(dh_ref, x_ref, g_ref, dn_ref, dx_ref, dxb_ref, dg_ref):
        i = pl.program_id(0)
        xhat, r = _rms(x_ref[...])
        dh_ = dh_ref[...]
        dxhat = dh_ * g_ref[...]
        dx = dn_ref[...] + r * (dxhat - xhat * jnp.mean(dxhat * xhat, axis=1, keepdims=True))
        dx_ref[...] = dx
        dxb_ref[...] = dx.astype(BF16)

        @pl.when(i == 0)
        def _():
            dg_ref[...] = jnp.zeros_like(dg_ref)

        dg_ref[...] += jnp.sum(dh_ * xhat, axis=0, keepdims=True)

    row = lambda i: (i, 0)
    fixed = lambda i: (0, 0)
    return pl.pallas_call(
        body, name=name,
        out_shape=(jax.ShapeDtypeStruct((s, d), F32), jax.ShapeDtypeStruct((s, d), BF16),
                   jax.ShapeDtypeStruct((1, d), F32)),
        grid=(s // ts,),
        in_specs=[pl.BlockSpec((ts, d), row), pl.BlockSpec((ts, d), row), pl.BlockSpec((1, d), fixed),
                  pl.BlockSpec((ts, d), row)],
        out_specs=(pl.BlockSpec((ts, d), row), pl.BlockSpec((ts, d), row), pl.BlockSpec((1, d), fixed)),
        compiler_params=_params("arbitrary"),
    )(dh, x, g, dx_next)


def _pool_counts(t0, rows, cols, window):
    t = t0 + lax.broadcasted_iota(jnp.int32, (rows, cols), 0)
    return jnp.minimum(t + 1, window).astype(F32)


def _pool_fwd(u, name, ts=1024, tc=256):
    s, e = u.shape
    ng = len(POOL_WINDOWS)
    gdim = e // ng
    ts, tc = _pick(s, ts), _pick(gdim, tc)
    cpg = gdim // tc
    hb = ts // POOL_HALO

    def body(u_ref, halo_ref, d_ref):
        i, grp = pl.program_id(0), pl.program_id(1)
        cur = u_ref[...]
        halo = jnp.where(i > 0, halo_ref[...], 0.0)
        ext = jnp.concatenate([halo, cur], axis=0)
        for gi, window in enumerate(POOL_WINDOWS):
            @pl.when(grp == gi)
            def _(window=window):
                acc = ext
                k = 1
                while k < window:
                    acc = acc + pltpu.roll(acc, k, 0)
                    k *= 2
                pooled = acc[POOL_HALO:, :] / _pool_counts(i * ts, ts, tc, window)
                d_ref[...] = (pooled - cur).astype(BF16)

    return pl.pallas_call(
        body, name=name,
        out_shape=jax.ShapeDtypeStruct((s, e), BF16),
        grid=(s // ts, ng, cpg),
        in_specs=[pl.BlockSpec((ts, tc), lambda i, g, j: (i, g * cpg + j)),
                  pl.BlockSpec((POOL_HALO, tc), lambda i, g, j: (jnp.maximum(i * hb - 1, 0), g * cpg + j))],
        out_specs=pl.BlockSpec((ts, tc), lambda i, g, j: (i, g * cpg + j)),
        compiler_params=_params("parallel", "parallel", "parallel"),
    )(u, u)


def _pool_bwd(dd, name, ts=1024, tc=256):
    s, e = dd.shape
    ng = len(POOL_WINDOWS)
    gdim = e // ng
    ts, tc = _pick(s, ts), _pick(gdim, tc)
    cpg = gdim // tc
    hb = ts // POOL_HALO
    n_halo = s // POOL_HALO
    nst = s // ts

    def body(dd_ref, halo_ref, du_ref):
        i, grp = pl.program_id(0), pl.program_id(1)
        cur = dd_ref[...]
        halo = jnp.where(i < nst - 1, halo_ref[...], 0.0)
        ext = jnp.concatenate([cur, halo], axis=0)
        rows = ts + POOL_HALO
        for gi, window in enumerate(POOL_WINDOWS):
            @pl.when(grp == gi)
            def _(window=window):
                acc = ext / _pool_counts(i * ts, rows, tc, window)
                k = 1
                while k < window:
                    acc = acc + pltpu.roll(acc, rows - k, 0)
                    k *= 2
                du_ref[...] = (acc[:ts, :] - cur).astype(BF16)

    return pl.pallas_call(
        body, name=name,
        out_shape=jax.ShapeDtypeStruct((s, e), BF16),
        grid=(nst, ng, cpg),
        in_specs=[pl.BlockSpec((ts, tc), lambda i, g, j: (i, g * cpg + j)),
                  pl.BlockSpec((POOL_HALO, tc),
                               lambda i, g, j: (jnp.minimum((i + 1) * hb, n_halo - 1), g * cpg + j))],
        out_specs=pl.BlockSpec((ts, tc), lambda i, g, j: (i, g * cpg + j)),
        compiler_params=_params("parallel", "parallel", "parallel"),
    )(dd, dd)


def _a_group_fwd(d, w, scale, gate, name, tm=1024):
    s, e = d.shape
    ng, g, _ = w.shape
    tm = _pick(s, tm)

    def body(d_ref, w_ref, s_ref, gate_ref, yr_ref, z_ref):
        yr = jnp.dot(d_ref[...], w_ref[...], preferred_element_type=F32)
        yr_ref[...] = yr
        gt = gate_ref[...]
        z_ref[...] = ((yr * s_ref[...]) * (gt * _sigmoid(gt))).astype(BF16)

    blk = lambda i, j: (i, j)
    return pl.pallas_call(
        body, name=name,
        out_shape=(jax.ShapeDtypeStruct((s, e), F32), jax.ShapeDtypeStruct((s, e), BF16)),
        grid=(s // tm, ng),
        in_specs=[pl.BlockSpec((tm, g), blk), pl.BlockSpec((None, g, g), lambda i, j: (j, 0, 0)),
                  pl.BlockSpec((1, g), lambda i, j: (0, j)), pl.BlockSpec((tm, g), blk)],
        out_specs=(pl.BlockSpec((tm, g), blk), pl.BlockSpec((tm, g), blk)),
        compiler_params=_params("parallel", "parallel"),
    )(d, w, scale, gate)


def _a_gate_bwd(dz, yr, gate, scale, name, ts=512, tc=512):
    s, e = dz.shape
    ts, tc = _pick(s, ts), _pick(e, tc)

    def body(dz_ref, yr_ref, gate_ref, s_ref, dgate_ref, dyr_ref, dscale_ref):
        i = pl.program_id(1)
        dz_, yr_, gt, sc = dz_ref[...], yr_ref[...], gate_ref[...], s_ref[...]
        sg = _sigmoid(gt)
        dy = dz_ * (gt * sg)
        dgate_ref[...] = (dz_ * (yr_ * sc) * (sg * (1.0 + gt * (1.0 - sg)))).astype(BF16)
        dyr_ref[...] = (dy * sc).astype(BF16)

        @pl.when(i == 0)
        def _():
            dscale_ref[...] = jnp.zeros_like(dscale_ref)

        dscale_ref[...] += jnp.sum(dy * yr_, axis=0, keepdims=True)

    blk = lambda j, i: (i, j)
    vec = lambda j, i: (0, j)
    return pl.pallas_call(
        body, name=name,
        out_shape=(jax.ShapeDtypeStruct((s, e), BF16), jax.ShapeDtypeStruct((s, e), BF16),
                   jax.ShapeDtypeStruct((1, e), F32)),
        grid=(e // tc, s // ts),
        in_specs=[pl.BlockSpec((ts, tc), blk), pl.BlockSpec((ts, tc), blk), pl.BlockSpec((ts, tc), blk),
                  pl.BlockSpec((1, tc), vec)],
        out_specs=(pl.BlockSpec((ts, tc), blk), pl.BlockSpec((ts, tc), blk), pl.BlockSpec((1, tc), vec)),
        compiler_params=_params("parallel", "arbitrary"),
    )(dz, yr, gate, scale)


def _gate_fwd(y, gate, name, ts=512, tc=512):
    s, e = y.shape
    ts, tc = _pick(s, ts), _pick(e, tc)

    def body(y_ref, gate_ref, z_ref):
        gt = gate_ref[...]
        z_ref[...] = (y_ref[...] * (gt * _sigmoid(gt))).astype(BF16)

    blk = lambda i, j: (i, j)
    return pl.pallas_call(
        body, name=name,
        out_shape=jax.ShapeDtypeStruct((s, e), BF16),
        grid=(s // ts, e // tc),
        in_specs=[pl.BlockSpec((ts, tc), blk)] * 2,
        out_specs=pl.BlockSpec((ts, tc), blk),
        compiler_params=_params("parallel", "parallel"),
    )(y, gate)


def _gate_bwd(dz, y, gate, name, ts=512, tc=512):
    s, e = dz.shape
    ts, tc = _pick(s, ts), _pick(e, tc)

    def body(dz_ref, y_ref, gate_ref, dgate_ref, dy_ref):
        dz_, gt = dz_ref[...], gate_ref[...]
        sg = _sigmoid(gt)
        dgate_ref[...] = (dz_ * y_ref[...] * (sg * (1.0 + gt * (1.0 - sg)))).astype(BF16)
        dy_ref[...] = (dz_ * (gt * sg)).astype(BF16)

    blk = lambda i, j: (i, j)
    return pl.pallas_call(
        body, name=name,
        out_shape=(jax.ShapeDtypeStruct((s, e), BF16), jax.ShapeDtypeStruct((s, e), BF16)),
        grid=(s // ts, e // tc),
        in_specs=[pl.BlockSpec((ts, tc), blk)] * 3,
        out_specs=(pl.BlockSpec((ts, tc), blk), pl.BlockSpec((ts, tc), blk)),
        compiler_params=_params("parallel", "parallel"),
    )(dz, y, gate)


def _merge_weights(l0, l1, l2):
    m = jnp.maximum(jnp.maximum(l0, l1), l2)
    e0, e1, e2 = jnp.exp(l0 - m), jnp.exp(l1 - m), jnp.exp(l2 - m)
    inv = 1.0 / (e0 + e1 + e2)
    return e0 * inv, e1 * inv, e2 * inv


def _merge_gate_fwd(outs, lses, gate, name, ts=512, tc=512):
    s, e = gate.shape
    ts, tc = _pick(s, ts), _pick(e, tc)

    def body(o0, o1, o2, l0, l1, l2, gate_ref, y_ref, z_ref):
        w0, w1, w2 = _merge_weights(l0[...], l1[...], l2[...])
        y = w0 * o0[...].astype(F32) + w1 * o1[...].astype(F32) + w2 * o2[...].astype(F32)
        y_ref[...] = y
        gt = gate_ref[...]
        z_ref[...] = (y * (gt * _sigmoid(gt))).astype(BF16)

    blk = lambda i, j: (i, j)
    return pl.pallas_call(
        body, name=name,
        out_shape=(jax.ShapeDtypeStruct((s, e), F32), jax.ShapeDtypeStruct((s, e), BF16)),
        grid=(s // ts, e // tc),
        in_specs=[pl.BlockSpec((ts, tc), blk)] * 7,
        out_specs=(pl.BlockSpec((ts, tc), blk), pl.BlockSpec((ts, tc), blk)),
        compiler_params=_params("parallel", "parallel"),
    )(*outs, *lses, gate)


def _merge_gate_bwd(dz, y, gate, lses, name, ts=512, tc=512):
    s, e = dz.shape
    ts, tc = _pick(s, ts), _pick(e, tc)

    def body(dz_ref, y_ref, gate_ref, l0, l1, l2, dgate_ref, d0, d1, d2):
        dz_, gt = dz_ref[...], gate_ref[...]
        sg = _sigmoid(gt)
        dgate_ref[...] = (dz_ * y_ref[...] * (sg * (1.0 + gt * (1.0 - sg)))).astype(BF16)
        dy = dz_ * (gt * sg)
        w0, w1, w2 = _merge_weights(l0[...], l1[...], l2[...])
        d0[...] = (w0 * dy).astype(BF16)
        d1[...] = (w1 * dy).astype(BF16)
        d2[...] = (w2 * dy).astype(BF16)

    blk = lambda i, j: (i, j)
    return pl.pallas_call(
        body, name=name,
        out_shape=tuple([jax.ShapeDtypeStruct((s, e), BF16)] * 4),
        grid=(s // ts, e // tc),
        in_specs=[pl.BlockSpec((ts, tc), blk)] * 6,
        out_specs=tuple([pl.BlockSpec((ts, tc), blk)] * 4),
        compiler_params=_params("parallel", "parallel"),
    )(dz, y, gate, *lses)


def _band_masks(max_dist):
    row = lax.broadcasted_iota(jnp.int32, (BLOCK, BLOCK), 0)
    col = lax.broadcasted_iota(jnp.int32, (BLOCK, BLOCK), 1)
    cur = col <= row
    prev = (col >= row) if max_dist == BLOCK else (col > row)
    return prev, cur


def _aligned(v):
    return v if isinstance(v, int) else pl.multiple_of(v, BLOCK)


def _head_mask(x, hm):
    return jnp.where(hm, x.astype(F32), 0.0).astype(BF16)


def _head_col(x, hm):
    return jnp.max(jnp.where(hm, x, NEG), axis=1, keepdims=True)


def _dot_nt(a, b):
    return lax.dot_general(a, b, (((1,), (1,)), ((), ())), preferred_element_type=F32)


def _attn_fwd(q, k, v, sinks, max_dist, rep, out_dtype, name, tq=1024):
    assert max_dist in (BLOCK - 1, BLOCK)
    l, w = q.shape
    n_pairs = w // LANES
    tq = _pick(l, tq)
    n = tq // BLOCK
    n_blk = l // BLOCK
    has_sink = sinks is not None
    scale = HEAD_DIM ** -0.5

    def body(*refs):
        if has_sink:
            sink_ref, refs = refs[0], refs[1:]
        q_ref, kc_ref, kh_ref, vc_ref, vh_ref, o_ref, lse_ref, kx, vx = refs
        i, p = pl.program_id(0), pl.program_id(1)
        kx[0:BLOCK, :] = kh_ref[...]
        kx[BLOCK:, :] = kc_ref[...]
        vx[0:BLOCK, :] = vh_ref[...]
        vx[BLOCK:, :] = vc_ref[...]
        lo = lax.broadcasted_iota(jnp.int32, (BLOCK, LANES), 1) < HEAD_DIM
        mask_prev0, mask_cur = _band_masks(max_dist)

        def step(j, carry):
            r0 = _aligned(j * BLOCK)
            r1 = _aligned(j * BLOCK + BLOCK)
            qb = q_ref[pl.ds(r0, BLOCK), :]
            kp, kc = kx[pl.ds(r0, BLOCK), :], kx[pl.ds(r1, BLOCK), :]
            vp, vc = vx[pl.ds(r0, BLOCK), :], vx[pl.ds(r1, BLOCK), :]
            mask_prev = jnp.logical_and(mask_prev0, jnp.logical_or(i > 0, j > 0))
            outs, lses = [], []
            for h in range(2):
                hm = lo if h == 0 else jnp.logical_not(lo)
                qh = _head_mask(qb, hm)
                sp = jnp.where(mask_prev, _dot_nt(qh, kp) * scale, NEG)
                sc = jnp.where(mask_cur, _dot_nt(qh, kc) * scale, NEG)
                m = jnp.maximum(jnp.max(sp, axis=1, keepdims=True), jnp.max(sc, axis=1, keepdims=True))
                if has_sink:
                    sk = sink_ref[2 * p + h]
                    m = jnp.maximum(m, sk)
                pp, pc = jnp.exp(sp - m), jnp.exp(sc - m)
                den = jnp.sum(pp, axis=1, keepdims=True) + jnp.sum(pc, axis=1, keepdims=True)
                if has_sink:
                    den = den + jnp.exp(sk - m)
                o = (jnp.dot(pp.astype(BF16), vp, preferred_element_type=F32)
                     + jnp.dot(pc.astype(BF16), vc, preferred_element_type=F32))
                outs.append(o * (1.0 / den))
                lses.append(m + jnp.log(den))
            o_ref[pl.ds(r0, BLOCK), :] = jnp.where(lo, outs[0], outs[1]).astype(o_ref.dtype)
            lse_ref[pl.ds(r0, BLOCK), :] = jnp.where(lo, lses[0], lses[1])
            return carry

        lax.fori_loop(0, n, step, 0)

    cur = lambda i, p: (i, p)
    kv_cur = lambda i, p: (i, p // rep)
    kv_halo = lambda i, p: (jnp.maximum(i * n - 1, 0), p // rep)
    in_specs = [pl.BlockSpec((tq, LANES), cur),
                pl.BlockSpec((tq, LANES), kv_cur), pl.BlockSpec((BLOCK, LANES), kv_halo),
                pl.BlockSpec((tq, LANES), kv_cur), pl.BlockSpec((BLOCK, LANES), kv_halo)]
    args = [q, k, k, v, v]
    if has_sink:
        in_specs = [pl.BlockSpec(memory_space=pltpu.SMEM)] + in_specs
        args = [sinks] + args
    del n_blk
    return pl.pallas_call(
        body, name=name,
        out_shape=(jax.ShapeDtypeStruct((l, w), out_dtype), jax.ShapeDtypeStruct((l, w), F32)),
        grid=(l // tq, n_pairs),
        in_specs=in_specs,
        out_specs=(pl.BlockSpec((tq, LANES), cur), pl.BlockSpec((tq, LANES), cur)),
        scratch_shapes=[pltpu.VMEM((tq + BLOCK, LANES), BF16), pltpu.VMEM((tq + BLOCK, LANES), BF16)],
        compiler_params=_params("parallel", "parallel"),
    )(*args)


def _attn_bwd(q, k, v, do, y, lse, sinks, max_dist, rep, name, tq=1024):
    l, w = q.shape
    n_pairs = w // LANES
    tq = _pick(l, tq)
    n = tq // BLOCK
    n_blk = l // BLOCK
    n_sb = l // tq
    has_sink = sinks is not None
    scale = HEAD_DIM ** -0.5
    kv_dtype = F32 if rep > 1 else BF16
    ext = tq + BLOCK

    def body(*refs):
        if has_sink:
            sink_ref, refs = refs[0], refs[1:]
        (q_ref, qn_ref, kc_ref, kh_ref, vc_ref, vh_ref, do_ref, don_ref, y_ref, yn_ref,
         lse_ref, lsen_ref) = refs[:12]
        refs = refs[12:]
        dq_ref, dk_ref, dv_ref = refs[:3]
        refs = refs[3:]
        if has_sink:
            dsink_ref, refs = refs[0], refs[1:]
        qx, dox, yx, lsex, kx, vx, dkx, dvx = refs
        i, p = pl.program_id(0), pl.program_id(1)
        qx[0:tq, :] = q_ref[...]
        qx[tq:, :] = qn_ref[...]
        dox[0:tq, :] = do_ref[...]
        dox[tq:, :] = don_ref[...]
        yx[0:tq, :] = y_ref[...]
        yx[tq:, :] = yn_ref[...]
        lsex[0:tq, :] = lse_ref[...]
        lsex[tq:, :] = lsen_ref[...]
        kx[0:BLOCK, :] = kh_ref[...]
        kx[BLOCK:, :] = kc_ref[...]
        vx[0:BLOCK, :] = vh_ref[...]
        vx[BLOCK:, :] = vc_ref[...]
        dkx[...] = jnp.zeros_like(dkx)
        dvx[...] = jnp.zeros_like(dvx)
        lo = lax.broadcasted_iota(jnp.int32, (BLOCK, LANES), 1) < HEAD_DIM
        mask_prev0, mask_cur0 = _band_masks(max_dist)

        def sub(j, with_cur, q_valid, sink_acc):
            r0 = _aligned(j * BLOCK)
            r1 = _aligned(j * BLOCK + BLOCK)
            qb, dob = qx[pl.ds(r0, BLOCK), :], dox[pl.ds(r0, BLOCK), :]
            yb, lseb = yx[pl.ds(r0, BLOCK), :].astype(F32), lsex[pl.ds(r0, BLOCK), :]
            kp, vp = kx[pl.ds(r0, BLOCK), :], vx[pl.ds(r0, BLOCK), :]
            mask_prev = jnp.logical_and(mask_prev0, jnp.logical_and(jnp.logical_or(i > 0, j > 0), q_valid))
            if with_cur:
                kc, vc = kx[pl.ds(r1, BLOCK), :], vx[pl.ds(r1, BLOCK), :]
            dq = jnp.zeros((BLOCK, LANES), F32)
            dkp = jnp.zeros((BLOCK, LANES), F32)
            dvp = jnp.zeros((BLOCK, LANES), F32)
            dkc = jnp.zeros((BLOCK, LANES), F32)
            dvc = jnp.zeros((BLOCK, LANES), F32)
            new_sink = []
            for h in range(2):
                hm = lo if h == 0 else jnp.logical_not(lo)
                qh = _head_mask(qb, hm)
                doh = _head_mask(dob, hm)
                delta = jnp.sum(doh.astype(F32) * yb, axis=1, keepdims=True)
                lse_c = _head_col(lseb, hm)
                pp = jnp.where(mask_prev, jnp.exp(_dot_nt(qh, kp) * scale - lse_c), 0.0)
                dsp = pp * (_dot_nt(doh, vp) - delta) * scale
                dq_h = jnp.dot(dsp.astype(BF16), kp, preferred_element_type=F32)
                dkp = dkp + jnp.dot(dsp.T.astype(BF16), qh, preferred_element_type=F32)
                dvp = dvp + jnp.dot(pp.T.astype(BF16), doh, preferred_element_type=F32)
                if with_cur:
                    pc = jnp.where(mask_cur0, jnp.exp(_dot_nt(qh, kc) * scale - lse_c), 0.0)
                    dsc = pc * (_dot_nt(doh, vc) - delta) * scale
                    dq_h = dq_h + jnp.dot(dsc.astype(BF16), kc, preferred_element_type=F32)
                    dkc = dkc + jnp.dot(dsc.T.astype(BF16), qh, preferred_element_type=F32)
                    dvc = dvc + jnp.dot(pc.T.astype(BF16), doh, preferred_element_type=F32)
                dq = jnp.where(hm, dq_h, dq)
                if has_sink:
                    ps = jnp.exp(sink_ref[2 * p + h] - lse_c)
                    new_sink.append(sink_acc[h] - ps * delta)
            dkx[pl.ds(r0, BLOCK), :] += dkp
            dvx[pl.ds(r0, BLOCK), :] += dvp
            if with_cur:
                dkx[pl.ds(r1, BLOCK), :] += dkc
                dvx[pl.ds(r1, BLOCK), :] += dvc
                dq_ref[pl.ds(r0, BLOCK), :] = dq.astype(dq_ref.dtype)
            return tuple(new_sink) if has_sink else sink_acc

        zero_col = jnp.zeros((BLOCK, 1), F32)
        sink_acc = lax.fori_loop(0, n, lambda j, acc: sub(j, True, True, acc), (zero_col, zero_col))
        sub(n, False, i < n_sb - 1, (zero_col, zero_col))

        dk_new, dv_new = dkx[BLOCK:, :], dvx[BLOCK:, :]
        if rep == 1:
            dk_ref[...] = dk_new.astype(dk_ref.dtype)
            dv_ref[...] = dv_new.astype(dv_ref.dtype)
        else:
            @pl.when(p % rep == 0)
            def _():
                dk_ref[...] = dk_new
                dv_ref[...] = dv_new

            @pl.when(p % rep != 0)
            def _():
                dk_ref[...] += dk_new
                dv_ref[...] += dv_new
        if has_sink:
            rowi = lax.broadcasted_iota(jnp.int32, (8, LANES), 0)
            s0 = jnp.sum(sink_acc[0], axis=0, keepdims=True)
            s1 = jnp.sum(sink_acc[1], axis=0, keepdims=True)
            dsink_ref[...] = jnp.where(rowi == 0, s0, jnp.where(rowi == 1, s1, 0.0))

    cur = lambda i, p: (i, p)
    nxt = lambda i, p: (jnp.minimum((i + 1) * n, n_blk - 1), p)
    kv_cur = lambda i, p: (i, p // rep)
    kv_halo = lambda i, p: (jnp.maximum(i * n - 1, 0), p // rep)
    big, small = (tq, LANES), (BLOCK, LANES)
    in_specs = [pl.BlockSpec(big, cur), pl.BlockSpec(small, nxt),
                pl.BlockSpec(big, kv_cur), pl.BlockSpec(small, kv_halo),
                pl.BlockSpec(big, kv_cur), pl.BlockSpec(small, kv_halo),
                pl.BlockSpec(big, cur), pl.BlockSpec(small, nxt),
                pl.BlockSpec(big, cur), pl.BlockSpec(small, nxt),
                pl.BlockSpec(big, cur), pl.BlockSpec(small, nxt)]
    args = [q, q, k, k, v, v, do, do, y, y, lse, lse]
    out_shape = [jax.ShapeDtypeStruct((l, w), BF16),
                 jax.ShapeDtypeStruct(k.shape, kv_dtype), jax.ShapeDtypeStruct(v.shape, kv_dtype)]
    out_specs = [pl.BlockSpec(big, cur), pl.BlockSpec(big, kv_cur), pl.BlockSpec(big, kv_cur)]
    if has_sink:
        in_specs = [pl.BlockSpec(memory_space=pltpu.SMEM)] + in_specs
        args = [sinks] + args
        out_shape.append(jax.ShapeDtypeStruct((n_sb, n_pairs, 8, LANES), F32))
        out_specs.append(pl.BlockSpec((None, None, 8, LANES), lambda i, p: (i, p, 0, 0)))
    return pl.pallas_call(
        body, name=name,
        out_shape=tuple(out_shape),
        grid=(n_sb, n_pairs),
        in_specs=in_specs,
        out_specs=tuple(out_specs),
        scratch_shapes=[pltpu.VMEM((ext, LANES), BF16), pltpu.VMEM((ext, LANES), BF16),
                        pltpu.VMEM((ext, LANES), y.dtype), pltpu.VMEM((ext, LANES), F32),
                        pltpu.VMEM((ext, LANES), BF16), pltpu.VMEM((ext, LANES), BF16),
                        pltpu.VMEM((ext, LANES), F32), pltpu.VMEM((ext, LANES), F32)],
        compiler_params=_params("parallel", "arbitrary"),
    )(*args)


def _sum_slots(recv, name, ts=256):
    nd, r, c = recv.shape
    ts = _pick(r, ts, 8)

    def body(r_ref, o_ref):
        acc = r_ref[0].astype(F32)
        for dev in range(1, nd):
            acc = acc + r_ref[dev].astype(F32)
        o_ref[...] = acc

    return pl.pallas_call(
        body, name=name,
        out_shape=jax.ShapeDtypeStruct((r, c), F32),
        grid=(r // ts,),
        in_specs=[pl.BlockSpec((nd, ts, c), lambda i: (0, i, 0))],
        out_specs=pl.BlockSpec((ts, c), lambda i: (i, 0)),
        compiler_params=_params("parallel"),
    )(recv)


def _adamw(w, g, m, v, name, ts=256):
    r, c = w.shape
    ts = _pick(r, ts, 8)
    c1 = 1.0 - ADAM_B1 ** ADAM_STEP
    c2 = 1.0 - ADAM_B2 ** ADAM_STEP

    def body(w_ref, g_ref, m_ref, v_ref, d_ref, mo_ref, vo_ref):
        g_ = g_ref[...]
        m_ = ADAM_B1 * m_ref[...] + (1.0 - ADAM_B1) * g_
        v_ = ADAM_B2 * v_ref[...] + (1.0 - ADAM_B2) * (g_ * g_)
        mo_ref[...] = m_
        vo_ref[...] = v_
        d_ref[...] = -ADAM_LR * ((m_ / c1) / (jnp.sqrt(v_ / c2) + ADAM_EPS) + ADAM_WD * w_ref[...])

    blk = pl.BlockSpec((ts, c), lambda i: (i, 0))
    return pl.pallas_call(
        body, name=name,
        out_shape=tuple([jax.ShapeDtypeStruct((r, c), F32)] * 3),
        grid=(r // ts,),
        in_specs=[blk] * 4,
        out_specs=(blk, blk, blk),
        compiler_params=_params("parallel"),
    )(w, g, m, v)


def _rows(a):
    flat = a.reshape(-1)
    pad = (-flat.shape[0]) % PACK_W
    if pad:
        flat = jnp.concatenate([flat, jnp.zeros((pad,), flat.dtype)])
    return flat.reshape(-1, PACK_W)


def _pad_rows(a, mult):
    pad = (-a.shape[-2]) % mult
    if pad:
        widths = [(0, 0)] * (a.ndim - 2) + [(0, pad), (0, 0)]
        a = jnp.pad(a, widths)
    return a


def _to_global(stack, axis):
    moved = jnp.moveaxis(stack, 0, axis)
    shp = list(moved.shape)
    shp[axis:axis + 2] = [shp[axis] * shp[axis + 1]]
    return moved.reshape(shp)


def _to_stack(full, axis):
    shp = list(full.shape)
    shp[axis:axis + 1] = [N_DEV, shp[axis] // N_DEV]
    return jnp.moveaxis(full.reshape(shp), axis, 0)


_BIG = (("w_out", 1), ("a_w_in", 2), ("a_w_group", 2), ("b_w_in", 2), ("c_w_in", 2))


def _dup_heads(wk, n_kv):
    d = wk.shape[0]
    return jnp.tile(wk.reshape(d, n_kv, 1, HEAD_DIM), (1, 1, 2, 1)).reshape(d, n_kv * LANES)


def _fold_heads(dwk, n_kv):
    d = dwk.shape[0]
    return dwk.reshape(d, n_kv, 2, HEAD_DIM).sum(axis=2).reshape(d, n_kv * HEAD_DIM)


def _view(a, dil):
    s, w = a.shape
    return a.reshape(s // dil, dil * w)


def _unview(a, dil):
    l, w = a.shape
    return a.reshape(l * dil, w // dil)


def kernel(x, norm_g, final_g, w_out, a_w_in, a_w_group, a_scale, b_w_in, b_sinks, c_w_in, loss_target, m_norm_g, m_final_g, m_w_out, m_a_w_in, m_a_w_group, m_a_scale, m_b_w_in, m_b_sinks, m_c_w_in, v_norm_g, v_final_g, v_w_out, v_a_w_in, v_a_w_group, v_a_scale, v_b_w_in, v_b_sinks, v_c_w_in):
    local = dict(w_out=w_out, a_w_in=a_w_in, a_w_group=a_w_group, b_w_in=b_w_in, c_w_in=c_w_in)
    mom_m = dict(w_out=m_w_out, a_w_in=m_a_w_in, a_w_group=m_a_w_group, b_w_in=m_b_w_in, c_w_in=m_c_w_in)
    mom_v = dict(w_out=v_w_out, a_w_in=v_a_w_in, a_w_group=v_a_w_group, b_w_in=v_b_w_in, c_w_in=v_c_w_in)
    s, d = x.shape[1], x.shape[2]
    depth = norm_g.shape[0]
    e = w_out.shape[1] * N_DEV
    n_heads = e // HEAD_DIM
    n_kv = n_heads // Q_PER_KV
    kv_w = n_kv * HEAD_DIM
    rep = Q_PER_KV // 2
    n_groups = len(POOL_WINDOWS)
    me = 4 * lax.axis_index("x") + 2 * lax.axis_index("y") + lax.axis_index("c")

    sizes = [local[n].size // PACK_W for n, _ in _BIG]
    offs = [sum(sizes[:k]) for k in range(len(sizes) + 1)]
    wpack = _pad_rows(jnp.concatenate([_rows(local[n].astype(BF16)) for n, _ in _BIG], axis=0), 16)
    r_pack = wpack.shape[0]
    spack = _pad_rows(_rows(a_scale), 8)
    wall, sall = _exchange([(wpack, True), (spack, True)], "gather_weights")
    full = {}
    for k, (name, axis) in enumerate(_BIG):
        stack = wall[:, offs[k]:offs[k + 1], :].reshape((N_DEV,) + local[name].shape)
        full[name] = _to_global(stack, axis)
    scale_full = _to_global(sall.reshape(N_DEV, -1)[:, :a_scale.size].reshape((N_DEV,) + a_scale.shape), 1)

    wout_t = jnp.swapaxes(full["w_out"], 1, 2)
    wa = full["a_w_in"]
    wa_t = jnp.swapaxes(wa, 1, 2)
    wg = full["a_w_group"]
    wg_t = jnp.swapaxes(wg, 2, 3)
    wb = full["b_w_in"][0]
    wb_ext = jnp.concatenate([wb[:, :e], _dup_heads(wb[:, e:e + kv_w], n_kv),
                              _dup_heads(wb[:, e + kv_w:e + 2 * kv_w], n_kv), wb[:, e + 2 * kv_w:]], axis=1)
    wb_ext_t = wb_ext.T
    kd_w = n_kv * LANES
    wc = full["c_w_in"][0]
    wc_t = wc.T

    xs, hs, zs, saved = [x.reshape(s, d)], [], [], []
    hs.append(_rmsnorm_fwd(xs[0], norm_g[0:1], "norm0"))
    loss_vec = dfinal = dx = dxb = None
    for i in range(depth):
        kind, j = i % 3, i // 3
        h = hs[i]
        tag = f"l{i}"
        if kind == 0:
            u = _matmul(h, wa[j][:, :e], F32, tag + "_in_u")
            gate = _matmul(h, wa[j][:, e:], F32, tag + "_in_gate")
            dpool = _pool_fwd(u, tag + "_pool")
            yr, z = _a_group_fwd(dpool, wg[j], scale_full[j:j + 1], gate, tag + "_group")
            saved.append(dict(dpool=dpool, yr=yr, gate=gate))
        elif kind == 1:
            q = _matmul(h, wb_ext[:, :e], BF16, tag + "_in_q")
            kd = _matmul(h, wb_ext[:, e:e + kd_w], BF16, tag + "_in_k")
            vd = _matmul(h, wb_ext[:, e + kd_w:e + 2 * kd_w], BF16, tag + "_in_v")
            gate = _matmul(h, wb_ext[:, e + 2 * kd_w:], F32, tag + "_in_gate")
            sinks = b_sinks[j]
            y, lse = _attn_fwd(q, kd, vd, sinks, SWA_MAX_DIST, rep, F32, tag + "_attn")
            z = _gate_fwd(y, gate, tag + "_gate")
            saved.append(dict(q=q, kd=kd, vd=vd, gate=gate, y=y, lse=lse, sinks=sinks))
        else:
            qkv, outs, lses = [], [], []
            for gi, (window, dil) in enumerate(DILATED_PAIRS):
                trio = [_view(_matmul(h, wc[:, (3 * gi + t) * e:(3 * gi + t + 1) * e], BF16,
                                      f"{tag}_in_{'qkv'[t]}{gi}"), dil) for t in range(3)]
                o, lse = _attn_fwd(trio[0], trio[1], trio[2], None, window // dil, 1, BF16, f"{tag}_attn{gi}")
                qkv.append(trio)
                outs.append(_unview(o, dil))
                lses.append(lse)
            gate = _matmul(h, wc[:, 9 * e:], F32, tag + "_in_gate")
            lses_tok = [_unview(lse, dil) for lse, (_, dil) in zip(lses, DILATED_PAIRS)]
            y, z = _merge_gate_fwd(outs, lses_tok, gate, tag + "_merge")
            saved.append(dict(qkv=qkv, lses=lses, lses_tok=lses_tok, gate=gate, y=y))
        zs.append(z)
        if i + 1 < depth:
            x_new, h_new = _outproj_norm(z, full["w_out"][i], xs[i], norm_g[i + 1:i + 2], tag + "_out")
            xs.append(x_new)
            hs.append(h_new)
        else:
            dx, dxb, dfinal, loss_vec = _outproj_loss(z, full["w_out"][i], xs[i], final_g.reshape(1, d),
                                                      loss_target.reshape(s, d), tag + "_out_loss")

    g_full = {"w_out": [None] * depth, "a_w_in": [None] * wa.shape[0], "a_w_group": [None] * wa.shape[0]}
    d_norm = [None] * depth
    d_scale = [None] * wa.shape[0]
    d_sinks = None
    for i in reversed(range(depth)):
        kind, j = i % 3, i // 3
        tag = f"b{i}"
        sv = saved[i]
        h_t = hs[i].T
        g_full["w_out"][i] = _matmul(zs[i].T, dxb, F32, tag + "_dwout")
        dz = _matmul(dxb, wout_t[i], F32, tag + "_dz")
        if kind == 0:
            dgate, dyr, dsc = _a_gate_bwd(dz, sv["yr"], sv["gate"], scale_full[j:j + 1], tag + "_gate")
            d_scale[j] = dsc
            dd = _grouped_matmul(dyr, wg_t[j], tag + "_dd")
            du = _pool_bwd(dd, tag + "_pool")
            g_full["a_w_group"][j] = _grouped_weight_grad(sv["dpool"].T, dyr, n_groups, tag + "_dwg")
            dp = jnp.concatenate([du, dgate], axis=1)
            w_t = wa_t[j]
        elif kind == 1:
            dgate, do = _gate_bwd(dz, sv["y"], sv["gate"], tag + "_gate")
            dq, dkd, dvd, dsink = _attn_bwd(sv["q"], sv["kd"], sv["vd"], do, sv["y"], sv["lse"], sv["sinks"],
                                            SWA_MAX_DIST, rep, tag + "_attn")
            d_sinks = dsink[:, :, 0:2, 0].sum(axis=0).reshape(1, n_heads)
            dp = jnp.concatenate([dq, dkd.astype(BF16), dvd.astype(BF16), dgate], axis=1)
            w_t = wb_ext_t
        else:
            dgate, *dos = _merge_gate_bwd(dz, sv["y"], sv["gate"], sv["lses_tok"], tag + "_merge")
            parts = []
            for gi, (window, dil) in enumerate(DILATED_PAIRS):
                qv, kv, vv = sv["qkv"][gi]
                grads = _attn_bwd(qv, kv, vv, _view(dos[gi], dil), _view(sv["y"], dil), sv["lses"][gi], None,
                                  window // dil, 1, f"{tag}_attn{gi}")
                parts += [_unview(g_, dil) for g_ in grads]
            dp = jnp.concatenate(parts + [dgate], axis=1)
            w_t = wc_t
        dw_in = _matmul(h_t, dp, F32, tag + "_dwin")
        if kind == 0:
            g_full["a_w_in"][j] = dw_in
        elif kind == 1:
            g_full["b_w_in"] = jnp.concatenate(
                [dw_in[:, :e], _fold_heads(dw_in[:, e:e + kd_w], n_kv),
                 _fold_heads(dw_in[:, e + kd_w:e + 2 * kd_w], n_kv), dw_in[:, e + 2 * kd_w:]], axis=1)[None]
        else:
            g_full["c_w_in"] = dw_in[None]
        dh = _matmul(dp, w_t, F32, tag + "_dh")
        dx, dxb, d_norm[i] = _rmsnorm_bwd(dh, xs[i], norm_g[i:i + 1], dx, tag + "_norm")
    grad_x = dx.reshape(x.shape)
    for name in ("w_out", "a_w_in", "a_w_group"):
        g_full[name] = jnp.stack(g_full[name], axis=0)

    gpack = jnp.concatenate(
        [_to_stack(g_full[n], axis).astype(BF16).reshape(N_DEV, -1, PACK_W) for n, axis in _BIG], axis=1)
    gpack = _pad_rows(gpack, 16)
    loss_local = (0.5 / d) * jnp.sum(loss_vec)
    small = [jnp.concatenate(d_norm, axis=0), dfinal, d_sinks, jnp.concatenate(d_scale, axis=0),
             loss_local.reshape(1, 1)]
    small_rows = [_rows(a) for a in small]
    small_offs = [sum(r.shape[0] for r in small_rows[:k]) for k in range(len(small_rows) + 1)]
    small_pack = _pad_rows(jnp.concatenate(small_rows, axis=0), 8)
    grecv, srecv = _exchange([(gpack, False), (small_pack, True)], "exchange_grads")
    gsum = _sum_slots(grecv, "sum_grads")
    ssum = _sum_slots(srecv, "sum_small")

    def small_part(k, like):
        return ssum[small_offs[k]:small_offs[k + 1]].reshape(-1)[:like.size].reshape(like.shape)

    g_norm = small_part(0, norm_g)
    g_final = small_part(1, final_g)
    g_sinks = small_part(2, b_sinks)
    g_scale_full = small_part(3, scale_full)
    loss = ssum[small_offs[4], 0]
    g_scale = lax.dynamic_slice_in_dim(g_scale_full, me * a_scale.shape[1], a_scale.shape[1], axis=1)

    small_w = [("norm_g", norm_g, m_norm_g, v_norm_g, g_norm), ("final_g", final_g, m_final_g, v_final_g, g_final),
               ("a_scale", a_scale, m_a_scale, v_a_scale, g_scale), ("b_sinks", b_sinks, m_b_sinks, v_b_sinks, g_sinks)]
    big_rows = lambda tree: _pad_rows(jnp.concatenate([_rows(tree[n]) for n, _ in _BIG], axis=0), 16)
    tail = lambda idx: [_rows(t[idx]) for t in small_w]
    tail_sizes = [r.shape[0] for r in tail(1)]
    tail_offs = [r_pack + sum(tail_sizes[:k]) for k in range(len(tail_sizes) + 1)]
    w_all = _pad_rows(jnp.concatenate([big_rows(local)] + tail(1), axis=0), 8)
    m_all = _pad_rows(jnp.concatenate([big_rows(mom_m)] + tail(2), axis=0), 8)
    v_all = _pad_rows(jnp.concatenate([big_rows(mom_v)] + tail(3), axis=0), 8)
    g_all = _pad_rows(jnp.concatenate([gsum] + tail(4), axis=0), 8)
    delta_all, m_new, v_new = _adamw(w_all, g_all, m_all, v_all, "adamw")

    def unpack(packed):
        out = {}
        for k, (name, _) in enumerate(_BIG):
            out[name] = packed[offs[k]:offs[k + 1]].reshape(local[name].shape)
        for k, (name, w_, _, _, _) in enumerate(small_w):
            out[name] = packed[tail_offs[k]:tail_offs[k + 1]].reshape(-1)[:w_.size].reshape(w_.shape)
        return out

    order = ("norm_g", "final_g", "w_out", "a_w_in", "a_w_group", "a_scale", "b_w_in", "b_sinks", "c_w_in")
    grads = unpack(g_all)
    deltas, new_m, new_v = unpack(delta_all), unpack(m_new), unpack(v_new)
    return (loss, grad_x, *[grads[n] for n in order], *[deltas[n] for n in order],
            *[new_m[n] for n in order], *[new_v[n] for n in order])
```

```python
import functools

import jax
import jax.numpy as jnp
from jax import lax
from jax.experimental import pallas as pl
from jax.experimental.pallas import tpu as pltpu

F32 = jnp.float32
BF16 = jnp.bfloat16

N_DEV = 8
HEAD_DIM = 64
LANES = 128
BLOCK = 128
Q_PER_KV = 8
POOL_WINDOWS = (2, 4, 8, 16)
POOL_HALO = 16
DILATED_PAIRS = ((128, 1), (512, 4), (2048, 16))
SWA_MAX_DIST = 127
RMS_EPS = 1e-5
PACK_W = 1024
ADAM_ROWS = 256
NEG = -1e30

ADAM_LR = 0.001
ADAM_B1 = 0.9
ADAM_B2 = 0.999
ADAM_EPS = 1e-08
ADAM_WD = 0.01
ADAM_STEP = 10

VMEM_LIMIT = 48 * 1024 * 1024


def _params(*sem):
    return pltpu.CompilerParams(dimension_semantics=sem if sem else None, vmem_limit_bytes=VMEM_LIMIT)


def _pick(dim, target, mult=LANES):
    if dim <= target:
        return dim
    t = target - target % mult
    while dim % t:
        t -= mult
    return t


def _sigmoid(x):
    return 1.0 / (1.0 + jnp.exp(-x))


CHIP_OFFSETS = (2, 4, 6)
ANY_SPEC = pl.BlockSpec(memory_space=pl.ANY)


def _where_am_i():
    x, y, c = lax.axis_index("x"), lax.axis_index("y"), lax.axis_index("c")
    return x, y, c, 4 * x + 2 * y + c


def _peer(x, y, c, r):
    return x ^ ((r >> 2) & 1), y ^ ((r >> 1) & 1), c ^ (r & 1)


def _gather(blocks, name):
    n = len(blocks)

    def body(*refs):
        send, recv = refs[:n], refs[n:2 * n]
        send_sems, recv_sems, local_sems = refs[2 * n:]
        x, y, c, me = _where_am_i()
        sib = _peer(x, y, c, 1)
        sib_id = me ^ 1

        def copy(k, slot_sem, src, slot, to):
            return pltpu.make_async_remote_copy(
                src_ref=src, dst_ref=recv[k].at[slot], send_sem=send_sems.at[k, slot_sem],
                recv_sem=recv_sems.at[k, slot_sem], device_id=to, device_id_type=pl.DeviceIdType.MESH)

        started = []
        for k in range(n):
            own = pltpu.make_async_copy(send[k], recv[k].at[me], local_sems.at[k])
            own.start()
            started.append(own)
        sends = []
        for k in range(n):
            sends.append(copy(k, 0, send[k], me, sib))
            for j, r in enumerate(CHIP_OFFSETS):
                sends.append(copy(k, 1 + j, send[k], me, _peer(x, y, c, r)))
        for cp in sends:
            cp.start()
        for j, r in enumerate(CHIP_OFFSETS):
            for k in range(n):
                src_id = me ^ r
                copy(k, 1 + j, send[k], src_id, sib).wait_recv()
                fwd = copy(k, 4 + j, recv[k].at[src_id], src_id, sib)
                fwd.start()
                sends.append(fwd)
        for k in range(n):
            copy(k, 0, send[k], sib_id, sib).wait_recv()
            for j, r in enumerate(CHIP_OFFSETS):
                copy(k, 4 + j, send[k], sib_id ^ r, sib).wait_recv()
        for cp in sends:
            cp.wait_send()
        for own in started:
            own.wait()

    return pl.pallas_call(
        body, name=name,
        out_shape=tuple(jax.ShapeDtypeStruct((N_DEV,) + b.shape, b.dtype) for b in blocks),
        in_specs=[ANY_SPEC] * n,
        out_specs=tuple([ANY_SPEC] * n),
        scratch_shapes=[pltpu.SemaphoreType.DMA((n, N_DEV - 1)), pltpu.SemaphoreType.DMA((n, N_DEV - 1)),
                        pltpu.SemaphoreType.DMA((n,))],
    )(*blocks)


def _sibling_exchange(gpack, name):
    n_chips = N_DEV // 2

    def body(g_ref, t_ref, send_sems, recv_sems):
        x, y, c, _ = _where_am_i()
        sib = _peer(x, y, c, 1)
        copies = [pltpu.make_async_remote_copy(
            src_ref=g_ref.at[2 * chip + (1 - c)], dst_ref=t_ref.at[chip], send_sem=send_sems.at[chip],
            recv_sem=recv_sems.at[chip], device_id=sib, device_id_type=pl.DeviceIdType.MESH)
            for chip in range(n_chips)]
        for cp in copies:
            cp.start()
        for cp in copies:
            cp.wait_recv()
        for cp in copies:
            cp.wait_send()

    return pl.pallas_call(
        body, name=name,
        out_shape=jax.ShapeDtypeStruct((n_chips,) + gpack.shape[1:], gpack.dtype),
        in_specs=[ANY_SPEC], out_specs=ANY_SPEC,
        scratch_shapes=[pltpu.SemaphoreType.DMA((n_chips,)), pltpu.SemaphoreType.DMA((n_chips,))],
    )(gpack)


def _chip_exchange(csum, small, name):
    n_chips = N_DEV // 2

    def body(c_ref, s_ref, r_ref, sr_ref, send_sems, recv_sems, small_send, small_recv, local_sems):
        x, y, c, me = _where_am_i()
        my_chip = 2 * x + y
        own = [pltpu.make_async_copy(c_ref.at[my_chip], r_ref.at[my_chip], local_sems.at[0]),
               pltpu.make_async_copy(s_ref, sr_ref.at[me], local_sems.at[1])]
        for cp in own:
            cp.start()
        sends, recvs = [], []
        for j, r in enumerate(CHIP_OFFSETS):
            to = _peer(x, y, c, r)
            chip = my_chip ^ (r >> 1)
            sends.append(pltpu.make_async_remote_copy(
                src_ref=c_ref.at[chip], dst_ref=r_ref.at[my_chip], send_sem=send_sems.at[j],
                recv_sem=recv_sems.at[j], device_id=to, device_id_type=pl.DeviceIdType.MESH))
            recvs.append(pltpu.make_async_remote_copy(
                src_ref=c_ref.at[chip], dst_ref=r_ref.at[chip], send_sem=send_sems.at[j],
                recv_sem=recv_sems.at[j], device_id=to, device_id_type=pl.DeviceIdType.MESH))
        for r in range(1, N_DEV):
            to = _peer(x, y, c, r)
            sends.append(pltpu.make_async_remote_copy(
                src_ref=s_ref, dst_ref=sr_ref.at[me], send_sem=small_send.at[r - 1],
                recv_sem=small_recv.at[r - 1], device_id=to, device_id_type=pl.DeviceIdType.MESH))
            recvs.append(pltpu.make_async_remote_copy(
                src_ref=s_ref, dst_ref=sr_ref.at[me ^ r], send_sem=small_send.at[r - 1],
                recv_sem=small_recv.at[r - 1], device_id=to, device_id_type=pl.DeviceIdType.MESH))
        for cp in sends:
            cp.start()
        for cp in recvs:
            cp.wait_recv()
        for cp in sends:
            cp.wait_send()
        for cp in own:
            cp.wait()

    n_off = len(CHIP_OFFSETS)
    return pl.pallas_call(
        body, name=name,
        out_shape=(jax.ShapeDtypeStruct(csum.shape, csum.dtype),
                   jax.ShapeDtypeStruct((N_DEV,) + small.shape, small.dtype)),
        in_specs=[ANY_SPEC, ANY_SPEC], out_specs=(ANY_SPEC, ANY_SPEC),
        scratch_shapes=[pltpu.SemaphoreType.DMA((n_off,)), pltpu.SemaphoreType.DMA((n_off,)),
                        pltpu.SemaphoreType.DMA((N_DEV - 1,)), pltpu.SemaphoreType.DMA((N_DEV - 1,)),
                        pltpu.SemaphoreType.DMA((2,))],
    )(csum, small)


def _pair_sum(gpack, other, core, name, ts=256):
    n_chips, r, c = other.shape
    ts = _pick(r, ts, 16)

    def body(core_ref, g_ref, o_ref, out_ref):
        del core_ref
        out_ref[...] = (g_ref[...].astype(F32) + o_ref[...].astype(F32)).astype(out_ref.dtype)

    return pl.pallas_call(
        body, name=name,
        out_shape=jax.ShapeDtypeStruct(other.shape, other.dtype),
        grid_spec=pltpu.PrefetchScalarGridSpec(
            num_scalar_prefetch=1, grid=(n_chips, r // ts),
            in_specs=[pl.BlockSpec((None, ts, c), lambda j, i, core_ref: (2 * j + core_ref[0], i, 0)),
                      pl.BlockSpec((None, ts, c), lambda j, i, core_ref: (j, i, 0))],
            out_specs=pl.BlockSpec((None, ts, c), lambda j, i, core_ref: (j, i, 0))),
        compiler_params=_params("parallel", "parallel"),
    )(core, gpack, other)


def _matmul(a, b, out_dtype, name, tm=1024, tn=1024, tk=1024):
    m, kdim = a.shape
    n = b.shape[1]
    tm, tn, tk = _pick(m, tm), _pick(n, tn), _pick(kdim, tk)
    nk = kdim // tk

    if nk == 1:
        def body(a_ref, b_ref, o_ref):
            o_ref[...] = jnp.dot(a_ref[...], b_ref[...], preferred_element_type=F32).astype(o_ref.dtype)
        scratch = []
    else:
        def body(a_ref, b_ref, o_ref, acc_ref):
            kk = pl.program_id(2)

            @pl.when(kk == 0)
            def _():
                acc_ref[...] = jnp.zeros_like(acc_ref)

            acc_ref[...] += jnp.dot(a_ref[...], b_ref[...], preferred_element_type=F32)

            @pl.when(kk == nk - 1)
            def _():
                o_ref[...] = acc_ref[...].astype(o_ref.dtype)
        scratch = [pltpu.VMEM((tm, tn), F32)]

    return pl.pallas_call(
        body, name=name,
        out_shape=jax.ShapeDtypeStruct((m, n), out_dtype),
        grid=(m // tm, n // tn, nk),
        in_specs=[pl.BlockSpec((tm, tk), lambda i, j, k: (i, k)),
                  pl.BlockSpec((tk, tn), lambda i, j, k: (k, j))],
        out_specs=pl.BlockSpec((tm, tn), lambda i, j, k: (i, j)),
        scratch_shapes=scratch,
        compiler_params=_params("parallel", "parallel", "arbitrary"),
    )(a, b)


def _grouped_matmul(a, w, name, tm=1024):
    s, e = a.shape
    ng, g, _ = w.shape
    tm = _pick(s, tm)

    def body(a_ref, w_ref, o_ref):
        o_ref[...] = jnp.dot(a_ref[...], w_ref[...], preferred_element_type=F32)

    return pl.pallas_call(
        body, name=name,
        out_shape=jax.ShapeDtypeStruct((s, e), F32),
        grid=(s // tm, ng),
        in_specs=[pl.BlockSpec((tm, g), lambda i, j: (i, j)),
                  pl.BlockSpec((None, g, g), lambda i, j: (j, 0, 0))],
        out_specs=pl.BlockSpec((tm, g), lambda i, j: (i, j)),
        compiler_params=_params("parallel", "parallel"),
    )(a, w)


def _grouped_weight_grad(at, b, ng, name, tk=1024):
    e, s = at.shape
    g = e // ng
    tk = _pick(s, tk)
    nk = s // tk

    def body(a_ref, b_ref, o_ref):
        kk = pl.program_id(1)

        @pl.when(kk == 0)
        def _():
            o_ref[...] = jnp.zeros_like(o_ref)

        o_ref[...] += jnp.dot(a_ref[...], b_ref[...], preferred_element_type=F32)

    return pl.pallas_call(
        body, name=name,
        out_shape=jax.ShapeDtypeStruct((ng, g, g), F32),
        grid=(ng, nk),
        in_specs=[pl.BlockSpec((g, tk), lambda j, k: (j, k)),
                  pl.BlockSpec((tk, g), lambda j, k: (k, j))],
        out_specs=pl.BlockSpec((None, g, g), lambda j, k: (j, 0, 0)),
        compiler_params=_params("parallel", "arbitrary"),
    )(at, b)


def _rms(x):
    r = lax.rsqrt(jnp.mean(x * x, axis=1, keepdims=True) + RMS_EPS)
    return x * r, r


def _rmsnorm_fwd(x, g, name, ts=256):
    s, d = x.shape
    ts = _pick(s, ts, 8)

    def body(x_ref, g_ref, h_ref):
        xhat, _ = _rms(x_ref[...])
        h_ref[...] = (xhat * g_ref[...]).astype(BF16)

    return pl.pallas_call(
        body, name=name,
        out_shape=jax.ShapeDtypeStruct((s, d), BF16),
        grid=(s // ts,),
        in_specs=[pl.BlockSpec((ts, d), lambda i: (i, 0)), pl.BlockSpec((1, d), lambda i: (0, 0))],
        out_specs=pl.BlockSpec((ts, d), lambda i: (i, 0)),
        compiler_params=_params("parallel"),
    )(x, g)


def _outproj_norm(z, w, x, g, name, tm=512):
    s, e = z.shape
    d = w.shape[1]
    tm = _pick(s, tm)

    def body(z_ref, w_ref, x_ref, g_ref, xo_ref, h_ref):
        xn = x_ref[...] + jnp.dot(z_ref[...], w_ref[...], preferred_element_type=F32)
        xo_ref[...] = xn
        xhat, _ = _rms(xn)
        h_ref[...] = (xhat * g_ref[...]).astype(BF16)

    return pl.pallas_call(
        body, name=name,
        out_shape=(jax.ShapeDtypeStruct((s, d), F32), jax.ShapeDtypeStruct((s, d), BF16)),
        grid=(s // tm,),
        in_specs=[pl.BlockSpec((tm, e), lambda i: (i, 0)), pl.BlockSpec((e, d), lambda i: (0, 0)),
                  pl.BlockSpec((tm, d), lambda i: (i, 0)), pl.BlockSpec((1, d), lambda i: (0, 0))],
        out_specs=(pl.BlockSpec((tm, d), lambda i: (i, 0)), pl.BlockSpec((tm, d), lambda i: (i, 0))),
        compiler_params=_params("parallel"),
    )(z, w, x, g)


def _outproj_loss(z, w, x, g, target, name, tm=512):
    s, e = z.shape
    d = w.shape[1]
    tm = _pick(s, tm)

    def body(z_ref, w_ref, x_ref, g_ref, t_ref, dx_ref, dxb_ref, dg_ref, loss_ref):
        i = pl.program_id(0)
        xn = x_ref[...] + jnp.dot(z_ref[...], w_ref[...], preferred_element_type=F32)
        xhat, r = _rms(xn)
        gain = g_ref[...]
        diff = xhat * gain - t_ref[...]
        dout = diff * (1.0 / d)
        dxhat = dout * gain
        dx = r * (dxhat - xhat * jnp.mean(dxhat * xhat, axis=1, keepdims=True))
        dx_ref[...] = dx
        dxb_ref[...] = dx.astype(BF16)

        @pl.when(i == 0)
        def _():
            dg_ref[...] = jnp.zeros_like(dg_ref)
            loss_ref[...] = jnp.zeros_like(loss_ref)

        dg_ref[...] += jnp.sum(dout * xhat, axis=0, keepdims=True)
        loss_ref[...] += jnp.sum(diff * diff, axis=0, keepdims=True)

    row = lambda i: (i, 0)
    fixed = lambda i: (0, 0)
    return pl.pallas_call(
        body, name=name,
        out_shape=(jax.ShapeDtypeStruct((s, d), F32), jax.ShapeDtypeStruct((s, d), BF16),
                   jax.ShapeDtypeStruct((1, d), F32), jax.ShapeDtypeStruct((1, d), F32)),
        grid=(s // tm,),
        in_specs=[pl.BlockSpec((tm, e), row), pl.BlockSpec((e, d), fixed), pl.BlockSpec((tm, d), row),
                  pl.BlockSpec((1, d), fixed), pl.BlockSpec((tm, d), row)],
        out_specs=(pl.BlockSpec((tm, d), row), pl.BlockSpec((tm, d), row),
                   pl.BlockSpec((1, d), fixed), pl.BlockSpec((1, d), fixed)),
        compiler_params=_params("arbitrary"),
    )(z, w, x, g, target)


def _rmsnorm_bwd(dh, x, g, dx_next, name, ts=256):
    s, d = x.shape
    ts = _pick(s, ts, 8)

    def body(dh_ref, x_ref, g_ref, dn_ref, dx_ref, dxb_ref, dg_ref):
        i = pl.program_id(0)
        xhat, r = _rms(x_ref[...])
        dh_ = dh_ref[...]
        dxhat = dh_ * g_ref[...]
        dx = dn_ref[...] + r * (dxhat - xhat * jnp.mean(dxhat * xhat, axis=1, keepdims=True))
        dx_ref[...] = dx
        dxb_ref[...] = dx.astype(BF16)

        @pl.when(i == 0)
        def _():
            dg_ref[...] = jnp.zeros_like(dg_ref)

        dg_ref[...] += jnp.sum(dh_ * xhat, axis=0, keepdims=True)

    row = lambda i: (i, 0)
    fixed = lambda i: (0, 0)
    return pl.pallas_call(
        body, name=name,
        out_shape=(jax.ShapeDtypeStruct((s, d), F32), jax.ShapeDtypeStruct((s, d), BF16),
                   jax.ShapeDtypeStruct((1, d), F32)),
        grid=(s // ts,),
        in_specs=[pl.BlockSpec((ts, d), row), pl.BlockSpec((ts, d), row), pl.BlockSpec((1, d), fixed),
                  pl.BlockSpec((ts, d), row)],
        out_specs=(pl.BlockSpec((ts, d), row), pl.BlockSpec((ts, d), row), pl.BlockSpec((1, d), fixed)),
        compiler_params=_params("arbitrary"),
    )(dh, x, g, dx_next)


def _pool_counts(t0, rows, cols, window):
    t = t0 + lax.broadcasted_iota(jnp.int32, (rows, cols), 0)
    return jnp.minimum(t + 1, window).astype(F32)


def _pool_fwd(u, name, ts=1024, tc=256):
    s, e = u.shape
    ng = len(POOL_WINDOWS)
    gdim = e // ng
    ts, tc = _pick(s, ts), _pick(gdim, tc)
    cpg = gdim // tc
    hb = ts // POOL_HALO

    def body(u_ref, halo_ref, d_ref):
        i, grp = pl.program_id(0), pl.program_id(1)
        cur = u_ref[...]
        halo = jnp.where(i > 0, halo_ref[...], 0.0)
        ext = jnp.concatenate([halo, cur], axis=0)
        for gi, window in enumerate(POOL_WINDOWS):
            @pl.when(grp == gi)
            def _(window=window):
                acc = ext
                k = 1
                while k < window:
                    acc = acc + pltpu.roll(acc, k, 0)
                    k *= 2
                pooled = acc[POOL_HALO:, :] / _pool_counts(i * ts, ts, tc, window)
                d_ref[...] = (pooled - cur).astype(BF16)

    return pl.pallas_call(
        body, name=name,
        out_shape=jax.ShapeDtypeStruct((s, e), BF16),
        grid=(s // ts, ng, cpg),
        in_specs=[pl.BlockSpec((ts, tc), lambda i, g, j: (i, g * cpg + j)),
                  pl.BlockSpec((POOL_HALO, tc), lambda i, g, j: (jnp.maximum(i * hb - 1, 0), g * cpg + j))],
        out_specs=pl.BlockSpec((ts, tc), lambda i, g, j: (i, g * cpg + j)),
        compiler_params=_params("parallel", "parallel", "parallel"),
    )(u, u)


def _pool_bwd(dd, name, ts=1024, tc=256):
    s, e = dd.shape
    ng = len(POOL_WINDOWS)
    gdim = e // ng
    ts, tc = _pick(s, ts), _pick(gdim, tc)
    cpg = gdim // tc
    hb = ts // POOL_HALO
    n_halo = s // POOL_HALO
    nst = s // ts

    def body(dd_ref, halo_ref, du_ref):
        i, grp = pl.program_id(0), pl.program_id(1)
        cur = dd_ref[...]
        halo = jnp.where(i < nst - 1, halo_ref[...], 0.0)
        ext = jnp.concatenate([cur, halo], axis=0)
        rows = ts + POOL_HALO
        for gi, window in enumerate(POOL_WINDOWS):
            @pl.when(grp == gi)
            def _(window=window):
                acc = ext / _pool_counts(i * ts, rows, tc, window)
                k = 1
                while k < window:
                    acc = acc + pltpu.roll(acc, rows - k, 0)
                    k *= 2
                du_ref[...] = (acc[:ts, :] - cur).astype(BF16)

    return pl.pallas_call(
        body, name=name,
        out_shape=jax.ShapeDtypeStruct((s, e), BF16),
        grid=(nst, ng, cpg),
        in_specs=[pl.BlockSpec((ts, tc), lambda i, g, j: (i, g * cpg + j)),
                  pl.BlockSpec((POOL_HALO, tc),
                               lambda i, g, j: (jnp.minimum((i + 1) * hb, n_halo - 1), g * cpg + j))],
        out_specs=pl.BlockSpec((ts, tc), lambda i, g, j: (i, g * cpg + j)),
        compiler_params=_params("parallel", "parallel", "parallel"),
    )(dd, dd)


def _a_group_fwd(d, w, scale, gate, name, tm=1024):
    s, e = d.shape
    ng, g, _ = w.shape
    tm = _pick(s, tm)

    def body(d_ref, w_ref, s_ref, gate_ref, yr_ref, z_ref):
        yr = jnp.dot(d_ref[...], w_ref[...], preferred_element_type=F32)
        yr_ref[...] = yr
        gt = gate_ref[...]
        z_ref[...] = ((yr * s_ref[...]) * (gt * _sigmoid(gt))).astype(BF16)

    blk = lambda i, j: (i, j)
    return pl.pallas_call(
        body, name=name,
        out_shape=(jax.ShapeDtypeStruct((s, e), F32), jax.ShapeDtypeStruct((s, e), BF16)),
        grid=(s // tm, ng),
        in_specs=[pl.BlockSpec((tm, g), blk), pl.BlockSpec((None, g, g), lambda i, j: (j, 0, 0)),
                  pl.BlockSpec((1, g), lambda i, j: (0, j)), pl.BlockSpec((tm, g), blk)],
        out_specs=(pl.BlockSpec((tm, g), blk), pl.BlockSpec((tm, g), blk)),
        compiler_params=_params("parallel", "parallel"),
    )(d, w, scale, gate)


def _a_gate_bwd(dz, yr, gate, scale, name, ts=512, tc=512):
    s, e = dz.shape
    ts, tc = _pick(s, ts), _pick(e, tc)

    def body(dz_ref, yr_ref, gate_ref, s_ref, dgate_ref, dyr_ref, dscale_ref):
        i = pl.program_id(1)
        dz_, yr_, gt, sc = dz_ref[...], yr_ref[...], gate_ref[...], s_ref[...]
        sg = _sigmoid(gt)
        dy = dz_ * (gt * sg)
        dgate_ref[...] = (dz_ * (yr_ * sc) * (sg * (1.0 + gt * (1.0 - sg)))).astype(BF16)
        dyr_ref[...] = (dy * sc).astype(BF16)

        @pl.when(i == 0)
        def _():
            dscale_ref[...] = jnp.zeros_like(dscale_ref)

        dscale_ref[...] += jnp.sum(dy * yr_, axis=0, keepdims=True)

    blk = lambda j, i: (i, j)
    vec = lambda j, i: (0, j)
    return pl.pallas_call(
        body, name=name,
        out_shape=(jax.ShapeDtypeStruct((s, e), BF16), jax.ShapeDtypeStruct((s, e), BF16),
                   jax.ShapeDtypeStruct((1, e), F32)),
        grid=(e // tc, s // ts),
        in_specs=[pl.BlockSpec((ts, tc), blk), pl.BlockSpec((ts, tc), blk), pl.BlockSpec((ts, tc), blk),
                  pl.BlockSpec((1, tc), vec)],
        out_specs=(pl.BlockSpec((ts, tc), blk), pl.BlockSpec((ts, tc), blk), pl.BlockSpec((1, tc), vec)),
        compiler_params=_params("parallel", "arbitrary"),
    )(dz, yr, gate, scale)


def _gate_fwd(y, gate, name, ts=512, tc=512):
    s, e = y.shape
    ts, tc = _pick(s, ts), _pick(e, tc)

    def body(y_ref, gate_ref, z_ref):
        gt = gate_ref[...]
        z_ref[...] = (y_ref[...] * (gt * _sigmoid(gt))).astype(BF16)

    blk = lambda i, j: (i, j)
    return pl.pallas_call(
        body, name=name,
        out_shape=jax.ShapeDtypeStruct((s, e), BF16),
        grid=(s // ts, e // tc),
        in_specs=[pl.BlockSpec((ts, tc), blk)] * 2,
        out_specs=pl.BlockSpec((ts, tc), blk),
        compiler_params=_params("parallel", "parallel"),
    )(y, gate)


def _gate_bwd(dz, y, gate, name, ts=512, tc=512):
    s, e = dz.shape
    ts, tc = _pick(s, ts), _pick(e, tc)

    def body(dz_ref, y_ref, gate_ref, dgate_ref, dy_ref):
        dz_, gt = dz_ref[...], gate_ref[...]
        sg = _sigmoid(gt)
        dgate_ref[...] = (dz_ * y_ref[...] * (sg * (1.0 + gt * (1.0 - sg)))).astype(BF16)
        dy_ref[...] = (dz_ * (gt * sg)).astype(BF16)

    blk = lambda i, j: (i, j)
    return pl.pallas_call(
        body, name=name,
        out_shape=(jax.ShapeDtypeStruct((s, e), BF16), jax.ShapeDtypeStruct((s, e), BF16)),
        grid=(s // ts, e // tc),
        in_specs=[pl.BlockSpec((ts, tc), blk)] * 3,
        out_specs=(pl.BlockSpec((ts, tc), blk), pl.BlockSpec((ts, tc), blk)),
        compiler_params=_params("parallel", "parallel"),
    )(dz, y, gate)


def _merge_weights(l0, l1, l2):
    m = jnp.maximum(jnp.maximum(l0, l1), l2)
    e0, e1, e2 = jnp.exp(l0 - m), jnp.exp(l1 - m), jnp.exp(l2 - m)
    inv = 1.0 / (e0 + e1 + e2)
    return e0 * inv, e1 * inv, e2 * inv


def _merge_gate_fwd(outs, lses, gate, name, ts=512, tc=512):
    s, e = gate.shape
    ts, tc = _pick(s, ts), _pick(e, tc)

    def body(o0, o1, o2, l0, l1, l2, gate_ref, y_ref, z_ref):
        w0, w1, w2 = _merge_weights(l0[...], l1[...], l2[...])
        y = w0 * o0[...].astype(F32) + w1 * o1[...].astype(F32) + w2 * o2[...].astype(F32)
        y_ref[...] = y
        gt = gate_ref[...]
        z_ref[...] = (y * (gt * _sigmoid(gt))).astype(BF16)

    blk = lambda i, j: (i, j)
    return pl.pallas_call(
        body, name=name,
        out_shape=(jax.ShapeDtypeStruct((s, e), F32), jax.ShapeDtypeStruct((s, e), BF16)),
        grid=(s // ts, e // tc),
        in_specs=[pl.BlockSpec((ts, tc), blk)] * 7,
        out_specs=(pl.BlockSpec((ts, tc), blk), pl.BlockSpec((ts, tc), blk)),
        compiler_params=_params("parallel", "parallel"),
    )(*outs, *lses, gate)


def _merge_gate_bwd(dz, y, gate, lses, name, ts=512, tc=512):
    s, e = dz.shape
    ts, tc = _pick(s, ts), _pick(e, tc)

    def body(dz_ref, y_ref, gate_ref, l0, l1, l2, dgate_ref, d0, d1, d2):
        dz_, gt = dz_ref[...], gate_ref[...]
        sg = _sigmoid(gt)
        dgate_ref[...] = (dz_ * y_ref[...] * (sg * (1.0 + gt * (1.0 - sg)))).astype(BF16)
        dy = dz_ * (gt * sg)
        w0, w1, w2 = _merge_weights(l0[...], l1[...], l2[...])
        d0[...] = (w0 * dy).astype(BF16)
        d1[...] = (w1 * dy).astype(BF16)
        d2[...] = (w2 * dy).astype(BF16)

    blk = lambda i, j: (i, j)
    return pl.pallas_call(
        body, name=name,
        out_shape=tuple([jax.ShapeDtypeStruct((s, e), BF16)] * 4),
        grid=(s // ts, e // tc),
        in_specs=[pl.BlockSpec((ts, tc), blk)] * 6,
        out_specs=tuple([pl.BlockSpec((ts, tc), blk)] * 4),
        compiler_params=_params("parallel", "parallel"),
    )(dz, y, gate, *lses)


def _band(max_dist, width):
    row = lax.broadcasted_iota(jnp.int32, (2 * BLOCK, width), 0) & (BLOCK - 1)
    col = lax.broadcasted_iota(jnp.int32, (2 * BLOCK, width), 1)
    low = row if max_dist == BLOCK else row + 1
    return jnp.logical_and(col >= low, col <= row + BLOCK), col >= BLOCK


def _fill_bias(bias_ref, max_dist):
    band, own = _band(max_dist, 2 * BLOCK)
    bias_ref[0] = jnp.where(band, 0.0, NEG)
    bias_ref[1] = jnp.where(jnp.logical_and(band, own), 0.0, NEG)


def _aligned(v):
    return v if isinstance(v, int) else pl.multiple_of(v, BLOCK)


def _stack_heads(x, lo):
    return jnp.concatenate([jnp.where(lo, x, 0.0), jnp.where(lo, 0.0, x)], axis=0).astype(BF16)


def _unstack_heads(x2, lo):
    return jnp.where(lo, x2[:BLOCK], x2[BLOCK:])


def _head_col(x, hm):
    return jnp.max(jnp.where(hm, x, NEG), axis=1, keepdims=True)


def _dot_nt(a, b):
    return lax.dot_general(a, b, (((1,), (1,)), ((), ())), preferred_element_type=F32)


def _attn_fwd(q, k, v, sinks, max_dist, rep, out_dtype, name, tq=1024):
    assert max_dist in (BLOCK - 1, BLOCK)
    l, w = q.shape
    n_pairs = w // LANES
    tq = _pick(l, tq)
    n = tq // BLOCK
    n_blk = l // BLOCK
    has_sink = sinks is not None
    scale = HEAD_DIM ** -0.5

    def body(*refs):
        if has_sink:
            sink_ref, refs = refs[0], refs[1:]
        q_ref, kc_ref, kh_ref, vc_ref, vh_ref, o_ref, lse_ref, kx, vx, bias_ref = refs
        i, p = pl.program_id(0), pl.program_id(1)
        kx[0:BLOCK, :] = kh_ref[...]
        kx[BLOCK:, :] = kc_ref[...]
        vx[0:BLOCK, :] = vh_ref[...]
        vx[BLOCK:, :] = vc_ref[...]
        lo = lax.broadcasted_iota(jnp.int32, (BLOCK, LANES), 1) < HEAD_DIM
        _fill_bias(bias_ref, max_dist)
        top = lax.broadcasted_iota(jnp.int32, (2 * BLOCK, 1), 0) < BLOCK

        def step(j, carry):
            r0 = _aligned(j * BLOCK)
            q2 = _stack_heads(q_ref[pl.ds(r0, BLOCK), :].astype(F32) * scale, lo)
            kw, vw = kx[pl.ds(r0, 2 * BLOCK), :], vx[pl.ds(r0, 2 * BLOCK), :]
            first = jnp.logical_and(i == 0, j == 0).astype(jnp.int32)
            s2 = _dot_nt(q2, kw) + bias_ref[first]
            m = jnp.max(s2, axis=1, keepdims=True)
            if has_sink:
                sk = jnp.where(top, sink_ref[2 * p], sink_ref[2 * p + 1])
                m = jnp.maximum(m, sk)
            pr = jnp.exp(s2 - m)
            den = jnp.sum(pr, axis=1, keepdims=True)
            if has_sink:
                den = den + jnp.exp(sk - m)
            o2 = jnp.dot(pr.astype(BF16), vw, preferred_element_type=F32) * (1.0 / den)
            lse2 = m + jnp.log(den)
            o_ref[pl.ds(r0, BLOCK), :] = _unstack_heads(o2, lo).astype(o_ref.dtype)
            lse_ref[pl.ds(r0, BLOCK), :] = _unstack_heads(lse2, lo)
            return carry

        lax.fori_loop(0, n, step, 0, unroll=2)

    cur = lambda i, p: (i, p)
    kv_cur = lambda i, p: (i, p // rep)
    kv_halo = lambda i, p: (jnp.maximum(i * n - 1, 0), p // rep)
    in_specs = [pl.BlockSpec((tq, LANES), cur),
                pl.BlockSpec((tq, LANES), kv_cur), pl.BlockSpec((BLOCK, LANES), kv_halo),
                pl.BlockSpec((tq, LANES), kv_cur), pl.BlockSpec((BLOCK, LANES), kv_halo)]
    args = [q, k, k, v, v]
    if has_sink:
        in_specs = [pl.BlockSpec(memory_space=pltpu.SMEM)] + in_specs
        args = [sinks] + args
    del n_blk
    return pl.pallas_call(
        body, name=name,
        out_shape=(jax.ShapeDtypeStruct((l, w), out_dtype), jax.ShapeDtypeStruct((l, w), F32)),
        grid=(l // tq, n_pairs),
        in_specs=in_specs,
        out_specs=(pl.BlockSpec((tq, LANES), cur), pl.BlockSpec((tq, LANES), cur)),
        scratch_shapes=[pltpu.VMEM((tq + BLOCK, LANES), BF16), pltpu.VMEM((tq + BLOCK, LANES), BF16),
                        pltpu.VMEM((2, 2 * BLOCK, 2 * BLOCK), F32)],
        compiler_params=_params("parallel", "parallel"),
    )(*args)


def _attn_bwd(q, k, v, do, y, lse, sinks, max_dist, rep, name, tq=1024):
    l, w = q.shape
    n_pairs = w // LANES
    tq = _pick(l, tq)
    n = tq // BLOCK
    n_blk = l // BLOCK
    n_sb = l // tq
    has_sink = sinks is not None
    scale = HEAD_DIM ** -0.5
    kv_dtype = F32 if rep > 1 else BF16
    ext = tq + BLOCK

    def body(*refs):
        if has_sink:
            sink_ref, refs = refs[0], refs[1:]
        (q_ref, qn_ref, kc_ref, kh_ref, vc_ref, vh_ref, do_ref, don_ref, y_ref, yn_ref,
         lse_ref, lsen_ref) = refs[:12]
        refs = refs[12:]
        dq_ref, dk_ref, dv_ref = refs[:3]
        refs = refs[3:]
        if has_sink:
            dsink_ref, refs = refs[0], refs[1:]
        qx, dox, yx, lsex, kx, vx, dkx, dvx, bias_ref = refs
        i, p = pl.program_id(0), pl.program_id(1)
        _fill_bias(bias_ref, max_dist)
        qx[0:tq, :] = q_ref[...]
        qx[tq:, :] = qn_ref[...]
        dox[0:tq, :] = do_ref[...]
        dox[tq:, :] = don_ref[...]
        yx[0:tq, :] = y_ref[...]
        yx[tq:, :] = yn_ref[...]
        lsex[0:tq, :] = lse_ref[...]
        lsex[tq:, :] = lsen_ref[...]
        kx[0:BLOCK, :] = kh_ref[...]
        kx[BLOCK:, :] = kc_ref[...]
        vx[0:BLOCK, :] = vh_ref[...]
        vx[BLOCK:, :] = vc_ref[...]
        dkx[...] = jnp.zeros_like(dkx)
        dvx[...] = jnp.zeros_like(dvx)
        lo = lax.broadcasted_iota(jnp.int32, (BLOCK, LANES), 1) < HEAD_DIM
        hi = jnp.logical_not(lo)
        top = lax.broadcasted_iota(jnp.int32, (2 * BLOCK, 1), 0) < BLOCK

        def sub(j, with_cur, q_valid, sink_acc):
            r0 = _aligned(j * BLOCK)
            width = 2 * BLOCK if with_cur else BLOCK
            first = jnp.logical_and(i == 0, j == 0).astype(jnp.int32)
            dof = dox[pl.ds(r0, BLOCK), :].astype(F32)
            yb, lseb = yx[pl.ds(r0, BLOCK), :].astype(F32), lsex[pl.ds(r0, BLOCK), :]
            q2 = _stack_heads(qx[pl.ds(r0, BLOCK), :].astype(F32) * scale, lo)
            do2 = _stack_heads(dof, lo)
            prod = dof * yb
            delta = jnp.concatenate([jnp.sum(jnp.where(lo, prod, 0.0), axis=1, keepdims=True),
                                     jnp.sum(jnp.where(lo, 0.0, prod), axis=1, keepdims=True)], axis=0)
            lse2 = jnp.concatenate([_head_col(lseb, lo), _head_col(lseb, hi)], axis=0)
            kw, vw = kx[pl.ds(r0, width), :], vx[pl.ds(r0, width), :]
            pr = jnp.exp(_dot_nt(q2, kw) + bias_ref[first, :, pl.ds(0, width)] - lse2)
            if q_valid is not True:
                pr = jnp.where(q_valid, pr, 0.0)
            ds = pr * (_dot_nt(do2, vw) - delta)
            dkx[pl.ds(r0, width), :] += jnp.dot(ds.T.astype(BF16), q2, preferred_element_type=F32)
            dvx[pl.ds(r0, width), :] += jnp.dot(pr.T.astype(BF16), do2, preferred_element_type=F32)
            if with_cur:
                dq2 = jnp.dot(ds.astype(BF16), kw, preferred_element_type=F32) * scale
                dq_ref[pl.ds(r0, BLOCK), :] = _unstack_heads(dq2, lo).astype(dq_ref.dtype)
            if has_sink:
                sk = jnp.where(top, sink_ref[2 * p], sink_ref[2 * p + 1])
                sink_acc = sink_acc - jnp.exp(sk - lse2) * delta
            return sink_acc

        zero_col = jnp.zeros((2 * BLOCK, 1), F32)
        sink_acc = lax.fori_loop(0, n, lambda j, acc: sub(j, True, True, acc), zero_col, unroll=2)
        sub(n, False, i < n_sb - 1, zero_col)

        dk_new, dv_new = dkx[BLOCK:, :], dvx[BLOCK:, :]
        if rep == 1:
            dk_ref[...] = dk_new.astype(dk_ref.dtype)
            dv_ref[...] = dv_new.astype(dv_ref.dtype)
        else:
            @pl.when(p % rep == 0)
            def _():
                dk_ref[...] = dk_new
                dv_ref[...] = dv_new

            @pl.when(p % rep != 0)
            def _():
                dk_ref[...] += dk_new
                dv_ref[...] += dv_new
        if has_sink:
            rowi = lax.broadcasted_iota(jnp.int32, (8, LANES), 0)
            s0 = jnp.sum(sink_acc[:BLOCK], axis=0, keepdims=True)
            s1 = jnp.sum(sink_acc[BLOCK:], axis=0, keepdims=True)
            dsink_ref[...] = jnp.where(rowi == 0, s0, jnp.where(rowi == 1, s1, 0.0))

    cur = lambda i, p: (i, p)
    nxt = lambda i, p: (jnp.minimum((i + 1) * n, n_blk - 1), p)
    kv_cur = lambda i, p: (i, p // rep)
    kv_halo = lambda i, p: (jnp.maximum(i * n - 1, 0), p // rep)
    big, small = (tq, LANES), (BLOCK, LANES)
    in_specs = [pl.BlockSpec(big, cur), pl.BlockSpec(small, nxt),
                pl.BlockSpec(big, kv_cur), pl.BlockSpec(small, kv_halo),
                pl.BlockSpec(big, kv_cur), pl.BlockSpec(small, kv_halo),
                pl.BlockSpec(big, cur), pl.BlockSpec(small, nxt),
                pl.BlockSpec(big, cur), pl.BlockSpec(small, nxt),
                pl.BlockSpec(big, cur), pl.BlockSpec(small, nxt)]
    args = [q, q, k, k, v, v, do, do, y, y, lse, lse]
    out_shape = [jax.ShapeDtypeStruct((l, w), BF16),
                 jax.ShapeDtypeStruct(k.shape, kv_dtype), jax.ShapeDtypeStruct(v.shape, kv_dtype)]
    out_specs = [pl.BlockSpec(big, cur), pl.BlockSpec(big, kv_cur), pl.BlockSpec(big, kv_cur)]
    if has_sink:
        in_specs = [pl.BlockSpec(memory_space=pltpu.SMEM)] + in_specs
        args = [sinks] + args
        out_shape.append(jax.ShapeDtypeStruct((n_sb, n_pairs, 8, LANES), F32))
        out_specs.append(pl.BlockSpec((None, None, 8, LANES), lambda i, p: (i, p, 0, 0)))
    return pl.pallas_call(
        body, name=name,
        out_shape=tuple(out_shape),
        grid=(n_sb, n_pairs),
        in_specs=in_specs,
        out_specs=tuple(out_specs),
        scratch_shapes=[pltpu.VMEM((ext, LANES), BF16), pltpu.VMEM((ext, LANES), BF16),
                        pltpu.VMEM((ext, LANES), y.dtype), pltpu.VMEM((ext, LANES), F32),
                        pltpu.VMEM((ext, LANES), BF16), pltpu.VMEM((ext, LANES), BF16),
                        pltpu.VMEM((ext, LANES), F32), pltpu.VMEM((ext, LANES), F32),
                        pltpu.VMEM((2, 2 * BLOCK, 2 * BLOCK), F32)],
        compiler_params=_params("parallel", "arbitrary"),
    )(*args)


def _sum_slots(recv, name, ts=256):
    nd, r, c = recv.shape
    ts = _pick(r, ts, 8)

    def body(r_ref, o_ref):
        acc = r_ref[0].astype(F32)
        for dev in range(1, nd):
            acc = acc + r_ref[dev].astype(F32)
        o_ref[...] = acc

    return pl.pallas_call(
        body, name=name,
        out_shape=jax.ShapeDtypeStruct((r, c), F32),
        grid=(r // ts,),
        in_specs=[pl.BlockSpec((nd, ts, c), lambda i: (0, i, 0))],
        out_specs=pl.BlockSpec((ts, c), lambda i: (i, 0)),
        compiler_params=_params("parallel"),
    )(recv)


def _adamw(w, g, m, v, name, ts=256):
    r, c = w.shape
    ts = _pick(r, ts, 8)
    c1 = 1.0 - ADAM_B1 ** ADAM_STEP
    c2 = 1.0 - ADAM_B2 ** ADAM_STEP

    def body(w_ref, g_ref, m_ref, v_ref, d_ref, mo_ref, vo_ref):
        g_ = g_ref[...]
        m_ = ADAM_B1 * m_ref[...] + (1.0 - ADAM_B1) * g_
        v_ = ADAM_B2 * v_ref[...] + (1.0 - ADAM_B2) * (g_ * g_)
        mo_ref[...] = m_
        vo_ref[...] = v_
        d_ref[...] = -ADAM_LR * ((m_ / c1) / (jnp.sqrt(v_ / c2) + ADAM_EPS) + ADAM_WD * w_ref[...])

    blk = pl.BlockSpec((ts, c), lambda i: (i, 0))
    return pl.pallas_call(
        body, name=name,
        out_shape=tuple([jax.ShapeDtypeStruct((r, c), F32)] * 3),
        grid=(r // ts,),
        in_specs=[blk] * 4,
        out_specs=(blk, blk, blk),
        compiler_params=_params("parallel"),
    )(w, g, m, v)


def _rows(a):
    flat = a.reshape(-1)
    pad = (-flat.shape[0]) % PACK_W
    if pad:
        flat = jnp.concatenate([flat, jnp.zeros((pad,), flat.dtype)])
    return flat.reshape(-1, PACK_W)


def _pad_rows(a, mult):
    pad = (-a.shape[-2]) % mult
    if pad:
        widths = [(0, 0)] * (a.ndim - 2) + [(0, pad), (0, 0)]
        a = jnp.pad(a, widths)
    return a


def _to_global(stack, axis):
    moved = jnp.moveaxis(stack, 0, axis)
    shp = list(moved.shape)
    shp[axis:axis + 2] = [shp[axis] * shp[axis + 1]]
    return moved.reshape(shp)


def _to_stack(full, axis):
    shp = list(full.shape)
    shp[axis:axis + 1] = [N_DEV, shp[axis] // N_DEV]
    return jnp.moveaxis(full.reshape(shp), axis, 0)


_BIG = (("w_out", 1), ("a_w_in", 2), ("a_w_group", 2), ("b_w_in", 2), ("c_w_in", 2))


def _dup_heads(wk, n_kv):
    d = wk.shape[0]
    return jnp.tile(wk.reshape(d, n_kv, 1, HEAD_DIM), (1, 1, 2, 1)).reshape(d, n_kv * LANES)


def _fold_heads(dwk, n_kv):
    d = dwk.shape[0]
    return dwk.reshape(d, n_kv, 2, HEAD_DIM).sum(axis=2).reshape(d, n_kv * HEAD_DIM)


def _view(a, dil):
    s, w = a.shape
    return a.reshape(s // dil, dil * w)


def _unview(a, dil):
    l, w = a.shape
    return a.reshape(l * dil, w // dil)


def kernel(x, norm_g, final_g, w_out, a_w_in, a_w_group, a_scale, b_w_in, b_sinks, c_w_in, loss_target, m_norm_g, m_final_g, m_w_out, m_a_w_in, m_a_w_group, m_a_scale, m_b_w_in, m_b_sinks, m_c_w_in, v_norm_g, v_final_g, v_w_out, v_a_w_in, v_a_w_group, v_a_scale, v_b_w_in, v_b_sinks, v_c_w_in):
    local = dict(w_out=w_out, a_w_in=a_w_in, a_w_group=a_w_group, b_w_in=b_w_in, c_w_in=c_w_in)
    mom_m = dict(w_out=m_w_out, a_w_in=m_a_w_in, a_w_group=m_a_w_group, b_w_in=m_b_w_in, c_w_in=m_c_w_in)
    mom_v = dict(w_out=v_w_out, a_w_in=v_a_w_in, a_w_group=v_a_w_group, b_w_in=v_b_w_in, c_w_in=v_c_w_in)
    s, d = x.shape[1], x.shape[2]
    depth = norm_g.shape[0]
    e = w_out.shape[1] * N_DEV
    n_heads = e // HEAD_DIM
    n_kv = n_heads // Q_PER_KV
    kv_w = n_kv * HEAD_DIM
    rep = Q_PER_KV // 2
    n_groups = len(POOL_WINDOWS)
    me = 4 * lax.axis_index("x") + 2 * lax.axis_index("y") + lax.axis_index("c")

    sizes = [local[n].size // PACK_W for n, _ in _BIG]
    offs = [sum(sizes[:k]) for k in range(len(sizes) + 1)]
    wpack = _pad_rows(jnp.concatenate([_rows(local[n].astype(BF16)) for n, _ in _BIG], axis=0), 16)
    r_pack = wpack.shape[0]
    spack = _pad_rows(_rows(a_scale), 8)
    wall, sall = _gather([wpack, spack], "gather_weights")
    full = {}
    for k, (name, axis) in enumerate(_BIG):
        stack = wall[:, offs[k]:offs[k + 1], :].reshape((N_DEV,) + local[name].shape)
        full[name] = _to_global(stack, axis)
    scale_full = _to_global(sall.reshape(N_DEV, -1)[:, :a_scale.size].reshape((N_DEV,) + a_scale.shape), 1)

    wout_t = jnp.swapaxes(full["w_out"], 1, 2)
    wa = full["a_w_in"]
    wa_t = jnp.swapaxes(wa, 1, 2)
    wg = full["a_w_group"]
    wg_t = jnp.swapaxes(wg, 2, 3)
    wb = full["b_w_in"][0]
    wb_ext = jnp.concatenate([wb[:, :e], _dup_heads(wb[:, e:e + kv_w], n_kv),
                              _dup_heads(wb[:, e + kv_w:e + 2 * kv_w], n_kv), wb[:, e + 2 * kv_w:]], axis=1)
    wb_ext_t = wb_ext.T
    kd_w = n_kv * LANES
    wc = full["c_w_in"][0]
    wc_t = wc.T

    xs, hs, zs, saved = [x.reshape(s, d)], [], [], []
    hs.append(_rmsnorm_fwd(xs[0], norm_g[0:1], "norm0"))
    loss_vec = dfinal = dx = dxb = None
    for i in range(depth):
        kind, j = i % 3, i // 3
        h = hs[i]
        tag = f"l{i}"
        if kind == 0:
            u = _matmul(h, wa[j][:, :e], F32, tag + "_in_u")
            gate = _matmul(h, wa[j][:, e:], F32, tag + "_in_gate")
            dpool = _pool_fwd(u, tag + "_pool")
            yr, z = _a_group_fwd(dpool, wg[j], scale_full[j:j + 1], gate, tag + "_group")
            saved.append(dict(dpool=dpool, yr=yr, gate=gate))
        elif kind == 1:
            q = _matmul(h, wb_ext[:, :e], BF16, tag + "_in_q")
            kd = _matmul(h, wb_ext[:, e:e + kd_w], BF16, tag + "_in_k")
            vd = _matmul(h, wb_ext[:, e + kd_w:e + 2 * kd_w], BF16, tag + "_in_v")
            gate = _matmul(h, wb_ext[:, e + 2 * kd_w:], F32, tag + "_in_gate")
            sinks = b_sinks[j]
            y, lse = _attn_fwd(q, kd, vd, sinks, SWA_MAX_DIST, rep, F32, tag + "_attn")
            z = _gate_fwd(y, gate, tag + "_gate")
            saved.append(dict(q=q, kd=kd, vd=vd, gate=gate, y=y, lse=lse, sinks=sinks))
        else:
            qkv, outs, lses = [], [], []
            for gi, (window, dil) in enumerate(DILATED_PAIRS):
                trio = [_view(_matmul(h, wc[:, (3 * gi + t) * e:(3 * gi + t + 1) * e], BF16,
                                      f"{tag}_in_{'qkv'[t]}{gi}"), dil) for t in range(3)]
                o, lse = _attn_fwd(trio[0], trio[1], trio[2], None, window // dil, 1, BF16, f"{tag}_attn{gi}")
                qkv.append(trio)
                outs.append(_unview(o, dil))
                lses.append(lse)
            gate = _matmul(h, wc[:, 9 * e:], F32, tag + "_in_gate")
            lses_tok = [_unview(lse, dil) for lse, (_, dil) in zip(lses, DILATED_PAIRS)]
            y, z = _merge_gate_fwd(outs, lses_tok, gate, tag + "_merge")
            saved.append(dict(qkv=qkv, lses=lses, lses_tok=lses_tok, gate=gate, y=y))
        zs.append(z)
        if i + 1 < depth:
            x_new, h_new = _outproj_norm(z, full["w_out"][i], xs[i], norm_g[i + 1:i + 2], tag + "_out")
            xs.append(x_new)
            hs.append(h_new)
        else:
            dx, dxb, dfinal, loss_vec = _outproj_loss(z, full["w_out"][i], xs[i], final_g.reshape(1, d),
                                                      loss_target.reshape(s, d), tag + "_out_loss")

    g_full = {"w_out": [None] * depth, "a_w_in": [None] * wa.shape[0], "a_w_group": [None] * wa.shape[0]}
    d_norm = [None] * depth
    d_scale = [None] * wa.shape[0]
    d_sinks = None
    for i in reversed(range(depth)):
        kind, j = i % 3, i // 3
        tag = f"b{i}"
        sv = saved[i]
        h_t = hs[i].T
        g_full["w_out"][i] = _matmul(zs[i].T, dxb, F32, tag + "_dwout")
        dz = _matmul(dxb, wout_t[i], F32, tag + "_dz")
        if kind == 0:
            dgate, dyr, dsc = _a_gate_bwd(dz, sv["yr"], sv["gate"], scale_full[j:j + 1], tag + "_gate")
            d_scale[j] = dsc
            dd = _grouped_matmul(dyr, wg_t[j], tag + "_dd")
            du = _pool_bwd(dd, tag + "_pool")
            g_full["a_w_group"][j] = _grouped_weight_grad(sv["dpool"].T, dyr, n_groups, tag + "_dwg")
            dp = jnp.concatenate([du, dgate], axis=1)
            w_t = wa_t[j]
        elif kind == 1:
            dgate, do = _gate_bwd(dz, sv["y"], sv["gate"], tag + "_gate")
            dq, dkd, dvd, dsink = _attn_bwd(sv["q"], sv["kd"], sv["vd"], do, sv["y"], sv["lse"], sv["sinks"],
                                            SWA_MAX_DIST, rep, tag + "_attn")
            d_sinks = dsink[:, :, 0:2, 0].sum(axis=0).reshape(1, n_heads)
            dp = jnp.concatenate([dq, dkd.astype(BF16), dvd.astype(BF16), dgate], axis=1)
            w_t = wb_ext_t
        else:
            dgate, *dos = _merge_gate_bwd(dz, sv["y"], sv["gate"], sv["lses_tok"], tag + "_merge")
            parts = []
            for gi, (window, dil) in enumerate(DILATED_PAIRS):
                qv, kv, vv = sv["qkv"][gi]
                grads = _attn_bwd(qv, kv, vv, _view(dos[gi], dil), _view(sv["y"], dil), sv["lses"][gi], None,
                                  window // dil, 1, f"{tag}_attn{gi}")
                parts += [_unview(g_, dil) for g_ in grads]
            dp = jnp.concatenate(parts + [dgate], axis=1)
            w_t = wc_t
        dw_in = _matmul(h_t, dp, F32, tag + "_dwin")
        if kind == 0:
            g_full["a_w_in"][j] = dw_in
        elif kind == 1:
            g_full["b_w_in"] = jnp.concatenate(
                [dw_in[:, :e], _fold_heads(dw_in[:, e:e + kd_w], n_kv),
                 _fold_heads(dw_in[:, e + kd_w:e + 2 * kd_w], n_kv), dw_in[:, e + 2 * kd_w:]], axis=1)[None]
        else:
            g_full["c_w_in"] = dw_in[None]
        dh = _matmul(dp, w_t, F32, tag + "_dh")
        dx, dxb, d_norm[i] = _rmsnorm_bwd(dh, xs[i], norm_g[i:i + 1], dx, tag + "_norm")
    grad_x = dx.reshape(x.shape)
    for name in ("w_out", "a_w_in", "a_w_group"):
        g_full[name] = jnp.stack(g_full[name], axis=0)

    gpack = jnp.concatenate(
        [_to_stack(g_full[n], axis).astype(BF16).reshape(N_DEV, -1, PACK_W) for n, axis in _BIG], axis=1)
    gpack = _pad_rows(gpack, 16)
    loss_local = (0.5 / d) * jnp.sum(loss_vec)
    small = [jnp.concatenate(d_norm, axis=0), dfinal, d_sinks, jnp.concatenate(d_scale, axis=0),
             loss_local.reshape(1, 1)]
    small_rows = [_rows(a) for a in small]
    small_offs = [sum(r.shape[0] for r in small_rows[:k]) for k in range(len(small_rows) + 1)]
    small_pack = _pad_rows(jnp.concatenate(small_rows, axis=0), 8)
    core = lax.axis_index("c").astype(jnp.int32).reshape(1)
    from_sibling = _sibling_exchange(gpack, "exchange_sibling")
    chip_sums = _pair_sum(gpack, from_sibling, core, "sum_pair")
    grecv, srecv = _chip_exchange(chip_sums, small_pack, "exchange_chips")
    gsum = _sum_slots(grecv, "sum_grads")
    ssum = _sum_slots(srecv, "sum_small")

    def small_part(k, like):
        return ssum[small_offs[k]:small_offs[k + 1]].reshape(-1)[:like.size].reshape(like.shape)

    g_norm = small_part(0, norm_g)
    g_final = small_part(1, final_g)
    g_sinks = small_part(2, b_sinks)
    g_scale_full = small_part(3, scale_full)
    loss = ssum[small_offs[4], 0]
    g_scale = lax.dynamic_slice_in_dim(g_scale_full, me * a_scale.shape[1], a_scale.shape[1], axis=1)

    small_w = [("norm_g", norm_g, m_norm_g, v_norm_g, g_norm), ("final_g", final_g, m_final_g, v_final_g, g_final),
               ("a_scale", a_scale, m_a_scale, v_a_scale, g_scale), ("b_sinks", b_sinks, m_b_sinks, v_b_sinks, g_sinks)]
    big_rows = lambda tree: _pad_rows(jnp.concatenate([_rows(tree[n]) for n, _ in _BIG], axis=0), 16)
    tail = lambda idx: [_rows(t[idx]) for t in small_w]
    tail_sizes = [r.shape[0] for r in tail(1)]
    tail_offs = [r_pack + sum(tail_sizes[:k]) for k in range(len(tail_sizes) + 1)]
    w_all = _pad_rows(jnp.concatenate([big_rows(local)] + tail(1), axis=0), ADAM_ROWS)
    m_all = _pad_rows(jnp.concatenate([big_rows(mom_m)] + tail(2), axis=0), ADAM_ROWS)
    v_all = _pad_rows(jnp.concatenate([big_rows(mom_v)] + tail(3), axis=0), ADAM_ROWS)
    g_all = _pad_rows(jnp.concatenate([gsum] + tail(4), axis=0), ADAM_ROWS)
    delta_all, m_new, v_new = _adamw(w_all, g_all, m_all, v_all, "adamw", ts=ADAM_ROWS)

    def unpack(packed):
        out = {}
        for k, (name, _) in enumerate(_BIG):
            out[name] = packed[offs[k]:offs[k + 1]].reshape(local[name].shape)
        for k, (name, w_, _, _, _) in enumerate(small_w):
            out[name] = packed[tail_offs[k]:tail_offs[k + 1]].reshape(-1)[:w_.size].reshape(w_.shape)
        return out

    order = ("norm_g", "final_g", "w_out", "a_w_in", "a_w_group", "a_scale", "b_w_in", "b_sinks", "c_w_in")
    grads = unpack(g_all)
    deltas, new_m, new_v = unpack(delta_all), unpack(m_new), unpack(v_new)
    return (loss, grad_x, *[grads[n] for n in order], *[deltas[n] for n in order],
            *[new_m[n] for n in order], *[new_v[n] for n in order])
```

```python
import functools

import jax
import jax.numpy as jnp
from jax import lax
from jax.experimental import pallas as pl
from jax.experimental.pallas import tpu as pltpu

F32 = jnp.float32
BF16 = jnp.bfloat16

N_DEV = 8
HEAD_DIM = 64
LANES = 128
BLOCK = 128
Q_PER_KV = 8
POOL_WINDOWS = (2, 4, 8, 16)
POOL_HALO = 16
DILATED_PAIRS = ((128, 1), (512, 4), (2048, 16))
SWA_MAX_DIST = 127
RMS_EPS = 1e-5
PACK_W = 1024
ADAM_ROWS = 256
NEG = -1e30

ADAM_LR = 0.001
ADAM_B1 = 0.9
ADAM_B2 = 0.999
ADAM_EPS = 1e-08
ADAM_WD = 0.01
ADAM_STEP = 10

VMEM_LIMIT = 48 * 1024 * 1024


def _params(*sem):
    return pltpu.CompilerParams(dimension_semantics=sem if sem else None, vmem_limit_bytes=VMEM_LIMIT)


def _pick(dim, target, mult=LANES):
    if dim <= target:
        return dim
    t = target - target % mult
    while dim % t:
        t -= mult
    return t


def _sigmoid(x):
    return 1.0 / (1.0 + jnp.exp(-x))


CHIP_OFFSETS = (2, 4, 6)
ANY_SPEC = pl.BlockSpec(memory_space=pl.ANY)


def _where_am_i():
    x, y, c = lax.axis_index("x"), lax.axis_index("y"), lax.axis_index("c")
    return x, y, c, 4 * x + 2 * y + c


def _peer(x, y, c, r):
    return x ^ ((r >> 2) & 1), y ^ ((r >> 1) & 1), c ^ (r & 1)


def _gather(blocks, name):
    n = len(blocks)

    def body(*refs):
        send, recv = refs[:n], refs[n:2 * n]
        send_sems, recv_sems, local_sems = refs[2 * n:]
        x, y, c, me = _where_am_i()
        sib = _peer(x, y, c, 1)
        sib_id = me ^ 1

        def copy(k, slot_sem, src, slot, to):
            return pltpu.make_async_remote_copy(
                src_ref=src, dst_ref=recv[k].at[slot], send_sem=send_sems.at[k, slot_sem],
                recv_sem=recv_sems.at[k, slot_sem], device_id=to, device_id_type=pl.DeviceIdType.MESH)

        started = []
        for k in range(n):
            own = pltpu.make_async_copy(send[k], recv[k].at[me], local_sems.at[k])
            own.start()
            started.append(own)
        sends = []
        for k in range(n):
            sends.append(copy(k, 0, send[k], me, sib))
            for j, r in enumerate(CHIP_OFFSETS):
                sends.append(copy(k, 1 + j, send[k], me, _peer(x, y, c, r)))
        for cp in sends:
            cp.start()
        for j, r in enumerate(CHIP_OFFSETS):
            for k in range(n):
                src_id = me ^ r
                copy(k, 1 + j, send[k], src_id, sib).wait_recv()
                fwd = copy(k, 4 + j, recv[k].at[src_id], src_id, sib)
                fwd.start()
                sends.append(fwd)
        for k in range(n):
            copy(k, 0, send[k], sib_id, sib).wait_recv()
            for j, r in enumerate(CHIP_OFFSETS):
                copy(k, 4 + j, send[k], sib_id ^ r, sib).wait_recv()
        for cp in sends:
            cp.wait_send()
        for own in started:
            own.wait()

    return pl.pallas_call(
        body, name=name,
        out_shape=tuple(jax.ShapeDtypeStruct((N_DEV,) + b.shape, b.dtype) for b in blocks),
        in_specs=[ANY_SPEC] * n,
        out_specs=tuple([ANY_SPEC] * n),
        scratch_shapes=[pltpu.SemaphoreType.DMA((n, N_DEV - 1)), pltpu.SemaphoreType.DMA((n, N_DEV - 1)),
                        pltpu.SemaphoreType.DMA((n,))],
    )(*blocks)


def _sibling_exchange(gpack, name):
    n_chips = N_DEV // 2

    def body(g_ref, t_ref, send_sems, recv_sems):
        x, y, c, _ = _where_am_i()
        sib = _peer(x, y, c, 1)
        copies = [pltpu.make_async_remote_copy(
            src_ref=g_ref.at[2 * chip + (1 - c)], dst_ref=t_ref.at[chip], send_sem=send_sems.at[chip],
            recv_sem=recv_sems.at[chip], device_id=sib, device_id_type=pl.DeviceIdType.MESH)
            for chip in range(n_chips)]
        for cp in copies:
            cp.start()
        for cp in copies:
            cp.wait_recv()
        for cp in copies:
            cp.wait_send()

    return pl.pallas_call(
        body, name=name,
        out_shape=jax.ShapeDtypeStruct((n_chips,) + gpack.shape[1:], gpack.dtype),
        in_specs=[ANY_SPEC], out_specs=ANY_SPEC,
        scratch_shapes=[pltpu.SemaphoreType.DMA((n_chips,)), pltpu.SemaphoreType.DMA((n_chips,))],
    )(gpack)


def _chip_exchange(csum, small, name):
    n_chips = N_DEV // 2

    def body(c_ref, s_ref, r_ref, sr_ref, send_sems, recv_sems, small_send, small_recv, local_sems):
        x, y, c, me = _where_am_i()
        my_chip = 2 * x + y
        own = [pltpu.make_async_copy(c_ref.at[my_chip], r_ref.at[my_chip], local_sems.at[0]),
               pltpu.make_async_copy(s_ref, sr_ref.at[me], local_sems.at[1])]
        for cp in own:
            cp.start()
        sends, recvs = [], []
        for j, r in enumerate(CHIP_OFFSETS):
            to = _peer(x, y, c, r)
            chip = my_chip ^ (r >> 1)
            sends.append(pltpu.make_async_remote_copy(
                src_ref=c_ref.at[chip], dst_ref=r_ref.at[my_chip], send_sem=send_sems.at[j],
                recv_sem=recv_sems.at[j], device_id=to, device_id_type=pl.DeviceIdType.MESH))
            recvs.append(pltpu.make_async_remote_copy(
                src_ref=c_ref.at[chip], dst_ref=r_ref.at[chip], send_sem=send_sems.at[j],
                recv_sem=recv_sems.at[j], device_id=to, device_id_type=pl.DeviceIdType.MESH))
        for r in range(1, N_DEV):
            to = _peer(x, y, c, r)
            sends.append(pltpu.make_async_remote_copy(
                src_ref=s_ref, dst_ref=sr_ref.at[me], send_sem=small_send.at[r - 1],
                recv_sem=small_recv.at[r - 1], device_id=to, device_id_type=pl.DeviceIdType.MESH))
            recvs.append(pltpu.make_async_remote_copy(
                src_ref=s_ref, dst_ref=sr_ref.at[me ^ r], send_sem=small_send.at[r - 1],
                recv_sem=small_recv.at[r - 1], device_id=to, device_id_type=pl.DeviceIdType.MESH))
        for cp in sends:
            cp.start()
        for cp in recvs:
            cp.wait_recv()
        for cp in sends:
            cp.wait_send()
        for cp in own:
            cp.wait()

    n_off = len(CHIP_OFFSETS)
    return pl.pallas_call(
        body, name=name,
        out_shape=(jax.ShapeDtypeStruct(csum.shape, csum.dtype),
                   jax.ShapeDtypeStruct((N_DEV,) + small.shape, small.dtype)),
        in_specs=[ANY_SPEC, ANY_SPEC], out_specs=(ANY_SPEC, ANY_SPEC),
        scratch_shapes=[pltpu.SemaphoreType.DMA((n_off,)), pltpu.SemaphoreType.DMA((n_off,)),
                        pltpu.SemaphoreType.DMA((N_DEV - 1,)), pltpu.SemaphoreType.DMA((N_DEV - 1,)),
                        pltpu.SemaphoreType.DMA((2,))],
    )(csum, small)


def _pair_sum(gpack, other, core, name, ts=256):
    n_chips, r, c = other.shape
    ts = _pick(r, ts, 16)

    def body(core_ref, g_ref, o_ref, out_ref):
        del core_ref
        out_ref[...] = (g_ref[...].astype(F32) + o_ref[...].astype(F32)).astype(out_ref.dtype)

    return pl.pallas_call(
        body, name=name,
        out_shape=jax.ShapeDtypeStruct(other.shape, other.dtype),
        grid_spec=pltpu.PrefetchScalarGridSpec(
            num_scalar_prefetch=1, grid=(n_chips, r // ts),
            in_specs=[pl.BlockSpec((None, ts, c), lambda j, i, core_ref: (2 * j + core_ref[0], i, 0)),
                      pl.BlockSpec((None, ts, c), lambda j, i, core_ref: (j, i, 0))],
            out_specs=pl.BlockSpec((None, ts, c), lambda j, i, core_ref: (j, i, 0))),
        compiler_params=_params("parallel", "parallel"),
    )(core, gpack, other)


def _matmul(a, b, out_dtype, name, tm=1024, tn=1024, tk=1024):
    m, kdim = a.shape
    n = b.shape[1]
    tm, tn, tk = _pick(m, tm), _pick(n, tn), _pick(kdim, tk)
    nk = kdim // tk

    if nk == 1:
        def body(a_ref, b_ref, o_ref):
            o_ref[...] = jnp.dot(a_ref[...], b_ref[...], preferred_element_type=F32).astype(o_ref.dtype)
        scratch = []
    else:
        def body(a_ref, b_ref, o_ref, acc_ref):
            kk = pl.program_id(2)

            @pl.when(kk == 0)
            def _():
                acc_ref[...] = jnp.zeros_like(acc_ref)

            acc_ref[...] += jnp.dot(a_ref[...], b_ref[...], preferred_element_type=F32)

            @pl.when(kk == nk - 1)
            def _():
                o_ref[...] = acc_ref[...].astype(o_ref.dtype)
        scratch = [pltpu.VMEM((tm, tn), F32)]

    return pl.pallas_call(
        body, name=name,
        out_shape=jax.ShapeDtypeStruct((m, n), out_dtype),
        grid=(m // tm, n // tn, nk),
        in_specs=[pl.BlockSpec((tm, tk), lambda i, j, k: (i, k)),
                  pl.BlockSpec((tk, tn), lambda i, j, k: (k, j))],
        out_specs=pl.BlockSpec((tm, tn), lambda i, j, k: (i, j)),
        scratch_shapes=scratch,
        compiler_params=_params("parallel", "parallel", "arbitrary"),
    )(a, b)


def _matmul_tn(a, b, name, tm=1024, tn=1024, tk=1024):
    kdim, m = a.shape
    n = b.shape[1]
    tm, tn, tk = _pick(m, tm), _pick(n, tn), _pick(kdim, tk)
    nk = kdim // tk

    def body(a_ref, b_ref, o_ref):
        kk = pl.program_id(2)

        @pl.when(kk == 0)
        def _():
            o_ref[...] = jnp.zeros_like(o_ref)

        o_ref[...] += lax.dot_general(a_ref[...], b_ref[...], (((0,), (0,)), ((), ())),
                                      preferred_element_type=F32)

    return pl.pallas_call(
        body, name=name,
        out_shape=jax.ShapeDtypeStruct((m, n), F32),
        grid=(m // tm, n // tn, nk),
        in_specs=[pl.BlockSpec((tk, tm), lambda i, j, k: (k, i)),
                  pl.BlockSpec((tk, tn), lambda i, j, k: (k, j))],
        out_specs=pl.BlockSpec((tm, tn), lambda i, j, k: (i, j)),
        compiler_params=_params("parallel", "parallel", "arbitrary"),
    )(a, b)


def _grouped_matmul(a, w, name, tm=1024):
    s, e = a.shape
    ng, g, _ = w.shape
    tm = _pick(s, tm)

    def body(a_ref, w_ref, o_ref):
        o_ref[...] = jnp.dot(a_ref[...], w_ref[...], preferred_element_type=F32)

    return pl.pallas_call(
        body, name=name,
        out_shape=jax.ShapeDtypeStruct((s, e), F32),
        grid=(s // tm, ng),
        in_specs=[pl.BlockSpec((tm, g), lambda i, j: (i, j)),
                  pl.BlockSpec((None, g, g), lambda i, j: (j, 0, 0))],
        out_specs=pl.BlockSpec((tm, g), lambda i, j: (i, j)),
        compiler_params=_params("parallel", "parallel"),
    )(a, w)


def _grouped_weight_grad(a, b, ng, name, tk=1024):
    s, e = a.shape
    g = e // ng
    tk = _pick(s, tk)
    nk = s // tk

    def body(a_ref, b_ref, o_ref):
        kk = pl.program_id(1)

        @pl.when(kk == 0)
        def _():
            o_ref[...] = jnp.zeros_like(o_ref)

        o_ref[...] += lax.dot_general(a_ref[...], b_ref[...], (((0,), (0,)), ((), ())),
                                      preferred_element_type=F32)

    return pl.pallas_call(
        body, name=name,
        out_shape=jax.ShapeDtypeStruct((ng, g, g), F32),
        grid=(ng, nk),
        in_specs=[pl.BlockSpec((tk, g), lambda j, k: (k, j)),
                  pl.BlockSpec((tk, g), lambda j, k: (k, j))],
        out_specs=pl.BlockSpec((None, g, g), lambda j, k: (j, 0, 0)),
        compiler_params=_params("parallel", "arbitrary"),
    )(a, b)


def _rms(x):
    r = lax.rsqrt(jnp.mean(x * x, axis=1, keepdims=True) + RMS_EPS)
    return x * r, r


def _rmsnorm_fwd(x, g, name, ts=256):
    s, d = x.shape
    ts = _pick(s, ts, 8)

    def body(x_ref, g_ref, h_ref):
        xhat, _ = _rms(x_ref[...])
        h_ref[...] = (xhat * g_ref[...]).astype(BF16)

    return pl.pallas_call(
        body, name=name,
        out_shape=jax.ShapeDtypeStruct((s, d), BF16),
        grid=(s // ts,),
        in_specs=[pl.BlockSpec((ts, d), lambda i: (i, 0)), pl.BlockSpec((1, d), lambda i: (0, 0))],
        out_specs=pl.BlockSpec((ts, d), lambda i: (i, 0)),
        compiler_params=_params("parallel"),
    )(x, g)


def _outproj_norm(z, w, x, g, name, tm=512):
    s, e = z.shape
    d = w.shape[1]
    tm = _pick(s, tm)

    def body(z_ref, w_ref, x_ref, g_ref, xo_ref, h_ref):
        xn = x_ref[...] + jnp.dot(z_ref[...], w_ref[...], preferred_element_type=F32)
        xo_ref[...] = xn
        xhat, _ = _rms(xn)
        h_ref[...] = (xhat * g_ref[...]).astype(BF16)

    return pl.pallas_call(
        body, name=name,
        out_shape=(jax.ShapeDtypeStruct((s, d), F32), jax.ShapeDtypeStruct((s, d), BF16)),
        grid=(s // tm,),
        in_specs=[pl.BlockSpec((tm, e), lambda i: (i, 0)), pl.BlockSpec((e, d), lambda i: (0, 0)),
                  pl.BlockSpec((tm, d), lambda i: (i, 0)), pl.BlockSpec((1, d), lambda i: (0, 0))],
        out_specs=(pl.BlockSpec((tm, d), lambda i: (i, 0)), pl.BlockSpec((tm, d), lambda i: (i, 0))),
        compiler_params=_params("parallel"),
    )(z, w, x, g)


def _outproj_loss(z, w, x, g, target, name, tm=512):
    s, e = z.shape
    d = w.shape[1]
    tm = _pick(s, tm)

    def body(z_ref, w_ref, x_ref, g_ref, t_ref, dx_ref, dxb_ref, dg_ref, loss_ref):
        i = pl.program_id(0)
        xn = x_ref[...] + jnp.dot(z_ref[...], w_ref[...], preferred_element_type=F32)
        xhat, r = _rms(xn)
        gain = g_ref[...]
        diff = xhat * gain - t_ref[...]
        dout = diff * (1.0 / d)
        dxhat = dout * gain
        dx = r * (dxhat - xhat * jnp.mean(dxhat * xhat, axis=1, keepdims=True))
        dx_ref[...] = dx
        dxb_ref[...] = dx.astype(BF16)

        @pl.when(i == 0)
        def _():
            dg_ref[...] = jnp.zeros_like(dg_ref)
            loss_ref[...] = jnp.zeros_like(loss_ref)

        dg_ref[...] += jnp.sum(dout * xhat, axis=0, keepdims=True)
        loss_ref[...] += jnp.sum(diff * diff, axis=0, keepdims=True)

    row = lambda i: (i, 0)
    fixed = lambda i: (0, 0)
    return pl.pallas_call(
        body, name=name,
        out_shape=(jax.ShapeDtypeStruct((s, d), F32), jax.ShapeDtypeStruct((s, d), BF16),
                   jax.ShapeDtypeStruct((1, d), F32), jax.ShapeDtypeStruct((1, d), F32)),
        grid=(s // tm,),
        in_specs=[pl.BlockSpec((tm, e), row), pl.BlockSpec((e, d), fixed), pl.BlockSpec((tm, d), row),
                  pl.BlockSpec((1, d), fixed), pl.BlockSpec((tm, d), row)],
        out_specs=(pl.BlockSpec((tm, d), row), pl.BlockSpec((tm, d), row),
                   pl.BlockSpec((1, d), fixed), pl.BlockSpec((1, d), fixed)),
        compiler_params=_params("arbitrary"),
    )(z, w, x, g, target)


def _rmsnorm_bwd(dh, x, g, dx_next, name, ts=256):
    s, d = x.shape
    ts = _pick(s, ts, 8)

    def body(dh_ref, x_ref, g_ref, dn_ref, dx_ref, dxb_ref, dg_ref):
        i = pl.program_id(0)
        xhat, r = _rms(x_ref[...])
        dh_ = dh_ref[...]
        dxhat = dh_ * g_ref[...]
        dx = dn_ref[...] + r * (dxhat - xhat * jnp.mean(dxhat * xhat, axis=1, keepdims=True))
        dx_ref[...] = dx
        dxb_ref[...] = dx.astype(BF16)

        @pl.when(i == 0)
        def _():
            dg_ref[...] = jnp.zeros_like(dg_ref)

        dg_ref[...] += jnp.sum(dh_ * xhat, axis=0, keepdims=True)

    row = lambda i: (i, 0)
    fixed = lambda i: (0, 0)
    return pl.pallas_call(
        body, name=name,
        out_shape=(jax.ShapeDtypeStruct((s, d), F32), jax.ShapeDtypeStruct((s, d), BF16),
                   jax.ShapeDtypeStruct((1, d), F32)),
        grid=(s // ts,),
        in_specs=[pl.BlockSpec((ts, d), row), pl.BlockSpec((ts, d), row), pl.BlockSpec((1, d), fixed),
                  pl.BlockSpec((ts, d), row)],
        out_specs=(pl.BlockSpec((ts, d), row), pl.BlockSpec((ts, d), row), pl.BlockSpec((1, d), fixed)),
        compiler_params=_params("arbitrary"),
    )(dh, x, g, dx_next)


def _pool_counts(t0, rows, cols, window):
    t = t0 + lax.broadcasted_iota(jnp.int32, (rows, cols), 0)
    return jnp.minimum(t + 1, window).astype(F32)


def _pool_fwd(u, name, ts=1024, tc=256):
    s, e = u.shape
    ng = len(POOL_WINDOWS)
    gdim = e // ng
    ts, tc = _pick(s, ts), _pick(gdim, tc)
    cpg = gdim // tc
    hb = ts // POOL_HALO

    def body(u_ref, halo_ref, d_ref):
        i, grp = pl.program_id(0), pl.program_id(1)
        cur = u_ref[...]
        halo = jnp.where(i > 0, halo_ref[...], 0.0)
        ext = jnp.concatenate([halo, cur], axis=0)
        for gi, window in enumerate(POOL_WINDOWS):
            @pl.when(grp == gi)
            def _(window=window):
                acc = ext
                k = 1
                while k < window:
                    acc = acc + pltpu.roll(acc, k, 0)
                    k *= 2
                pooled = acc[POOL_HALO:, :] / _pool_counts(i * ts, ts, tc, window)
                d_ref[...] = (pooled - cur).astype(BF16)

    return pl.pallas_call(
        body, name=name,
        out_shape=jax.ShapeDtypeStruct((s, e), BF16),
        grid=(s // ts, ng, cpg),
        in_specs=[pl.BlockSpec((ts, tc), lambda i, g, j: (i, g * cpg + j)),
                  pl.BlockSpec((POOL_HALO, tc), lambda i, g, j: (jnp.maximum(i * hb - 1, 0), g * cpg + j))],
        out_specs=pl.BlockSpec((ts, tc), lambda i, g, j: (i, g * cpg + j)),
        compiler_params=_params("parallel", "parallel", "parallel"),
    )(u, u)


def _pool_bwd(dd, name, ts=1024, tc=256):
    s, e = dd.shape
    ng = len(POOL_WINDOWS)
    gdim = e // ng
    ts, tc = _pick(s, ts), _pick(gdim, tc)
    cpg = gdim // tc
    hb = ts // POOL_HALO
    n_halo = s // POOL_HALO
    nst = s // ts

    def body(dd_ref, halo_ref, du_ref):
        i, grp = pl.program_id(0), pl.program_id(1)
        cur = dd_ref[...]
        halo = jnp.where(i < nst - 1, halo_ref[...], 0.0)
        ext = jnp.concatenate([cur, halo], axis=0)
        rows = ts + POOL_HALO
        for gi, window in enumerate(POOL_WINDOWS):
            @pl.when(grp == gi)
            def _(window=window):
                acc = ext / _pool_counts(i * ts, rows, tc, window)
                k = 1
                while k < window:
                    acc = acc + pltpu.roll(acc, rows - k, 0)
                    k *= 2
                du_ref[...] = (acc[:ts, :] - cur).astype(BF16)

    return pl.pallas_call(
        body, name=name,
        out_shape=jax.ShapeDtypeStruct((s, e), BF16),
        grid=(nst, ng, cpg),
        in_specs=[pl.BlockSpec((ts, tc), lambda i, g, j: (i, g * cpg + j)),
                  pl.BlockSpec((POOL_HALO, tc),
                               lambda i, g, j: (jnp.minimum((i + 1) * hb, n_halo - 1), g * cpg + j))],
        out_specs=pl.BlockSpec((ts, tc), lambda i, g, j: (i, g * cpg + j)),
        compiler_params=_params("parallel", "parallel", "parallel"),
    )(dd, dd)


def _a_group_fwd(d, w, scale, gate, name, tm=1024):
    s, e = d.shape
    ng, g, _ = w.shape
    tm = _pick(s, tm)

    def body(d_ref, w_ref, s_ref, gate_ref, yr_ref, z_ref):
        yr = jnp.dot(d_ref[...], w_ref[...], preferred_element_type=F32)
        yr_ref[...] = yr
        gt = gate_ref[...]
        z_ref[...] = ((yr * s_ref[...]) * (gt * _sigmoid(gt))).astype(BF16)

    blk = lambda i, j: (i, j)
    return pl.pallas_call(
        body, name=name,
        out_shape=(jax.ShapeDtypeStruct((s, e), F32), jax.ShapeDtypeStruct((s, e), BF16)),
        grid=(s // tm, ng),
        in_specs=[pl.BlockSpec((tm, g), blk), pl.BlockSpec((None, g, g), lambda i, j: (j, 0, 0)),
                  pl.BlockSpec((1, g), lambda i, j: (0, j)), pl.BlockSpec((tm, g), blk)],
        out_specs=(pl.BlockSpec((tm, g), blk), pl.BlockSpec((tm, g), blk)),
        compiler_params=_params("parallel", "parallel"),
    )(d, w, scale, gate)


def _a_gate_bwd(dz, yr, gate, scale, name, ts=512, tc=512):
    s, e = dz.shape
    ts, tc = _pick(s, ts), _pick(e, tc)

    def body(dz_ref, yr_ref, gate_ref, s_ref, dgate_ref, dyr_ref, dscale_ref):
        i = pl.program_id(1)
        dz_, yr_, gt, sc = dz_ref[...], yr_ref[...], gate_ref[...], s_ref[...]
        sg = _sigmoid(gt)
        dy = dz_ * (gt * sg)
        dgate_ref[...] = (dz_ * (yr_ * sc) * (sg * (1.0 + gt * (1.0 - sg)))).astype(BF16)
        dyr_ref[...] = (dy * sc).astype(BF16)

        @pl.when(i == 0)
        def _():
            dscale_ref[...] = jnp.zeros_like(dscale_ref)

        dscale_ref[...] += jnp.sum(dy * yr_, axis=0, keepdims=True)

    blk = lambda j, i: (i, j)
    vec = lambda j, i: (0, j)
    return pl.pallas_call(
        body, name=name,
        out_shape=(jax.ShapeDtypeStruct((s, e), BF16), jax.ShapeDtypeStruct((s, e), BF16),
                   jax.ShapeDtypeStruct((1, e), F32)),
        grid=(e // tc, s // ts),
        in_specs=[pl.BlockSpec((ts, tc), blk), pl.BlockSpec((ts, tc), blk), pl.BlockSpec((ts, tc), blk),
                  pl.BlockSpec((1, tc), vec)],
        out_specs=(pl.BlockSpec((ts, tc), blk), pl.BlockSpec((ts, tc), blk), pl.BlockSpec((1, tc), vec)),
        compiler_params=_params("parallel", "arbitrary"),
    )(dz, yr, gate, scale)


def _gate_fwd(y, gate, name, ts=512, tc=512):
    s, e = y.shape
    ts, tc = _pick(s, ts), _pick(e, tc)

    def body(y_ref, gate_ref, z_ref):
        gt = gate_ref[...]
        z_ref[...] = (y_ref[...] * (gt * _sigmoid(gt))).astype(BF16)

    blk = lambda i, j: (i, j)
    return pl.pallas_call(
        body, name=name,
        out_shape=jax.ShapeDtypeStruct((s, e), BF16),
        grid=(s // ts, e // tc),
        in_specs=[pl.BlockSpec((ts, tc), blk)] * 2,
        out_specs=pl.BlockSpec((ts, tc), blk),
        compiler_params=_params("parallel", "parallel"),
    )(y, gate)


def _gate_bwd(dz, y, gate, name, ts=512, tc=512):
    s, e = dz.shape
    ts, tc = _pick(s, ts), _pick(e, tc)

    def body(dz_ref, y_ref, gate_ref, dgate_ref, dy_ref):
        dz_, gt = dz_ref[...], gate_ref[...]
        sg = _sigmoid(gt)
        dgate_ref[...] = (dz_ * y_ref[...] * (sg * (1.0 + gt * (1.0 - sg)))).astype(BF16)
        dy_ref[...] = (dz_ * (gt * sg)).astype(BF16)

    blk = lambda i, j: (i, j)
    return pl.pallas_call(
        body, name=name,
        out_shape=(jax.ShapeDtypeStruct((s, e), BF16), jax.ShapeDtypeStruct((s, e), BF16)),
        grid=(s // ts, e // tc),
        in_specs=[pl.BlockSpec((ts, tc), blk)] * 3,
        out_specs=(pl.BlockSpec((ts, tc), blk), pl.BlockSpec((ts, tc), blk)),
        compiler_params=_params("parallel", "parallel"),
    )(dz, y, gate)


def _merge_weights(l0, l1, l2):
    m = jnp.maximum(jnp.maximum(l0, l1), l2)
    e0, e1, e2 = jnp.exp(l0 - m), jnp.exp(l1 - m), jnp.exp(l2 - m)
    inv = 1.0 / (e0 + e1 + e2)
    return e0 * inv, e1 * inv, e2 * inv


def _merge_gate_fwd(outs, lses, gate, name, ts=512, tc=512):
    s, e = gate.shape
    ts, tc = _pick(s, ts), _pick(e, tc)

    def body(o0, o1, o2, l0, l1, l2, gate_ref, y_ref, z_ref):
        w0, w1, w2 = _merge_weights(l0[...], l1[...], l2[...])
        y = w0 * o0[...].astype(F32) + w1 * o1[...].astype(F32) + w2 * o2[...].astype(F32)
        y_ref[...] = y
        gt = gate_ref[...]
        z_ref[...] = (y * (gt * _sigmoid(gt))).astype(BF16)

    blk = lambda i, j: (i, j)
    return pl.pallas_call(
        body, name=name,
        out_shape=(jax.ShapeDtypeStruct((s, e), F32), jax.ShapeDtypeStruct((s, e), BF16)),
        grid=(s // ts, e // tc),
        in_specs=[pl.BlockSpec((ts, tc), blk)] * 7,
        out_specs=(pl.BlockSpec((ts, tc), blk), pl.BlockSpec((ts, tc), blk)),
        compiler_params=_params("parallel", "parallel"),
    )(*outs, *lses, gate)


def _merge_gate_bwd(dz, y, gate, lses, name, ts=512, tc=512):
    s, e = dz.shape
    ts, tc = _pick(s, ts), _pick(e, tc)

    def body(dz_ref, y_ref, gate_ref, l0, l1, l2, dgate_ref, d0, d1, d2):
        dz_, gt = dz_ref[...], gate_ref[...]
        sg = _sigmoid(gt)
        dgate_ref[...] = (dz_ * y_ref[...] * (sg * (1.0 + gt * (1.0 - sg)))).astype(BF16)
        dy = dz_ * (gt * sg)
        w0, w1, w2 = _merge_weights(l0[...], l1[...], l2[...])
        d0[...] = (w0 * dy).astype(BF16)
        d1[...] = (w1 * dy).astype(BF16)
        d2[...] = (w2 * dy).astype(BF16)

    blk = lambda i, j: (i, j)
    return pl.pallas_call(
        body, name=name,
        out_shape=tuple([jax.ShapeDtypeStruct((s, e), BF16)] * 4),
        grid=(s // ts, e // tc),
        in_specs=[pl.BlockSpec((ts, tc), blk)] * 6,
        out_specs=tuple([pl.BlockSpec((ts, tc), blk)] * 4),
        compiler_params=_params("parallel", "parallel"),
    )(dz, y, gate, *lses)


def _band(max_dist, width):
    row = lax.broadcasted_iota(jnp.int32, (2 * BLOCK, width), 0) & (BLOCK - 1)
    col = lax.broadcasted_iota(jnp.int32, (2 * BLOCK, width), 1)
    low = row if max_dist == BLOCK else row + 1
    return jnp.logical_and(col >= low, col <= row + BLOCK), col >= BLOCK


def _fill_bias(bias_ref, max_dist):
    band, own = _band(max_dist, 2 * BLOCK)
    bias_ref[0] = jnp.where(band, 0.0, NEG)
    bias_ref[1] = jnp.where(jnp.logical_and(band, own), 0.0, NEG)


def _aligned(v):
    return v if isinstance(v, int) else pl.multiple_of(v, BLOCK)


def _stack_heads(x, lo):
    return jnp.concatenate([jnp.where(lo, x, 0.0), jnp.where(lo, 0.0, x)], axis=0).astype(BF16)


def _unstack_heads(x2, lo):
    return jnp.where(lo, x2[:BLOCK], x2[BLOCK:])


def _head_col(x, hm):
    return jnp.max(jnp.where(hm, x, NEG), axis=1, keepdims=True)


def _dot_nt(a, b):
    return lax.dot_general(a, b, (((1,), (1,)), ((), ())), preferred_element_type=F32)


def _dot_tn(a, b):
    return lax.dot_general(a, b, (((0,), (0,)), ((), ())), preferred_element_type=F32)


def _attn_fwd(q, k, v, sinks, max_dist, rep, out_dtype, name, tq=2048):
    assert max_dist in (BLOCK - 1, BLOCK)
    l, w = q.shape
    n_pairs = w // LANES
    tq = _pick(l, tq)
    n = tq // BLOCK
    n_blk = l // BLOCK
    has_sink = sinks is not None
    scale = HEAD_DIM ** -0.5

    def body(*refs):
        if has_sink:
            sink_ref, refs = refs[0], refs[1:]
        q_ref, kc_ref, kh_ref, vc_ref, vh_ref, o_ref, lse_ref, kx, vx, bias_ref = refs
        i, p = pl.program_id(0), pl.program_id(1)
        kx[0:BLOCK, :] = kh_ref[...]
        kx[BLOCK:, :] = kc_ref[...]
        vx[0:BLOCK, :] = vh_ref[...]
        vx[BLOCK:, :] = vc_ref[...]
        lo = lax.broadcasted_iota(jnp.int32, (BLOCK, LANES), 1) < HEAD_DIM
        _fill_bias(bias_ref, max_dist)
        top = lax.broadcasted_iota(jnp.int32, (2 * BLOCK, 1), 0) < BLOCK

        def scores(j):
            r0 = _aligned(j * BLOCK)
            q2 = _stack_heads(q_ref[pl.ds(r0, BLOCK), :].astype(F32) * scale, lo)
            first = jnp.logical_and(i == 0, j == 0).astype(jnp.int32)
            return _dot_nt(q2, kx[pl.ds(r0, 2 * BLOCK), :]) + bias_ref[first]

        per_step = 2 if n % 2 == 0 else 1

        def step(jj, carry):
            nxt = tuple(scores(jnp.minimum((jj + 1) * per_step + t, n - 1)) for t in range(per_step))
            for t in range(per_step):
                finish(jj * per_step + t, carry[t])
            return nxt

        def finish(j, s2):
            r0 = _aligned(j * BLOCK)
            vw = vx[pl.ds(r0, 2 * BLOCK), :]
            m = jnp.max(s2, axis=1, keepdims=True)
            if has_sink:
                sk = jnp.where(top, sink_ref[2 * p], sink_ref[2 * p + 1])
                m = jnp.maximum(m, sk)
            pr = jnp.exp(s2 - m)
            den = jnp.sum(pr, axis=1, keepdims=True)
            if has_sink:
                den = den + jnp.exp(sk - m)
            o2 = jnp.dot(pr.astype(BF16), vw, preferred_element_type=F32) * (1.0 / den)
            lse2 = m + jnp.log(den)
            o_ref[pl.ds(r0, BLOCK), :] = _unstack_heads(o2, lo).astype(o_ref.dtype)
            lse_ref[pl.ds(r0, BLOCK), :] = _unstack_heads(lse2, lo)

        lax.fori_loop(0, n // per_step, step, tuple(scores(t) for t in range(per_step)))

    cur = lambda i, p: (i, p)
    kv_cur = lambda i, p: (i, p // rep)
    kv_halo = lambda i, p: (jnp.maximum(i * n - 1, 0), p // rep)
    in_specs = [pl.BlockSpec((tq, LANES), cur),
                pl.BlockSpec((tq, LANES), kv_cur), pl.BlockSpec((BLOCK, LANES), kv_halo),
                pl.BlockSpec((tq, LANES), kv_cur), pl.BlockSpec((BLOCK, LANES), kv_halo)]
    args = [q, k, k, v, v]
    if has_sink:
        in_specs = [pl.BlockSpec(memory_space=pltpu.SMEM)] + in_specs
        args = [sinks] + args
    del n_blk
    return pl.pallas_call(
        body, name=name,
        out_shape=(jax.ShapeDtypeStruct((l, w), out_dtype), jax.ShapeDtypeStruct((l, w), F32)),
        grid=(l // tq, n_pairs),
        in_specs=in_specs,
        out_specs=(pl.BlockSpec((tq, LANES), cur), pl.BlockSpec((tq, LANES), cur)),
        scratch_shapes=[pltpu.VMEM((tq + BLOCK, LANES), BF16), pltpu.VMEM((tq + BLOCK, LANES), BF16),
                        pltpu.VMEM((2, 2 * BLOCK, 2 * BLOCK), F32)],
        compiler_params=_params("parallel", "parallel"),
    )(*args)


def _attn_bwd(q, k, v, do, y, lse, sinks, max_dist, rep, name, tq=2048):
    l, w = q.shape
    n_pairs = w // LANES
    tq = _pick(l, tq)
    n = tq // BLOCK
    n_blk = l // BLOCK
    n_sb = l // tq
    has_sink = sinks is not None
    scale = HEAD_DIM ** -0.5
    kv_dtype = F32 if rep > 1 else BF16
    ext = tq + BLOCK

    def body(*refs):
        if has_sink:
            sink_ref, refs = refs[0], refs[1:]
        (q_ref, qn_ref, kc_ref, kh_ref, vc_ref, vh_ref, do_ref, don_ref, y_ref, yn_ref,
         lse_ref, lsen_ref) = refs[:12]
        refs = refs[12:]
        dq_ref, dk_ref, dv_ref = refs[:3]
        refs = refs[3:]
        if has_sink:
            dsink_ref, refs = refs[0], refs[1:]
        kx, vx, dkx, dvx, bias_ref = refs
        i, p = pl.program_id(0), pl.program_id(1)
        _fill_bias(bias_ref, max_dist)
        own_rows = (q_ref, do_ref, y_ref, lse_ref)
        next_rows = (qn_ref, don_ref, yn_ref, lsen_ref)
        kx[0:BLOCK, :] = kh_ref[...]
        kx[BLOCK:, :] = kc_ref[...]
        vx[0:BLOCK, :] = vh_ref[...]
        vx[BLOCK:, :] = vc_ref[...]
        dkx[...] = jnp.zeros_like(dkx)
        dvx[...] = jnp.zeros_like(dvx)
        lo = lax.broadcasted_iota(jnp.int32, (BLOCK, LANES), 1) < HEAD_DIM
        hi = jnp.logical_not(lo)
        top = lax.broadcasted_iota(jnp.int32, (2 * BLOCK, 1), 0) < BLOCK

        def rows_of(j):
            if isinstance(j, int) and j == n:
                return next_rows, 0
            return own_rows, _aligned(j * BLOCK)

        def front(j, width):
            (qr, dor, _, _), q0 = rows_of(j)
            r0 = _aligned(j * BLOCK)
            first = jnp.logical_and(i == 0, j == 0).astype(jnp.int32)
            q2 = _stack_heads(qr[pl.ds(q0, BLOCK), :].astype(F32) * scale, lo)
            do2 = _stack_heads(dor[pl.ds(q0, BLOCK), :].astype(F32), lo)
            s2 = _dot_nt(q2, kx[pl.ds(r0, width), :]) + bias_ref[first, :, pl.ds(0, width)]
            return s2, _dot_nt(do2, vx[pl.ds(r0, width), :])

        def back(j, width, q_valid, s2, dp2, sink_acc):
            (qr, dor, yr, lser), q0 = rows_of(j)
            r0 = _aligned(j * BLOCK)
            dof = dor[pl.ds(q0, BLOCK), :].astype(F32)
            yb, lseb = yr[pl.ds(q0, BLOCK), :].astype(F32), lser[pl.ds(q0, BLOCK), :]
            q2 = _stack_heads(qr[pl.ds(q0, BLOCK), :].astype(F32) * scale, lo)
            do2 = _stack_heads(dof, lo)
            prod = dof * yb
            delta = jnp.concatenate([jnp.sum(jnp.where(lo, prod, 0.0), axis=1, keepdims=True),
                                     jnp.sum(jnp.where(lo, 0.0, prod), axis=1, keepdims=True)], axis=0)
            lse2 = jnp.concatenate([_head_col(lseb, lo), _head_col(lseb, hi)], axis=0)
            pr = jnp.exp(s2 - lse2)
            if q_valid is not True:
                pr = jnp.where(q_valid, pr, 0.0)
            ds = pr * (dp2 - delta)
            dkx[pl.ds(r0, width), :] += _dot_tn(ds.astype(BF16), q2)
            dvx[pl.ds(r0, width), :] += _dot_tn(pr.astype(BF16), do2)
            if width == 2 * BLOCK:
                dq2 = jnp.dot(ds.astype(BF16), kx[pl.ds(r0, width), :], preferred_element_type=F32) * scale
                dq_ref[pl.ds(r0, BLOCK), :] = _unstack_heads(dq2, lo).astype(dq_ref.dtype)
            if has_sink:
                sk = jnp.where(top, sink_ref[2 * p], sink_ref[2 * p + 1])
                sink_acc = sink_acc - jnp.exp(sk - lse2) * delta
            return sink_acc

        per_step = 2 if n % 2 == 0 else 1

        def step(jj, sink_acc):
            fronts = [front(jj * per_step + t, 2 * BLOCK) for t in range(per_step)]
            for t in range(per_step):
                sink_acc = back(jj * per_step + t, 2 * BLOCK, True, *fronts[t], sink_acc)
            return sink_acc

        zero_col = jnp.zeros((2 * BLOCK, 1), F32)
        sink_acc = lax.fori_loop(0, n // per_step, step, zero_col)
        if n_sb > 1:
            back(n, BLOCK, i < n_sb - 1, *front(n, BLOCK), zero_col)

        dk_new, dv_new = dkx[BLOCK:, :], dvx[BLOCK:, :]
        if rep == 1:
            dk_ref[...] = dk_new.astype(dk_ref.dtype)
            dv_ref[...] = dv_new.astype(dv_ref.dtype)
        else:
            @pl.when(p % rep == 0)
            def _():
                dk_ref[...] = dk_new
                dv_ref[...] = dv_new

            @pl.when(p % rep != 0)
            def _():
                dk_ref[...] += dk_new
                dv_ref[...] += dv_new
        if has_sink:
            rowi = lax.broadcasted_iota(jnp.int32, (8, LANES), 0)
            s0 = jnp.sum(sink_acc[:BLOCK], axis=0, keepdims=True)
            s1 = jnp.sum(sink_acc[BLOCK:], axis=0, keepdims=True)
            dsink_ref[...] = jnp.where(rowi == 0, s0, jnp.where(rowi == 1, s1, 0.0))

    cur = lambda i, p: (i, p)
    nxt = lambda i, p: (jnp.minimum((i + 1) * n, n_blk - 1), p)
    kv_cur = lambda i, p: (i, p // rep)
    kv_halo = lambda i, p: (jnp.maximum(i * n - 1, 0), p // rep)
    big, small = (tq, LANES), (BLOCK, LANES)
    in_specs = [pl.BlockSpec(big, cur), pl.BlockSpec(small, nxt),
                pl.BlockSpec(big, kv_cur), pl.BlockSpec(small, kv_halo),
                pl.BlockSpec(big, kv_cur), pl.BlockSpec(small, kv_halo),
                pl.BlockSpec(big, cur), pl.BlockSpec(small, nxt),
                pl.BlockSpec(big, cur), pl.BlockSpec(small, nxt),
                pl.BlockSpec(big, cur), pl.BlockSpec(small, nxt)]
    args = [q, q, k, k, v, v, do, do, y, y, lse, lse]
    out_shape = [jax.ShapeDtypeStruct((l, w), BF16),
                 jax.ShapeDtypeStruct(k.shape, kv_dtype), jax.ShapeDtypeStruct(v.shape, kv_dtype)]
    out_specs = [pl.BlockSpec(big, cur), pl.BlockSpec(big, kv_cur), pl.BlockSpec(big, kv_cur)]
    if has_sink:
        in_specs = [pl.BlockSpec(memory_space=pltpu.SMEM)] + in_specs
        args = [sinks] + args
        out_shape.append(jax.ShapeDtypeStruct((n_sb, n_pairs, 8, LANES), F32))
        out_specs.append(pl.BlockSpec((None, None, 8, LANES), lambda i, p: (i, p, 0, 0)))
    return pl.pallas_call(
        body, name=name,
        out_shape=tuple(out_shape),
        grid=(n_sb, n_pairs),
        in_specs=in_specs,
        out_specs=tuple(out_specs),
        scratch_shapes=[pltpu.VMEM((ext, LANES), BF16), pltpu.VMEM((ext, LANES), BF16),
                        pltpu.VMEM((ext, LANES), F32), pltpu.VMEM((ext, LANES), F32),
                        pltpu.VMEM((2, 2 * BLOCK, 2 * BLOCK), F32)],
        compiler_params=_params("parallel", "arbitrary"),
    )(*args)


def _sum_slots(recv, name, ts=256):
    nd, r, c = recv.shape
    ts = _pick(r, ts, 8)

    def body(r_ref, o_ref):
        acc = r_ref[0].astype(F32)
        for dev in range(1, nd):
            acc = acc + r_ref[dev].astype(F32)
        o_ref[...] = acc

    return pl.pallas_call(
        body, name=name,
        out_shape=jax.ShapeDtypeStruct((r, c), F32),
        grid=(r // ts,),
        in_specs=[pl.BlockSpec((nd, ts, c), lambda i: (0, i, 0))],
        out_specs=pl.BlockSpec((ts, c), lambda i: (i, 0)),
        compiler_params=_params("parallel"),
    )(recv)


def _adamw(w, g, m, v, name, ts=256):
    r, c = w.shape
    ts = _pick(r, ts, 8)
    c1 = 1.0 - ADAM_B1 ** ADAM_STEP
    c2 = 1.0 - ADAM_B2 ** ADAM_STEP

    def body(w_ref, g_ref, m_ref, v_ref, d_ref, mo_ref, vo_ref):
        g_ = g_ref[...]
        m_ = ADAM_B1 * m_ref[...] + (1.0 - ADAM_B1) * g_
        v_ = ADAM_B2 * v_ref[...] + (1.0 - ADAM_B2) * (g_ * g_)
        mo_ref[...] = m_
        vo_ref[...] = v_
        d_ref[...] = -ADAM_LR * ((m_ / c1) / (jnp.sqrt(v_ / c2) + ADAM_EPS) + ADAM_WD * w_ref[...])

    blk = pl.BlockSpec((ts, c), lambda i: (i, 0))
    return pl.pallas_call(
        body, name=name,
        out_shape=tuple([jax.ShapeDtypeStruct((r, c), F32)] * 3),
        grid=(r // ts,),
        in_specs=[blk] * 4,
        out_specs=(blk, blk, blk),
        compiler_params=_params("parallel"),
    )(w, g, m, v)


def _rows(a):
    flat = a.reshape(-1)
    pad = (-flat.shape[0]) % PACK_W
    if pad:
        flat = jnp.concatenate([flat, jnp.zeros((pad,), flat.dtype)])
    return flat.reshape(-1, PACK_W)


def _pad_rows(a, mult):
    pad = (-a.shape[-2]) % mult
    if pad:
        widths = [(0, 0)] * (a.ndim - 2) + [(0, pad), (0, 0)]
        a = jnp.pad(a, widths)
    return a


def _to_global(stack, axis):
    moved = jnp.moveaxis(stack, 0, axis)
    shp = list(moved.shape)
    shp[axis:axis + 2] = [shp[axis] * shp[axis + 1]]
    return moved.reshape(shp)


def _to_stack(full, axis):
    shp = list(full.shape)
    shp[axis:axis + 1] = [N_DEV, shp[axis] // N_DEV]
    return jnp.moveaxis(full.reshape(shp), axis, 0)


_BIG = (("w_out", 1), ("a_w_in", 2), ("a_w_group", 2), ("b_w_in", 2), ("c_w_in", 2))


def _dup_heads(wk, n_kv):
    d = wk.shape[0]
    return jnp.tile(wk.reshape(d, n_kv, 1, HEAD_DIM), (1, 1, 2, 1)).reshape(d, n_kv * LANES)


def _fold_heads(dwk, n_kv):
    d = dwk.shape[0]
    return dwk.reshape(d, n_kv, 2, HEAD_DIM).sum(axis=2).reshape(d, n_kv * HEAD_DIM)


def _view(a, dil):
    s, w = a.shape
    return a.reshape(s // dil, dil * w)


def _unview(a, dil):
    l, w = a.shape
    return a.reshape(l * dil, w // dil)


def kernel(x, norm_g, final_g, w_out, a_w_in, a_w_group, a_scale, b_w_in, b_sinks, c_w_in, loss_target, m_norm_g, m_final_g, m_w_out, m_a_w_in, m_a_w_group, m_a_scale, m_b_w_in, m_b_sinks, m_c_w_in, v_norm_g, v_final_g, v_w_out, v_a_w_in, v_a_w_group, v_a_scale, v_b_w_in, v_b_sinks, v_c_w_in):
    local = dict(w_out=w_out, a_w_in=a_w_in, a_w_group=a_w_group, b_w_in=b_w_in, c_w_in=c_w_in)
    mom_m = dict(w_out=m_w_out, a_w_in=m_a_w_in, a_w_group=m_a_w_group, b_w_in=m_b_w_in, c_w_in=m_c_w_in)
    mom_v = dict(w_out=v_w_out, a_w_in=v_a_w_in, a_w_group=v_a_w_group, b_w_in=v_b_w_in, c_w_in=v_c_w_in)
    s, d = x.shape[1], x.shape[2]
    depth = norm_g.shape[0]
    e = w_out.shape[1] * N_DEV
    n_heads = e // HEAD_DIM
    n_kv = n_heads // Q_PER_KV
    kv_w = n_kv * HEAD_DIM
    rep = Q_PER_KV // 2
    n_groups = len(POOL_WINDOWS)
    me = 4 * lax.axis_index("x") + 2 * lax.axis_index("y") + lax.axis_index("c")

    sizes = [local[n].size // PACK_W for n, _ in _BIG]
    offs = [sum(sizes[:k]) for k in range(len(sizes) + 1)]
    wpack = _pad_rows(jnp.concatenate([_rows(local[n].astype(BF16)) for n, _ in _BIG], axis=0), 16)
    r_pack = wpack.shape[0]
    spack = _pad_rows(_rows(a_scale), 8)
    wall, sall = _gather([wpack, spack], "gather_weights")
    full = {}
    for k, (name, axis) in enumerate(_BIG):
        stack = wall[:, offs[k]:offs[k + 1], :].reshape((N_DEV,) + local[name].shape)
        full[name] = _to_global(stack, axis)
    scale_full = _to_global(sall.reshape(N_DEV, -1)[:, :a_scale.size].reshape((N_DEV,) + a_scale.shape), 1)

    wout_t = jnp.swapaxes(full["w_out"], 1, 2)
    wa = full["a_w_in"]
    wa_t = jnp.swapaxes(wa, 1, 2)
    wg = full["a_w_group"]
    wg_t = jnp.swapaxes(wg, 2, 3)
    wb = full["b_w_in"][0]
    wb_ext = jnp.concatenate([wb[:, :e], _dup_heads(wb[:, e:e + kv_w], n_kv),
                              _dup_heads(wb[:, e + kv_w:e + 2 * kv_w], n_kv), wb[:, e + 2 * kv_w:]], axis=1)
    wb_ext_t = wb_ext.T
    kd_w = n_kv * LANES
    wc = full["c_w_in"][0]
    wc_t = wc.T

    xs, hs, zs, saved = [x.reshape(s, d)], [], [], []
    hs.append(_rmsnorm_fwd(xs[0], norm_g[0:1], "norm0"))
    loss_vec = dfinal = dx = dxb = None
    for i in range(depth):
        kind, j = i % 3, i // 3
        h = hs[i]
        tag = f"l{i}"
        if kind == 0:
            u = _matmul(h, wa[j][:, :e], F32, tag + "_in_u")
            gate = _matmul(h, wa[j][:, e:], F32, tag + "_in_gate")
            dpool = _pool_fwd(u, tag + "_pool")
            yr, z = _a_group_fwd(dpool, wg[j], scale_full[j:j + 1], gate, tag + "_group")
            saved.append(dict(dpool=dpool, yr=yr, gate=gate))
        elif kind == 1:
            q = _matmul(h, wb_ext[:, :e], BF16, tag + "_in_q")
            kd = _matmul(h, wb_ext[:, e:e + kd_w], BF16, tag + "_in_k")
            vd = _matmul(h, wb_ext[:, e + kd_w:e + 2 * kd_w], BF16, tag + "_in_v")
            gate = _matmul(h, wb_ext[:, e + 2 * kd_w:], F32, tag + "_in_gate")
            sinks = b_sinks[j]
            y, lse = _attn_fwd(q, kd, vd, sinks, SWA_MAX_DIST, rep, F32, tag + "_attn")
            z = _gate_fwd(y, gate, tag + "_gate")
            saved.append(dict(q=q, kd=kd, vd=vd, gate=gate, y=y, lse=lse, sinks=sinks))
        else:
            qkv, outs, lses = [], [], []
            for gi, (window, dil) in enumerate(DILATED_PAIRS):
                trio = [_view(_matmul(h, wc[:, (3 * gi + t) * e:(3 * gi + t + 1) * e], BF16,
                                      f"{tag}_in_{'qkv'[t]}{gi}"), dil) for t in range(3)]
                o, lse = _attn_fwd(trio[0], trio[1], trio[2], None, window // dil, 1, BF16, f"{tag}_attn{gi}")
                qkv.append(trio)
                outs.append(_unview(o, dil))
                lses.append(lse)
            gate = _matmul(h, wc[:, 9 * e:], F32, tag + "_in_gate")
            lses_tok = [_unview(lse, dil) for lse, (_, dil) in zip(lses, DILATED_PAIRS)]
            y, z = _merge_gate_fwd(outs, lses_tok, gate, tag + "_merge")
            saved.append(dict(qkv=qkv, lses=lses, lses_tok=lses_tok, gate=gate, y=y))
        zs.append(z)
        if i + 1 < depth:
            x_new, h_new = _outproj_norm(z, full["w_out"][i], xs[i], norm_g[i + 1:i + 2], tag + "_out")
            xs.append(x_new)
            hs.append(h_new)
        else:
            dx, dxb, dfinal, loss_vec = _outproj_loss(z, full["w_out"][i], xs[i], final_g.reshape(1, d),
                                                      loss_target.reshape(s, d), tag + "_out_loss")

    g_full = {"w_out": [None] * depth, "a_w_in": [None] * wa.shape[0], "a_w_group": [None] * wa.shape[0]}
    d_norm = [None] * depth
    d_scale = [None] * wa.shape[0]
    d_sinks = None
    for i in reversed(range(depth)):
        kind, j = i % 3, i // 3
        tag = f"b{i}"
        sv = saved[i]
        g_full["w_out"][i] = _matmul_tn(zs[i], dxb, tag + "_dwout")
        dz = _matmul(dxb, wout_t[i], F32, tag + "_dz")
        if kind == 0:
            dgate, dyr, dsc = _a_gate_bwd(dz, sv["yr"], sv["gate"], scale_full[j:j + 1], tag + "_gate")
            d_scale[j] = dsc
            dd = _grouped_matmul(dyr, wg_t[j], tag + "_dd")
            du = _pool_bwd(dd, tag + "_pool")
            g_full["a_w_group"][j] = _grouped_weight_grad(sv["dpool"], dyr, n_groups, tag + "_dwg")
            dp = jnp.concatenate([du, dgate], axis=1)
            w_t = wa_t[j]
        elif kind == 1:
            dgate, do = _gate_bwd(dz, sv["y"], sv["gate"], tag + "_gate")
            dq, dkd, dvd, dsink = _attn_bwd(sv["q"], sv["kd"], sv["vd"], do, sv["y"], sv["lse"], sv["sinks"],
                                            SWA_MAX_DIST, rep, tag + "_attn")
            d_sinks = dsink[:, :, 0:2, 0].sum(axis=0).reshape(1, n_heads)
            dp = jnp.concatenate([dq, dkd.astype(BF16), dvd.astype(BF16), dgate], axis=1)
            w_t = wb_ext_t
        else:
            dgate, *dos = _merge_gate_bwd(dz, sv["y"], sv["gate"], sv["lses_tok"], tag + "_merge")
            parts = []
            for gi, (window, dil) in enumerate(DILATED_PAIRS):
                qv, kv, vv = sv["qkv"][gi]
                grads = _attn_bwd(qv, kv, vv, _view(dos[gi], dil), _view(sv["y"], dil), sv["lses"][gi], None,
                                  window // dil, 1, f"{tag}_attn{gi}")
                parts += [_unview(g_, dil) for g_ in grads]
            dp = jnp.concatenate(parts + [dgate], axis=1)
            w_t = wc_t
        dw_in = _matmul_tn(hs[i], dp, tag + "_dwin")
        if kind == 0:
            g_full["a_w_in"][j] = dw_in
        elif kind == 1:
            g_full["b_w_in"] = jnp.concatenate(
                [dw_in[:, :e], _fold_heads(dw_in[:, e:e + kd_w], n_kv),
                 _fold_heads(dw_in[:, e + kd_w:e + 2 * kd_w], n_kv), dw_in[:, e + 2 * kd_w:]], axis=1)[None]
        else:
            g_full["c_w_in"] = dw_in[None]
        dh = _matmul(dp, w_t, F32, tag + "_dh")
        dx, dxb, d_norm[i] = _rmsnorm_bwd(dh, xs[i], norm_g[i:i + 1], dx, tag + "_norm")
    grad_x = dx.reshape(x.shape)
    for name in ("w_out", "a_w_in", "a_w_group"):
        g_full[name] = jnp.stack(g_full[name], axis=0)

    gpack = jnp.concatenate(
        [_to_stack(g_full[n], axis).astype(BF16).reshape(N_DEV, -1, PACK_W) for n, axis in _BIG], axis=1)
    gpack = _pad_rows(gpack, 16)
    loss_local = (0.5 / d) * jnp.sum(loss_vec)
    small = [jnp.concatenate(d_norm, axis=0), dfinal, d_sinks, jnp.concatenate(d_scale, axis=0),
             loss_local.reshape(1, 1)]
    small_rows = [_rows(a) for a in small]
    small_offs = [sum(r.shape[0] for r in small_rows[:k]) for k in range(len(small_rows) + 1)]
    small_pack = _pad_rows(jnp.concatenate(small_rows, axis=0), 8)
    core = lax.axis_index("c").astype(jnp.int32).reshape(1)
    from_sibling = _sibling_exchange(gpack, "exchange_sibling")
    chip_sums = _pair_sum(gpack, from_sibling, core, "sum_pair")
    grecv, srecv = _chip_exchange(chip_sums, small_pack, "exchange_chips")
    gsum = _sum_slots(grecv, "sum_grads")
    ssum = _sum_slots(srecv, "sum_small")

    def small_part(k, like):
        return ssum[small_offs[k]:small_offs[k + 1]].reshape(-1)[:like.size].reshape(like.shape)

    g_norm = small_part(0, norm_g)
    g_final = small_part(1, final_g)
    g_sinks = small_part(2, b_sinks)
    g_scale_full = small_part(3, scale_full)
    loss = ssum[small_offs[4], 0]
    g_scale = lax.dynamic_slice_in_dim(g_scale_full, me * a_scale.shape[1], a_scale.shape[1], axis=1)

    small_w = [("norm_g", norm_g, m_norm_g, v_norm_g, g_norm), ("final_g", final_g, m_final_g, v_final_g, g_final),
               ("a_scale", a_scale, m_a_scale, v_a_scale, g_scale), ("b_sinks", b_sinks, m_b_sinks, v_b_sinks, g_sinks)]
    big_rows = lambda tree: _pad_rows(jnp.concatenate([_rows(tree[n]) for n, _ in _BIG], axis=0), 16)
    tail = lambda idx: [_rows(t[idx]) for t in small_w]
    tail_sizes = [r.shape[0] for r in tail(1)]
    tail_offs = [r_pack + sum(tail_sizes[:k]) for k in range(len(tail_sizes) + 1)]
    w_all = _pad_rows(jnp.concatenate([big_rows(local)] + tail(1), axis=0), ADAM_ROWS)
    m_all = _pad_rows(jnp.concatenate([big_rows(mom_m)] + tail(2), axis=0), ADAM_ROWS)
    v_all = _pad_rows(jnp.concatenate([big_rows(mom_v)] + tail(3), axis=0), ADAM_ROWS)
    g_all = _pad_rows(jnp.concatenate([gsum] + tail(4), axis=0), ADAM_ROWS)
    delta_all, m_new, v_new = _adamw(w_all, g_all, m_all, v_all, "adamw", ts=ADAM_ROWS)

    def unpack(packed):
        out = {}
        for k, (name, _) in enumerate(_BIG):
            out[name] = packed[offs[k]:offs[k + 1]].reshape(local[name].shape)
        for k, (name, w_, _, _, _) in enumerate(small_w):
            out[name] = packed[tail_offs[k]:tail_offs[k + 1]].reshape(-1)[:w_.size].reshape(w_.shape)
        return out

    order = ("norm_g", "final_g", "w_out", "a_w_in", "a_w_group", "a_scale", "b_w_in", "b_sinks", "c_w_in")
    grads = unpack(g_all)
    deltas, new_m, new_v = unpack(delta_all), unpack(m_new), unpack(v_new)
    return (loss, grad_x, *[grads[n] for n in order], *[deltas[n] for n in order],
            *[new_m[n] for n in order], *[new_v[n] for n in order])
```

```python
import functools

import jax
import jax.numpy as jnp
from jax import lax
from jax.experimental import pallas as pl
from jax.experimental.pallas import tpu as pltpu

F32 = jnp.float32
BF16 = jnp.bfloat16

N_DEV = 8
HEAD_DIM = 64
LANES = 128
BLOCK = 128
Q_PER_KV = 8
POOL_WINDOWS = (2, 4, 8, 16)
POOL_HALO = 16
DILATED_PAIRS = ((128, 1), (512, 4), (2048, 16))
SWA_MAX_DIST = 127
RMS_EPS = 1e-5
PACK_W = 1024
ADAM_ROWS = 256
NEG = -1e30

ADAM_LR = 0.001
ADAM_B1 = 0.9
ADAM_B2 = 0.999
ADAM_EPS = 1e-08
ADAM_WD = 0.01
ADAM_STEP = 10

VMEM_LIMIT = 48 * 1024 * 1024


def _params(*sem):
    return pltpu.CompilerParams(dimension_semantics=sem if sem else None, vmem_limit_bytes=VMEM_LIMIT)


def _pick(dim, target, mult=LANES):
    if dim <= target:
        return dim
    t = target - target % mult
    while dim % t:
        t -= mult
    return t


def _sigmoid(x):
    return 1.0 / (1.0 + jnp.exp(-x))


CHIP_OFFSETS = (2, 4, 6)
ANY_SPEC = pl.BlockSpec(memory_space=pl.ANY)


def _where_am_i():
    x, y, c = lax.axis_index("x"), lax.axis_index("y"), lax.axis_index("c")
    return x, y, c, 4 * x + 2 * y + c


def _peer(x, y, c, r):
    return x ^ ((r >> 2) & 1), y ^ ((r >> 1) & 1), c ^ (r & 1)


def _gather(blocks, name):
    n = len(blocks)

    def body(*refs):
        send, recv = refs[:n], refs[n:2 * n]
        send_sems, recv_sems, local_sems = refs[2 * n:]
        x, y, c, me = _where_am_i()
        sib = _peer(x, y, c, 1)
        sib_id = me ^ 1

        def copy(k, slot_sem, src, slot, to):
            return pltpu.make_async_remote_copy(
                src_ref=src, dst_ref=recv[k].at[slot], send_sem=send_sems.at[k, slot_sem],
                recv_sem=recv_sems.at[k, slot_sem], device_id=to, device_id_type=pl.DeviceIdType.MESH)

        started = []
        for k in range(n):
            own = pltpu.make_async_copy(send[k], recv[k].at[me], local_sems.at[k])
            own.start()
            started.append(own)
        sends = []
        for k in range(n):
            sends.append(copy(k, 0, send[k], me, sib))
            for j, r in enumerate(CHIP_OFFSETS):
                sends.append(copy(k, 1 + j, send[k], me, _peer(x, y, c, r)))
        for cp in sends:
            cp.start()
        for j, r in enumerate(CHIP_OFFSETS):
            for k in range(n):
                src_id = me ^ r
                copy(k, 1 + j, send[k], src_id, sib).wait_recv()
                fwd = copy(k, 4 + j, recv[k].at[src_id], src_id, sib)
                fwd.start()
                sends.append(fwd)
        for k in range(n):
            copy(k, 0, send[k], sib_id, sib).wait_recv()
            for j, r in enumerate(CHIP_OFFSETS):
                copy(k, 4 + j, send[k], sib_id ^ r, sib).wait_recv()
        for cp in sends:
            cp.wait_send()
        for own in started:
            own.wait()

    return pl.pallas_call(
        body, name=name,
        out_shape=tuple(jax.ShapeDtypeStruct((N_DEV,) + b.shape, b.dtype) for b in blocks),
        in_specs=[ANY_SPEC] * n,
        out_specs=tuple([ANY_SPEC] * n),
        scratch_shapes=[pltpu.SemaphoreType.DMA((n, N_DEV - 1)), pltpu.SemaphoreType.DMA((n, N_DEV - 1)),
                        pltpu.SemaphoreType.DMA((n,))],
    )(*blocks)


def _sibling_exchange(gpack, name):
    n_chips = N_DEV // 2

    def body(g_ref, t_ref, send_sems, recv_sems):
        x, y, c, _ = _where_am_i()
        sib = _peer(x, y, c, 1)
        copies = [pltpu.make_async_remote_copy(
            src_ref=g_ref.at[2 * chip + (1 - c)], dst_ref=t_ref.at[chip], send_sem=send_sems.at[chip],
            recv_sem=recv_sems.at[chip], device_id=sib, device_id_type=pl.DeviceIdType.MESH)
            for chip in range(n_chips)]
        for cp in copies:
            cp.start()
        for cp in copies:
            cp.wait_recv()
        for cp in copies:
            cp.wait_send()

    return pl.pallas_call(
        body, name=name,
        out_shape=jax.ShapeDtypeStruct((n_chips,) + gpack.shape[1:], gpack.dtype),
        in_specs=[ANY_SPEC], out_specs=ANY_SPEC,
        scratch_shapes=[pltpu.SemaphoreType.DMA((n_chips,)), pltpu.SemaphoreType.DMA((n_chips,))],
    )(gpack)


def _chip_exchange(csum, small, name):
    n_chips = N_DEV // 2

    def body(c_ref, s_ref, r_ref, sr_ref, send_sems, recv_sems, small_send, small_recv, local_sems):
        x, y, c, me = _where_am_i()
        my_chip = 2 * x + y
        own = [pltpu.make_async_copy(c_ref.at[my_chip], r_ref.at[my_chip], local_sems.at[0]),
               pltpu.make_async_copy(s_ref, sr_ref.at[me], local_sems.at[1])]
        for cp in own:
            cp.start()
        sends, recvs = [], []
        for j, r in enumerate(CHIP_OFFSETS):
            to = _peer(x, y, c, r)
            chip = my_chip ^ (r >> 1)
            sends.append(pltpu.make_async_remote_copy(
                src_ref=c_ref.at[chip], dst_ref=r_ref.at[my_chip], send_sem=send_sems.at[j],
                recv_sem=recv_sems.at[j], device_id=to, device_id_type=pl.DeviceIdType.MESH))
            recvs.append(pltpu.make_async_remote_copy(
                src_ref=c_ref.at[chip], dst_ref=r_ref.at[chip], send_sem=send_sems.at[j],
                recv_sem=recv_sems.at[j], device_id=to, device_id_type=pl.DeviceIdType.MESH))
        for r in range(1, N_DEV):
            to = _peer(x, y, c, r)
            sends.append(pltpu.make_async_remote_copy(
                src_ref=s_ref, dst_ref=sr_ref.at[me], send_sem=small_send.at[r - 1],
                recv_sem=small_recv.at[r - 1], device_id=to, device_id_type=pl.DeviceIdType.MESH))
            recvs.append(pltpu.make_async_remote_copy(
                src_ref=s_ref, dst_ref=sr_ref.at[me ^ r], send_sem=small_send.at[r - 1],
                recv_sem=small_recv.at[r - 1], device_id=to, device_id_type=pl.DeviceIdType.MESH))
        for cp in sends:
            cp.start()
        for cp in recvs:
            cp.wait_recv()
        for cp in sends:
            cp.wait_send()
        for cp in own:
            cp.wait()

    n_off = len(CHIP_OFFSETS)
    return pl.pallas_call(
        body, name=name,
        out_shape=(jax.ShapeDtypeStruct(csum.shape, csum.dtype),
                   jax.ShapeDtypeStruct((N_DEV,) + small.shape, small.dtype)),
        in_specs=[ANY_SPEC, ANY_SPEC], out_specs=(ANY_SPEC, ANY_SPEC),
        scratch_shapes=[pltpu.SemaphoreType.DMA((n_off,)), pltpu.SemaphoreType.DMA((n_off,)),
                        pltpu.SemaphoreType.DMA((N_DEV - 1,)), pltpu.SemaphoreType.DMA((N_DEV - 1,)),
                        pltpu.SemaphoreType.DMA((2,))],
    )(csum, small)


def _pair_sum(gpack, other, core, name, ts=256):
    n_chips, r, c = other.shape
    ts = _pick(r, ts, 16)

    def body(core_ref, g_ref, o_ref, out_ref):
        del core_ref
        out_ref[...] = (g_ref[...].astype(F32) + o_ref[...].astype(F32)).astype(out_ref.dtype)

    return pl.pallas_call(
        body, name=name,
        out_shape=jax.ShapeDtypeStruct(other.shape, other.dtype),
        grid_spec=pltpu.PrefetchScalarGridSpec(
            num_scalar_prefetch=1, grid=(n_chips, r // ts),
            in_specs=[pl.BlockSpec((None, ts, c), lambda j, i, core_ref: (2 * j + core_ref[0], i, 0)),
                      pl.BlockSpec((None, ts, c), lambda j, i, core_ref: (j, i, 0))],
            out_specs=pl.BlockSpec((None, ts, c), lambda j, i, core_ref: (j, i, 0))),
        compiler_params=_params("parallel", "parallel"),
    )(core, gpack, other)


def _matmul(a, b, out_dtype, name, tm=1024, tn=1024, tk=1024):
    m, kdim = a.shape
    n = b.shape[1]
    tm, tn, tk = _pick(m, tm), _pick(n, tn), _pick(kdim, tk)
    nk = kdim // tk

    if nk == 1:
        def body(a_ref, b_ref, o_ref):
            o_ref[...] = jnp.dot(a_ref[...], b_ref[...], preferred_element_type=F32).astype(o_ref.dtype)
        scratch = []
    else:
        def body(a_ref, b_ref, o_ref, acc_ref):
            kk = pl.program_id(2)

            @pl.when(kk == 0)
            def _():
                acc_ref[...] = jnp.zeros_like(acc_ref)

            acc_ref[...] += jnp.dot(a_ref[...], b_ref[...], preferred_element_type=F32)

            @pl.when(kk == nk - 1)
            def _():
                o_ref[...] = acc_ref[...].astype(o_ref.dtype)
        scratch = [pltpu.VMEM((tm, tn), F32)]

    return pl.pallas_call(
        body, name=name,
        out_shape=jax.ShapeDtypeStruct((m, n), out_dtype),
        grid=(m // tm, n // tn, nk),
        in_specs=[pl.BlockSpec((tm, tk), lambda i, j, k: (i, k)),
                  pl.BlockSpec((tk, tn), lambda i, j, k: (k, j))],
        out_specs=pl.BlockSpec((tm, tn), lambda i, j, k: (i, j)),
        scratch_shapes=scratch,
        compiler_params=_params("parallel", "parallel", "arbitrary"),
    )(a, b)


def _matmul_tn(a, b, name, tm=1024, tn=1024, tk=1024):
    kdim, m = a.shape
    n = b.shape[1]
    tm, tn, tk = _pick(m, tm), _pick(n, tn), _pick(kdim, tk)
    nk = kdim // tk

    def body(a_ref, b_ref, o_ref):
        kk = pl.program_id(2)

        @pl.when(kk == 0)
        def _():
            o_ref[...] = jnp.zeros_like(o_ref)

        o_ref[...] += lax.dot_general(a_ref[...], b_ref[...], (((0,), (0,)), ((), ())),
                                      preferred_element_type=F32)

    return pl.pallas_call(
        body, name=name,
        out_shape=jax.ShapeDtypeStruct((m, n), F32),
        grid=(m // tm, n // tn, nk),
        in_specs=[pl.BlockSpec((tk, tm), lambda i, j, k: (k, i)),
                  pl.BlockSpec((tk, tn), lambda i, j, k: (k, j))],
        out_specs=pl.BlockSpec((tm, tn), lambda i, j, k: (i, j)),
        compiler_params=_params("parallel", "parallel", "arbitrary"),
    )(a, b)


def _grouped_matmul(a, w, name, tm=1024):
    s, e = a.shape
    ng, g, _ = w.shape
    tm = _pick(s, tm)

    def body(a_ref, w_ref, o_ref):
        o_ref[...] = jnp.dot(a_ref[...], w_ref[...], preferred_element_type=F32)

    return pl.pallas_call(
        body, name=name,
        out_shape=jax.ShapeDtypeStruct((s, e), F32),
        grid=(s // tm, ng),
        in_specs=[pl.BlockSpec((tm, g), lambda i, j: (i, j)),
                  pl.BlockSpec((None, g, g), lambda i, j: (j, 0, 0))],
        out_specs=pl.BlockSpec((tm, g), lambda i, j: (i, j)),
        compiler_params=_params("parallel", "parallel"),
    )(a, w)


def _grouped_weight_grad(a, b, ng, name, tk=1024):
    s, e = a.shape
    g = e // ng
    tk = _pick(s, tk)
    nk = s // tk

    def body(a_ref, b_ref, o_ref):
        kk = pl.program_id(1)

        @pl.when(kk == 0)
        def _():
            o_ref[...] = jnp.zeros_like(o_ref)

        o_ref[...] += lax.dot_general(a_ref[...], b_ref[...], (((0,), (0,)), ((), ())),
                                      preferred_element_type=F32)

    return pl.pallas_call(
        body, name=name,
        out_shape=jax.ShapeDtypeStruct((ng, g, g), F32),
        grid=(ng, nk),
        in_specs=[pl.BlockSpec((tk, g), lambda j, k: (k, j)),
                  pl.BlockSpec((tk, g), lambda j, k: (k, j))],
        out_specs=pl.BlockSpec((None, g, g), lambda j, k: (j, 0, 0)),
        compiler_params=_params("parallel", "arbitrary"),
    )(a, b)


def _rms(x):
    r = lax.rsqrt(jnp.mean(x * x, axis=1, keepdims=True) + RMS_EPS)
    return x * r, r


def _rmsnorm_fwd(x, g, name, ts=256):
    s, d = x.shape
    ts = _pick(s, ts, 8)

    def body(x_ref, g_ref, h_ref):
        xhat, _ = _rms(x_ref[...])
        h_ref[...] = (xhat * g_ref[...]).astype(BF16)

    return pl.pallas_call(
        body, name=name,
        out_shape=jax.ShapeDtypeStruct((s, d), BF16),
        grid=(s // ts,),
        in_specs=[pl.BlockSpec((ts, d), lambda i: (i, 0)), pl.BlockSpec((1, d), lambda i: (0, 0))],
        out_specs=pl.BlockSpec((ts, d), lambda i: (i, 0)),
        compiler_params=_params("parallel"),
    )(x, g)


def _outproj_norm(z, w, x, g, name, tm=512):
    s, e = z.shape
    d = w.shape[1]
    tm = _pick(s, tm)

    def body(z_ref, w_ref, x_ref, g_ref, xo_ref, h_ref):
        xn = x_ref[...] + jnp.dot(z_ref[...], w_ref[...], preferred_element_type=F32)
        xo_ref[...] = xn
        xhat, _ = _rms(xn)
        h_ref[...] = (xhat * g_ref[...]).astype(BF16)

    return pl.pallas_call(
        body, name=name,
        out_shape=(jax.ShapeDtypeStruct((s, d), F32), jax.ShapeDtypeStruct((s, d), BF16)),
        grid=(s // tm,),
        in_specs=[pl.BlockSpec((tm, e), lambda i: (i, 0)), pl.BlockSpec((e, d), lambda i: (0, 0)),
                  pl.BlockSpec((tm, d), lambda i: (i, 0)), pl.BlockSpec((1, d), lambda i: (0, 0))],
        out_specs=(pl.BlockSpec((tm, d), lambda i: (i, 0)), pl.BlockSpec((tm, d), lambda i: (i, 0))),
        compiler_params=_params("parallel"),
    )(z, w, x, g)


def _outproj_loss(z, w, x, g, target, name, tm=512):
    s, e = z.shape
    d = w.shape[1]
    tm = _pick(s, tm)

    def body(z_ref, w_ref, x_ref, g_ref, t_ref, dx_ref, dxb_ref, dg_ref, loss_ref):
        i = pl.program_id(0)
        xn = x_ref[...] + jnp.dot(z_ref[...], w_ref[...], preferred_element_type=F32)
        xhat, r = _rms(xn)
        gain = g_ref[...]
        diff = xhat * gain - t_ref[...]
        dout = diff * (1.0 / d)
        dxhat = dout * gain
        dx = r * (dxhat - xhat * jnp.mean(dxhat * xhat, axis=1, keepdims=True))
        dx_ref[...] = dx
        dxb_ref[...] = dx.astype(BF16)

        @pl.when(i == 0)
        def _():
            dg_ref[...] = jnp.zeros_like(dg_ref)
            loss_ref[...] = jnp.zeros_like(loss_ref)

        dg_ref[...] += jnp.sum(dout * xhat, axis=0, keepdims=True)
        loss_ref[...] += jnp.sum(diff * diff, axis=0, keepdims=True)

    row = lambda i: (i, 0)
    fixed = lambda i: (0, 0)
    return pl.pallas_call(
        body, name=name,
        out_shape=(jax.ShapeDtypeStruct((s, d), F32), jax.ShapeDtypeStruct((s, d), BF16),
                   jax.ShapeDtypeStruct((1, d), F32), jax.ShapeDtypeStruct((1, d), F32)),
        grid=(s // tm,),
        in_specs=[pl.BlockSpec((tm, e), row), pl.BlockSpec((e, d), fixed), pl.BlockSpec((tm, d), row),
                  pl.BlockSpec((1, d), fixed), pl.BlockSpec((tm, d), row)],
        out_specs=(pl.BlockSpec((tm, d), row), pl.BlockSpec((tm, d), row),
                   pl.BlockSpec((1, d), fixed), pl.BlockSpec((1, d), fixed)),
        compiler_params=_params("arbitrary"),
    )(z, w, x, g, target)


def _rmsnorm_bwd(dhs, x, g, dx_next, name, ts=256):
    s, d = x.shape
    ts = _pick(s, ts, 8)
    n_dh = len(dhs)

    def body(*refs):
        dh_refs = refs[:n_dh]
        x_ref, g_ref, dn_ref, dx_ref, dxb_ref, dg_ref = refs[n_dh:]
        i = pl.program_id(0)
        xhat, r = _rms(x_ref[...])
        dh_ = dh_refs[0][...]
        for extra in dh_refs[1:]:
            dh_ = dh_ + extra[...]
        dxhat = dh_ * g_ref[...]
        dx = dn_ref[...] + r * (dxhat - xhat * jnp.mean(dxhat * xhat, axis=1, keepdims=True))
        dx_ref[...] = dx
        dxb_ref[...] = dx.astype(BF16)

        @pl.when(i == 0)
        def _():
            dg_ref[...] = jnp.zeros_like(dg_ref)

        dg_ref[...] += jnp.sum(dh_ * xhat, axis=0, keepdims=True)

    row = lambda i: (i, 0)
    fixed = lambda i: (0, 0)
    return pl.pallas_call(
        body, name=name,
        out_shape=(jax.ShapeDtypeStruct((s, d), F32), jax.ShapeDtypeStruct((s, d), BF16),
                   jax.ShapeDtypeStruct((1, d), F32)),
        grid=(s // ts,),
        in_specs=[pl.BlockSpec((ts, d), row)] * n_dh + [pl.BlockSpec((ts, d), row), pl.BlockSpec((1, d), fixed),
                                                        pl.BlockSpec((ts, d), row)],
        out_specs=(pl.BlockSpec((ts, d), row), pl.BlockSpec((ts, d), row), pl.BlockSpec((1, d), fixed)),
        compiler_params=_params("arbitrary"),
    )(*dhs, x, g, dx_next)


def _pool_counts(t0, rows, cols, window):
    t = t0 + lax.broadcasted_iota(jnp.int32, (rows, cols), 0)
    return jnp.minimum(t + 1, window).astype(F32)


def _pool_fwd(u, name, ts=1024, tc=256):
    s, e = u.shape
    ng = len(POOL_WINDOWS)
    gdim = e // ng
    ts, tc = _pick(s, ts), _pick(gdim, tc)
    cpg = gdim // tc
    hb = ts // POOL_HALO

    def body(u_ref, halo_ref, d_ref):
        i, grp = pl.program_id(0), pl.program_id(1)
        cur = u_ref[...]
        halo = jnp.where(i > 0, halo_ref[...], 0.0)
        ext = jnp.concatenate([halo, cur], axis=0)
        for gi, window in enumerate(POOL_WINDOWS):
            @pl.when(grp == gi)
            def _(window=window):
                acc = ext
                k = 1
                while k < window:
                    acc = acc + pltpu.roll(acc, k, 0)
                    k *= 2
                pooled = acc[POOL_HALO:, :] / _pool_counts(i * ts, ts, tc, window)
                d_ref[...] = (pooled - cur).astype(BF16)

    return pl.pallas_call(
        body, name=name,
        out_shape=jax.ShapeDtypeStruct((s, e), BF16),
        grid=(s // ts, ng, cpg),
        in_specs=[pl.BlockSpec((ts, tc), lambda i, g, j: (i, g * cpg + j)),
                  pl.BlockSpec((POOL_HALO, tc), lambda i, g, j: (jnp.maximum(i * hb - 1, 0), g * cpg + j))],
        out_specs=pl.BlockSpec((ts, tc), lambda i, g, j: (i, g * cpg + j)),
        compiler_params=_params("parallel", "parallel", "parallel"),
    )(u, u)


def _pool_bwd(dd, name, ts=1024, tc=256):
    s, e = dd.shape
    ng = len(POOL_WINDOWS)
    gdim = e // ng
    ts, tc = _pick(s, ts), _pick(gdim, tc)
    cpg = gdim // tc
    hb = ts // POOL_HALO
    n_halo = s // POOL_HALO
    nst = s // ts

    def body(dd_ref, halo_ref, du_ref):
        i, grp = pl.program_id(0), pl.program_id(1)
        cur = dd_ref[...]
        halo = jnp.where(i < nst - 1, halo_ref[...], 0.0)
        ext = jnp.concatenate([cur, halo], axis=0)
        rows = ts + POOL_HALO
        for gi, window in enumerate(POOL_WINDOWS):
            @pl.when(grp == gi)
            def _(window=window):
                acc = ext / _pool_counts(i * ts, rows, tc, window)
                k = 1
                while k < window:
                    acc = acc + pltpu.roll(acc, rows - k, 0)
                    k *= 2
                du_ref[...] = (acc[:ts, :] - cur).astype(BF16)

    return pl.pallas_call(
        body, name=name,
        out_shape=jax.ShapeDtypeStruct((s, e), BF16),
        grid=(nst, ng, cpg),
        in_specs=[pl.BlockSpec((ts, tc), lambda i, g, j: (i, g * cpg + j)),
                  pl.BlockSpec((POOL_HALO, tc),
                               lambda i, g, j: (jnp.minimum((i + 1) * hb, n_halo - 1), g * cpg + j))],
        out_specs=pl.BlockSpec((ts, tc), lambda i, g, j: (i, g * cpg + j)),
        compiler_params=_params("parallel", "parallel", "parallel"),
    )(dd, dd)


def _a_group_fwd(d, w, scale, gate, name, tm=1024):
    s, e = d.shape
    ng, g, _ = w.shape
    tm = _pick(s, tm)

    def body(d_ref, w_ref, s_ref, gate_ref, yr_ref, z_ref):
        yr = jnp.dot(d_ref[...], w_ref[...], preferred_element_type=F32)
        yr_ref[...] = yr
        gt = gate_ref[...]
        z_ref[...] = ((yr * s_ref[...]) * (gt * _sigmoid(gt))).astype(BF16)

    blk = lambda i, j: (i, j)
    return pl.pallas_call(
        body, name=name,
        out_shape=(jax.ShapeDtypeStruct((s, e), F32), jax.ShapeDtypeStruct((s, e), BF16)),
        grid=(s // tm, ng),
        in_specs=[pl.BlockSpec((tm, g), blk), pl.BlockSpec((None, g, g), lambda i, j: (j, 0, 0)),
                  pl.BlockSpec((1, g), lambda i, j: (0, j)), pl.BlockSpec((tm, g), blk)],
        out_specs=(pl.BlockSpec((tm, g), blk), pl.BlockSpec((tm, g), blk)),
        compiler_params=_params("parallel", "parallel"),
    )(d, w, scale, gate)


def _a_gate_bwd(dz, yr, gate, scale, name, ts=512, tc=512):
    s, e = dz.shape
    ts, tc = _pick(s, ts), _pick(e, tc)

    def body(dz_ref, yr_ref, gate_ref, s_ref, dgate_ref, dyr_ref, dscale_ref):
        i = pl.program_id(1)
        dz_, yr_, gt, sc = dz_ref[...], yr_ref[...], gate_ref[...], s_ref[...]
        sg = _sigmoid(gt)
        dy = dz_ * (gt * sg)
        dgate_ref[...] = (dz_ * (yr_ * sc) * (sg * (1.0 + gt * (1.0 - sg)))).astype(BF16)
        dyr_ref[...] = (dy * sc).astype(BF16)

        @pl.when(i == 0)
        def _():
            dscale_ref[...] = jnp.zeros_like(dscale_ref)

        dscale_ref[...] += jnp.sum(dy * yr_, axis=0, keepdims=True)

    blk = lambda j, i: (i, j)
    vec = lambda j, i: (0, j)
    return pl.pallas_call(
        body, name=name,
        out_shape=(jax.ShapeDtypeStruct((s, e), BF16), jax.ShapeDtypeStruct((s, e), BF16),
                   jax.ShapeDtypeStruct((1, e), F32)),
        grid=(e // tc, s // ts),
        in_specs=[pl.BlockSpec((ts, tc), blk), pl.BlockSpec((ts, tc), blk), pl.BlockSpec((ts, tc), blk),
                  pl.BlockSpec((1, tc), vec)],
        out_specs=(pl.BlockSpec((ts, tc), blk), pl.BlockSpec((ts, tc), blk), pl.BlockSpec((1, tc), vec)),
        compiler_params=_params("parallel", "arbitrary"),
    )(dz, yr, gate, scale)


def _gate_fwd(y, gate, name, ts=512, tc=512):
    s, e = y.shape
    ts, tc = _pick(s, ts), _pick(e, tc)

    def body(y_ref, gate_ref, z_ref):
        gt = gate_ref[...]
        z_ref[...] = (y_ref[...] * (gt * _sigmoid(gt))).astype(BF16)

    blk = lambda i, j: (i, j)
    return pl.pallas_call(
        body, name=name,
        out_shape=jax.ShapeDtypeStruct((s, e), BF16),
        grid=(s // ts, e // tc),
        in_specs=[pl.BlockSpec((ts, tc), blk)] * 2,
        out_specs=pl.BlockSpec((ts, tc), blk),
        compiler_params=_params("parallel", "parallel"),
    )(y, gate)


def _gate_bwd(dz, y, gate, name, ts=512, tc=512):
    s, e = dz.shape
    ts, tc = _pick(s, ts), _pick(e, tc)

    def body(dz_ref, y_ref, gate_ref, dgate_ref, dy_ref):
        dz_, gt = dz_ref[...], gate_ref[...]
        sg = _sigmoid(gt)
        dgate_ref[...] = (dz_ * y_ref[...] * (sg * (1.0 + gt * (1.0 - sg)))).astype(BF16)
        dy_ref[...] = (dz_ * (gt * sg)).astype(BF16)

    blk = lambda i, j: (i, j)
    return pl.pallas_call(
        body, name=name,
        out_shape=(jax.ShapeDtypeStruct((s, e), BF16), jax.ShapeDtypeStruct((s, e), BF16)),
        grid=(s // ts, e // tc),
        in_specs=[pl.BlockSpec((ts, tc), blk)] * 3,
        out_specs=(pl.BlockSpec((ts, tc), blk), pl.BlockSpec((ts, tc), blk)),
        compiler_params=_params("parallel", "parallel"),
    )(dz, y, gate)


def _merge_weights(l0, l1, l2):
    m = jnp.maximum(jnp.maximum(l0, l1), l2)
    e0, e1, e2 = jnp.exp(l0 - m), jnp.exp(l1 - m), jnp.exp(l2 - m)
    inv = 1.0 / (e0 + e1 + e2)
    return e0 * inv, e1 * inv, e2 * inv


def _merge_gate_fwd(outs, lses, gate, name, ts=512, tc=512):
    s, e = gate.shape
    ts, tc = _pick(s, ts), _pick(e, tc)

    def body(o0, o1, o2, l0, l1, l2, gate_ref, y_ref, z_ref):
        w0, w1, w2 = _merge_weights(l0[...], l1[...], l2[...])
        y = w0 * o0[...].astype(F32) + w1 * o1[...].astype(F32) + w2 * o2[...].astype(F32)
        y_ref[...] = y
        gt = gate_ref[...]
        z_ref[...] = (y * (gt * _sigmoid(gt))).astype(BF16)

    blk = lambda i, j: (i, j)
    return pl.pallas_call(
        body, name=name,
        out_shape=(jax.ShapeDtypeStruct((s, e), F32), jax.ShapeDtypeStruct((s, e), BF16)),
        grid=(s // ts, e // tc),
        in_specs=[pl.BlockSpec((ts, tc), blk)] * 7,
        out_specs=(pl.BlockSpec((ts, tc), blk), pl.BlockSpec((ts, tc), blk)),
        compiler_params=_params("parallel", "parallel"),
    )(*outs, *lses, gate)


def _merge_gate_bwd(dz, y, gate, lses, name, ts=512, tc=512):
    s, e = dz.shape
    ts, tc = _pick(s, ts), _pick(e, tc)

    def body(dz_ref, y_ref, gate_ref, l0, l1, l2, dgate_ref, d0, d1, d2):
        dz_, gt = dz_ref[...], gate_ref[...]
        sg = _sigmoid(gt)
        dgate_ref[...] = (dz_ * y_ref[...] * (sg * (1.0 + gt * (1.0 - sg)))).astype(BF16)
        dy = dz_ * (gt * sg)
        w0, w1, w2 = _merge_weights(l0[...], l1[...], l2[...])
        d0[...] = (w0 * dy).astype(BF16)
        d1[...] = (w1 * dy).astype(BF16)
        d2[...] = (w2 * dy).astype(BF16)

    blk = lambda i, j: (i, j)
    return pl.pallas_call(
        body, name=name,
        out_shape=tuple([jax.ShapeDtypeStruct((s, e), BF16)] * 4),
        grid=(s // ts, e // tc),
        in_specs=[pl.BlockSpec((ts, tc), blk)] * 6,
        out_specs=tuple([pl.BlockSpec((ts, tc), blk)] * 4),
        compiler_params=_params("parallel", "parallel"),
    )(dz, y, gate, *lses)


def _band(max_dist, width):
    row = lax.broadcasted_iota(jnp.int32, (2 * BLOCK, width), 0) & (BLOCK - 1)
    col = lax.broadcasted_iota(jnp.int32, (2 * BLOCK, width), 1)
    low = row if max_dist == BLOCK else row + 1
    return jnp.logical_and(col >= low, col <= row + BLOCK), col >= BLOCK


def _fill_bias(bias_ref, max_dist):
    band, own = _band(max_dist, 2 * BLOCK)
    bias_ref[0] = jnp.where(band, 0.0, NEG)
    bias_ref[1] = jnp.where(jnp.logical_and(band, own), 0.0, NEG)


def _aligned(v):
    return v if isinstance(v, int) else pl.multiple_of(v, BLOCK)


def _stack_heads(x, lo):
    return jnp.concatenate([jnp.where(lo, x, 0.0), jnp.where(lo, 0.0, x)], axis=0).astype(BF16)


def _unstack_heads(x2, lo):
    return jnp.where(lo, x2[:BLOCK], x2[BLOCK:])


def _head_col(x, hm):
    return jnp.max(jnp.where(hm, x, NEG), axis=1, keepdims=True)


def _dot_nt(a, b):
    return lax.dot_general(a, b, (((1,), (1,)), ((), ())), preferred_element_type=F32)


def _dot_tn(a, b):
    return lax.dot_general(a, b, (((0,), (0,)), ((), ())), preferred_element_type=F32)


def _stream_view(a, dil):
    s, w = a.shape
    return a.reshape(s // (BLOCK * dil), dil, BLOCK, w)


def _fill_window(dst, halo_ref, cur_ref, n):
    dst[0:BLOCK, :] = halo_ref[0]
    for jc in range(n):
        dst[(jc + 1) * BLOCK:(jc + 2) * BLOCK, :] = cur_ref[jc]


def _attn_fwd(q, k, v, sinks, max_dist, rep, dil, out_dtype, name, tq=2048):
    assert max_dist in (BLOCK - 1, BLOCK)
    s, w = q.shape
    l = s // dil
    n_pairs = w // LANES
    tq = _pick(l, tq)
    n = tq // BLOCK
    has_sink = sinks is not None
    scale = HEAD_DIM ** -0.5

    def body(*refs):
        if has_sink:
            sink_ref, refs = refs[0], refs[1:]
        q_ref, kc_ref, kh_ref, vc_ref, vh_ref, o_ref, lse_ref, kx, vx, bias_ref = refs
        i, p = pl.program_id(0), pl.program_id(2)
        _fill_window(kx, kh_ref, kc_ref, n)
        _fill_window(vx, vh_ref, vc_ref, n)
        lo = lax.broadcasted_iota(jnp.int32, (BLOCK, LANES), 1) < HEAD_DIM
        _fill_bias(bias_ref, max_dist)
        top = lax.broadcasted_iota(jnp.int32, (2 * BLOCK, 1), 0) < BLOCK

        def scores(j):
            r0 = _aligned(j * BLOCK)
            q2 = _stack_heads(q_ref[j].astype(F32) * scale, lo)
            first = jnp.logical_and(i == 0, j == 0).astype(jnp.int32)
            return _dot_nt(q2, kx[pl.ds(r0, 2 * BLOCK), :]) + bias_ref[first]

        per_step = 2 if n % 2 == 0 else 1

        def step(jj, carry):
            nxt = tuple(scores(jnp.minimum((jj + 1) * per_step + t, n - 1)) for t in range(per_step))
            for t in range(per_step):
                finish(jj * per_step + t, carry[t])
            return nxt

        def finish(j, s2):
            r0 = _aligned(j * BLOCK)
            vw = vx[pl.ds(r0, 2 * BLOCK), :]
            m = jnp.max(s2, axis=1, keepdims=True)
            if has_sink:
                sk = jnp.where(top, sink_ref[2 * p], sink_ref[2 * p + 1])
                m = jnp.maximum(m, sk)
            pr = jnp.exp(s2 - m)
            den = jnp.sum(pr, axis=1, keepdims=True)
            if has_sink:
                den = den + jnp.exp(sk - m)
            o2 = jnp.dot(pr.astype(BF16), vw, preferred_element_type=F32) * (1.0 / den)
            lse2 = m + jnp.log(den)
            o_ref[j] = _unstack_heads(o2, lo).astype(o_ref.dtype)
            lse_ref[j] = _unstack_heads(lse2, lo)

        lax.fori_loop(0, n // per_step, step, tuple(scores(t) for t in range(per_step)))

    cur = lambda i, r, p: (i, r, 0, p)
    kv_cur = lambda i, r, p: (i, r, 0, p // rep)
    kv_halo = lambda i, r, p: (jnp.maximum(i * n - 1, 0), r, 0, p // rep)
    big, small = (n, None, BLOCK, LANES), (1, None, BLOCK, LANES)
    in_specs = [pl.BlockSpec(big, cur), pl.BlockSpec(big, kv_cur), pl.BlockSpec(small, kv_halo),
                pl.BlockSpec(big, kv_cur), pl.BlockSpec(small, kv_halo)]
    q4, k4, v4 = _stream_view(q, dil), _stream_view(k, dil), _stream_view(v, dil)
    args = [q4, k4, k4, v4, v4]
    if has_sink:
        in_specs = [pl.BlockSpec(memory_space=pltpu.SMEM)] + in_specs
        args = [sinks] + args
    o4, lse4 = pl.pallas_call(
        body, name=name,
        out_shape=(jax.ShapeDtypeStruct(q4.shape, out_dtype), jax.ShapeDtypeStruct(q4.shape, F32)),
        grid=(l // tq, dil, n_pairs),
        in_specs=in_specs,
        out_specs=(pl.BlockSpec(big, cur), pl.BlockSpec(big, cur)),
        scratch_shapes=[pltpu.VMEM((tq + BLOCK, LANES), BF16), pltpu.VMEM((tq + BLOCK, LANES), BF16),
                        pltpu.VMEM((2, 2 * BLOCK, 2 * BLOCK), F32)],
        compiler_params=_params("parallel", "parallel", "parallel"),
    )(*args)
    return o4.reshape(s, w), lse4.reshape(s, w)


def _attn_bwd(q, k, v, do, y, lse, sinks, max_dist, rep, dil, name, tq=2048):
    s, w = q.shape
    l = s // dil
    n_pairs = w // LANES
    tq = _pick(l, tq)
    n = tq // BLOCK
    n_blk = l // BLOCK
    n_sb = l // tq
    has_sink = sinks is not None
    scale = HEAD_DIM ** -0.5
    kv_dtype = F32 if rep > 1 else BF16
    ext = tq + BLOCK

    def body(*refs):
        if has_sink:
            sink_ref, refs = refs[0], refs[1:]
        (q_ref, qn_ref, kc_ref, kh_ref, vc_ref, vh_ref, do_ref, don_ref, y_ref, yn_ref,
         lse_ref, lsen_ref) = refs[:12]
        refs = refs[12:]
        dq_ref, dk_ref, dv_ref = refs[:3]
        refs = refs[3:]
        if has_sink:
            dsink_ref, refs = refs[0], refs[1:]
        kx, vx, dkx, dvx, bias_ref = refs
        i, p = pl.program_id(0), pl.program_id(2)
        _fill_bias(bias_ref, max_dist)
        own_rows = (q_ref, do_ref, y_ref, lse_ref)
        next_rows = (qn_ref, don_ref, yn_ref, lsen_ref)
        _fill_window(kx, kh_ref, kc_ref, n)
        _fill_window(vx, vh_ref, vc_ref, n)
        dkx[...] = jnp.zeros_like(dkx)
        dvx[...] = jnp.zeros_like(dvx)
        lo = lax.broadcasted_iota(jnp.int32, (BLOCK, LANES), 1) < HEAD_DIM
        hi = jnp.logical_not(lo)
        top = lax.broadcasted_iota(jnp.int32, (2 * BLOCK, 1), 0) < BLOCK

        def rows_of(j):
            if isinstance(j, int) and j == n:
                return next_rows, 0
            return own_rows, j

        def front(j, width):
            (qr, dor, _, _), jb = rows_of(j)
            r0 = _aligned(j * BLOCK)
            first = jnp.logical_and(i == 0, j == 0).astype(jnp.int32)
            q2 = _stack_heads(qr[jb].astype(F32) * scale, lo)
            do2 = _stack_heads(dor[jb].astype(F32), lo)
            s2 = _dot_nt(q2, kx[pl.ds(r0, width), :]) + bias_ref[first, :, pl.ds(0, width)]
            return s2, _dot_nt(do2, vx[pl.ds(r0, width), :])

        def back(j, width, q_valid, s2, dp2, sink_acc):
            (qr, dor, yr, lser), jb = rows_of(j)
            r0 = _aligned(j * BLOCK)
            dof = dor[jb].astype(F32)
            yb, lseb = yr[jb].astype(F32), lser[jb]
            q2 = _stack_heads(qr[jb].astype(F32) * scale, lo)
            do2 = _stack_heads(dof, lo)
            prod = dof * yb
            delta = jnp.concatenate([jnp.sum(jnp.where(lo, prod, 0.0), axis=1, keepdims=True),
                                     jnp.sum(jnp.where(lo, 0.0, prod), axis=1, keepdims=True)], axis=0)
            lse2 = jnp.concatenate([_head_col(lseb, lo), _head_col(lseb, hi)], axis=0)
            pr = jnp.exp(s2 - lse2)
            if q_valid is not True:
                pr = jnp.where(q_valid, pr, 0.0)
            ds = pr * (dp2 - delta)
            dkx[pl.ds(r0, width), :] += _dot_tn(ds.astype(BF16), q2)
            dvx[pl.ds(r0, width), :] += _dot_tn(pr.astype(BF16), do2)
            if width == 2 * BLOCK:
                dq2 = jnp.dot(ds.astype(BF16), kx[pl.ds(r0, width), :], preferred_element_type=F32) * scale
                dq_ref[jb] = _unstack_heads(dq2, lo).astype(dq_ref.dtype)
            if has_sink:
                sk = jnp.where(top, sink_ref[2 * p], sink_ref[2 * p + 1])
                sink_acc = sink_acc - jnp.exp(sk - lse2) * delta
            return sink_acc

        per_step = 2 if n % 2 == 0 else 1

        def step(jj, sink_acc):
            fronts = [front(jj * per_step + t, 2 * BLOCK) for t in range(per_step)]
            for t in range(per_step):
                sink_acc = back(jj * per_step + t, 2 * BLOCK, True, *fronts[t], sink_acc)
            return sink_acc

        zero_col = jnp.zeros((2 * BLOCK, 1), F32)
        sink_acc = lax.fori_loop(0, n // per_step, step, zero_col)
        if n_sb > 1:
            back(n, BLOCK, i < n_sb - 1, *front(n, BLOCK), zero_col)

        def write_out(accumulate):
            for jc in range(n):
                rows = slice((jc + 1) * BLOCK, (jc + 2) * BLOCK)
                if accumulate:
                    dk_ref[jc] += dkx[rows, :]
                    dv_ref[jc] += dvx[rows, :]
                else:
                    dk_ref[jc] = dkx[rows, :].astype(dk_ref.dtype)
                    dv_ref[jc] = dvx[rows, :].astype(dv_ref.dtype)

        if rep == 1:
            write_out(False)
        else:
            pl.when(p % rep == 0)(lambda: write_out(False))
            pl.when(p % rep != 0)(lambda: write_out(True))
        if has_sink:
            rowi = lax.broadcasted_iota(jnp.int32, (8, LANES), 0)
            s0 = jnp.sum(sink_acc[:BLOCK], axis=0, keepdims=True)
            s1 = jnp.sum(sink_acc[BLOCK:], axis=0, keepdims=True)
            dsink_ref[...] = jnp.where(rowi == 0, s0, jnp.where(rowi == 1, s1, 0.0))

    cur = lambda i, r, p: (i, r, 0, p)
    nxt = lambda i, r, p: (jnp.minimum((i + 1) * n, n_blk - 1), r, 0, p)
    kv_cur = lambda i, r, p: (i, r, 0, p // rep)
    kv_halo = lambda i, r, p: (jnp.maximum(i * n - 1, 0), r, 0, p // rep)
    big, small = (n, None, BLOCK, LANES), (1, None, BLOCK, LANES)
    in_specs = [pl.BlockSpec(big, cur), pl.BlockSpec(small, nxt),
                pl.BlockSpec(big, kv_cur), pl.BlockSpec(small, kv_halo),
                pl.BlockSpec(big, kv_cur), pl.BlockSpec(small, kv_halo),
                pl.BlockSpec(big, cur), pl.BlockSpec(small, nxt),
                pl.BlockSpec(big, cur), pl.BlockSpec(small, nxt),
                pl.BlockSpec(big, cur), pl.BlockSpec(small, nxt)]
    q4, k4, v4, do4, y4, lse4 = [_stream_view(a, dil) for a in (q, k, v, do, y, lse)]
    args = [q4, q4, k4, k4, v4, v4, do4, do4, y4, y4, lse4, lse4]
    out_shape = [jax.ShapeDtypeStruct(q4.shape, BF16),
                 jax.ShapeDtypeStruct(k4.shape, kv_dtype), jax.ShapeDtypeStruct(v4.shape, kv_dtype)]
    out_specs = [pl.BlockSpec(big, cur), pl.BlockSpec(big, kv_cur), pl.BlockSpec(big, kv_cur)]
    if has_sink:
        in_specs = [pl.BlockSpec(memory_space=pltpu.SMEM)] + in_specs
        args = [sinks] + args
        out_shape.append(jax.ShapeDtypeStruct((n_sb, dil, n_pairs, 8, LANES), F32))
        out_specs.append(pl.BlockSpec((None, None, None, 8, LANES), lambda i, r, p: (i, r, p, 0, 0)))
    outs = pl.pallas_call(
        body, name=name,
        out_shape=tuple(out_shape),
        grid=(n_sb, dil, n_pairs),
        in_specs=in_specs,
        out_specs=tuple(out_specs),
        scratch_shapes=[pltpu.VMEM((ext, LANES), BF16), pltpu.VMEM((ext, LANES), BF16),
                        pltpu.VMEM((ext, LANES), F32), pltpu.VMEM((ext, LANES), F32),
                        pltpu.VMEM((2, 2 * BLOCK, 2 * BLOCK), F32)],
        compiler_params=_params("parallel", "parallel", "arbitrary"),
    )(*args)
    grads = [outs[0].reshape(s, w), outs[1].reshape(k.shape), outs[2].reshape(v.shape)]
    if has_sink:
        grads.append(outs[3].sum(axis=(0, 1))[:, 0:2, 0].reshape(1, 2 * n_pairs))
    return grads


def _sum_slots(recv, name, ts=256):
    nd, r, c = recv.shape
    ts = _pick(r, ts, 8)

    def body(r_ref, o_ref):
        acc = r_ref[0].astype(F32)
        for dev in range(1, nd):
            acc = acc + r_ref[dev].astype(F32)
        o_ref[...] = acc

    return pl.pallas_call(
        body, name=name,
        out_shape=jax.ShapeDtypeStruct((r, c), F32),
        grid=(r // ts,),
        in_specs=[pl.BlockSpec((nd, ts, c), lambda i: (0, i, 0))],
        out_specs=pl.BlockSpec((ts, c), lambda i: (i, 0)),
        compiler_params=_params("parallel"),
    )(recv)


def _adamw(w, g, m, v, name, ts=256):
    r, c = w.shape
    ts = _pick(r, ts, 8)
    c1 = 1.0 - ADAM_B1 ** ADAM_STEP
    c2 = 1.0 - ADAM_B2 ** ADAM_STEP

    def body(w_ref, g_ref, m_ref, v_ref, d_ref, mo_ref, vo_ref):
        g_ = g_ref[...]
        m_ = ADAM_B1 * m_ref[...] + (1.0 - ADAM_B1) * g_
        v_ = ADAM_B2 * v_ref[...] + (1.0 - ADAM_B2) * (g_ * g_)
        mo_ref[...] = m_
        vo_ref[...] = v_
        d_ref[...] = -ADAM_LR * ((m_ / c1) / (jnp.sqrt(v_ / c2) + ADAM_EPS) + ADAM_WD * w_ref[...])

    blk = pl.BlockSpec((ts, c), lambda i: (i, 0))
    return pl.pallas_call(
        body, name=name,
        out_shape=tuple([jax.ShapeDtypeStruct((r, c), F32)] * 3),
        grid=(r // ts,),
        in_specs=[blk] * 4,
        out_specs=(blk, blk, blk),
        compiler_params=_params("parallel"),
    )(w, g, m, v)


def _rows(a):
    flat = a.reshape(-1)
    pad = (-flat.shape[0]) % PACK_W
    if pad:
        flat = jnp.concatenate([flat, jnp.zeros((pad,), flat.dtype)])
    return flat.reshape(-1, PACK_W)


def _pad_rows(a, mult):
    pad = (-a.shape[-2]) % mult
    if pad:
        widths = [(0, 0)] * (a.ndim - 2) + [(0, pad), (0, 0)]
        a = jnp.pad(a, widths)
    return a


def _to_global(stack, axis):
    moved = jnp.moveaxis(stack, 0, axis)
    shp = list(moved.shape)
    shp[axis:axis + 2] = [shp[axis] * shp[axis + 1]]
    return moved.reshape(shp)


def _to_stack(full, axis):
    shp = list(full.shape)
    shp[axis:axis + 1] = [N_DEV, shp[axis] // N_DEV]
    return jnp.moveaxis(full.reshape(shp), axis, 0)


_BIG = (("w_out", 1), ("a_w_in", 2), ("a_w_group", 2), ("b_w_in", 2), ("c_w_in", 2))


def _dup_heads(wk, n_kv):
    d = wk.shape[0]
    return jnp.tile(wk.reshape(d, n_kv, 1, HEAD_DIM), (1, 1, 2, 1)).reshape(d, n_kv * LANES)


def _fold_heads(dwk, n_kv):
    d = dwk.shape[0]
    return dwk.reshape(d, n_kv, 2, HEAD_DIM).sum(axis=2).reshape(d, n_kv * HEAD_DIM)


def _perm(a, dil):
    if dil == 1:
        return a
    s, w = a.shape
    return a.reshape(s // (BLOCK * dil), BLOCK, dil, w).transpose(0, 2, 1, 3).reshape(s, w)


def _unperm(a, dil):
    if dil == 1:
        return a
    s, w = a.shape
    return a.reshape(s // (BLOCK * dil), dil, BLOCK, w).transpose(0, 2, 1, 3).reshape(s, w)


def kernel(x, norm_g, final_g, w_out, a_w_in, a_w_group, a_scale, b_w_in, b_sinks, c_w_in, loss_target, m_norm_g, m_final_g, m_w_out, m_a_w_in, m_a_w_group, m_a_scale, m_b_w_in, m_b_sinks, m_c_w_in, v_norm_g, v_final_g, v_w_out, v_a_w_in, v_a_w_group, v_a_scale, v_b_w_in, v_b_sinks, v_c_w_in):
    local = dict(w_out=w_out, a_w_in=a_w_in, a_w_group=a_w_group, b_w_in=b_w_in, c_w_in=c_w_in)
    mom_m = dict(w_out=m_w_out, a_w_in=m_a_w_in, a_w_group=m_a_w_group, b_w_in=m_b_w_in, c_w_in=m_c_w_in)
    mom_v = dict(w_out=v_w_out, a_w_in=v_a_w_in, a_w_group=v_a_w_group, b_w_in=v_b_w_in, c_w_in=v_c_w_in)
    s, d = x.shape[1], x.shape[2]
    depth = norm_g.shape[0]
    e = w_out.shape[1] * N_DEV
    n_heads = e // HEAD_DIM
    n_kv = n_heads // Q_PER_KV
    kv_w = n_kv * HEAD_DIM
    rep = Q_PER_KV // 2
    n_groups = len(POOL_WINDOWS)
    me = 4 * lax.axis_index("x") + 2 * lax.axis_index("y") + lax.axis_index("c")

    sizes = [local[n].size // PACK_W for n, _ in _BIG]
    offs = [sum(sizes[:k]) for k in range(len(sizes) + 1)]
    wpack = _pad_rows(jnp.concatenate([_rows(local[n].astype(BF16)) for n, _ in _BIG], axis=0), 16)
    r_pack = wpack.shape[0]
    spack = _pad_rows(_rows(a_scale), 8)
    wall, sall = _gather([wpack, spack], "gather_weights")
    full = {}
    for k, (name, axis) in enumerate(_BIG):
        stack = wall[:, offs[k]:offs[k + 1], :].reshape((N_DEV,) + local[name].shape)
        full[name] = _to_global(stack, axis)
    scale_full = _to_global(sall.reshape(N_DEV, -1)[:, :a_scale.size].reshape((N_DEV,) + a_scale.shape), 1)

    wout_t = jnp.swapaxes(full["w_out"], 1, 2)
    wa = full["a_w_in"]
    wa_t = jnp.swapaxes(wa, 1, 2)
    wg = full["a_w_group"]
    wg_t = jnp.swapaxes(wg, 2, 3)
    wb = full["b_w_in"][0]
    wb_ext = jnp.concatenate([wb[:, :e], _dup_heads(wb[:, e:e + kv_w], n_kv),
                              _dup_heads(wb[:, e + kv_w:e + 2 * kv_w], n_kv), wb[:, e + 2 * kv_w:]], axis=1)
    wb_ext_t = wb_ext.T
    kd_w = n_kv * LANES
    wc = full["c_w_in"][0]
    wc_t = wc.T

    xs, hs, zs, saved = [x.reshape(s, d)], [], [], []
    hs.append(_rmsnorm_fwd(xs[0], norm_g[0:1], "norm0"))
    loss_vec = dfinal = dx = dxb = None
    for i in range(depth):
        kind, j = i % 3, i // 3
        h = hs[i]
        tag = f"l{i}"
        if kind == 0:
            u = _matmul(h, wa[j][:, :e], F32, tag + "_in_u")
            gate = _matmul(h, wa[j][:, e:], F32, tag + "_in_gate")
            dpool = _pool_fwd(u, tag + "_pool")
            yr, z = _a_group_fwd(dpool, wg[j], scale_full[j:j + 1], gate, tag + "_group")
            saved.append(dict(dpool=dpool, yr=yr, gate=gate))
        elif kind == 1:
            q = _matmul(h, wb_ext[:, :e], BF16, tag + "_in_q")
            kd = _matmul(h, wb_ext[:, e:e + kd_w], BF16, tag + "_in_k")
            vd = _matmul(h, wb_ext[:, e + kd_w:e + 2 * kd_w], BF16, tag + "_in_v")
            gate = _matmul(h, wb_ext[:, e + 2 * kd_w:], F32, tag + "_in_gate")
            sinks = b_sinks[j]
            y, lse = _attn_fwd(q, kd, vd, sinks, SWA_MAX_DIST, rep, 1, F32, tag + "_attn")
            z = _gate_fwd(y, gate, tag + "_gate")
            saved.append(dict(q=q, kd=kd, vd=vd, gate=gate, y=y, lse=lse, sinks=sinks))
        else:
            qkv, outs, lses, h_perm = [], [], [], []
            for gi, (window, dil) in enumerate(DILATED_PAIRS):
                hp = _perm(h, dil)
                trio = [_matmul(hp, wc[:, (3 * gi + t) * e:(3 * gi + t + 1) * e], BF16,
                                f"{tag}_in_{'qkv'[t]}{gi}") for t in range(3)]
                o, lse = _attn_fwd(trio[0], trio[1], trio[2], None, window // dil, 1, dil, BF16,
                                   f"{tag}_attn{gi}")
                qkv.append(trio)
                h_perm.append(hp)
                outs.append(_unperm(o, dil))
                lses.append(lse)
            gate = _matmul(h, wc[:, 9 * e:], F32, tag + "_in_gate")
            lses_tok = [_unperm(lse, dil) for lse, (_, dil) in zip(lses, DILATED_PAIRS)]
            y, z = _merge_gate_fwd(outs, lses_tok, gate, tag + "_merge")
            saved.append(dict(qkv=qkv, lses=lses, lses_tok=lses_tok, gate=gate, y=y, h_perm=h_perm))
        zs.append(z)
        if i + 1 < depth:
            x_new, h_new = _outproj_norm(z, full["w_out"][i], xs[i], norm_g[i + 1:i + 2], tag + "_out")
            xs.append(x_new)
            hs.append(h_new)
        else:
            dx, dxb, dfinal, loss_vec = _outproj_loss(z, full["w_out"][i], xs[i], final_g.reshape(1, d),
                                                      loss_target.reshape(s, d), tag + "_out_loss")

    g_full = {"w_out": [None] * depth, "a_w_in": [None] * wa.shape[0], "a_w_group": [None] * wa.shape[0]}
    d_norm = [None] * depth
    d_scale = [None] * wa.shape[0]
    d_sinks = None
    for i in reversed(range(depth)):
        kind, j = i % 3, i // 3
        tag = f"b{i}"
        sv = saved[i]
        g_full["w_out"][i] = _matmul_tn(zs[i], dxb, tag + "_dwout")
        dz = _matmul(dxb, wout_t[i], F32, tag + "_dz")
        if kind == 0:
            dgate, dyr, dsc = _a_gate_bwd(dz, sv["yr"], sv["gate"], scale_full[j:j + 1], tag + "_gate")
            d_scale[j] = dsc
            dd = _grouped_matmul(dyr, wg_t[j], tag + "_dd")
            du = _pool_bwd(dd, tag + "_pool")
            g_full["a_w_group"][j] = _grouped_weight_grad(sv["dpool"], dyr, n_groups, tag + "_dwg")
            dp = jnp.concatenate([du, dgate], axis=1)
            g_full["a_w_in"][j] = _matmul_tn(hs[i], dp, tag + "_dwin")
            dhs = [_matmul(dp, wa_t[j], F32, tag + "_dh")]
        elif kind == 1:
            dgate, do = _gate_bwd(dz, sv["y"], sv["gate"], tag + "_gate")
            dq, dkd, dvd, d_sinks = _attn_bwd(sv["q"], sv["kd"], sv["vd"], do, sv["y"], sv["lse"], sv["sinks"],
                                              SWA_MAX_DIST, rep, 1, tag + "_attn")
            dp = jnp.concatenate([dq, dkd.astype(BF16), dvd.astype(BF16), dgate], axis=1)
            dw_in = _matmul_tn(hs[i], dp, tag + "_dwin")
            g_full["b_w_in"] = jnp.concatenate(
                [dw_in[:, :e], _fold_heads(dw_in[:, e:e + kd_w], n_kv),
                 _fold_heads(dw_in[:, e + kd_w:e + 2 * kd_w], n_kv), dw_in[:, e + 2 * kd_w:]], axis=1)[None]
            dhs = [_matmul(dp, wb_ext_t, F32, tag + "_dh")]
        else:
            dgate, *dos = _merge_gate_bwd(dz, sv["y"], sv["gate"], sv["lses_tok"], tag + "_merge")
            y_bf = sv["y"].astype(BF16)
            dws, dhs = [], []
            for gi, (window, dil) in enumerate(DILATED_PAIRS):
                qv, kv, vv = sv["qkv"][gi]
                grads = _attn_bwd(qv, kv, vv, _perm(dos[gi], dil), _perm(y_bf, dil), sv["lses"][gi], None,
                                  window // dil, 1, dil, f"{tag}_attn{gi}")
                dp = jnp.concatenate(grads, axis=1)
                dws.append(_matmul_tn(sv["h_perm"][gi], dp, f"{tag}_dwin{gi}"))
                dhs.append(_unperm(_matmul(dp, wc_t[3 * gi * e:3 * (gi + 1) * e], F32, f"{tag}_dh{gi}"), dil))
            dws.append(_matmul_tn(hs[i], dgate, tag + "_dwin_gate"))
            dhs.append(_matmul(dgate, wc_t[9 * e:], F32, tag + "_dh_gate"))
            g_full["c_w_in"] = jnp.concatenate(dws, axis=1)[None]
        dx, dxb, d_norm[i] = _rmsnorm_bwd(dhs, xs[i], norm_g[i:i + 1], dx, tag + "_norm")
    grad_x = dx.reshape(x.shape)
    for name in ("w_out", "a_w_in", "a_w_group"):
        g_full[name] = jnp.stack(g_full[name], axis=0)

    gpack = jnp.concatenate(
        [_to_stack(g_full[n], axis).astype(BF16).reshape(N_DEV, -1, PACK_W) for n, axis in _BIG], axis=1)
    gpack = _pad_rows(gpack, 16)
    loss_local = (0.5 / d) * jnp.sum(loss_vec)
    small = [jnp.concatenate(d_norm, axis=0), dfinal, d_sinks, jnp.concatenate(d_scale, axis=0),
             loss_local.reshape(1, 1)]
    small_rows = [_rows(a) for a in small]
    small_offs = [sum(r.shape[0] for r in small_rows[:k]) for k in range(len(small_rows) + 1)]
    small_pack = _pad_rows(jnp.concatenate(small_rows, axis=0), 8)
    core = lax.axis_index("c").astype(jnp.int32).reshape(1)
    from_sibling = _sibling_exchange(gpack, "exchange_sibling")
    chip_sums = _pair_sum(gpack, from_sibling, core, "sum_pair")
    grecv, srecv = _chip_exchange(chip_sums, small_pack, "exchange_chips")
    gsum = _sum_slots(grecv, "sum_grads")
    ssum = _sum_slots(srecv, "sum_small")

    def small_part(k, like):
        return ssum[small_offs[k]:small_offs[k + 1]].reshape(-1)[:like.size].reshape(like.shape)

    g_norm = small_part(0, norm_g)
    g_final = small_part(1, final_g)
    g_sinks = small_part(2, b_sinks)
    g_scale_full = small_part(3, scale_full)
    loss = ssum[small_offs[4], 0]
    g_scale = lax.dynamic_slice_in_dim(g_scale_full, me * a_scale.shape[1], a_scale.shape[1], axis=1)

    small_w = [("norm_g", norm_g, m_norm_g, v_norm_g, g_norm), ("final_g", final_g, m_final_g, v_final_g, g_final),
               ("a_scale", a_scale, m_a_scale, v_a_scale, g_scale), ("b_sinks", b_sinks, m_b_sinks, v_b_sinks, g_sinks)]
    big_rows = lambda tree: _pad_rows(jnp.concatenate([_rows(tree[n]) for n, _ in _BIG], axis=0), 16)
    tail = lambda idx: [_rows(t[idx]) for t in small_w]
    tail_sizes = [r.shape[0] for r in tail(1)]
    tail_offs = [r_pack + sum(tail_sizes[:k]) for k in range(len(tail_sizes) + 1)]
    w_all = _pad_rows(jnp.concatenate([big_rows(local)] + tail(1), axis=0), ADAM_ROWS)
    m_all = _pad_rows(jnp.concatenate([big_rows(mom_m)] + tail(2), axis=0), ADAM_ROWS)
    v_all = _pad_rows(jnp.concatenate([big_rows(mom_v)] + tail(3), axis=0), ADAM_ROWS)
    g_all = _pad_rows(jnp.concatenate([gsum] + tail(4), axis=0), ADAM_ROWS)
    delta_all, m_new, v_new = _adamw(w_all, g_all, m_all, v_all, "adamw", ts=ADAM_ROWS)

    def unpack(packed):
        out = {}
        for k, (name, _) in enumerate(_BIG):
            out[name] = packed[offs[k]:offs[k + 1]].reshape(local[name].shape)
        for k, (name, w_, _, _, _) in enumerate(small_w):
            out[name] = packed[tail_offs[k]:tail_offs[k + 1]].reshape(-1)[:w_.size].reshape(w_.shape)
        return out

    order = ("norm_g", "final_g", "w_out", "a_w_in", "a_w_group", "a_scale", "b_w_in", "b_sinks", "c_w_in")
    grads = unpack(g_all)
    deltas, new_m, new_v = unpack(delta_all), unpack(m_new), unpack(v_new)
    return (loss, grad_x, *[grads[n] for n in order], *[deltas[n] for n in order],
            *[new_m[n] for n in order], *[new_v[n] for n in order])
```

```python
import functools

import jax
import jax.numpy as jnp
from jax import lax
from jax.experimental import pallas as pl
from jax.experimental.pallas import tpu as pltpu

F32 = jnp.float32
BF16 = jnp.bfloat16

N_DEV = 8
HEAD_DIM = 64
LANES = 128
BLOCK = 128
Q_PER_KV = 8
POOL_WINDOWS = (2, 4, 8, 16)
POOL_HALO = 16
DILATED_PAIRS = ((128, 1), (512, 4), (2048, 16))
SWA_MAX_DIST = 127
RMS_EPS = 1e-5
PACK_W = 1024
ADAM_ROWS = 256
NEG = -1e30

ADAM_LR = 0.001
ADAM_B1 = 0.9
ADAM_B2 = 0.999
ADAM_EPS = 1e-08
ADAM_WD = 0.01
ADAM_STEP = 10

VMEM_LIMIT = 48 * 1024 * 1024


def _params(*sem):
    return pltpu.CompilerParams(dimension_semantics=sem if sem else None, vmem_limit_bytes=VMEM_LIMIT)


def _pick(dim, target, mult=LANES):
    if dim <= target:
        return dim
    t = target - target % mult
    while dim % t:
        t -= mult
    return t


def _sigmoid(x):
    return 1.0 / (1.0 + jnp.exp(-x))


CHIP_OFFSETS = (2, 4, 6)
ANY_SPEC = pl.BlockSpec(memory_space=pl.ANY)


def _where_am_i():
    x, y, c = lax.axis_index("x"), lax.axis_index("y"), lax.axis_index("c")
    return x, y, c, 4 * x + 2 * y + c


def _peer(x, y, c, r):
    return x ^ ((r >> 2) & 1), y ^ ((r >> 1) & 1), c ^ (r & 1)


def _gather(blocks, name):
    n = len(blocks)

    def body(*refs):
        send, recv = refs[:n], refs[n:2 * n]
        send_sems, recv_sems, local_sems = refs[2 * n:]
        x, y, c, me = _where_am_i()
        sib = _peer(x, y, c, 1)
        sib_id = me ^ 1

        def copy(k, slot_sem, src, slot, to):
            return pltpu.make_async_remote_copy(
                src_ref=src, dst_ref=recv[k].at[slot], send_sem=send_sems.at[k, slot_sem],
                recv_sem=recv_sems.at[k, slot_sem], device_id=to, device_id_type=pl.DeviceIdType.MESH)

        started = []
        for k in range(n):
            own = pltpu.make_async_copy(send[k], recv[k].at[me], local_sems.at[k])
            own.start()
            started.append(own)
        sends = []
        for k in range(n):
            sends.append(copy(k, 0, send[k], me, sib))
            for j, r in enumerate(CHIP_OFFSETS):
                sends.append(copy(k, 1 + j, send[k], me, _peer(x, y, c, r)))
        for cp in sends:
            cp.start()
        for j, r in enumerate(CHIP_OFFSETS):
            for k in range(n):
                src_id = me ^ r
                copy(k, 1 + j, send[k], src_id, sib).wait_recv()
                fwd = copy(k, 4 + j, recv[k].at[src_id], src_id, sib)
                fwd.start()
                sends.append(fwd)
        for k in range(n):
            copy(k, 0, send[k], sib_id, sib).wait_recv()
            for j, r in enumerate(CHIP_OFFSETS):
                copy(k, 4 + j, send[k], sib_id ^ r, sib).wait_recv()
        for cp in sends:
            cp.wait_send()
        for own in started:
            own.wait()

    return pl.pallas_call(
        body, name=name,
        out_shape=tuple(jax.ShapeDtypeStruct((N_DEV,) + b.shape, b.dtype) for b in blocks),
        in_specs=[ANY_SPEC] * n,
        out_specs=tuple([ANY_SPEC] * n),
        scratch_shapes=[pltpu.SemaphoreType.DMA((n, N_DEV - 1)), pltpu.SemaphoreType.DMA((n, N_DEV - 1)),
                        pltpu.SemaphoreType.DMA((n,))],
    )(*blocks)


def _sibling_exchange(gpack, name):
    n_chips = N_DEV // 2

    def body(g_ref, t_ref, send_sems, recv_sems):
        x, y, c, _ = _where_am_i()
        sib = _peer(x, y, c, 1)
        copies = [pltpu.make_async_remote_copy(
            src_ref=g_ref.at[2 * chip + (1 - c)], dst_ref=t_ref.at[chip], send_sem=send_sems.at[chip],
            recv_sem=recv_sems.at[chip], device_id=sib, device_id_type=pl.DeviceIdType.MESH)
            for chip in range(n_chips)]
        for cp in copies:
            cp.start()
        for cp in copies:
            cp.wait_recv()
        for cp in copies:
            cp.wait_send()

    return pl.pallas_call(
        body, name=name,
        out_shape=jax.ShapeDtypeStruct((n_chips,) + gpack.shape[1:], gpack.dtype),
        in_specs=[ANY_SPEC], out_specs=ANY_SPEC,
        scratch_shapes=[pltpu.SemaphoreType.DMA((n_chips,)), pltpu.SemaphoreType.DMA((n_chips,))],
    )(gpack)


def _chip_exchange(csum, small, name):
    n_chips = N_DEV // 2

    def body(c_ref, s_ref, r_ref, sr_ref, send_sems, recv_sems, small_send, small_recv, local_sems):
        x, y, c, me = _where_am_i()
        my_chip = 2 * x + y
        own = [pltpu.make_async_copy(c_ref.at[my_chip], r_ref.at[my_chip], local_sems.at[0]),
               pltpu.make_async_copy(s_ref, sr_ref.at[me], local_sems.at[1])]
        for cp in own:
            cp.start()
        sends, recvs = [], []
        for j, r in enumerate(CHIP_OFFSETS):
            to = _peer(x, y, c, r)
            chip = my_chip ^ (r >> 1)
            sends.append(pltpu.make_async_remote_copy(
                src_ref=c_ref.at[chip], dst_ref=r_ref.at[my_chip], send_sem=send_sems.at[j],
                recv_sem=recv_sems.at[j], device_id=to, device_id_type=pl.DeviceIdType.MESH))
            recvs.append(pltpu.make_async_remote_copy(
                src_ref=c_ref.at[chip], dst_ref=r_ref.at[chip], send_sem=send_sems.at[j],
                recv_sem=recv_sems.at[j], device_id=to, device_id_type=pl.DeviceIdType.MESH))
        for r in range(1, N_DEV):
            to = _peer(x, y, c, r)
            sends.append(pltpu.make_async_remote_copy(
                src_ref=s_ref, dst_ref=sr_ref.at[me], send_sem=small_send.at[r - 1],
                recv_sem=small_recv.at[r - 1], device_id=to, device_id_type=pl.DeviceIdType.MESH))
            recvs.append(pltpu.make_async_remote_copy(
                src_ref=s_ref, dst_ref=sr_ref.at[me ^ r], send_sem=small_send.at[r - 1],
                recv_sem=small_recv.at[r - 1], device_id=to, device_id_type=pl.DeviceIdType.MESH))
        for cp in sends:
            cp.start()
        for cp in recvs:
            cp.wait_recv()
        for cp in sends:
            cp.wait_send()
        for cp in own:
            cp.wait()

    n_off = len(CHIP_OFFSETS)
    return pl.pallas_call(
        body, name=name,
        out_shape=(jax.ShapeDtypeStruct(csum.shape, csum.dtype),
                   jax.ShapeDtypeStruct((N_DEV,) + small.shape, small.dtype)),
        in_specs=[ANY_SPEC, ANY_SPEC], out_specs=(ANY_SPEC, ANY_SPEC),
        scratch_shapes=[pltpu.SemaphoreType.DMA((n_off,)), pltpu.SemaphoreType.DMA((n_off,)),
                        pltpu.SemaphoreType.DMA((N_DEV - 1,)), pltpu.SemaphoreType.DMA((N_DEV - 1,)),
                        pltpu.SemaphoreType.DMA((2,))],
    )(csum, small)


def _pair_sum(gpack, other, core, name, ts=256):
    n_chips, r, c = other.shape
    ts = _pick(r, ts, 16)

    def body(core_ref, g_ref, o_ref, out_ref):
        del core_ref
        out_ref[...] = (g_ref[...].astype(F32) + o_ref[...].astype(F32)).astype(out_ref.dtype)

    return pl.pallas_call(
        body, name=name,
        out_shape=jax.ShapeDtypeStruct(other.shape, other.dtype),
        grid_spec=pltpu.PrefetchScalarGridSpec(
            num_scalar_prefetch=1, grid=(n_chips, r // ts),
            in_specs=[pl.BlockSpec((None, ts, c), lambda j, i, core_ref: (2 * j + core_ref[0], i, 0)),
                      pl.BlockSpec((None, ts, c), lambda j, i, core_ref: (j, i, 0))],
            out_specs=pl.BlockSpec((None, ts, c), lambda j, i, core_ref: (j, i, 0))),
        compiler_params=_params("parallel", "parallel"),
    )(core, gpack, other)


def _matmul(a, b, out_dtype, name, tm=1024, tn=1024, tk=1024):
    m, kdim = a.shape
    n = b.shape[1]
    tm, tn, tk = _pick(m, tm), _pick(n, tn), _pick(kdim, tk)
    nk = kdim // tk

    if nk == 1:
        def body(a_ref, b_ref, o_ref):
            o_ref[...] = jnp.dot(a_ref[...], b_ref[...], preferred_element_type=F32).astype(o_ref.dtype)
        scratch = []
    else:
        def body(a_ref, b_ref, o_ref, acc_ref):
            kk = pl.program_id(2)

            @pl.when(kk == 0)
            def _():
                acc_ref[...] = jnp.zeros_like(acc_ref)

            acc_ref[...] += jnp.dot(a_ref[...], b_ref[...], preferred_element_type=F32)

            @pl.when(kk == nk - 1)
            def _():
                o_ref[...] = acc_ref[...].astype(o_ref.dtype)
        scratch = [pltpu.VMEM((tm, tn), F32)]

    return pl.pallas_call(
        body, name=name,
        out_shape=jax.ShapeDtypeStruct((m, n), out_dtype),
        grid=(m // tm, n // tn, nk),
        in_specs=[pl.BlockSpec((tm, tk), lambda i, j, k: (i, k)),
                  pl.BlockSpec((tk, tn), lambda i, j, k: (k, j))],
        out_specs=pl.BlockSpec((tm, tn), lambda i, j, k: (i, j)),
        scratch_shapes=scratch,
        compiler_params=_params("parallel", "parallel", "arbitrary"),
    )(a, b)


def _matmul_cat(parts, b, out_dtype, name, tm=1024, tn=1024, tk=1024):
    m = parts[0].shape[0]
    n = b.shape[1]
    tm, tn = _pick(m, tm), _pick(n, tn)
    tk = min(_pick(p.shape[1], tk) for p in parts)
    steps = [p.shape[1] // tk for p in parts]
    assert all(p.shape[1] % tk == 0 for p in parts)
    starts = [sum(steps[:t]) for t in range(len(parts))]
    nk = sum(steps)
    n_parts = len(parts)

    def body(*refs):
        a_refs, b_ref, o_ref, acc_ref = refs[:n_parts], refs[n_parts], refs[n_parts + 1], refs[n_parts + 2]
        kk = pl.program_id(2)

        @pl.when(kk == 0)
        def _():
            acc_ref[...] = jnp.zeros_like(acc_ref)

        for t in range(n_parts):
            @pl.when(jnp.logical_and(kk >= starts[t], kk < starts[t] + steps[t]))
            def _(t=t):
                acc_ref[...] += jnp.dot(a_refs[t][...], b_ref[...], preferred_element_type=F32)

        @pl.when(kk == nk - 1)
        def _():
            o_ref[...] = acc_ref[...].astype(o_ref.dtype)

    def part_map(t):
        return lambda i, j, k: (i, jnp.clip(k - starts[t], 0, steps[t] - 1))

    return pl.pallas_call(
        body, name=name,
        out_shape=jax.ShapeDtypeStruct((m, n), out_dtype),
        grid=(m // tm, n // tn, nk),
        in_specs=[pl.BlockSpec((tm, tk), part_map(t)) for t in range(n_parts)]
        + [pl.BlockSpec((tk, tn), lambda i, j, k: (k, j))],
        out_specs=pl.BlockSpec((tm, tn), lambda i, j, k: (i, j)),
        scratch_shapes=[pltpu.VMEM((tm, tn), F32)],
        compiler_params=_params("parallel", "parallel", "arbitrary"),
    )(*parts, b)


def _matmul_tn(a, b, name, tm=1024, tn=1024, tk=1024, out_dtype=F32):
    kdim, m = a.shape
    n = b.shape[1]
    tm, tn, tk = _pick(m, tm), _pick(n, tn), _pick(kdim, tk)
    nk = kdim // tk

    def body(a_ref, b_ref, o_ref, acc_ref):
        kk = pl.program_id(2)

        @pl.when(kk == 0)
        def _():
            acc_ref[...] = jnp.zeros_like(acc_ref)

        acc_ref[...] += lax.dot_general(a_ref[...], b_ref[...], (((0,), (0,)), ((), ())),
                                        preferred_element_type=F32)

        @pl.when(kk == nk - 1)
        def _():
            o_ref[...] = acc_ref[...].astype(o_ref.dtype)

    return pl.pallas_call(
        body, name=name,
        out_shape=jax.ShapeDtypeStruct((m, n), out_dtype),
        grid=(m // tm, n // tn, nk),
        in_specs=[pl.BlockSpec((tk, tm), lambda i, j, k: (k, i)),
                  pl.BlockSpec((tk, tn), lambda i, j, k: (k, j))],
        out_specs=pl.BlockSpec((tm, tn), lambda i, j, k: (i, j)),
        scratch_shapes=[pltpu.VMEM((tm, tn), F32)],
        compiler_params=_params("parallel", "parallel", "arbitrary"),
    )(a, b)


def _grouped_matmul(a, w, name, tm=1024):
    s, e = a.shape
    ng, g, _ = w.shape
    tm = _pick(s, tm)

    def body(a_ref, w_ref, o_ref):
        o_ref[...] = jnp.dot(a_ref[...], w_ref[...], preferred_element_type=F32)

    return pl.pallas_call(
        body, name=name,
        out_shape=jax.ShapeDtypeStruct((s, e), F32),
        grid=(s // tm, ng),
        in_specs=[pl.BlockSpec((tm, g), lambda i, j: (i, j)),
                  pl.BlockSpec((None, g, g), lambda i, j: (j, 0, 0))],
        out_specs=pl.BlockSpec((tm, g), lambda i, j: (i, j)),
        compiler_params=_params("parallel", "parallel"),
    )(a, w)


def _grouped_weight_grad(a, b, ng, name, tk=1024):
    s, e = a.shape
    g = e // ng
    tk = _pick(s, tk)
    nk = s // tk

    def body(a_ref, b_ref, o_ref):
        kk = pl.program_id(1)

        @pl.when(kk == 0)
        def _():
            o_ref[...] = jnp.zeros_like(o_ref)

        o_ref[...] += lax.dot_general(a_ref[...], b_ref[...], (((0,), (0,)), ((), ())),
                                      preferred_element_type=F32)

    return pl.pallas_call(
        body, name=name,
        out_shape=jax.ShapeDtypeStruct((ng, g, g), F32),
        grid=(ng, nk),
        in_specs=[pl.BlockSpec((tk, g), lambda j, k: (k, j)),
                  pl.BlockSpec((tk, g), lambda j, k: (k, j))],
        out_specs=pl.BlockSpec((None, g, g), lambda j, k: (j, 0, 0)),
        compiler_params=_params("parallel", "arbitrary"),
    )(a, b)


def _rms(x):
    r = lax.rsqrt(jnp.mean(x * x, axis=1, keepdims=True) + RMS_EPS)
    return x * r, r


def _rmsnorm_fwd(x, g, name, ts=256):
    s, d = x.shape
    ts = _pick(s, ts, 8)

    def body(x_ref, g_ref, h_ref):
        xhat, _ = _rms(x_ref[...])
        h_ref[...] = (xhat * g_ref[...]).astype(BF16)

    return pl.pallas_call(
        body, name=name,
        out_shape=jax.ShapeDtypeStruct((s, d), BF16),
        grid=(s // ts,),
        in_specs=[pl.BlockSpec((ts, d), lambda i: (i, 0)), pl.BlockSpec((1, d), lambda i: (0, 0))],
        out_specs=pl.BlockSpec((ts, d), lambda i: (i, 0)),
        compiler_params=_params("parallel"),
    )(x, g)


def _outproj_norm(z, w, x, g, name, tm=512):
    s, e = z.shape
    d = w.shape[1]
    tm = _pick(s, tm)

    def body(z_ref, w_ref, x_ref, g_ref, xo_ref, h_ref):
        xn = x_ref[...] + jnp.dot(z_ref[...], w_ref[...], preferred_element_type=F32)
        xo_ref[...] = xn
        xhat, _ = _rms(xn)
        h_ref[...] = (xhat * g_ref[...]).astype(BF16)

    return pl.pallas_call(
        body, name=name,
        out_shape=(jax.ShapeDtypeStruct((s, d), F32), jax.ShapeDtypeStruct((s, d), BF16)),
        grid=(s // tm,),
        in_specs=[pl.BlockSpec((tm, e), lambda i: (i, 0)), pl.BlockSpec((e, d), lambda i: (0, 0)),
                  pl.BlockSpec((tm, d), lambda i: (i, 0)), pl.BlockSpec((1, d), lambda i: (0, 0))],
        out_specs=(pl.BlockSpec((tm, d), lambda i: (i, 0)), pl.BlockSpec((tm, d), lambda i: (i, 0))),
        compiler_params=_params("parallel"),
    )(z, w, x, g)


def _outproj_loss(z, w, x, g, target, name, tm=512):
    s, e = z.shape
    d = w.shape[1]
    tm = _pick(s, tm)

    def body(z_ref, w_ref, x_ref, g_ref, t_ref, dx_ref, dxb_ref, dg_ref, loss_ref):
        i = pl.program_id(0)
        xn = x_ref[...] + jnp.dot(z_ref[...], w_ref[...], preferred_element_type=F32)
        xhat, r = _rms(xn)
        gain = g_ref[...]
        diff = xhat * gain - t_ref[...]
        dout = diff * (1.0 / d)
        dxhat = dout * gain
        dx = r * (dxhat - xhat * jnp.mean(dxhat * xhat, axis=1, keepdims=True))
        dx_ref[...] = dx
        dxb_ref[...] = dx.astype(BF16)

        @pl.when(i == 0)
        def _():
            dg_ref[...] = jnp.zeros_like(dg_ref)
            loss_ref[...] = jnp.zeros_like(loss_ref)

        dg_ref[...] += jnp.sum(dout * xhat, axis=0, keepdims=True)
        loss_ref[...] += jnp.sum(diff * diff, axis=0, keepdims=True)

    row = lambda i: (i, 0)
    fixed = lambda i: (0, 0)
    return pl.pallas_call(
        body, name=name,
        out_shape=(jax.ShapeDtypeStruct((s, d), F32), jax.ShapeDtypeStruct((s, d), BF16),
                   jax.ShapeDtypeStruct((1, d), F32), jax.ShapeDtypeStruct((1, d), F32)),
        grid=(s // tm,),
        in_specs=[pl.BlockSpec((tm, e), row), pl.BlockSpec((e, d), fixed), pl.BlockSpec((tm, d), row),
                  pl.BlockSpec((1, d), fixed), pl.BlockSpec((tm, d), row)],
        out_specs=(pl.BlockSpec((tm, d), row), pl.BlockSpec((tm, d), row),
                   pl.BlockSpec((1, d), fixed), pl.BlockSpec((1, d), fixed)),
        compiler_params=_params("arbitrary"),
    )(z, w, x, g, target)


def _rmsnorm_bwd(dhs, x, g, dx_next, name, ts=256):
    s, d = x.shape
    ts = _pick(s, ts, 8)
    n_dh = len(dhs)

    def body(*refs):
        dh_refs = refs[:n_dh]
        x_ref, g_ref, dn_ref, dx_ref, dxb_ref, dg_ref = refs[n_dh:]
        i = pl.program_id(0)
        xhat, r = _rms(x_ref[...])
        dh_ = dh_refs[0][...]
        for extra in dh_refs[1:]:
            dh_ = dh_ + extra[...]
        dxhat = dh_ * g_ref[...]
        dx = dn_ref[...] + r * (dxhat - xhat * jnp.mean(dxhat * xhat, axis=1, keepdims=True))
        dx_ref[...] = dx
        dxb_ref[...] = dx.astype(BF16)

        @pl.when(i == 0)
        def _():
            dg_ref[...] = jnp.zeros_like(dg_ref)

        dg_ref[...] += jnp.sum(dh_ * xhat, axis=0, keepdims=True)

    row = lambda i: (i, 0)
    fixed = lambda i: (0, 0)
    return pl.pallas_call(
        body, name=name,
        out_shape=(jax.ShapeDtypeStruct((s, d), F32), jax.ShapeDtypeStruct((s, d), BF16),
                   jax.ShapeDtypeStruct((1, d), F32)),
        grid=(s // ts,),
        in_specs=[pl.BlockSpec((ts, d), row)] * n_dh + [pl.BlockSpec((ts, d), row), pl.BlockSpec((1, d), fixed),
                                                        pl.BlockSpec((ts, d), row)],
        out_specs=(pl.BlockSpec((ts, d), row), pl.BlockSpec((ts, d), row), pl.BlockSpec((1, d), fixed)),
        compiler_params=_params("arbitrary"),
    )(*dhs, x, g, dx_next)


def _pool_counts(t0, rows, cols, window):
    t = t0 + lax.broadcasted_iota(jnp.int32, (rows, cols), 0)
    return jnp.minimum(t + 1, window).astype(F32)


def _pool_fwd(u, name, ts=1024, tc=256):
    s, e = u.shape
    ng = len(POOL_WINDOWS)
    gdim = e // ng
    ts, tc = _pick(s, ts), _pick(gdim, tc)
    cpg = gdim // tc
    hb = ts // POOL_HALO

    def body(u_ref, halo_ref, d_ref):
        i, grp = pl.program_id(0), pl.program_id(1)
        cur = u_ref[...]
        halo = jnp.where(i > 0, halo_ref[...], 0.0)
        ext = jnp.concatenate([halo, cur], axis=0)
        for gi, window in enumerate(POOL_WINDOWS):
            @pl.when(grp == gi)
            def _(window=window):
                acc = ext
                k = 1
                while k < window:
                    acc = acc + pltpu.roll(acc, k, 0)
                    k *= 2
                pooled = acc[POOL_HALO:, :] / _pool_counts(i * ts, ts, tc, window)
                d_ref[...] = (pooled - cur).astype(BF16)

    return pl.pallas_call(
        body, name=name,
        out_shape=jax.ShapeDtypeStruct((s, e), BF16),
        grid=(s // ts, ng, cpg),
        in_specs=[pl.BlockSpec((ts, tc), lambda i, g, j: (i, g * cpg + j)),
                  pl.BlockSpec((POOL_HALO, tc), lambda i, g, j: (jnp.maximum(i * hb - 1, 0), g * cpg + j))],
        out_specs=pl.BlockSpec((ts, tc), lambda i, g, j: (i, g * cpg + j)),
        compiler_params=_params("parallel", "parallel", "parallel"),
    )(u, u)


def _pool_bwd(dd, name, ts=1024, tc=256):
    s, e = dd.shape
    ng = len(POOL_WINDOWS)
    gdim = e // ng
    ts, tc = _pick(s, ts), _pick(gdim, tc)
    cpg = gdim // tc
    hb = ts // POOL_HALO
    n_halo = s // POOL_HALO
    nst = s // ts

    def body(dd_ref, halo_ref, du_ref):
        i, grp = pl.program_id(0), pl.program_id(1)
        cur = dd_ref[...]
        halo = jnp.where(i < nst - 1, halo_ref[...], 0.0)
        ext = jnp.concatenate([cur, halo], axis=0)
        rows = ts + POOL_HALO
        for gi, window in enumerate(POOL_WINDOWS):
            @pl.when(grp == gi)
            def _(window=window):
                acc = ext / _pool_counts(i * ts, rows, tc, window)
                k = 1
                while k < window:
                    acc = acc + pltpu.roll(acc, rows - k, 0)
                    k *= 2
                du_ref[...] = (acc[:ts, :] - cur).astype(BF16)

    return pl.pallas_call(
        body, name=name,
        out_shape=jax.ShapeDtypeStruct((s, e), BF16),
        grid=(nst, ng, cpg),
        in_specs=[pl.BlockSpec((ts, tc), lambda i, g, j: (i, g * cpg + j)),
                  pl.BlockSpec((POOL_HALO, tc),
                               lambda i, g, j: (jnp.minimum((i + 1) * hb, n_halo - 1), g * cpg + j))],
        out_specs=pl.BlockSpec((ts, tc), lambda i, g, j: (i, g * cpg + j)),
        compiler_params=_params("parallel", "parallel", "parallel"),
    )(dd, dd)


def _a_group_fwd(d, w, scale, gate, name, tm=1024):
    s, e = d.shape
    ng, g, _ = w.shape
    tm = _pick(s, tm)

    def body(d_ref, w_ref, s_ref, gate_ref, yr_ref, z_ref):
        yr = jnp.dot(d_ref[...], w_ref[...], preferred_element_type=F32)
        yr_ref[...] = yr.astype(yr_ref.dtype)
        gt = gate_ref[...].astype(F32)
        z_ref[...] = ((yr * s_ref[...]) * (gt * _sigmoid(gt))).astype(BF16)

    blk = lambda i, j: (i, j)
    return pl.pallas_call(
        body, name=name,
        out_shape=(jax.ShapeDtypeStruct((s, e), BF16), jax.ShapeDtypeStruct((s, e), BF16)),
        grid=(s // tm, ng),
        in_specs=[pl.BlockSpec((tm, g), blk), pl.BlockSpec((None, g, g), lambda i, j: (j, 0, 0)),
                  pl.BlockSpec((1, g), lambda i, j: (0, j)), pl.BlockSpec((tm, g), blk)],
        out_specs=(pl.BlockSpec((tm, g), blk), pl.BlockSpec((tm, g), blk)),
        compiler_params=_params("parallel", "parallel"),
    )(d, w, scale, gate)


def _a_gate_bwd(dz, yr, gate, scale, name, ts=512, tc=512):
    s, e = dz.shape
    ts, tc = _pick(s, ts), _pick(e, tc)

    def body(dz_ref, yr_ref, gate_ref, s_ref, dgate_ref, dyr_ref, dscale_ref):
        i = pl.program_id(1)
        dz_, yr_, gt, sc = dz_ref[...], yr_ref[...].astype(F32), gate_ref[...].astype(F32), s_ref[...]
        sg = _sigmoid(gt)
        dy = dz_ * (gt * sg)
        dgate_ref[...] = (dz_ * (yr_ * sc) * (sg * (1.0 + gt * (1.0 - sg)))).astype(BF16)
        dyr_ref[...] = (dy * sc).astype(BF16)

        @pl.when(i == 0)
        def _():
            dscale_ref[...] = jnp.zeros_like(dscale_ref)

        dscale_ref[...] += jnp.sum(dy * yr_, axis=0, keepdims=True)

    blk = lambda j, i: (i, j)
    vec = lambda j, i: (0, j)
    return pl.pallas_call(
        body, name=name,
        out_shape=(jax.ShapeDtypeStruct((s, e), BF16), jax.ShapeDtypeStruct((s, e), BF16),
                   jax.ShapeDtypeStruct((1, e), F32)),
        grid=(e // tc, s // ts),
        in_specs=[pl.BlockSpec((ts, tc), blk), pl.BlockSpec((ts, tc), blk), pl.BlockSpec((ts, tc), blk),
                  pl.BlockSpec((1, tc), vec)],
        out_specs=(pl.BlockSpec((ts, tc), blk), pl.BlockSpec((ts, tc), blk), pl.BlockSpec((1, tc), vec)),
        compiler_params=_params("parallel", "arbitrary"),
    )(dz, yr, gate, scale)


def _gate_fwd(y, gate, name, ts=512, tc=512):
    s, e = y.shape
    ts, tc = _pick(s, ts), _pick(e, tc)

    def body(y_ref, gate_ref, z_ref):
        gt = gate_ref[...].astype(F32)
        z_ref[...] = (y_ref[...].astype(F32) * (gt * _sigmoid(gt))).astype(BF16)

    blk = lambda i, j: (i, j)
    return pl.pallas_call(
        body, name=name,
        out_shape=jax.ShapeDtypeStruct((s, e), BF16),
        grid=(s // ts, e // tc),
        in_specs=[pl.BlockSpec((ts, tc), blk)] * 2,
        out_specs=pl.BlockSpec((ts, tc), blk),
        compiler_params=_params("parallel", "parallel"),
    )(y, gate)


def _gate_bwd(dz, y, gate, name, ts=512, tc=512):
    s, e = dz.shape
    ts, tc = _pick(s, ts), _pick(e, tc)

    def body(dz_ref, y_ref, gate_ref, dgate_ref, dy_ref):
        dz_, gt = dz_ref[...], gate_ref[...].astype(F32)
        sg = _sigmoid(gt)
        dgate_ref[...] = (dz_ * y_ref[...].astype(F32) * (sg * (1.0 + gt * (1.0 - sg)))).astype(BF16)
        dy_ref[...] = (dz_ * (gt * sg)).astype(BF16)

    blk = lambda i, j: (i, j)
    return pl.pallas_call(
        body, name=name,
        out_shape=(jax.ShapeDtypeStruct((s, e), BF16), jax.ShapeDtypeStruct((s, e), BF16)),
        grid=(s // ts, e // tc),
        in_specs=[pl.BlockSpec((ts, tc), blk)] * 3,
        out_specs=(pl.BlockSpec((ts, tc), blk), pl.BlockSpec((ts, tc), blk)),
        compiler_params=_params("parallel", "parallel"),
    )(dz, y, gate)


def _merge_weights(l0, l1, l2):
    m = jnp.maximum(jnp.maximum(l0, l1), l2)
    e0, e1, e2 = jnp.exp(l0 - m), jnp.exp(l1 - m), jnp.exp(l2 - m)
    inv = 1.0 / (e0 + e1 + e2)
    return e0 * inv, e1 * inv, e2 * inv


def _merge_gate_fwd(outs, lses, gate, name, ts=512, tc=512):
    s, e = gate.shape
    ts, tc = _pick(s, ts), _pick(e, tc)

    def body(o0, o1, o2, l0, l1, l2, gate_ref, y_ref, z_ref):
        w0, w1, w2 = _merge_weights(l0[...], l1[...], l2[...])
        y = w0 * o0[...].astype(F32) + w1 * o1[...].astype(F32) + w2 * o2[...].astype(F32)
        y_ref[...] = y.astype(y_ref.dtype)
        gt = gate_ref[...].astype(F32)
        z_ref[...] = (y * (gt * _sigmoid(gt))).astype(BF16)

    blk = lambda i, j: (i, j)
    return pl.pallas_call(
        body, name=name,
        out_shape=(jax.ShapeDtypeStruct((s, e), BF16), jax.ShapeDtypeStruct((s, e), BF16)),
        grid=(s // ts, e // tc),
        in_specs=[pl.BlockSpec((ts, tc), blk)] * 7,
        out_specs=(pl.BlockSpec((ts, tc), blk), pl.BlockSpec((ts, tc), blk)),
        compiler_params=_params("parallel", "parallel"),
    )(*outs, *lses, gate)


def _merge_gate_bwd(dz, y, gate, lses, name, ts=512, tc=512):
    s, e = dz.shape
    ts, tc = _pick(s, ts), _pick(e, tc)

    def body(dz_ref, y_ref, gate_ref, l0, l1, l2, dgate_ref, d0, d1, d2):
        dz_, gt = dz_ref[...], gate_ref[...].astype(F32)
        sg = _sigmoid(gt)
        dgate_ref[...] = (dz_ * y_ref[...].astype(F32) * (sg * (1.0 + gt * (1.0 - sg)))).astype(BF16)
        dy = dz_ * (gt * sg)
        w0, w1, w2 = _merge_weights(l0[...], l1[...], l2[...])
        d0[...] = (w0 * dy).astype(BF16)
        d1[...] = (w1 * dy).astype(BF16)
        d2[...] = (w2 * dy).astype(BF16)

    blk = lambda i, j: (i, j)
    return pl.pallas_call(
        body, name=name,
        out_shape=tuple([jax.ShapeDtypeStruct((s, e), BF16)] * 4),
        grid=(s // ts, e // tc),
        in_specs=[pl.BlockSpec((ts, tc), blk)] * 6,
        out_specs=tuple([pl.BlockSpec((ts, tc), blk)] * 4),
        compiler_params=_params("parallel", "parallel"),
    )(dz, y, gate, *lses)


def _band(max_dist, width):
    row = lax.broadcasted_iota(jnp.int32, (2 * BLOCK, width), 0) & (BLOCK - 1)
    col = lax.broadcasted_iota(jnp.int32, (2 * BLOCK, width), 1)
    low = row if max_dist == BLOCK else row + 1
    return jnp.logical_and(col >= low, col <= row + BLOCK), col >= BLOCK


def _fill_bias(bias_ref, max_dist):
    band, own = _band(max_dist, 2 * BLOCK)
    bias_ref[0] = jnp.where(band, 0.0, NEG)
    bias_ref[1] = jnp.where(jnp.logical_and(band, own), 0.0, NEG)


def _aligned(v):
    return v if isinstance(v, int) else pl.multiple_of(v, BLOCK)


def _stack_heads(x, lo):
    return jnp.concatenate([jnp.where(lo, x, 0.0), jnp.where(lo, 0.0, x)], axis=0).astype(BF16)


def _unstack_heads(x2, lo):
    return jnp.where(lo, x2[:BLOCK], x2[BLOCK:])


def _head_col(x, hm):
    return jnp.max(jnp.where(hm, x, NEG), axis=1, keepdims=True)


def _dot_nt(a, b):
    return lax.dot_general(a, b, (((1,), (1,)), ((), ())), preferred_element_type=F32)


def _dot_tn(a, b):
    return lax.dot_general(a, b, (((0,), (0,)), ((), ())), preferred_element_type=F32)


def _stream_view(a, dil):
    s, w = a.shape
    return a.reshape(s // (BLOCK * dil), dil, BLOCK, w)


def _fill_window(dst, halo_ref, cur_ref, n):
    dst[0:BLOCK, :] = halo_ref[0]
    for jc in range(n):
        dst[(jc + 1) * BLOCK:(jc + 2) * BLOCK, :] = cur_ref[jc]


def _attn_fwd(q, k, v, sinks, max_dist, rep, dil, out_dtype, name, tq=2048):
    assert max_dist in (BLOCK - 1, BLOCK)
    s, w = q.shape
    l = s // dil
    n_pairs = w // LANES
    tq = _pick(l, tq)
    n = tq // BLOCK
    has_sink = sinks is not None
    scale = HEAD_DIM ** -0.5

    def body(*refs):
        if has_sink:
            sink_ref, refs = refs[0], refs[1:]
        q_ref, kc_ref, kh_ref, vc_ref, vh_ref, o_ref, lse_ref, kx, vx, bias_ref = refs
        i, p = pl.program_id(0), pl.program_id(2)
        _fill_window(kx, kh_ref, kc_ref, n)
        _fill_window(vx, vh_ref, vc_ref, n)
        lo = lax.broadcasted_iota(jnp.int32, (BLOCK, LANES), 1) < HEAD_DIM
        _fill_bias(bias_ref, max_dist)
        top = lax.broadcasted_iota(jnp.int32, (2 * BLOCK, 1), 0) < BLOCK

        def scores(j):
            r0 = _aligned(j * BLOCK)
            q2 = _stack_heads(q_ref[j].astype(F32) * scale, lo)
            first = jnp.logical_and(i == 0, j == 0).astype(jnp.int32)
            return _dot_nt(q2, kx[pl.ds(r0, 2 * BLOCK), :]) + bias_ref[first]

        per_step = 2 if n % 2 == 0 else 1

        def step(jj, carry):
            nxt = tuple(scores(jnp.minimum((jj + 1) * per_step + t, n - 1)) for t in range(per_step))
            for t in range(per_step):
                finish(jj * per_step + t, carry[t])
            return nxt

        def finish(j, s2):
            r0 = _aligned(j * BLOCK)
            vw = vx[pl.ds(r0, 2 * BLOCK), :]
            m = jnp.max(s2, axis=1, keepdims=True)
            if has_sink:
                sk = jnp.where(top, sink_ref[2 * p], sink_ref[2 * p + 1])
                m = jnp.maximum(m, sk)
            pr = jnp.exp(s2 - m)
            den = jnp.sum(pr, axis=1, keepdims=True)
            if has_sink:
                den = den + jnp.exp(sk - m)
            o2 = jnp.dot(pr.astype(BF16), vw, preferred_element_type=F32) * (1.0 / den)
            lse2 = m + jnp.log(den)
            o_ref[j] = _unstack_heads(o2, lo).astype(o_ref.dtype)
            lse_ref[j] = _unstack_heads(lse2, lo)

        lax.fori_loop(0, n // per_step, step, tuple(scores(t) for t in range(per_step)))

    cur = lambda i, r, p: (i, r, 0, p)
    kv_cur = lambda i, r, p: (i, r, 0, p // rep)
    kv_halo = lambda i, r, p: (jnp.maximum(i * n - 1, 0), r, 0, p // rep)
    big, small = (n, None, BLOCK, LANES), (1, None, BLOCK, LANES)
    in_specs = [pl.BlockSpec(big, cur), pl.BlockSpec(big, kv_cur), pl.BlockSpec(small, kv_halo),
                pl.BlockSpec(big, kv_cur), pl.BlockSpec(small, kv_halo)]
    q4, k4, v4 = _stream_view(q, dil), _stream_view(k, dil), _stream_view(v, dil)
    args = [q4, k4, k4, v4, v4]
    if has_sink:
        in_specs = [pl.BlockSpec(memory_space=pltpu.SMEM)] + in_specs
        args = [sinks] + args
    o4, lse4 = pl.pallas_call(
        body, name=name,
        out_shape=(jax.ShapeDtypeStruct(q4.shape, out_dtype), jax.ShapeDtypeStruct(q4.shape, F32)),
        grid=(l // tq, dil, n_pairs),
        in_specs=in_specs,
        out_specs=(pl.BlockSpec(big, cur), pl.BlockSpec(big, cur)),
        scratch_shapes=[pltpu.VMEM((tq + BLOCK, LANES), BF16), pltpu.VMEM((tq + BLOCK, LANES), BF16),
                        pltpu.VMEM((2, 2 * BLOCK, 2 * BLOCK), F32)],
        compiler_params=_params("parallel", "parallel", "parallel"),
    )(*args)
    return o4.reshape(s, w), lse4.reshape(s, w)


def _attn_bwd(q, k, v, do, y, lse, sinks, max_dist, rep, dil, name, tq=2048):
    s, w = q.shape
    l = s // dil
    n_pairs = w // LANES
    tq = _pick(l, tq)
    n = tq // BLOCK
    n_blk = l // BLOCK
    n_sb = l // tq
    has_sink = sinks is not None
    scale = HEAD_DIM ** -0.5
    kv_dtype = BF16
    ext = tq + BLOCK

    def body(*refs):
        if has_sink:
            sink_ref, refs = refs[0], refs[1:]
        (q_ref, qn_ref, kc_ref, kh_ref, vc_ref, vh_ref, do_ref, don_ref, y_ref, yn_ref,
         lse_ref, lsen_ref) = refs[:12]
        refs = refs[12:]
        dq_ref, dk_ref, dv_ref = refs[:3]
        refs = refs[3:]
        if has_sink:
            dsink_ref, refs = refs[0], refs[1:]
        kx, vx, dkx, dvx, bias_ref = refs[:5]
        if rep > 1:
            dk_acc, dv_acc = refs[5:]
        i, p = pl.program_id(0), pl.program_id(2)
        _fill_bias(bias_ref, max_dist)
        own_rows = (q_ref, do_ref, y_ref, lse_ref)
        next_rows = (qn_ref, don_ref, yn_ref, lsen_ref)
        _fill_window(kx, kh_ref, kc_ref, n)
        _fill_window(vx, vh_ref, vc_ref, n)
        dkx[...] = jnp.zeros_like(dkx)
        dvx[...] = jnp.zeros_like(dvx)
        lo = lax.broadcasted_iota(jnp.int32, (BLOCK, LANES), 1) < HEAD_DIM
        hi = jnp.logical_not(lo)
        top = lax.broadcasted_iota(jnp.int32, (2 * BLOCK, 1), 0) < BLOCK

        def rows_of(j):
            if isinstance(j, int) and j == n:
                return next_rows, 0
            return own_rows, j

        def front(j, width):
            (qr, dor, _, _), jb = rows_of(j)
            r0 = _aligned(j * BLOCK)
            first = jnp.logical_and(i == 0, j == 0).astype(jnp.int32)
            q2 = _stack_heads(qr[jb].astype(F32) * scale, lo)
            do2 = _stack_heads(dor[jb].astype(F32), lo)
            s2 = _dot_nt(q2, kx[pl.ds(r0, width), :]) + bias_ref[first, :, pl.ds(0, width)]
            return s2, _dot_nt(do2, vx[pl.ds(r0, width), :])

        def back(j, width, q_valid, s2, dp2, sink_acc):
            (qr, dor, yr, lser), jb = rows_of(j)
            r0 = _aligned(j * BLOCK)
            dof = dor[jb].astype(F32)
            yb, lseb = yr[jb].astype(F32), lser[jb]
            q2 = _stack_heads(qr[jb].astype(F32) * scale, lo)
            do2 = _stack_heads(dof, lo)
            prod = dof * yb
            delta = jnp.concatenate([jnp.sum(jnp.where(lo, prod, 0.0), axis=1, keepdims=True),
                                     jnp.sum(jnp.where(lo, 0.0, prod), axis=1, keepdims=True)], axis=0)
            lse2 = jnp.concatenate([_head_col(lseb, lo), _head_col(lseb, hi)], axis=0)
            pr = jnp.exp(s2 - lse2)
            if q_valid is not True:
                pr = jnp.where(q_valid, pr, 0.0)
            ds = pr * (dp2 - delta)
            dkx[pl.ds(r0, width), :] += _dot_tn(ds.astype(BF16), q2)
            dvx[pl.ds(r0, width), :] += _dot_tn(pr.astype(BF16), do2)
            if width == 2 * BLOCK:
                dq2 = jnp.dot(ds.astype(BF16), kx[pl.ds(r0, width), :], preferred_element_type=F32) * scale
                dq_ref[jb] = _unstack_heads(dq2, lo).astype(dq_ref.dtype)
            if has_sink:
                sk = jnp.where(top, sink_ref[2 * p], sink_ref[2 * p + 1])
                sink_acc = sink_acc - jnp.exp(sk - lse2) * delta
            return sink_acc

        per_step = 2 if n % 2 == 0 else 1

        def step(jj, sink_acc):
            fronts = [front(jj * per_step + t, 2 * BLOCK) for t in range(per_step)]
            for t in range(per_step):
                sink_acc = back(jj * per_step + t, 2 * BLOCK, True, *fronts[t], sink_acc)
            return sink_acc

        zero_col = jnp.zeros((2 * BLOCK, 1), F32)
        sink_acc = lax.fori_loop(0, n // per_step, step, zero_col)
        if n_sb > 1:
            back(n, BLOCK, i < n_sb - 1, *front(n, BLOCK), zero_col)

        def write_out(dk_src, dv_src, first_row):
            for jc in range(n):
                rows = slice(first_row + jc * BLOCK, first_row + (jc + 1) * BLOCK)
                dk_ref[jc] = dk_src[rows, :].astype(dk_ref.dtype)
                dv_ref[jc] = dv_src[rows, :].astype(dv_ref.dtype)

        if rep == 1:
            write_out(dkx, dvx, BLOCK)
        else:
            @pl.when(p % rep == 0)
            def _():
                dk_acc[...] = dkx[BLOCK:, :]
                dv_acc[...] = dvx[BLOCK:, :]

            @pl.when(p % rep != 0)
            def _():
                dk_acc[...] += dkx[BLOCK:, :]
                dv_acc[...] += dvx[BLOCK:, :]

            pl.when(p % rep == rep - 1)(lambda: write_out(dk_acc, dv_acc, 0))
        if has_sink:
            rowi = lax.broadcasted_iota(jnp.int32, (8, LANES), 0)
            s0 = jnp.sum(sink_acc[:BLOCK], axis=0, keepdims=True)
            s1 = jnp.sum(sink_acc[BLOCK:], axis=0, keepdims=True)
            dsink_ref[...] = jnp.where(rowi == 0, s0, jnp.where(rowi == 1, s1, 0.0))

    cur = lambda i, r, p: (i, r, 0, p)
    nxt = lambda i, r, p: (jnp.minimum((i + 1) * n, n_blk - 1), r, 0, p)
    kv_cur = lambda i, r, p: (i, r, 0, p // rep)
    kv_halo = lambda i, r, p: (jnp.maximum(i * n - 1, 0), r, 0, p // rep)
    big, small = (n, None, BLOCK, LANES), (1, None, BLOCK, LANES)
    in_specs = [pl.BlockSpec(big, cur), pl.BlockSpec(small, nxt),
                pl.BlockSpec(big, kv_cur), pl.BlockSpec(small, kv_halo),
                pl.BlockSpec(big, kv_cur), pl.BlockSpec(small, kv_halo),
                pl.BlockSpec(big, cur), pl.BlockSpec(small, nxt),
                pl.BlockSpec(big, cur), pl.BlockSpec(small, nxt),
                pl.BlockSpec(big, cur), pl.BlockSpec(small, nxt)]
    q4, k4, v4, do4, y4, lse4 = [_stream_view(a, dil) for a in (q, k, v, do, y, lse)]
    args = [q4, q4, k4, k4, v4, v4, do4, do4, y4, y4, lse4, lse4]
    out_shape = [jax.ShapeDtypeStruct(q4.shape, BF16),
                 jax.ShapeDtypeStruct(k4.shape, kv_dtype), jax.ShapeDtypeStruct(v4.shape, kv_dtype)]
    out_specs = [pl.BlockSpec(big, cur), pl.BlockSpec(big, kv_cur), pl.BlockSpec(big, kv_cur)]
    if has_sink:
        in_specs = [pl.BlockSpec(memory_space=pltpu.SMEM)] + in_specs
        args = [sinks] + args
        out_shape.append(jax.ShapeDtypeStruct((n_sb, dil, n_pairs, 8, LANES), F32))
        out_specs.append(pl.BlockSpec((None, None, None, 8, LANES), lambda i, r, p: (i, r, p, 0, 0)))
    outs = pl.pallas_call(
        body, name=name,
        out_shape=tuple(out_shape),
        grid=(n_sb, dil, n_pairs),
        in_specs=in_specs,
        out_specs=tuple(out_specs),
        scratch_shapes=[pltpu.VMEM((ext, LANES), BF16), pltpu.VMEM((ext, LANES), BF16),
                        pltpu.VMEM((ext, LANES), F32), pltpu.VMEM((ext, LANES), F32),
                        pltpu.VMEM((2, 2 * BLOCK, 2 * BLOCK), F32)]
        + ([pltpu.VMEM((tq, LANES), F32), pltpu.VMEM((tq, LANES), F32)] if rep > 1 else []),
        compiler_params=_params("parallel", "parallel", "arbitrary"),
    )(*args)
    grads =[outs[0].reshape(s, w), outs[1].reshape(k.shape), outs[2].reshape(v.shape)]
    if has_sink:
        grads.append(outs[3].sum(axis=(0, 1))[:, 0:2, 0].reshape(1, 2 * n_pairs))
    return grads


def _sum_slots(recv, name, ts=256):
    nd, r, c = recv.shape
    ts = _pick(r, ts, 8)

    def body(r_ref, o_ref):
        acc = r_ref[0].astype(F32)
        for dev in range(1, nd):
            acc = acc + r_ref[dev].astype(F32)
        o_ref[...] = acc

    return pl.pallas_call(
        body, name=name,
        out_shape=jax.ShapeDtypeStruct((r, c), F32),
        grid=(r // ts,),
        in_specs=[pl.BlockSpec((nd, ts, c), lambda i: (0, i, 0))],
        out_specs=pl.BlockSpec((ts, c), lambda i: (i, 0)),
        compiler_params=_params("parallel"),
    )(recv)


def _adamw(w, g, m, v, name, ts=256):
    r, c = w.shape
    ts = _pick(r, ts, 8)
    c1 = 1.0 - ADAM_B1 ** ADAM_STEP
    c2 = 1.0 - ADAM_B2 ** ADAM_STEP

    def body(w_ref, g_ref, m_ref, v_ref, d_ref, mo_ref, vo_ref):
        g_ = g_ref[...]
        m_ = ADAM_B1 * m_ref[...] + (1.0 - ADAM_B1) * g_
        v_ = ADAM_B2 * v_ref[...] + (1.0 - ADAM_B2) * (g_ * g_)
        mo_ref[...] = m_
        vo_ref[...] = v_
        d_ref[...] = -ADAM_LR * ((m_ / c1) / (jnp.sqrt(v_ / c2) + ADAM_EPS) + ADAM_WD * w_ref[...])

    blk = pl.BlockSpec((ts, c), lambda i: (i, 0))
    return pl.pallas_call(
        body, name=name,
        out_shape=tuple([jax.ShapeDtypeStruct((r, c), F32)] * 3),
        grid=(r // ts,),
        in_specs=[blk] * 4,
        out_specs=(blk, blk, blk),
        compiler_params=_params("parallel"),
    )(w, g, m, v)


def _rows(a):
    flat = a.reshape(-1)
    pad = (-flat.shape[0]) % PACK_W
    if pad:
        flat = jnp.concatenate([flat, jnp.zeros((pad,), flat.dtype)])
    return flat.reshape(-1, PACK_W)


def _pad_rows(a, mult):
    pad = (-a.shape[-2]) % mult
    if pad:
        widths = [(0, 0)] * (a.ndim - 2) + [(0, pad), (0, 0)]
        a = jnp.pad(a, widths)
    return a


def _to_global(stack, axis):
    moved = jnp.moveaxis(stack, 0, axis)
    shp = list(moved.shape)
    shp[axis:axis + 2] = [shp[axis] * shp[axis + 1]]
    return moved.reshape(shp)


def _to_stack(full, axis):
    shp = list(full.shape)
    shp[axis:axis + 1] = [N_DEV, shp[axis] // N_DEV]
    return jnp.moveaxis(full.reshape(shp), axis, 0)


_BIG = (("w_out", 1), ("a_w_in", 2), ("a_w_group", 2), ("b_w_in", 2), ("c_w_in", 2))


def _dup_heads(wk, n_kv):
    d = wk.shape[0]
    return jnp.tile(wk.reshape(d, n_kv, 1, HEAD_DIM), (1, 1, 2, 1)).reshape(d, n_kv * LANES)


def _fold_heads(dwk, n_kv):
    d = dwk.shape[0]
    folded = dwk.astype(F32).reshape(d, n_kv, 2, HEAD_DIM).sum(axis=2)
    return folded.reshape(d, n_kv * HEAD_DIM).astype(dwk.dtype)


def _perm(a, dil):
    if dil == 1:
        return a
    s, w = a.shape
    return a.reshape(s // (BLOCK * dil), BLOCK, dil, w).transpose(0, 2, 1, 3).reshape(s, w)


def _unperm(a, dil):
    if dil == 1:
        return a
    s, w = a.shape
    return a.reshape(s // (BLOCK * dil), dil, BLOCK, w).transpose(0, 2, 1, 3).reshape(s, w)


def kernel(x, norm_g, final_g, w_out, a_w_in, a_w_group, a_scale, b_w_in, b_sinks, c_w_in, loss_target, m_norm_g, m_final_g, m_w_out, m_a_w_in, m_a_w_group, m_a_scale, m_b_w_in, m_b_sinks, m_c_w_in, v_norm_g, v_final_g, v_w_out, v_a_w_in, v_a_w_group, v_a_scale, v_b_w_in, v_b_sinks, v_c_w_in):
    local = dict(w_out=w_out, a_w_in=a_w_in, a_w_group=a_w_group, b_w_in=b_w_in, c_w_in=c_w_in)
    mom_m = dict(w_out=m_w_out, a_w_in=m_a_w_in, a_w_group=m_a_w_group, b_w_in=m_b_w_in, c_w_in=m_c_w_in)
    mom_v = dict(w_out=v_w_out, a_w_in=v_a_w_in, a_w_group=v_a_w_group, b_w_in=v_b_w_in, c_w_in=v_c_w_in)
    s, d = x.shape[1], x.shape[2]
    depth = norm_g.shape[0]
    e = w_out.shape[1] * N_DEV
    n_heads = e // HEAD_DIM
    n_kv = n_heads // Q_PER_KV
    kv_w = n_kv * HEAD_DIM
    rep = Q_PER_KV // 2
    n_groups = len(POOL_WINDOWS)
    me = 4 * lax.axis_index("x") + 2 * lax.axis_index("y") + lax.axis_index("c")

    sizes = [local[n].size // PACK_W for n, _ in _BIG]
    offs = [sum(sizes[:k]) for k in range(len(sizes) + 1)]
    wpack = _pad_rows(jnp.concatenate([_rows(local[n].astype(BF16)) for n, _ in _BIG], axis=0), 16)
    r_pack = wpack.shape[0]
    spack = _pad_rows(_rows(a_scale), 8)
    wall, sall = _gather([wpack, spack], "gather_weights")
    full = {}
    for k, (name, axis) in enumerate(_BIG):
        stack = wall[:, offs[k]:offs[k + 1], :].reshape((N_DEV,) + local[name].shape)
        full[name] = _to_global(stack, axis)
    scale_full = _to_global(sall.reshape(N_DEV, -1)[:, :a_scale.size].reshape((N_DEV,) + a_scale.shape), 1)

    wout_t = jnp.swapaxes(full["w_out"], 1, 2)
    wa = full["a_w_in"]
    wa_t = jnp.swapaxes(wa, 1, 2)
    wg = full["a_w_group"]
    wg_t = jnp.swapaxes(wg, 2, 3)
    wb = full["b_w_in"][0]
    wb_ext = jnp.concatenate([wb[:, :e], _dup_heads(wb[:, e:e + kv_w], n_kv),
                              _dup_heads(wb[:, e + kv_w:e + 2 * kv_w], n_kv), wb[:, e + 2 * kv_w:]], axis=1)
    wb_ext_t = wb_ext.T
    kd_w = n_kv * LANES
    wc = full["c_w_in"][0]
    wc_t = wc.T

    xs, hs, zs, saved = [x.reshape(s, d)], [], [], []
    hs.append(_rmsnorm_fwd(xs[0], norm_g[0:1], "norm0"))
    loss_vec = dfinal = dx = dxb = None
    for i in range(depth):
        kind, j = i % 3, i // 3
        h = hs[i]
        tag = f"l{i}"
        if kind == 0:
            u = _matmul(h, wa[j][:, :e], F32, tag + "_in_u")
            gate = _matmul(h, wa[j][:, e:], BF16, tag + "_in_gate")
            dpool = _pool_fwd(u, tag + "_pool")
            yr, z = _a_group_fwd(dpool, wg[j], scale_full[j:j + 1], gate, tag + "_group")
            saved.append(dict(dpool=dpool, yr=yr, gate=gate))
        elif kind == 1:
            q = _matmul(h, wb_ext[:, :e], BF16, tag + "_in_q")
            kd = _matmul(h, wb_ext[:, e:e + kd_w], BF16, tag + "_in_k")
            vd = _matmul(h, wb_ext[:, e + kd_w:e + 2 * kd_w], BF16, tag + "_in_v")
            gate = _matmul(h, wb_ext[:, e + 2 * kd_w:], BF16, tag + "_in_gate")
            sinks = b_sinks[j]
            y, lse = _attn_fwd(q, kd, vd, sinks, SWA_MAX_DIST, rep, 1, BF16, tag + "_attn")
            z = _gate_fwd(y, gate, tag + "_gate")
            saved.append(dict(q=q, kd=kd, vd=vd, gate=gate, y=y, lse=lse, sinks=sinks))
        else:
            qkv, outs, lses, h_perm = [], [], [], []
            for gi, (window, dil) in enumerate(DILATED_PAIRS):
                hp = _perm(h, dil)
                trio = [_matmul(hp, wc[:, (3 * gi + t) * e:(3 * gi + t + 1) * e], BF16,
                                f"{tag}_in_{'qkv'[t]}{gi}") for t in range(3)]
                o, lse = _attn_fwd(trio[0], trio[1], trio[2], None, window // dil, 1, dil, BF16,
                                   f"{tag}_attn{gi}")
                qkv.append(trio)
                h_perm.append(hp)
                outs.append(_unperm(o, dil))
                lses.append(lse)
            gate = _matmul(h, wc[:, 9 * e:], BF16, tag + "_in_gate")
            lses_tok = [_unperm(lse, dil) for lse, (_, dil) in zip(lses, DILATED_PAIRS)]
            y, z = _merge_gate_fwd(outs, lses_tok, gate, tag + "_merge")
            saved.append(dict(qkv=qkv, lses=lses, lses_tok=lses_tok, gate=gate, y=y, h_perm=h_perm))
        zs.append(z)
        if i + 1 < depth:
            x_new, h_new = _outproj_norm(z, full["w_out"][i], xs[i], norm_g[i + 1:i + 2], tag + "_out")
            xs.append(x_new)
            hs.append(h_new)
        else:
            dx, dxb, dfinal, loss_vec = _outproj_loss(z, full["w_out"][i], xs[i], final_g.reshape(1, d),
                                                      loss_target.reshape(s, d), tag + "_out_loss")

    g_full = {"w_out": [None] * depth, "a_w_in": [None] * wa.shape[0], "a_w_group": [None] * wa.shape[0]}
    d_norm = [None] * depth
    d_scale = [None] * wa.shape[0]
    d_sinks = None
    for i in reversed(range(depth)):
        kind, j = i % 3, i // 3
        tag = f"b{i}"
        sv = saved[i]
        g_full["w_out"][i] = _matmul_tn(zs[i], dxb, tag + "_dwout", out_dtype=BF16)
        dz = _matmul(dxb, wout_t[i], F32, tag + "_dz")
        if kind == 0:
            dgate, dyr, dsc = _a_gate_bwd(dz, sv["yr"], sv["gate"], scale_full[j:j + 1], tag + "_gate")
            d_scale[j] = dsc
            dd = _grouped_matmul(dyr, wg_t[j], tag + "_dd")
            du = _pool_bwd(dd, tag + "_pool")
            g_full["a_w_group"][j] = _grouped_weight_grad(sv["dpool"], dyr, n_groups, tag + "_dwg")
            parts = [du, dgate]
            g_full["a_w_in"][j] = jnp.concatenate(
                [_matmul_tn(hs[i], part, f"{tag}_dwin{t}", out_dtype=BF16) for t, part in enumerate(parts)], axis=1)
            dhs = [_matmul_cat(parts, wa_t[j], F32, tag + "_dh")]
        elif kind == 1:
            dgate, do = _gate_bwd(dz, sv["y"], sv["gate"], tag + "_gate")
            dq, dkd, dvd, d_sinks = _attn_bwd(sv["q"], sv["kd"], sv["vd"], do, sv["y"], sv["lse"], sv["sinks"],
                                              SWA_MAX_DIST, rep, 1, tag + "_attn")
            parts = [dq, dkd, dvd, dgate]
            dws = [_matmul_tn(hs[i], part, f"{tag}_dwin{t}", out_dtype=BF16) for t, part in enumerate(parts)]
            g_full["b_w_in"] = jnp.concatenate(
                [dws[0], _fold_heads(dws[1], n_kv), _fold_heads(dws[2], n_kv), dws[3]], axis=1)[None]
            dhs = [_matmul_cat(parts, wb_ext_t, F32, tag + "_dh")]
        else:
            dgate, *dos = _merge_gate_bwd(dz, sv["y"], sv["gate"], sv["lses_tok"], tag + "_merge")
            y_bf = sv["y"]
            dws, dhs = [], []
            for gi, (window, dil) in enumerate(DILATED_PAIRS):
                qv, kv, vv = sv["qkv"][gi]
                grads = _attn_bwd(qv, kv, vv, _perm(dos[gi], dil), _perm(y_bf, dil), sv["lses"][gi], None,
                                  window // dil, 1, dil, f"{tag}_attn{gi}")
                dws += [_matmul_tn(sv["h_perm"][gi], part, f"{tag}_dwin{gi}{'qkv'[t]}", out_dtype=BF16)
                        for t, part in enumerate(grads)]
                dhs.append(_unperm(_matmul_cat(grads, wc_t[3 * gi * e:3 * (gi + 1) * e], F32, f"{tag}_dh{gi}"),
                                   dil))
            dws.append(_matmul_tn(hs[i], dgate, tag + "_dwin_gate", out_dtype=BF16))
            dhs.append(_matmul(dgate, wc_t[9 * e:], F32, tag + "_dh_gate"))
            g_full["c_w_in"] = jnp.concatenate(dws, axis=1)[None]
        dx, dxb, d_norm[i] = _rmsnorm_bwd(dhs, xs[i], norm_g[i:i + 1], dx, tag + "_norm")
    grad_x = dx.reshape(x.shape)
    for name in ("w_out", "a_w_in", "a_w_group"):
        g_full[name] = jnp.stack(g_full[name], axis=0)

    gpack = jnp.concatenate(
        [_to_stack(g_full[n], axis).astype(BF16).reshape(N_DEV, -1, PACK_W) for n, axis in _BIG], axis=1)
    gpack = _pad_rows(gpack, 16)
    loss_local = (0.5 / d) * jnp.sum(loss_vec)
    small = [jnp.concatenate(d_norm, axis=0), dfinal, d_sinks, jnp.concatenate(d_scale, axis=0),
             loss_local.reshape(1, 1)]
    small_rows = [_rows(a) for a in small]
    small_offs = [sum(r.shape[0] for r in small_rows[:k]) for k in range(len(small_rows) + 1)]
    small_pack = _pad_rows(jnp.concatenate(small_rows, axis=0), 8)
    core = lax.axis_index("c").astype(jnp.int32).reshape(1)
    from_sibling = _sibling_exchange(gpack, "exchange_sibling")
    chip_sums = _pair_sum(gpack, from_sibling, core, "sum_pair")
    grecv, srecv = _chip_exchange(chip_sums, small_pack, "exchange_chips")
    gsum = _sum_slots(grecv, "sum_grads")
    ssum = _sum_slots(srecv, "sum_small")

    def small_part(k, like):
        return ssum[small_offs[k]:small_offs[k + 1]].reshape(-1)[:like.size].reshape(like.shape)

    g_norm = small_part(0, norm_g)
    g_final = small_part(1, final_g)
    g_sinks = small_part(2, b_sinks)
    g_scale_full = small_part(3, scale_full)
    loss = ssum[small_offs[4], 0]
    g_scale = lax.dynamic_slice_in_dim(g_scale_full, me * a_scale.shape[1], a_scale.shape[1], axis=1)

    small_w = [("norm_g", norm_g, m_norm_g, v_norm_g, g_norm), ("final_g", final_g, m_final_g, v_final_g, g_final),
               ("a_scale", a_scale, m_a_scale, v_a_scale, g_scale), ("b_sinks", b_sinks, m_b_sinks, v_b_sinks, g_sinks)]
    big_rows = lambda tree: _pad_rows(jnp.concatenate([_rows(tree[n]) for n, _ in _BIG], axis=0), 16)
    tail = lambda idx: [_rows(t[idx]) for t in small_w]
    tail_sizes = [r.shape[0] for r in tail(1)]
    tail_offs = [r_pack + sum(tail_sizes[:k]) for k in range(len(tail_sizes) + 1)]
    w_all = _pad_rows(jnp.concatenate([big_rows(local)] + tail(1), axis=0), ADAM_ROWS)
    m_all = _pad_rows(jnp.concatenate([big_rows(mom_m)] + tail(2), axis=0), ADAM_ROWS)
    v_all = _pad_rows(jnp.concatenate([big_rows(mom_v)] + tail(3), axis=0), ADAM_ROWS)
    g_all = _pad_rows(jnp.concatenate([gsum] + tail(4), axis=0), ADAM_ROWS)
    delta_all, m_new, v_new = _adamw(w_all, g_all, m_all, v_all, "adamw", ts=ADAM_ROWS)

    def unpack(packed):
        out = {}
        for k, (name, _) in enumerate(_BIG):
            out[name] = packed[offs[k]:offs[k + 1]].reshape(local[name].shape)
        for k, (name, w_, _, _, _) in enumerate(small_w):
            out[name] = packed[tail_offs[k]:tail_offs[k + 1]].reshape(-1)[:w_.size].reshape(w_.shape)
        return out

    order = ("norm_g", "final_g", "w_out", "a_w_in", "a_w_group", "a_scale", "b_w_in", "b_sinks", "c_w_in")
    grads = unpack(g_all)
    deltas, new_m, new_v = unpack(delta_all), unpack(m_new), unpack(v_new)
    return (loss, grad_x, *[grads[n] for n in order], *[deltas[n] for n in order],
            *[new_m[n] for n in order], *[new_v[n] for n in order])
```

```python
import functools

import jax
import jax.numpy as jnp
from jax import lax
from jax.experimental import pallas as pl
from jax.experimental.pallas import tpu as pltpu

F32 = jnp.float32
BF16 = jnp.bfloat16

N_DEV = 8
HEAD_DIM = 64
LANES = 128
BLOCK = 128
Q_PER_KV = 8
POOL_WINDOWS = (2, 4, 8, 16)
POOL_HALO = 16
DILATED_PAIRS = ((128, 1), (512, 4), (2048, 16))
SWA_MAX_DIST = 127
RMS_EPS = 1e-5
PACK_W = 1024
NEG = -1e30

ADAM_LR = 0.001
ADAM_B1 = 0.9
ADAM_B2 = 0.999
ADAM_EPS = 1e-08
ADAM_WD = 0.01
ADAM_STEP = 10

VMEM_LIMIT = 48 * 1024 * 1024


def _params(*sem):
    return pltpu.CompilerParams(dimension_semantics=sem if sem else None, vmem_limit_bytes=VMEM_LIMIT)


def _pick(dim, target, mult=LANES):
    if dim <= target:
        return dim
    t = target - target % mult
    while dim % t:
        t -= mult
    return t


def _sigmoid(x):
    return 1.0 / (1.0 + jnp.exp(-x))


CHIP_OFFSETS = (2, 4, 6)
ANY_SPEC = pl.BlockSpec(memory_space=pl.ANY)


def _where_am_i():
    x, y, c = lax.axis_index("x"), lax.axis_index("y"), lax.axis_index("c")
    return x, y, c, 4 * x + 2 * y + c


def _peer(x, y, c, r):
    return x ^ ((r >> 2) & 1), y ^ ((r >> 1) & 1), c ^ (r & 1)


def _gather(blocks, name):
    n = len(blocks)

    def body(*refs):
        send, recv = refs[:n], refs[n:2 * n]
        send_sems, recv_sems, local_sems = refs[2 * n:]
        x, y, c, me = _where_am_i()
        sib = _peer(x, y, c, 1)
        sib_id = me ^ 1

        def copy(k, slot_sem, src, slot, to):
            return pltpu.make_async_remote_copy(
                src_ref=src, dst_ref=recv[k].at[slot], send_sem=send_sems.at[k, slot_sem],
                recv_sem=recv_sems.at[k, slot_sem], device_id=to, device_id_type=pl.DeviceIdType.MESH)

        started = []
        for k in range(n):
            own = pltpu.make_async_copy(send[k], recv[k].at[me], local_sems.at[k])
            own.start()
            started.append(own)
        sends = []
        for k in range(n):
            sends.append(copy(k, 0, send[k], me, sib))
            for j, r in enumerate(CHIP_OFFSETS):
                sends.append(copy(k, 1 + j, send[k], me, _peer(x, y, c, r)))
        for cp in sends:
            cp.start()
        for j, r in enumerate(CHIP_OFFSETS):
            for k in range(n):
                src_id = me ^ r
                copy(k, 1 + j, send[k], src_id, sib).wait_recv()
                fwd = copy(k, 4 + j, recv[k].at[src_id], src_id, sib)
                fwd.start()
                sends.append(fwd)
        for k in range(n):
            copy(k, 0, send[k], sib_id, sib).wait_recv()
            for j, r in enumerate(CHIP_OFFSETS):
                copy(k, 4 + j, send[k], sib_id ^ r, sib).wait_recv()
        for cp in sends:
            cp.wait_send()
        for own in started:
            own.wait()

    return pl.pallas_call(
        body, name=name,
        out_shape=tuple(jax.ShapeDtypeStruct((N_DEV,) + b.shape, b.dtype) for b in blocks),
        in_specs=[ANY_SPEC] * n,
        out_specs=tuple([ANY_SPEC] * n),
        scratch_shapes=[pltpu.SemaphoreType.DMA((n, N_DEV - 1)), pltpu.SemaphoreType.DMA((n, N_DEV - 1)),
                        pltpu.SemaphoreType.DMA((n,))],
    )(*blocks)


def _sibling_exchange(stacks, name):
    n_chips = N_DEV // 2
    n = len(stacks)

    def body(*refs):
        g_refs, t_refs = refs[:n], refs[n:2 * n]
        send_sems, recv_sems = refs[2 * n:]
        x, y, c, _ = _where_am_i()
        sib = _peer(x, y, c, 1)
        copies = [pltpu.make_async_remote_copy(
            src_ref=g_refs[k].at[2 * chip + (1 - c)], dst_ref=t_refs[k].at[chip], send_sem=send_sems.at[k, chip],
            recv_sem=recv_sems.at[k, chip], device_id=sib, device_id_type=pl.DeviceIdType.MESH)
            for k in range(n) for chip in range(n_chips)]
        for cp in copies:
            cp.start()
        for cp in copies:
            cp.wait_recv()
        for cp in copies:
            cp.wait_send()

    return pl.pallas_call(
        body, name=name,
        out_shape=tuple(jax.ShapeDtypeStruct((n_chips,) + g.shape[1:], g.dtype) for g in stacks),
        in_specs=[ANY_SPEC] * n, out_specs=tuple([ANY_SPEC] * n),
        scratch_shapes=[pltpu.SemaphoreType.DMA((n, n_chips)), pltpu.SemaphoreType.DMA((n, n_chips))],
    )(*stacks)


def _chip_exchange(csums, small, name):
    n = len(csums)

    def body(*refs):
        c_refs, s_ref = refs[:n], refs[n]
        r_refs, sr_ref = refs[n + 1:2 * n + 1], refs[2 * n + 1]
        send_sems, recv_sems, small_send, small_recv, local_sems = refs[2 * n + 2:]
        x, y, c, me = _where_am_i()
        my_chip = 2 * x + y
        own = [pltpu.make_async_copy(c_refs[k].at[my_chip], r_refs[k].at[my_chip], local_sems.at[k])
               for k in range(n)]
        own.append(pltpu.make_async_copy(s_ref, sr_ref.at[me], local_sems.at[n]))
        for cp in own:
            cp.start()
        sends, recvs = [], []
        for j, r in enumerate(CHIP_OFFSETS):
            to = _peer(x, y, c, r)
            chip = my_chip ^ (r >> 1)
            for k in range(n):
                sends.append(pltpu.make_async_remote_copy(
                    src_ref=c_refs[k].at[chip], dst_ref=r_refs[k].at[my_chip], send_sem=send_sems.at[k, j],
                    recv_sem=recv_sems.at[k, j], device_id=to, device_id_type=pl.DeviceIdType.MESH))
                recvs.append(pltpu.make_async_remote_copy(
                    src_ref=c_refs[k].at[chip], dst_ref=r_refs[k].at[chip], send_sem=send_sems.at[k, j],
                    recv_sem=recv_sems.at[k, j], device_id=to, device_id_type=pl.DeviceIdType.MESH))
        for r in range(1, N_DEV):
            to = _peer(x, y, c, r)
            sends.append(pltpu.make_async_remote_copy(
                src_ref=s_ref, dst_ref=sr_ref.at[me], send_sem=small_send.at[r - 1],
                recv_sem=small_recv.at[r - 1], device_id=to, device_id_type=pl.DeviceIdType.MESH))
            recvs.append(pltpu.make_async_remote_copy(
                src_ref=s_ref, dst_ref=sr_ref.at[me ^ r], send_sem=small_send.at[r - 1],
                recv_sem=small_recv.at[r - 1], device_id=to, device_id_type=pl.DeviceIdType.MESH))
        for cp in sends:
            cp.start()
        for cp in recvs:
            cp.wait_recv()
        for cp in sends:
            cp.wait_send()
        for cp in own:
            cp.wait()

    n_off = len(CHIP_OFFSETS)
    outs = pl.pallas_call(
        body, name=name,
        out_shape=tuple(jax.ShapeDtypeStruct(cs.shape, cs.dtype) for cs in csums)
        + (jax.ShapeDtypeStruct((N_DEV,) + small.shape, small.dtype),),
        in_specs=[ANY_SPEC] * (n + 1), out_specs=tuple([ANY_SPEC] * (n + 1)),
        scratch_shapes=[pltpu.SemaphoreType.DMA((n, n_off)), pltpu.SemaphoreType.DMA((n, n_off)),
                        pltpu.SemaphoreType.DMA((N_DEV - 1,)), pltpu.SemaphoreType.DMA((N_DEV - 1,)),
                        pltpu.SemaphoreType.DMA((n + 1,))],
    )(*csums, small)
    return list(outs[:n]), outs[n]


def _pair_sum(gpack, other, core, name, ts=256):
    n_chips, r, c = other.shape
    ts = _pick(r, ts, 16)

    def body(core_ref, g_ref, o_ref, out_ref):
        del core_ref
        out_ref[...] = (g_ref[...].astype(F32) + o_ref[...].astype(F32)).astype(out_ref.dtype)

    return pl.pallas_call(
        body, name=name,
        out_shape=jax.ShapeDtypeStruct(other.shape, other.dtype),
        grid_spec=pltpu.PrefetchScalarGridSpec(
            num_scalar_prefetch=1, grid=(n_chips, r // ts),
            in_specs=[pl.BlockSpec((None, ts, c), lambda j, i, core_ref: (2 * j + core_ref[0], i, 0)),
                      pl.BlockSpec((None, ts, c), lambda j, i, core_ref: (j, i, 0))],
            out_specs=pl.BlockSpec((None, ts, c), lambda j, i, core_ref: (j, i, 0))),
        compiler_params=_params("parallel", "parallel"),
    )(core, gpack, other)


def _matmul(a, b, out_dtype, name, tm=1024, tn=1024, tk=1024):
    m, kdim = a.shape
    n = b.shape[1]
    tm, tn, tk = _pick(m, tm), _pick(n, tn), _pick(kdim, tk)
    nk = kdim // tk

    if nk == 1:
        def body(a_ref, b_ref, o_ref):
            o_ref[...] = jnp.dot(a_ref[...], b_ref[...], preferred_element_type=F32).astype(o_ref.dtype)
        scratch = []
    else:
        def body(a_ref, b_ref, o_ref, acc_ref):
            kk = pl.program_id(2)

            @pl.when(kk == 0)
            def _():
                acc_ref[...] = jnp.zeros_like(acc_ref)

            acc_ref[...] += jnp.dot(a_ref[...], b_ref[...], preferred_element_type=F32)

            @pl.when(kk == nk - 1)
            def _():
                o_ref[...] = acc_ref[...].astype(o_ref.dtype)
        scratch = [pltpu.VMEM((tm, tn), F32)]

    return pl.pallas_call(
        body, name=name,
        out_shape=jax.ShapeDtypeStruct((m, n), out_dtype),
        grid=(m // tm, n // tn, nk),
        in_specs=[pl.BlockSpec((tm, tk), lambda i, j, k: (i, k)),
                  pl.BlockSpec((tk, tn), lambda i, j, k: (k, j))],
        out_specs=pl.BlockSpec((tm, tn), lambda i, j, k: (i, j)),
        scratch_shapes=scratch,
        compiler_params=_params("parallel", "parallel", "arbitrary"),
    )(a, b)


def _matmul_cat(parts, b, out_dtype, name, tm=1024, tn=1024, tk=1024):
    m = parts[0].shape[0]
    n = b.shape[1]
    tm, tn = _pick(m, tm), _pick(n, tn)
    tk = min(_pick(p.shape[1], tk) for p in parts)
    steps = [p.shape[1] // tk for p in parts]
    assert all(p.shape[1] % tk == 0 for p in parts)
    starts = [sum(steps[:t]) for t in range(len(parts))]
    nk = sum(steps)
    n_parts = len(parts)

    def body(*refs):
        a_refs, b_ref, o_ref, acc_ref = refs[:n_parts], refs[n_parts], refs[n_parts + 1], refs[n_parts + 2]
        kk = pl.program_id(2)

        @pl.when(kk == 0)
        def _():
            acc_ref[...] = jnp.zeros_like(acc_ref)

        for t in range(n_parts):
            @pl.when(jnp.logical_and(kk >= starts[t], kk < starts[t] + steps[t]))
            def _(t=t):
                acc_ref[...] += jnp.dot(a_refs[t][...], b_ref[...], preferred_element_type=F32)

        @pl.when(kk == nk - 1)
        def _():
            o_ref[...] = acc_ref[...].astype(o_ref.dtype)

    def part_map(t):
        return lambda i, j, k: (i, jnp.clip(k - starts[t], 0, steps[t] - 1))

    return pl.pallas_call(
        body, name=name,
        out_shape=jax.ShapeDtypeStruct((m, n), out_dtype),
        grid=(m // tm, n // tn, nk),
        in_specs=[pl.BlockSpec((tm, tk), part_map(t)) for t in range(n_parts)]
        + [pl.BlockSpec((tk, tn), lambda i, j, k: (k, j))],
        out_specs=pl.BlockSpec((tm, tn), lambda i, j, k: (i, j)),
        scratch_shapes=[pltpu.VMEM((tm, tn), F32)],
        compiler_params=_params("parallel", "parallel", "arbitrary"),
    )(*parts, b)


def _matmul_tn(a, b, name, tm=1024, tn=1024, tk=1024, out_dtype=F32):
    kdim, m = a.shape
    n = b.shape[1]
    tm, tn, tk = _pick(m, tm), _pick(n, tn), _pick(kdim, tk)
    nk = kdim // tk

    def body(a_ref, b_ref, o_ref, acc_ref):
        kk = pl.program_id(2)

        @pl.when(kk == 0)
        def _():
            acc_ref[...] = jnp.zeros_like(acc_ref)

        acc_ref[...] += lax.dot_general(a_ref[...], b_ref[...], (((0,), (0,)), ((), ())),
                                        preferred_element_type=F32)

        @pl.when(kk == nk - 1)
        def _():
            o_ref[...] = acc_ref[...].astype(o_ref.dtype)

    return pl.pallas_call(
        body, name=name,
        out_shape=jax.ShapeDtypeStruct((m, n), out_dtype),
        grid=(m // tm, n // tn, nk),
        in_specs=[pl.BlockSpec((tk, tm), lambda i, j, k: (k, i)),
                  pl.BlockSpec((tk, tn), lambda i, j, k: (k, j))],
        out_specs=pl.BlockSpec((tm, tn), lambda i, j, k: (i, j)),
        scratch_shapes=[pltpu.VMEM((tm, tn), F32)],
        compiler_params=_params("parallel", "parallel", "arbitrary"),
    )(a, b)


def _grouped_matmul(a, w, name, tm=1024):
    s, e = a.shape
    ng, g, _ = w.shape
    tm = _pick(s, tm)

    def body(a_ref, w_ref, o_ref):
        o_ref[...] = jnp.dot(a_ref[...], w_ref[...], preferred_element_type=F32)

    return pl.pallas_call(
        body, name=name,
        out_shape=jax.ShapeDtypeStruct((s, e), F32),
        grid=(s // tm, ng),
        in_specs=[pl.BlockSpec((tm, g), lambda i, j: (i, j)),
                  pl.BlockSpec((None, g, g), lambda i, j: (j, 0, 0))],
        out_specs=pl.BlockSpec((tm, g), lambda i, j: (i, j)),
        compiler_params=_params("parallel", "parallel"),
    )(a, w)


def _grouped_weight_grad(a, b, ng, name, tk=1024):
    s, e = a.shape
    g = e // ng
    tk = _pick(s, tk)
    nk = s // tk

    def body(a_ref, b_ref, o_ref):
        kk = pl.program_id(1)

        @pl.when(kk == 0)
        def _():
            o_ref[...] = jnp.zeros_like(o_ref)

        o_ref[...] += lax.dot_general(a_ref[...], b_ref[...], (((0,), (0,)), ((), ())),
                                      preferred_element_type=F32)

    return pl.pallas_call(
        body, name=name,
        out_shape=jax.ShapeDtypeStruct((ng, g, g), F32),
        grid=(ng, nk),
        in_specs=[pl.BlockSpec((tk, g), lambda j, k: (k, j)),
                  pl.BlockSpec((tk, g), lambda j, k: (k, j))],
        out_specs=pl.BlockSpec((None, g, g), lambda j, k: (j, 0, 0)),
        compiler_params=_params("parallel", "arbitrary"),
    )(a, b)


def _rms(x):
    r = lax.rsqrt(jnp.mean(x * x, axis=1, keepdims=True) + RMS_EPS)
    return x * r, r


def _rmsnorm_fwd(x, g, name, ts=256):
    s, d = x.shape
    ts = _pick(s, ts, 8)

    def body(x_ref, g_ref, h_ref):
        xhat, _ = _rms(x_ref[...])
        h_ref[...] = (xhat * g_ref[...]).astype(BF16)

    return pl.pallas_call(
        body, name=name,
        out_shape=jax.ShapeDtypeStruct((s, d), BF16),
        grid=(s // ts,),
        in_specs=[pl.BlockSpec((ts, d), lambda i: (i, 0)), pl.BlockSpec((1, d), lambda i: (0, 0))],
        out_specs=pl.BlockSpec((ts, d), lambda i: (i, 0)),
        compiler_params=_params("parallel"),
    )(x, g)


def _outproj_norm(z, w, x, g, name, tm=512):
    s, e = z.shape
    d = w.shape[1]
    tm = _pick(s, tm)

    def body(z_ref, w_ref, x_ref, g_ref, xo_ref, h_ref):
        xn = x_ref[...] + jnp.dot(z_ref[...], w_ref[...], preferred_element_type=F32)
        xo_ref[...] = xn
        xhat, _ = _rms(xn)
        h_ref[...] = (xhat * g_ref[...]).astype(BF16)

    return pl.pallas_call(
        body, name=name,
        out_shape=(jax.ShapeDtypeStruct((s, d), F32), jax.ShapeDtypeStruct((s, d), BF16)),
        grid=(s // tm,),
        in_specs=[pl.BlockSpec((tm, e), lambda i: (i, 0)), pl.BlockSpec((e, d), lambda i: (0, 0)),
                  pl.BlockSpec((tm, d), lambda i: (i, 0)), pl.BlockSpec((1, d), lambda i: (0, 0))],
        out_specs=(pl.BlockSpec((tm, d), lambda i: (i, 0)), pl.BlockSpec((tm, d), lambda i: (i, 0))),
        compiler_params=_params("parallel"),
    )(z, w, x, g)


def _outproj_loss(z, w, x, g, target, name, tm=512):
    s, e = z.shape
    d = w.shape[1]
    tm = _pick(s, tm)

    def body(z_ref, w_ref, x_ref, g_ref, t_ref, dx_ref, dxb_ref, dg_ref, loss_ref):
        i = pl.program_id(0)
        xn = x_ref[...] + jnp.dot(z_ref[...], w_ref[...], preferred_element_type=F32)
        xhat, r = _rms(xn)
        gain = g_ref[...]
        diff = xhat * gain - t_ref[...]
        dout = diff * (1.0 / d)
        dxhat = dout * gain
        dx = r * (dxhat - xhat * jnp.mean(dxhat * xhat, axis=1, keepdims=True))
        dx_ref[...] = dx
        dxb_ref[...] = dx.astype(BF16)

        @pl.when(i == 0)
        def _():
            dg_ref[...] = jnp.zeros_like(dg_ref)
            loss_ref[...] = jnp.zeros_like(loss_ref)

        dg_ref[...] += jnp.sum(dout * xhat, axis=0, keepdims=True)
        loss_ref[...] += jnp.sum(diff * diff, axis=0, keepdims=True)

    row = lambda i: (i, 0)
    fixed = lambda i: (0, 0)
    return pl.pallas_call(
        body, name=name,
        out_shape=(jax.ShapeDtypeStruct((s, d), F32), jax.ShapeDtypeStruct((s, d), BF16),
                   jax.ShapeDtypeStruct((1, d), F32), jax.ShapeDtypeStruct((1, d), F32)),
        grid=(s // tm,),
        in_specs=[pl.BlockSpec((tm, e), row), pl.BlockSpec((e, d), fixed), pl.BlockSpec((tm, d), row),
                  pl.BlockSpec((1, d), fixed), pl.BlockSpec((tm, d), row)],
        out_specs=(pl.BlockSpec((tm, d), row), pl.BlockSpec((tm, d), row),
                   pl.BlockSpec((1, d), fixed), pl.BlockSpec((1, d), fixed)),
        compiler_params=_params("arbitrary"),
    )(z, w, x, g, target)


def _rmsnorm_bwd(dhs, x, g, dx_next, name, ts=256):
    s, d = x.shape
    ts = _pick(s, ts, 8)
    n_dh = len(dhs)

    def body(*refs):
        dh_refs = refs[:n_dh]
        x_ref, g_ref, dn_ref, dx_ref, dxb_ref, dg_ref = refs[n_dh:]
        i = pl.program_id(0)
        xhat, r = _rms(x_ref[...])
        dh_ = dh_refs[0][...]
        for extra in dh_refs[1:]:
            dh_ = dh_ + extra[...]
        dxhat = dh_ * g_ref[...]
        dx = dn_ref[...] + r * (dxhat - xhat * jnp.mean(dxhat * xhat, axis=1, keepdims=True))
        dx_ref[...] = dx
        dxb_ref[...] = dx.astype(BF16)

        @pl.when(i == 0)
        def _():
            dg_ref[...] = jnp.zeros_like(dg_ref)

        dg_ref[...] += jnp.sum(dh_ * xhat, axis=0, keepdims=True)

    row = lambda i: (i, 0)
    fixed = lambda i: (0, 0)
    return pl.pallas_call(
        body, name=name,
        out_shape=(jax.ShapeDtypeStruct((s, d), F32), jax.ShapeDtypeStruct((s, d), BF16),
                   jax.ShapeDtypeStruct((1, d), F32)),
        grid=(s // ts,),
        in_specs=[pl.BlockSpec((ts, d), row)] * n_dh + [pl.BlockSpec((ts, d), row), pl.BlockSpec((1, d), fixed),
                                                        pl.BlockSpec((ts, d), row)],
        out_specs=(pl.BlockSpec((ts, d), row), pl.BlockSpec((ts, d), row), pl.BlockSpec((1, d), fixed)),
        compiler_params=_params("arbitrary"),
    )(*dhs, x, g, dx_next)


def _pool_counts(t0, rows, cols, window):
    t = t0 + lax.broadcasted_iota(jnp.int32, (rows, cols), 0)
    return jnp.minimum(t + 1, window).astype(F32)


def _pool_fwd(u, name, ts=1024, tc=256):
    s, e = u.shape
    ng = len(POOL_WINDOWS)
    gdim = e // ng
    ts, tc = _pick(s, ts), _pick(gdim, tc)
    cpg = gdim // tc
    hb = ts // POOL_HALO

    def body(u_ref, halo_ref, d_ref):
        i, grp = pl.program_id(0), pl.program_id(1)
        cur = u_ref[...]
        halo = jnp.where(i > 0, halo_ref[...], 0.0)
        ext = jnp.concatenate([halo, cur], axis=0)
        for gi, window in enumerate(POOL_WINDOWS):
            @pl.when(grp == gi)
            def _(window=window):
                acc = ext
                k = 1
                while k < window:
                    acc = acc + pltpu.roll(acc, k, 0)
                    k *= 2
                pooled = acc[POOL_HALO:, :] / _pool_counts(i * ts, ts, tc, window)
                d_ref[...] = (pooled - cur).astype(BF16)

    return pl.pallas_call(
        body, name=name,
        out_shape=jax.ShapeDtypeStruct((s, e), BF16),
        grid=(s // ts, ng, cpg),
        in_specs=[pl.BlockSpec((ts, tc), lambda i, g, j: (i, g * cpg + j)),
                  pl.BlockSpec((POOL_HALO, tc), lambda i, g, j: (jnp.maximum(i * hb - 1, 0), g * cpg + j))],
        out_specs=pl.BlockSpec((ts, tc), lambda i, g, j: (i, g * cpg + j)),
        compiler_params=_params("parallel", "parallel", "parallel"),
    )(u, u)


def _pool_bwd(dd, name, ts=1024, tc=256):
    s, e = dd.shape
    ng = len(POOL_WINDOWS)
    gdim = e // ng
    ts, tc = _pick(s, ts), _pick(gdim, tc)
    cpg = gdim // tc
    hb = ts // POOL_HALO
    n_halo = s // POOL_HALO
    nst = s // ts

    def body(dd_ref, halo_ref, du_ref):
        i, grp = pl.program_id(0), pl.program_id(1)
        cur = dd_ref[...]
        halo = jnp.where(i < nst - 1, halo_ref[...], 0.0)
        ext = jnp.concatenate([cur, halo], axis=0)
        rows = ts + POOL_HALO
        for gi, window in enumerate(POOL_WINDOWS):
            @pl.when(grp == gi)
            def _(window=window):
                acc = ext / _pool_counts(i * ts, rows, tc, window)
                k = 1
                while k < window:
                    acc = acc + pltpu.roll(acc, rows - k, 0)
                    k *= 2
                du_ref[...] = (acc[:ts, :] - cur).astype(BF16)

    return pl.pallas_call(
        body, name=name,
        out_shape=jax.ShapeDtypeStruct((s, e), BF16),
        grid=(nst, ng, cpg),
        in_specs=[pl.BlockSpec((ts, tc), lambda i, g, j: (i, g * cpg + j)),
                  pl.BlockSpec((POOL_HALO, tc),
                               lambda i, g, j: (jnp.minimum((i + 1) * hb, n_halo - 1), g * cpg + j))],
        out_specs=pl.BlockSpec((ts, tc), lambda i, g, j: (i, g * cpg + j)),
        compiler_params=_params("parallel", "parallel", "parallel"),
    )(dd, dd)


def _a_group_fwd(d, w, scale, gate, name, tm=1024):
    s, e = d.shape
    ng, g, _ = w.shape
    tm = _pick(s, tm)

    def body(d_ref, w_ref, s_ref, gate_ref, yr_ref, z_ref):
        yr = jnp.dot(d_ref[...], w_ref[...], preferred_element_type=F32)
        yr_ref[...] = yr.astype(yr_ref.dtype)
        gt = gate_ref[...].astype(F32)
        z_ref[...] = ((yr * s_ref[...]) * (gt * _sigmoid(gt))).astype(BF16)

    blk = lambda i, j: (i, j)
    return pl.pallas_call(
        body, name=name,
        out_shape=(jax.ShapeDtypeStruct((s, e), BF16), jax.ShapeDtypeStruct((s, e), BF16)),
        grid=(s // tm, ng),
        in_specs=[pl.BlockSpec((tm, g), blk), pl.BlockSpec((None, g, g), lambda i, j: (j, 0, 0)),
                  pl.BlockSpec((1, g), lambda i, j: (0, j)), pl.BlockSpec((tm, g), blk)],
        out_specs=(pl.BlockSpec((tm, g), blk), pl.BlockSpec((tm, g), blk)),
        compiler_params=_params("parallel", "parallel"),
    )(d, w, scale, gate)


def _a_gate_bwd(dz, yr, gate, scale, name, ts=512, tc=512):
    s, e = dz.shape
    ts, tc = _pick(s, ts), _pick(e, tc)

    def body(dz_ref, yr_ref, gate_ref, s_ref, dgate_ref, dyr_ref, dscale_ref):
        i = pl.program_id(1)
        dz_, yr_, gt, sc = dz_ref[...], yr_ref[...].astype(F32), gate_ref[...].astype(F32), s_ref[...]
        sg = _sigmoid(gt)
        dy = dz_ * (gt * sg)
        dgate_ref[...] = (dz_ * (yr_ * sc) * (sg * (1.0 + gt * (1.0 - sg)))).astype(BF16)
        dyr_ref[...] = (dy * sc).astype(BF16)

        @pl.when(i == 0)
        def _():
            dscale_ref[...] = jnp.zeros_like(dscale_ref)

        dscale_ref[...] += jnp.sum(dy * yr_, axis=0, keepdims=True)

    blk = lambda j, i: (i, j)
    vec = lambda j, i: (0, j)
    return pl.pallas_call(
        body, name=name,
        out_shape=(jax.ShapeDtypeStruct((s, e), BF16), jax.ShapeDtypeStruct((s, e), BF16),
                   jax.ShapeDtypeStruct((1, e), F32)),
        grid=(e // tc, s // ts),
        in_specs=[pl.BlockSpec((ts, tc), blk), pl.BlockSpec((ts, tc), blk), pl.BlockSpec((ts, tc), blk),
                  pl.BlockSpec((1, tc), vec)],
        out_specs=(pl.BlockSpec((ts, tc), blk), pl.BlockSpec((ts, tc), blk), pl.BlockSpec((1, tc), vec)),
        compiler_params=_params("parallel", "arbitrary"),
    )(dz, yr, gate, scale)


def _gate_fwd(y, gate, name, ts=512, tc=512):
    s, e = y.shape
    ts, tc = _pick(s, ts), _pick(e, tc)

    def body(y_ref, gate_ref, z_ref):
        gt = gate_ref[...].astype(F32)
        z_ref[...] = (y_ref[...].astype(F32) * (gt * _sigmoid(gt))).astype(BF16)

    blk = lambda i, j: (i, j)
    return pl.pallas_call(
        body, name=name,
        out_shape=jax.ShapeDtypeStruct((s, e), BF16),
        grid=(s // ts, e // tc),
        in_specs=[pl.BlockSpec((ts, tc), blk)] * 2,
        out_specs=pl.BlockSpec((ts, tc), blk),
        compiler_params=_params("parallel", "parallel"),
    )(y, gate)


def _gate_bwd(dz, y, gate, name, ts=512, tc=512):
    s, e = dz.shape
    ts, tc = _pick(s, ts), _pick(e, tc)

    def body(dz_ref, y_ref, gate_ref, dgate_ref, dy_ref):
        dz_, gt = dz_ref[...], gate_ref[...].astype(F32)
        sg = _sigmoid(gt)
        dgate_ref[...] = (dz_ * y_ref[...].astype(F32) * (sg * (1.0 + gt * (1.0 - sg)))).astype(BF16)
        dy_ref[...] = (dz_ * (gt * sg)).astype(BF16)

    blk = lambda i, j: (i, j)
    return pl.pallas_call(
        body, name=name,
        out_shape=(jax.ShapeDtypeStruct((s, e), BF16), jax.ShapeDtypeStruct((s, e), BF16)),
        grid=(s // ts, e // tc),
        in_specs=[pl.BlockSpec((ts, tc), blk)] * 3,
        out_specs=(pl.BlockSpec((ts, tc), blk), pl.BlockSpec((ts, tc), blk)),
        compiler_params=_params("parallel", "parallel"),
    )(dz, y, gate)


def _merge_weights(l0, l1, l2):
    m = jnp.maximum(jnp.maximum(l0, l1), l2)
    e0, e1, e2 = jnp.exp(l0 - m), jnp.exp(l1 - m), jnp.exp(l2 - m)
    inv = 1.0 / (e0 + e1 + e2)
    return e0 * inv, e1 * inv, e2 * inv


def _merge_gate_fwd(outs, lses, gate, name, ts=512, tc=512):
    s, e = gate.shape
    ts, tc = _pick(s, ts), _pick(e, tc)

    def body(o0, o1, o2, l0, l1, l2, gate_ref, y_ref, z_ref):
        w0, w1, w2 = _merge_weights(l0[...], l1[...], l2[...])
        y = w0 * o0[...].astype(F32) + w1 * o1[...].astype(F32) + w2 * o2[...].astype(F32)
        y_ref[...] = y.astype(y_ref.dtype)
        gt = gate_ref[...].astype(F32)
        z_ref[...] = (y * (gt * _sigmoid(gt))).astype(BF16)

    blk = lambda i, j: (i, j)
    return pl.pallas_call(
        body, name=name,
        out_shape=(jax.ShapeDtypeStruct((s, e), BF16), jax.ShapeDtypeStruct((s, e), BF16)),
        grid=(s // ts, e // tc),
        in_specs=[pl.BlockSpec((ts, tc), blk)] * 7,
        out_specs=(pl.BlockSpec((ts, tc), blk), pl.BlockSpec((ts, tc), blk)),
        compiler_params=_params("parallel", "parallel"),
    )(*outs, *lses, gate)


def _merge_gate_bwd(dz, y, gate, lses, name, ts=512, tc=512):
    s, e = dz.shape
    ts, tc = _pick(s, ts), _pick(e, tc)

    def body(dz_ref, y_ref, gate_ref, l0, l1, l2, dgate_ref, d0, d1, d2):
        dz_, gt = dz_ref[...], gate_ref[...].astype(F32)
        sg = _sigmoid(gt)
        dgate_ref[...] = (dz_ * y_ref[...].astype(F32) * (sg * (1.0 + gt * (1.0 - sg)))).astype(BF16)
        dy = dz_ * (gt * sg)
        w0, w1, w2 = _merge_weights(l0[...], l1[...], l2[...])
        d0[...] = (w0 * dy).astype(BF16)
        d1[...] = (w1 * dy).astype(BF16)
        d2[...] = (w2 * dy).astype(BF16)

    blk = lambda i, j: (i, j)
    return pl.pallas_call(
        body, name=name,
        out_shape=tuple([jax.ShapeDtypeStruct((s, e), BF16)] * 4),
        grid=(s // ts, e // tc),
        in_specs=[pl.BlockSpec((ts, tc), blk)] * 6,
        out_specs=tuple([pl.BlockSpec((ts, tc), blk)] * 4),
        compiler_params=_params("parallel", "parallel"),
    )(dz, y, gate, *lses)


def _band(max_dist, width):
    row = lax.broadcasted_iota(jnp.int32, (2 * BLOCK, width), 0) & (BLOCK - 1)
    col = lax.broadcasted_iota(jnp.int32, (2 * BLOCK, width), 1)
    low = row if max_dist == BLOCK else row + 1
    return jnp.logical_and(col >= low, col <= row + BLOCK), col >= BLOCK


def _fill_bias(bias_ref, max_dist):
    band, own = _band(max_dist, 2 * BLOCK)
    bias_ref[0] = jnp.where(band, 0.0, NEG)
    bias_ref[1] = jnp.where(jnp.logical_and(band, own), 0.0, NEG)


def _aligned(v):
    return v if isinstance(v, int) else pl.multiple_of(v, BLOCK)


def _stack_heads(x, lo):
    return jnp.concatenate([jnp.where(lo, x, 0.0), jnp.where(lo, 0.0, x)], axis=0).astype(BF16)


def _unstack_heads(x2, lo):
    return jnp.where(lo, x2[:BLOCK], x2[BLOCK:])


def _head_col(x, hm):
    return jnp.max(jnp.where(hm, x, NEG), axis=1, keepdims=True)


def _dot_nt(a, b):
    return lax.dot_general(a, b, (((1,), (1,)), ((), ())), preferred_element_type=F32)


def _dot_tn(a, b):
    return lax.dot_general(a, b, (((0,), (0,)), ((), ())), preferred_element_type=F32)


def _stream_view(a, dil):
    s, w = a.shape
    return a.reshape(s // (BLOCK * dil), dil, BLOCK, w)


def _fill_window(dst, halo_ref, cur_ref, n):
    dst[0:BLOCK, :] = halo_ref[0]
    for jc in range(n):
        dst[(jc + 1) * BLOCK:(jc + 2) * BLOCK, :] = cur_ref[jc]


def _attn_fwd(q, k, v, sinks, max_dist, rep, dil, out_dtype, name, tq=2048):
    assert max_dist in (BLOCK - 1, BLOCK)
    s, w = q.shape
    l = s // dil
    n_pairs = w // LANES
    tq = _pick(l, tq)
    n = tq // BLOCK
    has_sink = sinks is not None
    scale = HEAD_DIM ** -0.5

    def body(*refs):
        if has_sink:
            sink_ref, refs = refs[0], refs[1:]
        q_ref, kc_ref, kh_ref, vc_ref, vh_ref, o_ref, lse_ref, kx, vx, bias_ref = refs
        i, p = pl.program_id(0), pl.program_id(2)
        _fill_window(kx, kh_ref, kc_ref, n)
        _fill_window(vx, vh_ref, vc_ref, n)
        lo = lax.broadcasted_iota(jnp.int32, (BLOCK, LANES), 1) < HEAD_DIM
        _fill_bias(bias_ref, max_dist)
        top = lax.broadcasted_iota(jnp.int32, (2 * BLOCK, 1), 0) < BLOCK

        def scores(j):
            r0 = _aligned(j * BLOCK)
            q2 = _stack_heads(q_ref[j].astype(F32) * scale, lo)
            first = jnp.logical_and(i == 0, j == 0).astype(jnp.int32)
            return _dot_nt(q2, kx[pl.ds(r0, 2 * BLOCK), :]) + bias_ref[first]

        per_step = 2 if n % 2 == 0 else 1

        def step(jj, carry):
            nxt = tuple(scores(jnp.minimum((jj + 1) * per_step + t, n - 1)) for t in range(per_step))
            for t in range(per_step):
                finish(jj * per_step + t, carry[t])
            return nxt

        def finish(j, s2):
            r0 = _aligned(j * BLOCK)
            vw = vx[pl.ds(r0, 2 * BLOCK), :]
            m = jnp.max(s2, axis=1, keepdims=True)
            if has_sink:
                sk = jnp.where(top, sink_ref[2 * p], sink_ref[2 * p + 1])
                m = jnp.maximum(m, sk)
            pr = jnp.exp(s2 - m)
            den = jnp.sum(pr, axis=1, keepdims=True)
            if has_sink:
                den = den + jnp.exp(sk - m)
            o2 = jnp.dot(pr.astype(BF16), vw, preferred_element_type=F32) * (1.0 / den)
            lse2 = m + jnp.log(den)
            o_ref[j] = _unstack_heads(o2, lo).astype(o_ref.dtype)
            lse_ref[j] = _unstack_heads(lse2, lo)

        lax.fori_loop(0, n // per_step, step, tuple(scores(t) for t in range(per_step)))

    cur = lambda i, r, p: (i, r, 0, p)
    kv_cur = lambda i, r, p: (i, r, 0, p // rep)
    kv_halo = lambda i, r, p: (jnp.maximum(i * n - 1, 0), r, 0, p // rep)
    big, small = (n, None, BLOCK, LANES), (1, None, BLOCK, LANES)
    in_specs = [pl.BlockSpec(big, cur), pl.BlockSpec(big, kv_cur), pl.BlockSpec(small, kv_halo),
                pl.BlockSpec(big, kv_cur), pl.BlockSpec(small, kv_halo)]
    q4, k4, v4 = _stream_view(q, dil), _stream_view(k, dil), _stream_view(v, dil)
    args = [q4, k4, k4, v4, v4]
    if has_sink:
        in_specs = [pl.BlockSpec(memory_space=pltpu.SMEM)] + in_specs
        args = [sinks] + args
    o4, lse4 = pl.pallas_call(
        body, name=name,
        out_shape=(jax.ShapeDtypeStruct(q4.shape, out_dtype), jax.ShapeDtypeStruct(q4.shape, F32)),
        grid=(l // tq, dil, n_pairs),
        in_specs=in_specs,
        out_specs=(pl.BlockSpec(big, cur), pl.BlockSpec(big, cur)),
        scratch_shapes=[pltpu.VMEM((tq + BLOCK, LANES), BF16), pltpu.VMEM((tq + BLOCK, LANES), BF16),
                        pltpu.VMEM((2, 2 * BLOCK, 2 * BLOCK), F32)],
        compiler_params=_params("parallel", "parallel", "parallel"),
    )(*args)
    return o4.reshape(s, w), lse4.reshape(s, w)


def _attn_bwd(q, k, v, do, y, lse, sinks, max_dist, rep, dil, name, tq=2048):
    s, w = q.shape
    l = s // dil
    n_pairs = w // LANES
    tq = _pick(l, tq)
    n = tq // BLOCK
    n_blk = l // BLOCK
    n_sb = l // tq
    has_sink = sinks is not None
    scale = HEAD_DIM ** -0.5
    kv_dtype = BF16
    ext = tq + BLOCK

    def body(*refs):
        if has_sink:
            sink_ref, refs = refs[0], refs[1:]
        (q_ref, qn_ref, kc_ref, kh_ref, vc_ref, vh_ref, do_ref, don_ref, y_ref, yn_ref,
         lse_ref, lsen_ref) = refs[:12]
        refs = refs[12:]
        dq_ref, dk_ref, dv_ref = refs[:3]
        refs = refs[3:]
        if has_sink:
            dsink_ref, refs = refs[0], refs[1:]
        kx, vx, dkx, dvx, bias_ref = refs[:5]
        if rep > 1:
            dk_acc, dv_acc = refs[5:]
        i, p = pl.program_id(0), pl.program_id(2)
        _fill_bias(bias_ref, max_dist)
        own_rows = (q_ref, do_ref, y_ref, lse_ref)
        next_rows = (qn_ref, don_ref, yn_ref, lsen_ref)
        _fill_window(kx, kh_ref, kc_ref, n)
        _fill_window(vx, vh_ref, vc_ref, n)
        dkx[...] = jnp.zeros_like(dkx)
        dvx[...] = jnp.zeros_like(dvx)
        lo = lax.broadcasted_iota(jnp.int32, (BLOCK, LANES), 1) < HEAD_DIM
        hi = jnp.logical_not(lo)
        top = lax.broadcasted_iota(jnp.int32, (2 * BLOCK, 1), 0) < BLOCK

        def rows_of(j):
            if isinstance(j, int) and j == n:
                return next_rows, 0
            return own_rows, j

        def front(j, width):
            (qr, dor, _, _), jb = rows_of(j)
            r0 = _aligned(j * BLOCK)
            first = jnp.logical_and(i == 0, j == 0).astype(jnp.int32)
            q2 = _stack_heads(qr[jb].astype(F32) * scale, lo)
            do2 = _stack_heads(dor[jb].astype(F32), lo)
            s2 = _dot_nt(q2, kx[pl.ds(r0, width), :]) + bias_ref[first, :, pl.ds(0, width)]
            return s2, _dot_nt(do2, vx[pl.ds(r0, width), :])

        def back(j, width, q_valid, s2, dp2, sink_acc):
            (qr, dor, yr, lser), jb = rows_of(j)
            r0 = _aligned(j * BLOCK)
            dof = dor[jb].astype(F32)
            yb, lseb = yr[jb].astype(F32), lser[jb]
            q2 = _stack_heads(qr[jb].astype(F32) * scale, lo)
            do2 = _stack_heads(dof, lo)
            prod = dof * yb
            delta = jnp.concatenate([jnp.sum(jnp.where(lo, prod, 0.0), axis=1, keepdims=True),
                                     jnp.sum(jnp.where(lo, 0.0, prod), axis=1, keepdims=True)], axis=0)
            lse2 = jnp.concatenate([_head_col(lseb, lo), _head_col(lseb, hi)], axis=0)
            pr = jnp.exp(s2 - lse2)
            if q_valid is not True:
                pr = jnp.where(q_valid, pr, 0.0)
            ds = pr * (dp2 - delta)
            dkx[pl.ds(r0, width), :] += jnp.dot(ds.astype(BF16).T, q2, preferred_element_type=F32)
            dvx[pl.ds(r0, width), :] += jnp.dot(pr.astype(BF16).T, do2, preferred_element_type=F32)
            if width == 2 * BLOCK:
                dq2 = jnp.dot(ds.astype(BF16), kx[pl.ds(r0, width), :], preferred_element_type=F32) * scale
                dq_ref[jb] = _unstack_heads(dq2, lo).astype(dq_ref.dtype)
            if has_sink:
                sk = jnp.where(top, sink_ref[2 * p], sink_ref[2 * p + 1])
                sink_acc = sink_acc - jnp.exp(sk - lse2) * delta
            return sink_acc

        per_step = 2 if n % 2 == 0 else 1

        def step(jj, sink_acc):
            fronts = [front(jj * per_step + t, 2 * BLOCK) for t in range(per_step)]
            for t in range(per_step):
                sink_acc = back(jj * per_step + t, 2 * BLOCK, True, *fronts[t], sink_acc)
            return sink_acc

        zero_col = jnp.zeros((2 * BLOCK, 1), F32)
        sink_acc = lax.fori_loop(0, n // per_step, step, zero_col)
        if n_sb > 1:
            back(n, BLOCK, i < n_sb - 1, *front(n, BLOCK), zero_col)

        def write_out(dk_src, dv_src, first_row):
            for jc in range(n):
                rows = slice(first_row + jc * BLOCK, first_row + (jc + 1) * BLOCK)
                dk_ref[jc] = dk_src[rows, :].astype(dk_ref.dtype)
                dv_ref[jc] = dv_src[rows, :].astype(dv_ref.dtype)

        if rep == 1:
            write_out(dkx, dvx, BLOCK)
        else:
            @pl.when(p % rep == 0)
            def _():
                dk_acc[...] = dkx[BLOCK:, :]
                dv_acc[...] = dvx[BLOCK:, :]

            @pl.when(p % rep != 0)
            def _():
                dk_acc[...] += dkx[BLOCK:, :]
                dv_acc[...] += dvx[BLOCK:, :]

            pl.when(p % rep == rep - 1)(lambda: write_out(dk_acc, dv_acc, 0))
        if has_sink:
            rowi = lax.broadcasted_iota(jnp.int32, (8, LANES), 0)
            s0 = jnp.sum(sink_acc[:BLOCK], axis=0, keepdims=True)
            s1 = jnp.sum(sink_acc[BLOCK:], axis=0, keepdims=True)
            dsink_ref[...] = jnp.where(rowi == 0, s0, jnp.where(rowi == 1, s1, 0.0))

    cur = lambda i, r, p: (i, r, 0, p)
    nxt = lambda i, r, p: (jnp.minimum((i + 1) * n, n_blk - 1), r, 0, p)
    kv_cur = lambda i, r, p: (i, r, 0, p // rep)
    kv_halo = lambda i, r, p: (jnp.maximum(i * n - 1, 0), r, 0, p // rep)
    big, small = (n, None, BLOCK, LANES), (1, None, BLOCK, LANES)
    in_specs = [pl.BlockSpec(big, cur), pl.BlockSpec(small, nxt),
                pl.BlockSpec(big, kv_cur), pl.BlockSpec(small, kv_halo),
                pl.BlockSpec(big, kv_cur), pl.BlockSpec(small, kv_halo),
                pl.BlockSpec(big, cur), pl.BlockSpec(small, nxt),
                pl.BlockSpec(big, cur), pl.BlockSpec(small, nxt),
                pl.BlockSpec(big, cur), pl.BlockSpec(small, nxt)]
    q4, k4, v4, do4, y4, lse4 = [_stream_view(a, dil) for a in (q, k, v, do, y, lse)]
    args = [q4, q4, k4, k4, v4, v4, do4, do4, y4, y4, lse4, lse4]
    out_shape = [jax.ShapeDtypeStruct(q4.shape, BF16),
                 jax.ShapeDtypeStruct(k4.shape, kv_dtype), jax.ShapeDtypeStruct(v4.shape, kv_dtype)]
    out_specs = [pl.BlockSpec(big, cur), pl.BlockSpec(big, kv_cur), pl.BlockSpec(big, kv_cur)]
    if has_sink:
        in_specs = [pl.BlockSpec(memory_space=pltpu.SMEM)] + in_specs
        args = [sinks] + args
        out_shape.append(jax.ShapeDtypeStruct((n_sb, dil, n_pairs, 8, LANES), F32))
        out_specs.append(pl.BlockSpec((None, None, None, 8, LANES), lambda i, r, p: (i, r, p, 0, 0)))
    outs = pl.pallas_call(
        body, name=name,
        out_shape=tuple(out_shape),
        grid=(n_sb, dil, n_pairs),
        in_specs=in_specs,
        out_specs=tuple(out_specs),
        scratch_shapes=[pltpu.VMEM((ext, LANES), BF16), pltpu.VMEM((ext, LANES), BF16),
                        pltpu.VMEM((ext, LANES), F32), pltpu.VMEM((ext, LANES), F32),
                        pltpu.VMEM((2, 2 * BLOCK, 2 * BLOCK), F32)]
        + ([pltpu.VMEM((tq, LANES), F32), pltpu.VMEM((tq, LANES), F32)] if rep > 1 else []),
        compiler_params=_params("parallel", "parallel", "arbitrary"),
    )(*args)
    grads =[outs[0].reshape(s, w), outs[1].reshape(k.shape), outs[2].reshape(v.shape)]
    if has_sink:
        grads.append(outs[3].sum(axis=(0, 1))[:, 0:2, 0].reshape(1, 2 * n_pairs))
    return grads


def _sum_slots(recv, name, ts=256):
    nd, r, c = recv.shape
    ts = _pick(r, ts, 8)

    def body(r_ref, o_ref):
        acc = r_ref[0].astype(F32)
        for dev in range(1, nd):
            acc = acc + r_ref[dev].astype(F32)
        o_ref[...] = acc

    return pl.pallas_call(
        body, name=name,
        out_shape=jax.ShapeDtypeStruct((r, c), F32),
        grid=(r // ts,),
        in_specs=[pl.BlockSpec((nd, ts, c), lambda i: (0, i, 0))],
        out_specs=pl.BlockSpec((ts, c), lambda i: (i, 0)),
        compiler_params=_params("parallel"),
    )(recv)


def _adamw_math(w, g, m, v):
    c1 = 1.0 - ADAM_B1 ** ADAM_STEP
    c2 = 1.0 - ADAM_B2 ** ADAM_STEP
    m_ = ADAM_B1 * m + (1.0 - ADAM_B1) * g
    v_ = ADAM_B2 * v + (1.0 - ADAM_B2) * (g * g)
    return -ADAM_LR * ((m_ / c1) / (jnp.sqrt(v_ / c2) + ADAM_EPS) + ADAM_WD * w), m_, v_


def _row_tile(r, c, budget=1 << 18):
    return _pick(r, max(8, min(256, budget // c // 8 * 8)), 8)


def _adamw(w, g, m, v, name):
    r, c = w.shape
    ts = _row_tile(r, c)

    def body(w_ref, g_ref, m_ref, v_ref, d_ref, mo_ref, vo_ref):
        d_ref[...], mo_ref[...], vo_ref[...] = _adamw_math(w_ref[...], g_ref[...], m_ref[...], v_ref[...])

    blk = pl.BlockSpec((ts, c), lambda i: (i, 0))
    return pl.pallas_call(
        body, name=name,
        out_shape=tuple([jax.ShapeDtypeStruct((r, c), F32)] * 3),
        grid=(r // ts,),
        in_specs=[blk] * 4,
        out_specs=(blk, blk, blk),
        compiler_params=_params("parallel"),
    )(w, g, m, v)


def _adamw_slots(w, slots, m, v, name):
    r, c = w.shape
    nd = slots.shape[0]
    ts = _row_tile(r, c)

    def body(w_ref, s_ref, m_ref, v_ref, g_ref, d_ref, mo_ref, vo_ref):
        g = s_ref[0].astype(F32)
        for slot in range(1, nd):
            g = g + s_ref[slot].astype(F32)
        g_ref[...] = g
        d_ref[...], mo_ref[...], vo_ref[...] = _adamw_math(w_ref[...], g, m_ref[...], v_ref[...])

    blk = pl.BlockSpec((ts, c), lambda i: (i, 0))
    return pl.pallas_call(
        body, name=name,
        out_shape=tuple([jax.ShapeDtypeStruct((r, c), F32)] * 4),
        grid=(r // ts,),
        in_specs=[blk, pl.BlockSpec((nd, ts, c), lambda i: (0, i, 0)), blk, blk],
        out_specs=(blk, blk, blk, blk),
        compiler_params=_params("parallel"),
    )(w, slots, m, v)


def _rows(a):
    flat = a.reshape(-1)
    pad = (-flat.shape[0]) % PACK_W
    if pad:
        flat = jnp.concatenate([flat, jnp.zeros((pad,), flat.dtype)])
    return flat.reshape(-1, PACK_W)


def _pad_rows(a, mult):
    pad = (-a.shape[-2]) % mult
    if pad:
        widths = [(0, 0)] * (a.ndim - 2) + [(0, pad), (0, 0)]
        a = jnp.pad(a, widths)
    return a


def _to_global(stack, axis):
    moved = jnp.moveaxis(stack, 0, axis)
    shp = list(moved.shape)
    shp[axis:axis + 2] = [shp[axis] * shp[axis + 1]]
    return moved.reshape(shp)


def _to_stack(full, axis):
    shp = list(full.shape)
    shp[axis:axis + 1] = [N_DEV, shp[axis] // N_DEV]
    return jnp.moveaxis(full.reshape(shp), axis, 0)


_BIG = (("w_out", 1), ("a_w_in", 2), ("a_w_group", 2), ("b_w_in", 2), ("c_w_in", 2))


def _dup_heads(wk, n_kv):
    d = wk.shape[0]
    return jnp.tile(wk.reshape(d, n_kv, 1, HEAD_DIM), (1, 1, 2, 1)).reshape(d, n_kv * LANES)


def _fold_heads(dwk, n_kv):
    d = dwk.shape[0]
    folded = dwk.astype(F32).reshape(d, n_kv, 2, HEAD_DIM).sum(axis=2)
    return folded.reshape(d, n_kv * HEAD_DIM).astype(dwk.dtype)


def _perm(a, dil):
    if dil == 1:
        return a
    s, w = a.shape
    return a.reshape(s // (BLOCK * dil), BLOCK, dil, w).transpose(0, 2, 1, 3).reshape(s, w)


def _unperm(a, dil):
    if dil == 1:
        return a
    s, w = a.shape
    return a.reshape(s // (BLOCK * dil), dil, BLOCK, w).transpose(0, 2, 1, 3).reshape(s, w)


def kernel(x, norm_g, final_g, w_out, a_w_in, a_w_group, a_scale, b_w_in, b_sinks, c_w_in, loss_target, m_norm_g, m_final_g, m_w_out, m_a_w_in, m_a_w_group, m_a_scale, m_b_w_in, m_b_sinks, m_c_w_in, v_norm_g, v_final_g, v_w_out, v_a_w_in, v_a_w_group, v_a_scale, v_b_w_in, v_b_sinks, v_c_w_in):
    local = dict(w_out=w_out, a_w_in=a_w_in, a_w_group=a_w_group, b_w_in=b_w_in, c_w_in=c_w_in)
    mom_m = dict(w_out=m_w_out, a_w_in=m_a_w_in, a_w_group=m_a_w_group, b_w_in=m_b_w_in, c_w_in=m_c_w_in)
    mom_v = dict(w_out=v_w_out, a_w_in=v_a_w_in, a_w_group=v_a_w_group, b_w_in=v_b_w_in, c_w_in=v_c_w_in)
    s, d = x.shape[1], x.shape[2]
    depth = norm_g.shape[0]
    e = w_out.shape[1] * N_DEV
    n_heads = e // HEAD_DIM
    n_kv = n_heads // Q_PER_KV
    kv_w = n_kv * HEAD_DIM
    rep = Q_PER_KV // 2
    n_groups = len(POOL_WINDOWS)
    me = 4 * lax.axis_index("x") + 2 * lax.axis_index("y") + lax.axis_index("c")

    flat = {n: local[n].reshape(-1, local[n].shape[-1]) for n, _ in _BIG}
    spack = _pad_rows(_rows(a_scale), 8)
    *walls, sall = _gather([flat[n].astype(BF16) for n, _ in _BIG] + [spack], "gather_weights")
    full = {}
    for k, (name, axis) in enumerate(_BIG):
        full[name] = _to_global(walls[k].reshape((N_DEV,) + local[name].shape), axis)
    scale_full = _to_global(sall.reshape(N_DEV, -1)[:, :a_scale.size].reshape((N_DEV,) + a_scale.shape), 1)

    wout_t = jnp.swapaxes(full["w_out"], 1, 2)
    wa = full["a_w_in"]
    wa_t = jnp.swapaxes(wa, 1, 2)
    wg = full["a_w_group"]
    wg_t = jnp.swapaxes(wg, 2, 3)
    wb = full["b_w_in"][0]
    wb_ext = jnp.concatenate([wb[:, :e], _dup_heads(wb[:, e:e + kv_w], n_kv),
                              _dup_heads(wb[:, e + kv_w:e + 2 * kv_w], n_kv), wb[:, e + 2 * kv_w:]], axis=1)
    wb_ext_t = wb_ext.T
    kd_w = n_kv * LANES
    wc = full["c_w_in"][0]
    wc_t = wc.T

    xs, hs, zs, saved = [x.reshape(s, d)], [], [], []
    hs.append(_rmsnorm_fwd(xs[0], norm_g[0:1], "norm0"))
    loss_vec = dfinal = dx = dxb = None
    for i in range(depth):
        kind, j = i % 3, i // 3
        h = hs[i]
        tag = f"l{i}"
        if kind == 0:
            u = _matmul(h, wa[j][:, :e], F32, tag + "_in_u")
            gate = _matmul(h, wa[j][:, e:], BF16, tag + "_in_gate")
            dpool = _pool_fwd(u, tag + "_pool")
            yr, z = _a_group_fwd(dpool, wg[j], scale_full[j:j + 1], gate, tag + "_group")
            saved.append(dict(dpool=dpool, yr=yr, gate=gate))
        elif kind == 1:
            q = _matmul(h, wb_ext[:, :e], BF16, tag + "_in_q")
            kd = _matmul(h, wb_ext[:, e:e + kd_w], BF16, tag + "_in_k")
            vd = _matmul(h, wb_ext[:, e + kd_w:e + 2 * kd_w], BF16, tag + "_in_v")
            gate = _matmul(h, wb_ext[:, e + 2 * kd_w:], BF16, tag + "_in_gate")
            sinks = b_sinks[j]
            y, lse = _attn_fwd(q, kd, vd, sinks, SWA_MAX_DIST, rep, 1, BF16, tag + "_attn")
            z = _gate_fwd(y, gate, tag + "_gate")
            saved.append(dict(q=q, kd=kd, vd=vd, gate=gate, y=y, lse=lse, sinks=sinks))
        else:
            qkv, outs, lses, h_perm = [], [], [], []
            for gi, (window, dil) in enumerate(DILATED_PAIRS):
                hp = _perm(h, dil)
                trio = [_matmul(hp, wc[:, (3 * gi + t) * e:(3 * gi + t + 1) * e], BF16,
                                f"{tag}_in_{'qkv'[t]}{gi}") for t in range(3)]
                o, lse = _attn_fwd(trio[0], trio[1], trio[2], None, window // dil, 1, dil, BF16,
                                   f"{tag}_attn{gi}")
                qkv.append(trio)
                h_perm.append(hp)
                outs.append(_unperm(o, dil))
                lses.append(lse)
            gate = _matmul(h, wc[:, 9 * e:], BF16, tag + "_in_gate")
            lses_tok = [_unperm(lse, dil) for lse, (_, dil) in zip(lses, DILATED_PAIRS)]
            y, z = _merge_gate_fwd(outs, lses_tok, gate, tag + "_merge")
            saved.append(dict(qkv=qkv, lses=lses, lses_tok=lses_tok, gate=gate, y=y, h_perm=h_perm))
        zs.append(z)
        if i + 1 < depth:
            x_new, h_new = _outproj_norm(z, full["w_out"][i], xs[i], norm_g[i + 1:i + 2], tag + "_out")
            xs.append(x_new)
            hs.append(h_new)
        else:
            dx, dxb, dfinal, loss_vec = _outproj_loss(z, full["w_out"][i], xs[i], final_g.reshape(1, d),
                                                      loss_target.reshape(s, d), tag + "_out_loss")

    g_full = {"w_out": [None] * depth, "a_w_in": [None] * wa.shape[0], "a_w_group": [None] * wa.shape[0]}
    d_norm = [None] * depth
    d_scale = [None] * wa.shape[0]
    d_sinks = None
    for i in reversed(range(depth)):
        kind, j = i % 3, i // 3
        tag = f"b{i}"
        sv = saved[i]
        g_full["w_out"][i] = _matmul_tn(zs[i], dxb, tag + "_dwout", out_dtype=BF16)
        dz = _matmul(dxb, wout_t[i], F32, tag + "_dz")
        if kind == 0:
            dgate, dyr, dsc = _a_gate_bwd(dz, sv["yr"], sv["gate"], scale_full[j:j + 1], tag + "_gate")
            d_scale[j] = dsc
            dd = _grouped_matmul(dyr, wg_t[j], tag + "_dd")
            du = _pool_bwd(dd, tag + "_pool")
            g_full["a_w_group"][j] = _grouped_weight_grad(sv["dpool"], dyr, n_groups, tag + "_dwg")
            parts = [du, dgate]
            g_full["a_w_in"][j] = jnp.concatenate(
                [_matmul_tn(hs[i], part, f"{tag}_dwin{t}", out_dtype=BF16) for t, part in enumerate(parts)], axis=1)
            dhs = [_matmul_cat(parts, wa_t[j], F32, tag + "_dh")]
        elif kind == 1:
            dgate, do = _gate_bwd(dz, sv["y"], sv["gate"], tag + "_gate")
            dq, dkd, dvd, d_sinks = _attn_bwd(sv["q"], sv["kd"], sv["vd"], do, sv["y"], sv["lse"], sv["sinks"],
                                              SWA_MAX_DIST, rep, 1, tag + "_attn")
            parts = [dq, dkd, dvd, dgate]
            dws = [_matmul_tn(hs[i], part, f"{tag}_dwin{t}", out_dtype=BF16) for t, part in enumerate(parts)]
            g_full["b_w_in"] = jnp.concatenate(
                [dws[0], _fold_heads(dws[1], n_kv), _fold_heads(dws[2], n_kv), dws[3]], axis=1)[None]
            dhs = [_matmul_cat(parts, wb_ext_t, F32, tag + "_dh")]
        else:
            dgate, *dos = _merge_gate_bwd(dz, sv["y"], sv["gate"], sv["lses_tok"], tag + "_merge")
            y_bf = sv["y"]
            dws, dhs = [], []
            for gi, (window, dil) in enumerate(DILATED_PAIRS):
                qv, kv, vv = sv["qkv"][gi]
                grads = _attn_bwd(qv, kv, vv, _perm(dos[gi], dil), _perm(y_bf, dil), sv["lses"][gi], None,
                                  window // dil, 1, dil, f"{tag}_attn{gi}")
                dws += [_matmul_tn(sv["h_perm"][gi], part, f"{tag}_dwin{gi}{'qkv'[t]}", out_dtype=BF16)
                        for t, part in enumerate(grads)]
                dhs.append(_unperm(_matmul_cat(grads, wc_t[3 * gi * e:3 * (gi + 1) * e], F32, f"{tag}_dh{gi}"),
                                   dil))
            dws.append(_matmul_tn(hs[i], dgate, tag + "_dwin_gate", out_dtype=BF16))
            dhs.append(_matmul(dgate, wc_t[9 * e:], F32, tag + "_dh_gate"))
            g_full["c_w_in"] = jnp.concatenate(dws, axis=1)[None]
        dx, dxb, d_norm[i] = _rmsnorm_bwd(dhs, xs[i], norm_g[i:i + 1], dx, tag + "_norm")
    grad_x = dx.reshape(x.shape)
    for name in ("w_out", "a_w_in", "a_w_group"):
        g_full[name] = jnp.stack(g_full[name], axis=0)

    stacks = [_to_stack(g_full[n], axis).astype(BF16).reshape((N_DEV,) + flat[n].shape) for n, axis in _BIG]
    loss_local = (0.5 / d) * jnp.sum(loss_vec)
    small = [jnp.concatenate(d_norm, axis=0), dfinal, d_sinks, jnp.concatenate(d_scale, axis=0),
             loss_local.reshape(1, 1)]
    small_rows = [_rows(a) for a in small]
    small_offs = [sum(r.shape[0] for r in small_rows[:k]) for k in range(len(small_rows) + 1)]
    small_pack = _pad_rows(jnp.concatenate(small_rows, axis=0), 8)
    core = lax.axis_index("c").astype(jnp.int32).reshape(1)
    from_sibling = _sibling_exchange(stacks, "exchange_sibling")
    chip_sums = [_pair_sum(stacks[k], from_sibling[k], core, "sum_pair_" + n) for k, (n, _) in enumerate(_BIG)]
    grecv, srecv = _chip_exchange(chip_sums, small_pack, "exchange_chips")
    ssum = _sum_slots(srecv, "sum_small")

    def small_part(k, like):
        return ssum[small_offs[k]:small_offs[k + 1]].reshape(-1)[:like.size].reshape(like.shape)

    g_norm = small_part(0, norm_g)
    g_final = small_part(1, final_g)
    g_sinks = small_part(2, b_sinks)
    g_scale_full = small_part(3, scale_full)
    loss = ssum[small_offs[4], 0]
    g_scale = lax.dynamic_slice_in_dim(g_scale_full, me * a_scale.shape[1], a_scale.shape[1], axis=1)

    small_w = [("norm_g", norm_g, m_norm_g, v_norm_g, g_norm), ("final_g", final_g, m_final_g, v_final_g, g_final),
               ("a_scale", a_scale, m_a_scale, v_a_scale, g_scale), ("b_sinks", b_sinks, m_b_sinks, v_b_sinks, g_sinks)]
    tail = lambda idx: _pad_rows(jnp.concatenate([_rows(t[idx]) for t in small_w], axis=0), 8)
    tail_sizes = [_rows(t[1]).shape[0] for t in small_w]
    tail_offs = [sum(tail_sizes[:k]) for k in range(len(tail_sizes) + 1)]
    g_tail = tail(4)
    tails = (g_tail,) + _adamw(tail(1), g_tail, tail(2), tail(3), "adamw_small")
    grads, deltas, new_m, new_v = {}, {}, {}, {}
    for k, (name, w_, _, _, _) in enumerate(small_w):
        for out, packed in zip((grads, deltas, new_m, new_v), tails):
            out[name] = packed[tail_offs[k]:tail_offs[k + 1]].reshape(-1)[:w_.size].reshape(w_.shape)
    for k, (name, _) in enumerate(_BIG):
        shape2d = flat[name].shape
        res = _adamw_slots(flat[name], grecv[k], mom_m[name].reshape(shape2d), mom_v[name].reshape(shape2d),
                           "adamw_" + name)
        for out, val in zip((grads, deltas, new_m, new_v), res):
            out[name] = val.reshape(local[name].shape)

    order = ("norm_g", "final_g", "w_out", "a_w_in", "a_w_group", "a_scale", "b_w_in", "b_sinks", "c_w_in")
    return (loss, grad_x, *[grads[n] for n in order], *[deltas[n] for n in order],
            *[new_m[n] for n in order], *[new_v[n] for n in order])
```

```python
import functools

import jax
import jax.numpy as jnp
from jax import lax
from jax.experimental import pallas as pl
from jax.experimental.pallas import tpu as pltpu

F32 = jnp.float32
BF16 = jnp.bfloat16

N_DEV = 8
HEAD_DIM = 64
LANES = 128
BLOCK = 128
Q_PER_KV = 8
POOL_WINDOWS = (2, 4, 8, 16)
POOL_HALO = 16
DILATED_PAIRS = ((128, 1), (512, 4), (2048, 16))
SWA_MAX_DIST = 127
RMS_EPS = 1e-5
PACK_W = 1024
NEG = -1e30

ADAM_LR = 0.001
ADAM_B1 = 0.9
ADAM_B2 = 0.999
ADAM_EPS = 1e-08
ADAM_WD = 0.01
ADAM_STEP = 10

VMEM_LIMIT = 48 * 1024 * 1024


def _params(*sem):
    return pltpu.CompilerParams(dimension_semantics=sem if sem else None, vmem_limit_bytes=VMEM_LIMIT)


def _pick(dim, target, mult=LANES):
    if dim <= target:
        return dim
    t = target - target % mult
    while dim % t:
        t -= mult
    return t


def _sigmoid(x):
    return 1.0 / (1.0 + jnp.exp(-x))


CHIP_OFFSETS = (2, 4, 6)
ANY_SPEC = pl.BlockSpec(memory_space=pl.ANY)


def _where_am_i():
    x, y, c = lax.axis_index("x"), lax.axis_index("y"), lax.axis_index("c")
    return x, y, c, 4 * x + 2 * y + c


def _peer(x, y, c, r):
    return x ^ ((r >> 2) & 1), y ^ ((r >> 1) & 1), c ^ (r & 1)


def _gather(blocks, name):
    n = len(blocks)

    def body(*refs):
        send, recv = refs[:n], refs[n:2 * n]
        send_sems, recv_sems, local_sems = refs[2 * n:]
        x, y, c, me = _where_am_i()
        sib = _peer(x, y, c, 1)
        sib_id = me ^ 1

        def copy(k, slot_sem, src, slot, to):
            return pltpu.make_async_remote_copy(
                src_ref=src, dst_ref=recv[k].at[slot], send_sem=send_sems.at[k, slot_sem],
                recv_sem=recv_sems.at[k, slot_sem], device_id=to, device_id_type=pl.DeviceIdType.MESH)

        started = []
        for k in range(n):
            own = pltpu.make_async_copy(send[k], recv[k].at[me], local_sems.at[k])
            own.start()
            started.append(own)
        sends = []
        for k in range(n):
            sends.append(copy(k, 0, send[k], me, sib))
            for j, r in enumerate(CHIP_OFFSETS):
                sends.append(copy(k, 1 + j, send[k], me, _peer(x, y, c, r)))
        for cp in sends:
            cp.start()
        for j, r in enumerate(CHIP_OFFSETS):
            for k in range(n):
                src_id = me ^ r
                copy(k, 1 + j, send[k], src_id, sib).wait_recv()
                fwd = copy(k, 4 + j, recv[k].at[src_id], src_id, sib)
                fwd.start()
                sends.append(fwd)
        for k in range(n):
            copy(k, 0, send[k], sib_id, sib).wait_recv()
            for j, r in enumerate(CHIP_OFFSETS):
                copy(k, 4 + j, send[k], sib_id ^ r, sib).wait_recv()
        for cp in sends:
            cp.wait_send()
        for own in started:
            own.wait()

    return pl.pallas_call(
        body, name=name,
        out_shape=tuple(jax.ShapeDtypeStruct((N_DEV,) + b.shape, b.dtype) for b in blocks),
        in_specs=[ANY_SPEC] * n,
        out_specs=tuple([ANY_SPEC] * n),
        scratch_shapes=[pltpu.SemaphoreType.DMA((n, N_DEV - 1)), pltpu.SemaphoreType.DMA((n, N_DEV - 1)),
                        pltpu.SemaphoreType.DMA((n,))],
    )(*blocks)


def _sibling_exchange(stacks, name):
    n_chips = N_DEV // 2
    n = len(stacks)

    def body(*refs):
        g_refs, t_refs = refs[:n], refs[n:2 * n]
        send_sems, recv_sems = refs[2 * n:]
        x, y, c, _ = _where_am_i()
        sib = _peer(x, y, c, 1)
        copies = [pltpu.make_async_remote_copy(
            src_ref=g_refs[k].at[2 * chip + (1 - c)], dst_ref=t_refs[k].at[chip], send_sem=send_sems.at[k, chip],
            recv_sem=recv_sems.at[k, chip], device_id=sib, device_id_type=pl.DeviceIdType.MESH)
            for k in range(n) for chip in range(n_chips)]
        for cp in copies:
            cp.start()
        for cp in copies:
            cp.wait_recv()
        for cp in copies:
            cp.wait_send()

    return pl.pallas_call(
        body, name=name,
        out_shape=tuple(jax.ShapeDtypeStruct((n_chips,) + g.shape[1:], g.dtype) for g in stacks),
        in_specs=[ANY_SPEC] * n, out_specs=tuple([ANY_SPEC] * n),
        scratch_shapes=[pltpu.SemaphoreType.DMA((n, n_chips)), pltpu.SemaphoreType.DMA((n, n_chips))],
    )(*stacks)


def _chip_exchange(csums, small, name):
    n = len(csums)

    def body(*refs):
        c_refs, s_ref = refs[:n], refs[n]
        r_refs, sr_ref = refs[n + 1:2 * n + 1], refs[2 * n + 1]
        send_sems, recv_sems, small_send, small_recv, local_sems = refs[2 * n + 2:]
        x, y, c, me = _where_am_i()
        my_chip = 2 * x + y
        own = [pltpu.make_async_copy(c_refs[k].at[my_chip], r_refs[k].at[my_chip], local_sems.at[k])
               for k in range(n)]
        own.append(pltpu.make_async_copy(s_ref, sr_ref.at[me], local_sems.at[n]))
        for cp in own:
            cp.start()
        sends, recvs = [], []
        for j, r in enumerate(CHIP_OFFSETS):
            to = _peer(x, y, c, r)
            chip = my_chip ^ (r >> 1)
            for k in range(n):
                sends.append(pltpu.make_async_remote_copy(
                    src_ref=c_refs[k].at[chip], dst_ref=r_refs[k].at[my_chip], send_sem=send_sems.at[k, j],
                    recv_sem=recv_sems.at[k, j], device_id=to, device_id_type=pl.DeviceIdType.MESH))
                recvs.append(pltpu.make_async_remote_copy(
                    src_ref=c_refs[k].at[chip], dst_ref=r_refs[k].at[chip], send_sem=send_sems.at[k, j],
                    recv_sem=recv_sems.at[k, j], device_id=to, device_id_type=pl.DeviceIdType.MESH))
        for r in range(1, N_DEV):
            to = _peer(x, y, c, r)
            sends.append(pltpu.make_async_remote_copy(
                src_ref=s_ref, dst_ref=sr_ref.at[me], send_sem=small_send.at[r - 1],
                recv_sem=small_recv.at[r - 1], device_id=to, device_id_type=pl.DeviceIdType.MESH))
            recvs.append(pltpu.make_async_remote_copy(
                src_ref=s_ref, dst_ref=sr_ref.at[me ^ r], send_sem=small_send.at[r - 1],
                recv_sem=small_recv.at[r - 1], device_id=to, device_id_type=pl.DeviceIdType.MESH))
        for cp in sends:
            cp.start()
        for cp in recvs:
            cp.wait_recv()
        for cp in sends:
            cp.wait_send()
        for cp in own:
            cp.wait()

    n_off = len(CHIP_OFFSETS)
    outs = pl.pallas_call(
        body, name=name,
        out_shape=tuple(jax.ShapeDtypeStruct(cs.shape, cs.dtype) for cs in csums)
        + (jax.ShapeDtypeStruct((N_DEV,) + small.shape, small.dtype),),
        in_specs=[ANY_SPEC] * (n + 1), out_specs=tuple([ANY_SPEC] * (n + 1)),
        scratch_shapes=[pltpu.SemaphoreType.DMA((n, n_off)), pltpu.SemaphoreType.DMA((n, n_off)),
                        pltpu.SemaphoreType.DMA((N_DEV - 1,)), pltpu.SemaphoreType.DMA((N_DEV - 1,)),
                        pltpu.SemaphoreType.DMA((n + 1,))],
    )(*csums, small)
    return list(outs[:n]), outs[n]


def _pair_sum(gpack, other, core, name, ts=256):
    n_chips, r, c = other.shape
    ts = _pick(r, ts, 16)

    def body(core_ref, g_ref, o_ref, out_ref):
        del core_ref
        out_ref[...] = (g_ref[...].astype(F32) + o_ref[...].astype(F32)).astype(out_ref.dtype)

    return pl.pallas_call(
        body, name=name,
        out_shape=jax.ShapeDtypeStruct(other.shape, other.dtype),
        grid_spec=pltpu.PrefetchScalarGridSpec(
            num_scalar_prefetch=1, grid=(n_chips, r // ts),
            in_specs=[pl.BlockSpec((None, ts, c), lambda j, i, core_ref: (2 * j + core_ref[0], i, 0)),
                      pl.BlockSpec((None, ts, c), lambda j, i, core_ref: (j, i, 0))],
            out_specs=pl.BlockSpec((None, ts, c), lambda j, i, core_ref: (j, i, 0))),
        compiler_params=_params("parallel", "parallel"),
    )(core, gpack, other)


def _matmul(a, b, out_dtype, name, tm=1024, tn=1024, tk=1024):
    m, kdim = a.shape
    n = b.shape[1]
    tm, tn, tk = _pick(m, tm), _pick(n, tn), _pick(kdim, tk)
    nk = kdim // tk

    if nk == 1:
        def body(a_ref, b_ref, o_ref):
            o_ref[...] = jnp.dot(a_ref[...], b_ref[...], preferred_element_type=F32).astype(o_ref.dtype)
        scratch = []
    else:
        def body(a_ref, b_ref, o_ref, acc_ref):
            kk = pl.program_id(2)

            @pl.when(kk == 0)
            def _():
                acc_ref[...] = jnp.zeros_like(acc_ref)

            acc_ref[...] += jnp.dot(a_ref[...], b_ref[...], preferred_element_type=F32)

            @pl.when(kk == nk - 1)
            def _():
                o_ref[...] = acc_ref[...].astype(o_ref.dtype)
        scratch = [pltpu.VMEM((tm, tn), F32)]

    return pl.pallas_call(
        body, name=name,
        out_shape=jax.ShapeDtypeStruct((m, n), out_dtype),
        grid=(m // tm, n // tn, nk),
        in_specs=[pl.BlockSpec((tm, tk), lambda i, j, k: (i, k)),
                  pl.BlockSpec((tk, tn), lambda i, j, k: (k, j))],
        out_specs=pl.BlockSpec((tm, tn), lambda i, j, k: (i, j)),
        scratch_shapes=scratch,
        compiler_params=_params("parallel", "parallel", "arbitrary"),
    )(a, b)


def _matmul_cat(parts, b, out_dtype, name, tm=1024, tn=1024, tk=1024):
    m = parts[0].shape[0]
    n = b.shape[1]
    tm, tn = _pick(m, tm), _pick(n, tn)
    tk = min(_pick(p.shape[1], tk) for p in parts)
    steps = [p.shape[1] // tk for p in parts]
    assert all(p.shape[1] % tk == 0 for p in parts)
    starts = [sum(steps[:t]) for t in range(len(parts))]
    nk = sum(steps)
    n_parts = len(parts)

    def body(*refs):
        a_refs, b_ref, o_ref, acc_ref = refs[:n_parts], refs[n_parts], refs[n_parts + 1], refs[n_parts + 2]
        kk = pl.program_id(2)

        @pl.when(kk == 0)
        def _():
            acc_ref[...] = jnp.zeros_like(acc_ref)

        for t in range(n_parts):
            @pl.when(jnp.logical_and(kk >= starts[t], kk < starts[t] + steps[t]))
            def _(t=t):
                acc_ref[...] += jnp.dot(a_refs[t][...], b_ref[...], preferred_element_type=F32)

        @pl.when(kk == nk - 1)
        def _():
            o_ref[...] = acc_ref[...].astype(o_ref.dtype)

    def part_map(t):
        return lambda i, j, k: (i, jnp.clip(k - starts[t], 0, steps[t] - 1))

    return pl.pallas_call(
        body, name=name,
        out_shape=jax.ShapeDtypeStruct((m, n), out_dtype),
        grid=(m // tm, n // tn, nk),
        in_specs=[pl.BlockSpec((tm, tk), part_map(t)) for t in range(n_parts)]
        + [pl.BlockSpec((tk, tn), lambda i, j, k: (k, j))],
        out_specs=pl.BlockSpec((tm, tn), lambda i, j, k: (i, j)),
        scratch_shapes=[pltpu.VMEM((tm, tn), F32)],
        compiler_params=_params("parallel", "parallel", "arbitrary"),
    )(*parts, b)


def _matmul_tn(a, b, name, tm=1024, tn=1024, tk=1024, out_dtype=F32):
    kdim, m = a.shape
    n = b.shape[1]
    tm, tn, tk = _pick(m, tm), _pick(n, tn), _pick(kdim, tk)
    nk = kdim // tk

    def body(a_ref, b_ref, o_ref, acc_ref):
        kk = pl.program_id(2)

        @pl.when(kk == 0)
        def _():
            acc_ref[...] = jnp.zeros_like(acc_ref)

        acc_ref[...] += lax.dot_general(a_ref[...], b_ref[...], (((0,), (0,)), ((), ())),
                                        preferred_element_type=F32)

        @pl.when(kk == nk - 1)
        def _():
            o_ref[...] = acc_ref[...].astype(o_ref.dtype)

    return pl.pallas_call(
        body, name=name,
        out_shape=jax.ShapeDtypeStruct((m, n), out_dtype),
        grid=(m // tm, n // tn, nk),
        in_specs=[pl.BlockSpec((tk, tm), lambda i, j, k: (k, i)),
                  pl.BlockSpec((tk, tn), lambda i, j, k: (k, j))],
        out_specs=pl.BlockSpec((tm, tn), lambda i, j, k: (i, j)),
        scratch_shapes=[pltpu.VMEM((tm, tn), F32)],
        compiler_params=_params("parallel", "parallel", "arbitrary"),
    )(a, b)


def _grouped_weight_grad(a, b, ng, name, tk=1024):
    s, e = a.shape
    g = e // ng
    tk = _pick(s, tk)
    nk = s // tk

    def body(a_ref, b_ref, o_ref):
        kk = pl.program_id(1)

        @pl.when(kk == 0)
        def _():
            o_ref[...] = jnp.zeros_like(o_ref)

        o_ref[...] += lax.dot_general(a_ref[...], b_ref[...], (((0,), (0,)), ((), ())),
                                      preferred_element_type=F32)

    return pl.pallas_call(
        body, name=name,
        out_shape=jax.ShapeDtypeStruct((ng, g, g), F32),
        grid=(ng, nk),
        in_specs=[pl.BlockSpec((tk, g), lambda j, k: (k, j)),
                  pl.BlockSpec((tk, g), lambda j, k: (k, j))],
        out_specs=pl.BlockSpec((None, g, g), lambda j, k: (j, 0, 0)),
        compiler_params=_params("parallel", "arbitrary"),
    )(a, b)


def _rms(x):
    r = lax.rsqrt(jnp.mean(x * x, axis=1, keepdims=True) + RMS_EPS)
    return x * r, r


def _rmsnorm_fwd(x, g, name, ts=256):
    s, d = x.shape
    ts = _pick(s, ts, 8)

    def body(x_ref, g_ref, h_ref):
        xhat, _ = _rms(x_ref[...])
        h_ref[...] = (xhat * g_ref[...]).astype(BF16)

    return pl.pallas_call(
        body, name=name,
        out_shape=jax.ShapeDtypeStruct((s, d), BF16),
        grid=(s // ts,),
        in_specs=[pl.BlockSpec((ts, d), lambda i: (i, 0)), pl.BlockSpec((1, d), lambda i: (0, 0))],
        out_specs=pl.BlockSpec((ts, d), lambda i: (i, 0)),
        compiler_params=_params("parallel"),
    )(x, g)


def _outproj_norm(z, w, x, g, name, tm=512):
    s, e = z.shape
    d = w.shape[1]
    tm = _pick(s, tm)

    def body(z_ref, w_ref, x_ref, g_ref, xo_ref, h_ref):
        xn = x_ref[...] + jnp.dot(z_ref[...], w_ref[...], preferred_element_type=F32)
        xo_ref[...] = xn
        xhat, _ = _rms(xn)
        h_ref[...] = (xhat * g_ref[...]).astype(BF16)

    return pl.pallas_call(
        body, name=name,
        out_shape=(jax.ShapeDtypeStruct((s, d), F32), jax.ShapeDtypeStruct((s, d), BF16)),
        grid=(s // tm,),
        in_specs=[pl.BlockSpec((tm, e), lambda i: (i, 0)), pl.BlockSpec((e, d), lambda i: (0, 0)),
                  pl.BlockSpec((tm, d), lambda i: (i, 0)), pl.BlockSpec((1, d), lambda i: (0, 0))],
        out_specs=(pl.BlockSpec((tm, d), lambda i: (i, 0)), pl.BlockSpec((tm, d), lambda i: (i, 0))),
        compiler_params=_params("parallel"),
    )(z, w, x, g)


def _outproj_loss(z, w, x, g, target, name, tm=512):
    s, e = z.shape
    d = w.shape[1]
    tm = _pick(s, tm)

    def body(z_ref, w_ref, x_ref, g_ref, t_ref, dx_ref, dxb_ref, dg_ref, loss_ref):
        i = pl.program_id(0)
        xn = x_ref[...] + jnp.dot(z_ref[...], w_ref[...], preferred_element_type=F32)
        xhat, r = _rms(xn)
        gain = g_ref[...]
        diff = xhat * gain - t_ref[...]
        dout = diff * (1.0 / d)
        dxhat = dout * gain
        dx = r * (dxhat - xhat * jnp.mean(dxhat * xhat, axis=1, keepdims=True))
        dx_ref[...] = dx
        dxb_ref[...] = dx.astype(BF16)

        @pl.when(i == 0)
        def _():
            dg_ref[...] = jnp.zeros_like(dg_ref)
            loss_ref[...] = jnp.zeros_like(loss_ref)

        dg_ref[...] += jnp.sum(dout * xhat, axis=0, keepdims=True)
        loss_ref[...] += jnp.sum(diff * diff, axis=0, keepdims=True)

    row = lambda i: (i, 0)
    fixed = lambda i: (0, 0)
    return pl.pallas_call(
        body, name=name,
        out_shape=(jax.ShapeDtypeStruct((s, d), F32), jax.ShapeDtypeStruct((s, d), BF16),
                   jax.ShapeDtypeStruct((1, d), F32), jax.ShapeDtypeStruct((1, d), F32)),
        grid=(s // tm,),
        in_specs=[pl.BlockSpec((tm, e), row), pl.BlockSpec((e, d), fixed), pl.BlockSpec((tm, d), row),
                  pl.BlockSpec((1, d), fixed), pl.BlockSpec((tm, d), row)],
        out_specs=(pl.BlockSpec((tm, d), row), pl.BlockSpec((tm, d), row),
                   pl.BlockSpec((1, d), fixed), pl.BlockSpec((1, d), fixed)),
        compiler_params=_params("arbitrary"),
    )(z, w, x, g, target)


def _rmsnorm_bwd(dhs, x, g, dx_next, name, ts=256):
    s, d = x.shape
    ts = _pick(s, ts, 8)
    n_dh = len(dhs)

    def body(*refs):
        dh_refs = refs[:n_dh]
        x_ref, g_ref, dn_ref, dx_ref, dxb_ref, dg_ref = refs[n_dh:]
        i = pl.program_id(0)
        xhat, r = _rms(x_ref[...])
        dh_ = dh_refs[0][...]
        for extra in dh_refs[1:]:
            dh_ = dh_ + extra[...]
        dxhat = dh_ * g_ref[...]
        dx = dn_ref[...] + r * (dxhat - xhat * jnp.mean(dxhat * xhat, axis=1, keepdims=True))
        dx_ref[...] = dx
        dxb_ref[...] = dx.astype(BF16)

        @pl.when(i == 0)
        def _():
            dg_ref[...] = jnp.zeros_like(dg_ref)

        dg_ref[...] += jnp.sum(dh_ * xhat, axis=0, keepdims=True)

    row = lambda i: (i, 0)
    fixed = lambda i: (0, 0)
    return pl.pallas_call(
        body, name=name,
        out_shape=(jax.ShapeDtypeStruct((s, d), F32), jax.ShapeDtypeStruct((s, d), BF16),
                   jax.ShapeDtypeStruct((1, d), F32)),
        grid=(s // ts,),
        in_specs=[pl.BlockSpec((ts, d), row)] * n_dh + [pl.BlockSpec((ts, d), row), pl.BlockSpec((1, d), fixed),
                                                        pl.BlockSpec((ts, d), row)],
        out_specs=(pl.BlockSpec((ts, d), row), pl.BlockSpec((ts, d), row), pl.BlockSpec((1, d), fixed)),
        compiler_params=_params("arbitrary"),
    )(*dhs, x, g, dx_next)


def _pool_counts(t0, rows, cols, window):
    t = t0 + lax.broadcasted_iota(jnp.int32, (rows, cols), 0)
    return jnp.minimum(t + 1, window).astype(F32)


def _proj_pool_fwd(h, w, name, ts=1024, tc=512):
    s, dm = h.shape
    e = w.shape[1]
    ng = len(POOL_WINDOWS)
    gdim = e // ng
    ts, tc = _pick(s, ts), _pick(gdim, tc)
    cpg = gdim // tc
    hb = ts // POOL_HALO

    def body(h_ref, halo_ref, w_ref, d_ref):
        i, grp = pl.program_id(0), pl.program_id(1)
        cur = jnp.dot(h_ref[...], w_ref[...], preferred_element_type=F32)
        halo = jnp.dot(halo_ref[...], w_ref[...], preferred_element_type=F32)
        ext = jnp.concatenate([jnp.where(i > 0, halo, 0.0), cur], axis=0)
        for gi, window in enumerate(POOL_WINDOWS):
            @pl.when(grp == gi)
            def _(window=window):
                acc = ext
                k = 1
                while k < window:
                    acc = acc + pltpu.roll(acc, k, 0)
                    k *= 2
                pooled = acc[POOL_HALO:, :] / _pool_counts(i * ts, ts, tc, window)
                d_ref[...] = (pooled - cur).astype(BF16)

    return pl.pallas_call(
        body, name=name,
        out_shape=jax.ShapeDtypeStruct((s, e), BF16),
        grid=(s // ts, ng, cpg),
        in_specs=[pl.BlockSpec((ts, dm), lambda i, g, j: (i, 0)),
                  pl.BlockSpec((POOL_HALO, dm), lambda i, g, j: (jnp.maximum(i * hb - 1, 0), 0)),
                  pl.BlockSpec((dm, tc), lambda i, g, j: (0, g * cpg + j))],
        out_specs=pl.BlockSpec((ts, tc), lambda i, g, j: (i, g * cpg + j)),
        compiler_params=_params("parallel", "parallel", "parallel"),
    )(h, h, w)


def _dz_fused(dxb, w_t, tiles, vecs, n_out, epilogue, name, tm, tn, with_col_sum=False):
    s, dm = dxb.shape
    e = w_t.shape[1]
    tm, tn = _pick(s, tm), _pick(e, tn)
    n_t, n_v = len(tiles), len(vecs)

    def body(*refs):
        a_ref, b_ref = refs[:2]
        tile_refs, vec_refs = refs[2:2 + n_t], refs[2 + n_t:2 + n_t + n_v]
        out_refs = refs[2 + n_t + n_v:]
        i = pl.program_id(1)
        dz = jnp.dot(a_ref[...], b_ref[...], preferred_element_type=F32)
        res = epilogue(dz, [t[...] for t in tile_refs], [v[...] for v in vec_refs])
        for o_ref, val in zip(out_refs[:n_out], res[:n_out]):
            o_ref[...] = val.astype(o_ref.dtype)
        if with_col_sum:
            sum_ref = out_refs[n_out]

            @pl.when(i == 0)
            def _():
                sum_ref[...] = jnp.zeros_like(sum_ref)

            sum_ref[...] += jnp.sum(res[n_out], axis=0, keepdims=True)

    blk = lambda j, i: (i, j)
    vec = lambda j, i: (0, j)
    out_shape = [jax.ShapeDtypeStruct((s, e), BF16)] * n_out
    out_specs = [pl.BlockSpec((tm, tn), blk)] * n_out
    if with_col_sum:
        out_shape.append(jax.ShapeDtypeStruct((1, e), F32))
        out_specs.append(pl.BlockSpec((1, tn), vec))
    return pl.pallas_call(
        body, name=name,
        out_shape=tuple(out_shape),
        grid=(e // tn, s // tm),
        in_specs=[pl.BlockSpec((tm, dm), lambda j, i: (i, 0)), pl.BlockSpec((dm, tn), lambda j, i: (0, j))]
        + [pl.BlockSpec((tm, tn), blk)] * n_t + [pl.BlockSpec((1, tn), vec)] * n_v,
        out_specs=tuple(out_specs),
        compiler_params=_params("parallel", "arbitrary"),
    )(dxb, w_t, *tiles, *vecs)


def _group_pool_bwd(dyr, w_t, name, ts=1024, tc=512):
    s, e = dyr.shape
    ng = len(POOL_WINDOWS)
    gdim = e // ng
    ts, tc = _pick(s, ts), _pick(gdim, tc)
    cpg = gdim // tc
    hb = ts // POOL_HALO
    n_halo = s // POOL_HALO
    nst = s // ts

    def body(dy_ref, halo_ref, w_ref, du_ref):
        i, grp = pl.program_id(0), pl.program_id(1)
        cur = jnp.dot(dy_ref[...], w_ref[...], preferred_element_type=F32)
        halo = jnp.dot(halo_ref[...], w_ref[...], preferred_element_type=F32)
        ext = jnp.concatenate([cur, jnp.where(i < nst - 1, halo, 0.0)], axis=0)
        rows = ts + POOL_HALO
        for gi, window in enumerate(POOL_WINDOWS):
            @pl.when(grp == gi)
            def _(window=window):
                acc = ext / _pool_counts(i * ts, rows, tc, window)
                k = 1
                while k < window:
                    acc = acc + pltpu.roll(acc, rows - k, 0)
                    k *= 2
                du_ref[...] = (acc[:ts, :] - cur).astype(BF16)

    return pl.pallas_call(
        body, name=name,
        out_shape=jax.ShapeDtypeStruct((s, e), BF16),
        grid=(nst, ng, cpg),
        in_specs=[pl.BlockSpec((ts, gdim), lambda i, g, j: (i, g)),
                  pl.BlockSpec((POOL_HALO, gdim), lambda i, g, j: (jnp.minimum((i + 1) * hb, n_halo - 1), g)),
                  pl.BlockSpec((None, gdim, tc), lambda i, g, j: (g, 0, j))],
        out_specs=pl.BlockSpec((ts, tc), lambda i, g, j: (i, g * cpg + j)),
        compiler_params=_params("parallel", "parallel", "parallel"),
    )(dyr, dyr, w_t)


def _a_group_fwd(d, w, scale, gate, name, tm=1024):
    s, e = d.shape
    ng, g, _ = w.shape
    tm = _pick(s, tm)

    def body(d_ref, w_ref, s_ref, gate_ref, yr_ref, z_ref):
        yr = jnp.dot(d_ref[...], w_ref[...], preferred_element_type=F32)
        yr_ref[...] = yr.astype(yr_ref.dtype)
        gt = gate_ref[...].astype(F32)
        z_ref[...] = ((yr * s_ref[...]) * (gt * _sigmoid(gt))).astype(BF16)

    blk = lambda i, j: (i, j)
    return pl.pallas_call(
        body, name=name,
        out_shape=(jax.ShapeDtypeStruct((s, e), BF16), jax.ShapeDtypeStruct((s, e), BF16)),
        grid=(s // tm, ng),
        in_specs=[pl.BlockSpec((tm, g), blk), pl.BlockSpec((None, g, g), lambda i, j: (j, 0, 0)),
                  pl.BlockSpec((1, g), lambda i, j: (0, j)), pl.BlockSpec((tm, g), blk)],
        out_specs=(pl.BlockSpec((tm, g), blk), pl.BlockSpec((tm, g), blk)),
        compiler_params=_params("parallel", "parallel"),
    )(d, w, scale, gate)


def _silu_and_slope(gt):
    sg = _sigmoid(gt)
    return gt * sg, sg * (1.0 + gt * (1.0 - sg))


def _a_gate_epilogue(dz, tiles, vecs):
    yr, gt = tiles[0].astype(F32), tiles[1].astype(F32)
    sc = vecs[0]
    silu, slope = _silu_and_slope(gt)
    dy = dz * silu
    return dz * (yr * sc) * slope, dy * sc, dy * yr


def _gate_epilogue(dz, tiles, vecs):
    y, gt = tiles[0].astype(F32), tiles[1].astype(F32)
    silu, slope = _silu_and_slope(gt)
    return dz * y * slope, dz * silu


def _merge_gate_epilogue(dz, tiles, vecs):
    y, gt = tiles[0].astype(F32), tiles[1].astype(F32)
    silu, slope = _silu_and_slope(gt)
    dy = dz * silu
    w0, w1, w2 = _merge_weights(tiles[2], tiles[3], tiles[4])
    return dz * y * slope, w0 * dy, w1 * dy, w2 * dy


def _gate_fwd(y, gate, name, ts=512, tc=512):
    s, e = y.shape
    ts, tc = _pick(s, ts), _pick(e, tc)

    def body(y_ref, gate_ref, z_ref):
        gt = gate_ref[...].astype(F32)
        z_ref[...] = (y_ref[...].astype(F32) * (gt * _sigmoid(gt))).astype(BF16)

    blk = lambda i, j: (i, j)
    return pl.pallas_call(
        body, name=name,
        out_shape=jax.ShapeDtypeStruct((s, e), BF16),
        grid=(s // ts, e // tc),
        in_specs=[pl.BlockSpec((ts, tc), blk)] * 2,
        out_specs=pl.BlockSpec((ts, tc), blk),
        compiler_params=_params("parallel", "parallel"),
    )(y, gate)


def _merge_weights(l0, l1, l2):
    m = jnp.maximum(jnp.maximum(l0, l1), l2)
    e0, e1, e2 = jnp.exp(l0 - m), jnp.exp(l1 - m), jnp.exp(l2 - m)
    inv = 1.0 / (e0 + e1 + e2)
    return e0 * inv, e1 * inv, e2 * inv


def _merge_gate_fwd(outs, lses, gate, name, ts=512, tc=512):
    s, e = gate.shape
    ts, tc = _pick(s, ts), _pick(e, tc)

    def body(o0, o1, o2, l0, l1, l2, gate_ref, y_ref, z_ref):
        w0, w1, w2 = _merge_weights(l0[...], l1[...], l2[...])
        y = w0 * o0[...].astype(F32) + w1 * o1[...].astype(F32) + w2 * o2[...].astype(F32)
        y_ref[...] = y.astype(y_ref.dtype)
        gt = gate_ref[...].astype(F32)
        z_ref[...] = (y * (gt * _sigmoid(gt))).astype(BF16)

    blk = lambda i, j: (i, j)
    return pl.pallas_call(
        body, name=name,
        out_shape=(jax.ShapeDtypeStruct((s, e), BF16), jax.ShapeDtypeStruct((s, e), BF16)),
        grid=(s // ts, e // tc),
        in_specs=[pl.BlockSpec((ts, tc), blk)] * 7,
        out_specs=(pl.BlockSpec((ts, tc), blk), pl.BlockSpec((ts, tc), blk)),
        compiler_params=_params("parallel", "parallel"),
    )(*outs, *lses, gate)


def _band(max_dist, width):
    row = lax.broadcasted_iota(jnp.int32, (2 * BLOCK, width), 0) & (BLOCK - 1)
    col = lax.broadcasted_iota(jnp.int32, (2 * BLOCK, width), 1)
    low = row if max_dist == BLOCK else row + 1
    return jnp.logical_and(col >= low, col <= row + BLOCK), col >= BLOCK


def _fill_bias(bias_ref, max_dist):
    band, own = _band(max_dist, 2 * BLOCK)
    bias_ref[0] = jnp.where(band, 0.0, NEG)
    bias_ref[1] = jnp.where(jnp.logical_and(band, own), 0.0, NEG)


def _aligned(v):
    return v if isinstance(v, int) else pl.multiple_of(v, BLOCK)


def _stack_heads(x, lo):
    return jnp.concatenate([jnp.where(lo, x, 0.0), jnp.where(lo, 0.0, x)], axis=0).astype(BF16)


def _unstack_heads(x2, lo):
    return jnp.where(lo, x2[:BLOCK], x2[BLOCK:])


def _head_col(x, hm):
    return jnp.max(jnp.where(hm, x, NEG), axis=1, keepdims=True)


def _dot_nt(a, b):
    return lax.dot_general(a, b, (((1,), (1,)), ((), ())), preferred_element_type=F32)


def _dot_tn(a, b):
    return lax.dot_general(a, b, (((0,), (0,)), ((), ())), preferred_element_type=F32)


def _stream_view(a, dil):
    s, w = a.shape
    return a.reshape(s // (BLOCK * dil), dil, BLOCK, w)


def _fill_window(dst, halo_ref, cur_ref, n):
    dst[0:BLOCK, :] = halo_ref[0]
    for jc in range(n):
        dst[(jc + 1) * BLOCK:(jc + 2) * BLOCK, :] = cur_ref[jc]


def _attn_fwd(q, k, v, sinks, max_dist, rep, dil, out_dtype, name, tq=2048):
    assert max_dist in (BLOCK - 1, BLOCK)
    s, w = q.shape
    l = s // dil
    n_pairs = w // LANES
    tq = _pick(l, tq)
    n = tq // BLOCK
    has_sink = sinks is not None
    scale = HEAD_DIM ** -0.5

    def body(*refs):
        if has_sink:
            sink_ref, refs = refs[0], refs[1:]
        q_ref, kc_ref, kh_ref, vc_ref, vh_ref, o_ref, lse_ref, kx, vx, bias_ref = refs
        i, p = pl.program_id(0), pl.program_id(2)
        _fill_window(kx, kh_ref, kc_ref, n)
        _fill_window(vx, vh_ref, vc_ref, n)
        lo = lax.broadcasted_iota(jnp.int32, (BLOCK, LANES), 1) < HEAD_DIM
        _fill_bias(bias_ref, max_dist)
        top = lax.broadcasted_iota(jnp.int32, (2 * BLOCK, 1), 0) < BLOCK

        def scores(j):
            r0 = _aligned(j * BLOCK)
            q2 = _stack_heads(q_ref[j].astype(F32) * scale, lo)
            first = jnp.logical_and(i == 0, j == 0).astype(jnp.int32)
            return _dot_nt(q2, kx[pl.ds(r0, 2 * BLOCK), :]) + bias_ref[first]

        per_step = 2 if n % 2 == 0 else 1

        def step(jj, carry):
            nxt = tuple(scores(jnp.minimum((jj + 1) * per_step + t, n - 1)) for t in range(per_step))
            for t in range(per_step):
                finish(jj * per_step + t, carry[t])
            return nxt

        def finish(j, s2):
            r0 = _aligned(j * BLOCK)
            vw = vx[pl.ds(r0, 2 * BLOCK), :]
            m = jnp.max(s2, axis=1, keepdims=True)
            if has_sink:
                sk = jnp.where(top, sink_ref[2 * p], sink_ref[2 * p + 1])
                m = jnp.maximum(m, sk)
            pr = jnp.exp(s2 - m)
            den = jnp.sum(pr, axis=1, keepdims=True)
            if has_sink:
                den = den + jnp.exp(sk - m)
            o2 = jnp.dot(pr.astype(BF16), vw, preferred_element_type=F32) * (1.0 / den)
            lse2 = m + jnp.log(den)
            o_ref[j] = _unstack_heads(o2, lo).astype(o_ref.dtype)
            lse_ref[j] = _unstack_heads(lse2, lo)

        lax.fori_loop(0, n // per_step, step, tuple(scores(t) for t in range(per_step)))

    cur = lambda i, r, p: (i, r, 0, p)
    kv_cur = lambda i, r, p: (i, r, 0, p // rep)
    kv_halo = lambda i, r, p: (jnp.maximum(i * n - 1, 0), r, 0, p // rep)
    big, small = (n, None, BLOCK, LANES), (1, None, BLOCK, LANES)
    in_specs = [pl.BlockSpec(big, cur), pl.BlockSpec(big, kv_cur), pl.BlockSpec(small, kv_halo),
                pl.BlockSpec(big, kv_cur), pl.BlockSpec(small, kv_halo)]
    q4, k4, v4 = _stream_view(q, dil), _stream_view(k, dil), _stream_view(v, dil)
    args = [q4, k4, k4, v4, v4]
    if has_sink:
        in_specs = [pl.BlockSpec(memory_space=pltpu.SMEM)] + in_specs
        args = [sinks] + args
    o4, lse4 = pl.pallas_call(
        body, name=name,
        out_shape=(jax.ShapeDtypeStruct(q4.shape, out_dtype), jax.ShapeDtypeStruct(q4.shape, F32)),
        grid=(l // tq, dil, n_pairs),
        in_specs=in_specs,
        out_specs=(pl.BlockSpec(big, cur), pl.BlockSpec(big, cur)),
        scratch_shapes=[pltpu.VMEM((tq + BLOCK, LANES), BF16), pltpu.VMEM((tq + BLOCK, LANES), BF16),
                        pltpu.VMEM((2, 2 * BLOCK, 2 * BLOCK), F32)],
        compiler_params=_params("parallel", "parallel", "parallel"),
    )(*args)
    return o4.reshape(s, w), lse4.reshape(s, w)


def _attn_bwd(q, k, v, do, y, lse, sinks, max_dist, rep, dil, name, tq=2048):
    s, w = q.shape
    l = s // dil
    n_pairs = w // LANES
    tq = _pick(l, tq)
    n = tq // BLOCK
    n_blk = l // BLOCK
    n_sb = l // tq
    has_sink = sinks is not None
    scale = HEAD_DIM ** -0.5
    kv_dtype = BF16
    ext = tq + BLOCK

    def body(*refs):
        if has_sink:
            sink_ref, refs = refs[0], refs[1:]
        (q_ref, qn_ref, kc_ref, kh_ref, vc_ref, vh_ref, do_ref, don_ref, y_ref, yn_ref,
         lse_ref, lsen_ref) = refs[:12]
        refs = refs[12:]
        dq_ref, dk_ref, dv_ref = refs[:3]
        refs = refs[3:]
        if has_sink:
            dsink_ref, refs = refs[0], refs[1:]
        kx, vx, dkx, dvx, bias_ref = refs[:5]
        if rep > 1:
            dk_acc, dv_acc = refs[5:]
        i, p = pl.program_id(0), pl.program_id(2)
        _fill_bias(bias_ref, max_dist)
        own_rows = (q_ref, do_ref, y_ref, lse_ref)
        next_rows = (qn_ref, don_ref, yn_ref, lsen_ref)
        _fill_window(kx, kh_ref, kc_ref, n)
        _fill_window(vx, vh_ref, vc_ref, n)
        dkx[...] = jnp.zeros_like(dkx)
        dvx[...] = jnp.zeros_like(dvx)
        lo = lax.broadcasted_iota(jnp.int32, (BLOCK, LANES), 1) < HEAD_DIM
        hi = jnp.logical_not(lo)
        top = lax.broadcasted_iota(jnp.int32, (2 * BLOCK, 1), 0) < BLOCK

        def rows_of(j):
            if isinstance(j, int) and j == n:
                return next_rows, 0
            return own_rows, j

        def front(j, width):
            (qr, dor, _, _), jb = rows_of(j)
            r0 = _aligned(j * BLOCK)
            first = jnp.logical_and(i == 0, j == 0).astype(jnp.int32)
            q2 = _stack_heads(qr[jb].astype(F32) * scale, lo)
            do2 = _stack_heads(dor[jb].astype(F32), lo)
            s2 = _dot_nt(q2, kx[pl.ds(r0, width), :]) + bias_ref[first, :, pl.ds(0, width)]
            return s2, _dot_nt(do2, vx[pl.ds(r0, width), :])

        def back(j, width, q_valid, s2, dp2, sink_acc):
            (qr, dor, yr, lser), jb = rows_of(j)
            r0 = _aligned(j * BLOCK)
            dof = dor[jb].astype(F32)
            yb, lseb = yr[jb].astype(F32), lser[jb]
            q2 = _stack_heads(qr[jb].astype(F32) * scale, lo)
            do2 = _stack_heads(dof, lo)
            prod = dof * yb
            delta = jnp.concatenate([jnp.sum(jnp.where(lo, prod, 0.0), axis=1, keepdims=True),
                                     jnp.sum(jnp.where(lo, 0.0, prod), axis=1, keepdims=True)], axis=0)
            lse2 = jnp.concatenate([_head_col(lseb, lo), _head_col(lseb, hi)], axis=0)
            pr = jnp.exp(s2 - lse2)
            if q_valid is not True:
                pr = jnp.where(q_valid, pr, 0.0)
            ds = pr * (dp2 - delta)
            dkx[pl.ds(r0, width), :] += jnp.dot(ds.astype(BF16).T, q2, preferred_element_type=F32)
            dvx[pl.ds(r0, width), :] += jnp.dot(pr.astype(BF16).T, do2, preferred_element_type=F32)
            if width == 2 * BLOCK:
                dq2 = jnp.dot(ds.astype(BF16), kx[pl.ds(r0, width), :], preferred_element_type=F32) * scale
                dq_ref[jb] = _unstack_heads(dq2, lo).astype(dq_ref.dtype)
            if has_sink:
                sk = jnp.where(top, sink_ref[2 * p], sink_ref[2 * p + 1])
                sink_acc = sink_acc - jnp.exp(sk - lse2) * delta
            return sink_acc

        per_step = 2 if n % 2 == 0 else 1

        def step(jj, sink_acc):
            fronts = [front(jj * per_step + t, 2 * BLOCK) for t in range(per_step)]
            for t in range(per_step):
                sink_acc = back(jj * per_step + t, 2 * BLOCK, True, *fronts[t], sink_acc)
            return sink_acc

        zero_col = jnp.zeros((2 * BLOCK, 1), F32)
        sink_acc = lax.fori_loop(0, n // per_step, step, zero_col)
        if n_sb > 1:
            back(n, BLOCK, i < n_sb - 1, *front(n, BLOCK), zero_col)

        def write_out(dk_src, dv_src, first_row):
            for jc in range(n):
                rows = slice(first_row + jc * BLOCK, first_row + (jc + 1) * BLOCK)
                dk_ref[jc] = dk_src[rows, :].astype(dk_ref.dtype)
                dv_ref[jc] = dv_src[rows, :].astype(dv_ref.dtype)

        if rep == 1:
            write_out(dkx, dvx, BLOCK)
        else:
            @pl.when(p % rep == 0)
            def _():
                dk_acc[...] = dkx[BLOCK:, :]
                dv_acc[...] = dvx[BLOCK:, :]

            @pl.when(p % rep != 0)
            def _():
                dk_acc[...] += dkx[BLOCK:, :]
                dv_acc[...] += dvx[BLOCK:, :]

            pl.when(p % rep == rep - 1)(lambda: write_out(dk_acc, dv_acc, 0))
        if has_sink:
            rowi = lax.broadcasted_iota(jnp.int32, (8, LANES), 0)
            s0 = jnp.sum(sink_acc[:BLOCK], axis=0, keepdims=True)
            s1 = jnp.sum(sink_acc[BLOCK:], axis=0, keepdims=True)
            dsink_ref[...] = jnp.where(rowi == 0, s0, jnp.where(rowi == 1, s1, 0.0))

    cur = lambda i, r, p: (i, r, 0, p)
    nxt = lambda i, r, p: (jnp.minimum((i + 1) * n, n_blk - 1), r, 0, p)
    kv_cur = lambda i, r, p: (i, r, 0, p // rep)
    kv_halo = lambda i, r, p: (jnp.maximum(i * n - 1, 0), r, 0, p // rep)
    big, small = (n, None, BLOCK, LANES), (1, None, BLOCK, LANES)
    in_specs = [pl.BlockSpec(big, cur), pl.BlockSpec(small, nxt),
                pl.BlockSpec(big, kv_cur), pl.BlockSpec(small, kv_halo),
                pl.BlockSpec(big, kv_cur), pl.BlockSpec(small, kv_halo),
                pl.BlockSpec(big, cur), pl.BlockSpec(small, nxt),
                pl.BlockSpec(big, cur), pl.BlockSpec(small, nxt),
                pl.BlockSpec(big, cur), pl.BlockSpec(small, nxt)]
    q4, k4, v4, do4, y4, lse4 = [_stream_view(a, dil) for a in (q, k, v, do, y, lse)]
    args = [q4, q4, k4, k4, v4, v4, do4, do4, y4, y4, lse4, lse4]
    out_shape = [jax.ShapeDtypeStruct(q4.shape, BF16),
                 jax.ShapeDtypeStruct(k4.shape, kv_dtype), jax.ShapeDtypeStruct(v4.shape, kv_dtype)]
    out_specs = [pl.BlockSpec(big, cur), pl.BlockSpec(big, kv_cur), pl.BlockSpec(big, kv_cur)]
    if has_sink:
        in_specs = [pl.BlockSpec(memory_space=pltpu.SMEM)] + in_specs
        args = [sinks] + args
        out_shape.append(jax.ShapeDtypeStruct((n_sb, dil, n_pairs, 8, LANES), F32))
        out_specs.append(pl.BlockSpec((None, None, None, 8, LANES), lambda i, r, p: (i, r, p, 0, 0)))
    outs = pl.pallas_call(
        body, name=name,
        out_shape=tuple(out_shape),
        grid=(n_sb, dil, n_pairs),
        in_specs=in_specs,
        out_specs=tuple(out_specs),
        scratch_shapes=[pltpu.VMEM((ext, LANES), BF16), pltpu.VMEM((ext, LANES), BF16),
                        pltpu.VMEM((ext, LANES), F32), pltpu.VMEM((ext, LANES), F32),
                        pltpu.VMEM((2, 2 * BLOCK, 2 * BLOCK), F32)]
        + ([pltpu.VMEM((tq, LANES), F32), pltpu.VMEM((tq, LANES), F32)] if rep > 1 else []),
        compiler_params=_params("parallel", "parallel", "arbitrary"),
    )(*args)
    grads =[outs[0].reshape(s, w), outs[1].reshape(k.shape), outs[2].reshape(v.shape)]
    if has_sink:
        grads.append(outs[3].sum(axis=(0, 1))[:, 0:2, 0].reshape(1, 2 * n_pairs))
    return grads


def _sum_slots(recv, name, ts=256):
    nd, r, c = recv.shape
    ts = _pick(r, ts, 8)

    def body(r_ref, o_ref):
        acc = r_ref[0].astype(F32)
        for dev in range(1, nd):
            acc = acc + r_ref[dev].astype(F32)
        o_ref[...] = acc

    return pl.pallas_call(
        body, name=name,
        out_shape=jax.ShapeDtypeStruct((r, c), F32),
        grid=(r // ts,),
        in_specs=[pl.BlockSpec((nd, ts, c), lambda i: (0, i, 0))],
        out_specs=pl.BlockSpec((ts, c), lambda i: (i, 0)),
        compiler_params=_params("parallel"),
    )(recv)


def _adamw_math(w, g, m, v):
    c1 = 1.0 - ADAM_B1 ** ADAM_STEP
    c2 = 1.0 - ADAM_B2 ** ADAM_STEP
    m_ = ADAM_B1 * m + (1.0 - ADAM_B1) * g
    v_ = ADAM_B2 * v + (1.0 - ADAM_B2) * (g * g)
    return -ADAM_LR * ((m_ / c1) / (jnp.sqrt(v_ / c2) + ADAM_EPS) + ADAM_WD * w), m_, v_


def _row_tile(r, c, budget=1 << 18):
    return _pick(r, max(8, min(256, budget // c // 8 * 8)), 8)


def _adamw(w, g, m, v, name):
    r, c = w.shape
    ts = _row_tile(r, c)

    def body(w_ref, g_ref, m_ref, v_ref, d_ref, mo_ref, vo_ref):
        d_ref[...], mo_ref[...], vo_ref[...] = _adamw_math(w_ref[...], g_ref[...], m_ref[...], v_ref[...])

    blk = pl.BlockSpec((ts, c), lambda i: (i, 0))
    return pl.pallas_call(
        body, name=name,
        out_shape=tuple([jax.ShapeDtypeStruct((r, c), F32)] * 3),
        grid=(r // ts,),
        in_specs=[blk] * 4,
        out_specs=(blk, blk, blk),
        compiler_params=_params("parallel"),
    )(w, g, m, v)


def _adamw_slots(w, slots, m, v, name):
    r, c = w.shape
    nd = slots.shape[0]
    ts = _row_tile(r, c)

    def body(w_ref, s_ref, m_ref, v_ref, g_ref, d_ref, mo_ref, vo_ref):
        g = s_ref[0].astype(F32)
        for slot in range(1, nd):
            g = g + s_ref[slot].astype(F32)
        g_ref[...] = g
        d_ref[...], mo_ref[...], vo_ref[...] = _adamw_math(w_ref[...], g, m_ref[...], v_ref[...])

    blk = pl.BlockSpec((ts, c), lambda i: (i, 0))
    return pl.pallas_call(
        body, name=name,
        out_shape=tuple([jax.ShapeDtypeStruct((r, c), F32)] * 4),
        grid=(r // ts,),
        in_specs=[blk, pl.BlockSpec((nd, ts, c), lambda i: (0, i, 0)), blk, blk],
        out_specs=(blk, blk, blk, blk),
        compiler_params=_params("parallel"),
    )(w, slots, m, v)


def _rows(a):
    flat = a.reshape(-1)
    pad = (-flat.shape[0]) % PACK_W
    if pad:
        flat = jnp.concatenate([flat, jnp.zeros((pad,), flat.dtype)])
    return flat.reshape(-1, PACK_W)


def _pad_rows(a, mult):
    pad = (-a.shape[-2]) % mult
    if pad:
        widths = [(0, 0)] * (a.ndim - 2) + [(0, pad), (0, 0)]
        a = jnp.pad(a, widths)
    return a


def _to_global(stack, axis):
    moved = jnp.moveaxis(stack, 0, axis)
    shp = list(moved.shape)
    shp[axis:axis + 2] = [shp[axis] * shp[axis + 1]]
    return moved.reshape(shp)


def _to_stack(full, axis):
    shp = list(full.shape)
    shp[axis:axis + 1] = [N_DEV, shp[axis] // N_DEV]
    return jnp.moveaxis(full.reshape(shp), axis, 0)


_BIG = (("w_out", 1), ("a_w_in", 2), ("a_w_group", 2), ("b_w_in", 2), ("c_w_in", 2))


def _dup_heads(wk, n_kv):
    d = wk.shape[0]
    return jnp.tile(wk.reshape(d, n_kv, 1, HEAD_DIM), (1, 1, 2, 1)).reshape(d, n_kv * LANES)


def _fold_heads(dwk, n_kv):
    d = dwk.shape[0]
    folded = dwk.astype(F32).reshape(d, n_kv, 2, HEAD_DIM).sum(axis=2)
    return folded.reshape(d, n_kv * HEAD_DIM).astype(dwk.dtype)


def _perm(a, dil):
    if dil == 1:
        return a
    s, w = a.shape
    return a.reshape(s // (BLOCK * dil), BLOCK, dil, w).transpose(0, 2, 1, 3).reshape(s, w)


def _unperm(a, dil):
    if dil == 1:
        return a
    s, w = a.shape
    return a.reshape(s // (BLOCK * dil), dil, BLOCK, w).transpose(0, 2, 1, 3).reshape(s, w)


def kernel(x, norm_g, final_g, w_out, a_w_in, a_w_group, a_scale, b_w_in, b_sinks, c_w_in, loss_target, m_norm_g, m_final_g, m_w_out, m_a_w_in, m_a_w_group, m_a_scale, m_b_w_in, m_b_sinks, m_c_w_in, v_norm_g, v_final_g, v_w_out, v_a_w_in, v_a_w_group, v_a_scale, v_b_w_in, v_b_sinks, v_c_w_in):
    local = dict(w_out=w_out, a_w_in=a_w_in, a_w_group=a_w_group, b_w_in=b_w_in, c_w_in=c_w_in)
    mom_m = dict(w_out=m_w_out, a_w_in=m_a_w_in, a_w_group=m_a_w_group, b_w_in=m_b_w_in, c_w_in=m_c_w_in)
    mom_v = dict(w_out=v_w_out, a_w_in=v_a_w_in, a_w_group=v_a_w_group, b_w_in=v_b_w_in, c_w_in=v_c_w_in)
    s, d = x.shape[1], x.shape[2]
    depth = norm_g.shape[0]
    e = w_out.shape[1] * N_DEV
    n_heads = e // HEAD_DIM
    n_kv = n_heads // Q_PER_KV
    kv_w = n_kv * HEAD_DIM
    rep = Q_PER_KV // 2
    n_groups = len(POOL_WINDOWS)
    me = 4 * lax.axis_index("x") + 2 * lax.axis_index("y") + lax.axis_index("c")

    flat = {n: local[n].reshape(-1, local[n].shape[-1]) for n, _ in _BIG}
    spack = _pad_rows(_rows(a_scale), 8)
    *walls, sall = _gather([flat[n].astype(BF16) for n, _ in _BIG] + [spack], "gather_weights")
    full = {}
    for k, (name, axis) in enumerate(_BIG):
        full[name] = _to_global(walls[k].reshape((N_DEV,) + local[name].shape), axis)
    scale_full = _to_global(sall.reshape(N_DEV, -1)[:, :a_scale.size].reshape((N_DEV,) + a_scale.shape), 1)

    wout_t = jnp.swapaxes(full["w_out"], 1, 2)
    wa = full["a_w_in"]
    wa_t = jnp.swapaxes(wa, 1, 2)
    wg = full["a_w_group"]
    wg_t = jnp.swapaxes(wg, 2, 3)
    wb = full["b_w_in"][0]
    wb_ext = jnp.concatenate([wb[:, :e], _dup_heads(wb[:, e:e + kv_w], n_kv),
                              _dup_heads(wb[:, e + kv_w:e + 2 * kv_w], n_kv), wb[:, e + 2 * kv_w:]], axis=1)
    wb_ext_t = wb_ext.T
    kd_w = n_kv * LANES
    wc = full["c_w_in"][0]
    wc_t = wc.T

    xs, hs, zs, saved = [x.reshape(s, d)], [], [], []
    hs.append(_rmsnorm_fwd(xs[0], norm_g[0:1], "norm0"))
    loss_vec = dfinal = dx = dxb = None
    for i in range(depth):
        kind, j = i % 3, i // 3
        h = hs[i]
        tag = f"l{i}"
        if kind == 0:
            dpool = _proj_pool_fwd(h, wa[j][:, :e], tag + "_in_pool")
            gate = _matmul(h, wa[j][:, e:], BF16, tag + "_in_gate")
            yr, z = _a_group_fwd(dpool, wg[j], scale_full[j:j + 1], gate, tag + "_group")
            saved.append(dict(dpool=dpool, yr=yr, gate=gate))
        elif kind == 1:
            q = _matmul(h, wb_ext[:, :e], BF16, tag + "_in_q")
            kd = _matmul(h, wb_ext[:, e:e + kd_w], BF16, tag + "_in_k")
            vd = _matmul(h, wb_ext[:, e + kd_w:e + 2 * kd_w], BF16, tag + "_in_v")
            gate = _matmul(h, wb_ext[:, e + 2 * kd_w:], BF16, tag + "_in_gate")
            sinks = b_sinks[j]
            y, lse = _attn_fwd(q, kd, vd, sinks, SWA_MAX_DIST, rep, 1, BF16, tag + "_attn")
            z = _gate_fwd(y, gate, tag + "_gate")
            saved.append(dict(q=q, kd=kd, vd=vd, gate=gate, y=y, lse=lse, sinks=sinks))
        else:
            qkv, outs, lses, h_perm = [], [], [], []
            for gi, (window, dil) in enumerate(DILATED_PAIRS):
                hp = _perm(h, dil)
                trio = [_matmul(hp, wc[:, (3 * gi + t) * e:(3 * gi + t + 1) * e], BF16,
                                f"{tag}_in_{'qkv'[t]}{gi}") for t in range(3)]
                o, lse = _attn_fwd(trio[0], trio[1], trio[2], None, window // dil, 1, dil, BF16,
                                   f"{tag}_attn{gi}")
                qkv.append(trio)
                h_perm.append(hp)
                outs.append(_unperm(o, dil))
                lses.append(lse)
            gate = _matmul(h, wc[:, 9 * e:], BF16, tag + "_in_gate")
            lses_tok = [_unperm(lse, dil) for lse, (_, dil) in zip(lses, DILATED_PAIRS)]
            y, z = _merge_gate_fwd(outs, lses_tok, gate, tag + "_merge")
            saved.append(dict(qkv=qkv, lses=lses, lses_tok=lses_tok, gate=gate, y=y, h_perm=h_perm))
        zs.append(z)
        if i + 1 < depth:
            x_new, h_new = _outproj_norm(z, full["w_out"][i], xs[i], norm_g[i + 1:i + 2], tag + "_out")
            xs.append(x_new)
            hs.append(h_new)
        else:
            dx, dxb, dfinal, loss_vec = _outproj_loss(z, full["w_out"][i], xs[i], final_g.reshape(1, d),
                                                      loss_target.reshape(s, d), tag + "_out_loss")

    g_full = {"w_out": [None] * depth, "a_w_in": [None] * wa.shape[0], "a_w_group": [None] * wa.shape[0]}
    d_norm = [None] * depth
    d_scale = [None] * wa.shape[0]
    d_sinks = None
    for i in reversed(range(depth)):
        kind, j = i % 3, i // 3
        tag = f"b{i}"
        sv = saved[i]
        g_full["w_out"][i] = _matmul_tn(zs[i], dxb, tag + "_dwout", out_dtype=BF16)
        if kind == 0:
            dgate, dyr, dsc = _dz_fused(dxb, wout_t[i], [sv["yr"], sv["gate"]], [scale_full[j:j + 1]], 2,
                                        _a_gate_epilogue, tag + "_dz_gate", 1024, 1024, with_col_sum=True)
            d_scale[j] = dsc
            du = _group_pool_bwd(dyr, wg_t[j], tag + "_dd_pool")
            g_full["a_w_group"][j] = _grouped_weight_grad(sv["dpool"], dyr, n_groups, tag + "_dwg")
            parts = [du, dgate]
            g_full["a_w_in"][j] = jnp.concatenate(
                [_matmul_tn(hs[i], part, f"{tag}_dwin{t}", out_dtype=BF16) for t, part in enumerate(parts)], axis=1)
            dhs = [_matmul_cat(parts, wa_t[j], F32, tag + "_dh")]
        elif kind == 1:
            dgate, do = _dz_fused(dxb, wout_t[i], [sv["y"], sv["gate"]], [], 2, _gate_epilogue,
                                  tag + "_dz_gate", 1024, 1024)
            dq, dkd, dvd, d_sinks = _attn_bwd(sv["q"], sv["kd"], sv["vd"], do, sv["y"], sv["lse"], sv["sinks"],
                                              SWA_MAX_DIST, rep, 1, tag + "_attn")
            parts = [dq, dkd, dvd, dgate]
            dws = [_matmul_tn(hs[i], part, f"{tag}_dwin{t}", out_dtype=BF16) for t, part in enumerate(parts)]
            g_full["b_w_in"] = jnp.concatenate(
                [dws[0], _fold_heads(dws[1], n_kv), _fold_heads(dws[2], n_kv), dws[3]], axis=1)[None]
            dhs = [_matmul_cat(parts, wb_ext_t, F32, tag + "_dh")]
        else:
            dgate, *dos = _dz_fused(dxb, wout_t[i], [sv["y"], sv["gate"], *sv["lses_tok"]], [], 4,
                                    _merge_gate_epilogue, tag + "_dz_merge", 512, 512)
            y_bf = sv["y"]
            dws, dhs = [], []
            for gi, (window, dil) in enumerate(DILATED_PAIRS):
                qv, kv, vv = sv["qkv"][gi]
                grads = _attn_bwd(qv, kv, vv, _perm(dos[gi], dil), _perm(y_bf, dil), sv["lses"][gi], None,
                                  window // dil, 1, dil, f"{tag}_attn{gi}")
                dws += [_matmul_tn(sv["h_perm"][gi], part, f"{tag}_dwin{gi}{'qkv'[t]}", out_dtype=BF16)
                        for t, part in enumerate(grads)]
                dhs.append(_unperm(_matmul_cat(grads, wc_t[3 * gi * e:3 * (gi + 1) * e], F32, f"{tag}_dh{gi}"),
                                   dil))
            dws.append(_matmul_tn(hs[i], dgate, tag + "_dwin_gate", out_dtype=BF16))
            dhs.append(_matmul(dgate, wc_t[9 * e:], F32, tag + "_dh_gate"))
            g_full["c_w_in"] = jnp.concatenate(dws, axis=1)[None]
        dx, dxb, d_norm[i] = _rmsnorm_bwd(dhs, xs[i], norm_g[i:i + 1], dx, tag + "_norm")
    grad_x = dx.reshape(x.shape)
    for name in ("w_out", "a_w_in", "a_w_group"):
        g_full[name] = jnp.stack(g_full[name], axis=0)

    stacks = [_to_stack(g_full[n], axis).astype(BF16).reshape((N_DEV,) + flat[n].shape) for n, axis in _BIG]
    loss_local = (0.5 / d) * jnp.sum(loss_vec)
    small = [jnp.concatenate(d_norm, axis=0), dfinal, d_sinks, jnp.concatenate(d_scale, axis=0),
             loss_local.reshape(1, 1)]
    small_rows = [_rows(a) for a in small]
    small_offs = [sum(r.shape[0] for r in small_rows[:k]) for k in range(len(small_rows) + 1)]
    small_pack = _pad_rows(jnp.concatenate(small_rows, axis=0), 8)
    core = lax.axis_index("c").astype(jnp.int32).reshape(1)
    from_sibling = _sibling_exchange(stacks, "exchange_sibling")
    chip_sums = [_pair_sum(stacks[k], from_sibling[k], core, "sum_pair_" + n) for k, (n, _) in enumerate(_BIG)]
    grecv, srecv = _chip_exchange(chip_sums, small_pack, "exchange_chips")
    ssum = _sum_slots(srecv, "sum_small")

    def small_part(k, like):
        return ssum[small_offs[k]:small_offs[k + 1]].reshape(-1)[:like.size].reshape(like.shape)

    g_norm = small_part(0, norm_g)
    g_final = small_part(1, final_g)
    g_sinks = small_part(2, b_sinks)
    g_scale_full = small_part(3, scale_full)
    loss = ssum[small_offs[4], 0]
    g_scale = lax.dynamic_slice_in_dim(g_scale_full, me * a_scale.shape[1], a_scale.shape[1], axis=1)

    small_w = [("norm_g", norm_g, m_norm_g, v_norm_g, g_norm), ("final_g", final_g, m_final_g, v_final_g, g_final),
               ("a_scale", a_scale, m_a_scale, v_a_scale, g_scale), ("b_sinks", b_sinks, m_b_sinks, v_b_sinks, g_sinks)]
    tail = lambda idx: _pad_rows(jnp.concatenate([_rows(t[idx]) for t in small_w], axis=0), 8)
    tail_sizes = [_rows(t[1]).shape[0] for t in small_w]
    tail_offs = [sum(tail_sizes[:k]) for k in range(len(tail_sizes) + 1)]
    g_tail = tail(4)
    tails = (g_tail,) + _adamw(tail(1), g_tail, tail(2), tail(3), "adamw_small")
    grads, deltas, new_m, new_v = {}, {}, {}, {}
    for k, (name, w_, _, _, _) in enumerate(small_w):
        for out, packed in zip((grads, deltas, new_m, new_v), tails):
            out[name] = packed[tail_offs[k]:tail_offs[k + 1]].reshape(-1)[:w_.size].reshape(w_.shape)
    for k, (name, _) in enumerate(_BIG):
        shape2d = flat[name].shape
        res = _adamw_slots(flat[name], grecv[k], mom_m[name].reshape(shape2d), mom_v[name].reshape(shape2d),
                           "adamw_" + name)
        for out, val in zip((grads, deltas, new_m, new_v), res):
            out[name] = val.reshape(local[name].shape)

    order = ("norm_g", "final_g", "w_out", "a_w_in", "a_w_group", "a_scale", "b_w_in", "b_sinks", "c_w_in")
    return (loss, grad_x, *[grads[n] for n in order], *[deltas[n] for n in order],
            *[new_m[n] for n in order], *[new_v[n] for n in order])
```

```python
import functools

import jax
import jax.numpy as jnp
from jax import lax
from jax.experimental import pallas as pl
from jax.experimental.pallas import tpu as pltpu

F32 = jnp.float32
BF16 = jnp.bfloat16

N_DEV = 8
HEAD_DIM = 64
LANES = 128
BLOCK = 128
Q_PER_KV = 8
POOL_WINDOWS = (2, 4, 8, 16)
POOL_HALO = 16
DILATED_PAIRS = ((128, 1), (512, 4), (2048, 16))
SWA_MAX_DIST = 127
RMS_EPS = 1e-5
PACK_W = 1024
NEG = -1e30

ADAM_LR = 0.001
ADAM_B1 = 0.9
ADAM_B2 = 0.999
ADAM_EPS = 1e-08
ADAM_WD = 0.01
ADAM_STEP = 10

VMEM_LIMIT = 48 * 1024 * 1024


def _params(*sem):
    return pltpu.CompilerParams(dimension_semantics=sem if sem else None, vmem_limit_bytes=VMEM_LIMIT)


def _pick(dim, target, mult=LANES):
    if dim <= target:
        return dim
    t = target - target % mult
    while dim % t:
        t -= mult
    return t


def _sigmoid(x):
    return 1.0 / (1.0 + jnp.exp(-x))


CHIP_OFFSETS = (2, 4, 6)
ANY_SPEC = pl.BlockSpec(memory_space=pl.ANY)


def _where_am_i():
    x, y, c = lax.axis_index("x"), lax.axis_index("y"), lax.axis_index("c")
    return x, y, c, 4 * x + 2 * y + c


def _peer(x, y, c, r):
    return x ^ ((r >> 2) & 1), y ^ ((r >> 1) & 1), c ^ (r & 1)


def _gather(blocks, name):
    n = len(blocks)

    def body(*refs):
        send, recv = refs[:n], refs[n:2 * n]
        send_sems, recv_sems, local_sems = refs[2 * n:]
        x, y, c, me = _where_am_i()
        sib = _peer(x, y, c, 1)
        sib_id = me ^ 1

        def copy(k, slot_sem, src, slot, to):
            return pltpu.make_async_remote_copy(
                src_ref=src, dst_ref=recv[k].at[slot], send_sem=send_sems.at[k, slot_sem],
                recv_sem=recv_sems.at[k, slot_sem], device_id=to, device_id_type=pl.DeviceIdType.MESH)

        started = []
        for k in range(n):
            own = pltpu.make_async_copy(send[k], recv[k].at[me], local_sems.at[k])
            own.start()
            started.append(own)
        sends = []
        for k in range(n):
            sends.append(copy(k, 0, send[k], me, sib))
            for j, r in enumerate(CHIP_OFFSETS):
                sends.append(copy(k, 1 + j, send[k], me, _peer(x, y, c, r)))
        for cp in sends:
            cp.start()
        for j, r in enumerate(CHIP_OFFSETS):
            for k in range(n):
                src_id = me ^ r
                copy(k, 1 + j, send[k], src_id, sib).wait_recv()
                fwd = copy(k, 4 + j, recv[k].at[src_id], src_id, sib)
                fwd.start()
                sends.append(fwd)
        for k in range(n):
            copy(k, 0, send[k], sib_id, sib).wait_recv()
            for j, r in enumerate(CHIP_OFFSETS):
                copy(k, 4 + j, send[k], sib_id ^ r, sib).wait_recv()
        for cp in sends:
            cp.wait_send()
        for own in started:
            own.wait()

    return pl.pallas_call(
        body, name=name,
        out_shape=tuple(jax.ShapeDtypeStruct((N_DEV,) + b.shape, b.dtype) for b in blocks),
        in_specs=[ANY_SPEC] * n,
        out_specs=tuple([ANY_SPEC] * n),
        scratch_shapes=[pltpu.SemaphoreType.DMA((n, N_DEV - 1)), pltpu.SemaphoreType.DMA((n, N_DEV - 1)),
                        pltpu.SemaphoreType.DMA((n,))],
    )(*blocks)


def _sibling_exchange(stacks, name):
    n_chips = N_DEV // 2
    n = len(stacks)

    def body(*refs):
        g_refs, t_refs = refs[:n], refs[n:2 * n]
        send_sems, recv_sems = refs[2 * n:]
        x, y, c, _ = _where_am_i()
        sib = _peer(x, y, c, 1)
        copies = [pltpu.make_async_remote_copy(
            src_ref=g_refs[k].at[2 * chip + (1 - c)], dst_ref=t_refs[k].at[chip], send_sem=send_sems.at[k, chip],
            recv_sem=recv_sems.at[k, chip], device_id=sib, device_id_type=pl.DeviceIdType.MESH)
            for k in range(n) for chip in range(n_chips)]
        for cp in copies:
            cp.start()
        for cp in copies:
            cp.wait_recv()
        for cp in copies:
            cp.wait_send()

    return pl.pallas_call(
        body, name=name,
        out_shape=tuple(jax.ShapeDtypeStruct((n_chips,) + g.shape[1:], g.dtype) for g in stacks),
        in_specs=[ANY_SPEC] * n, out_specs=tuple([ANY_SPEC] * n),
        scratch_shapes=[pltpu.SemaphoreType.DMA((n, n_chips)), pltpu.SemaphoreType.DMA((n, n_chips))],
    )(*stacks)


def _chip_exchange(csums, small, name):
    n = len(csums)

    def body(*refs):
        c_refs, s_ref = refs[:n], refs[n]
        r_refs, sr_ref = refs[n + 1:2 * n + 1], refs[2 * n + 1]
        send_sems, recv_sems, small_send, small_recv, local_sems = refs[2 * n + 2:]
        x, y, c, me = _where_am_i()
        my_chip = 2 * x + y
        own = [pltpu.make_async_copy(c_refs[k].at[my_chip], r_refs[k].at[my_chip], local_sems.at[k])
               for k in range(n)]
        own.append(pltpu.make_async_copy(s_ref, sr_ref.at[me], local_sems.at[n]))
        for cp in own:
            cp.start()
        sends, recvs = [], []
        for j, r in enumerate(CHIP_OFFSETS):
            to = _peer(x, y, c, r)
            chip = my_chip ^ (r >> 1)
            for k in range(n):
                sends.append(pltpu.make_async_remote_copy(
                    src_ref=c_refs[k].at[chip], dst_ref=r_refs[k].at[my_chip], send_sem=send_sems.at[k, j],
                    recv_sem=recv_sems.at[k, j], device_id=to, device_id_type=pl.DeviceIdType.MESH))
                recvs.append(pltpu.make_async_remote_copy(
                    src_ref=c_refs[k].at[chip], dst_ref=r_refs[k].at[chip], send_sem=send_sems.at[k, j],
                    recv_sem=recv_sems.at[k, j], device_id=to, device_id_type=pl.DeviceIdType.MESH))
        for r in range(1, N_DEV):
            to = _peer(x, y, c, r)
            sends.append(pltpu.make_async_remote_copy(
                src_ref=s_ref, dst_ref=sr_ref.at[me], send_sem=small_send.at[r - 1],
                recv_sem=small_recv.at[r - 1], device_id=to, device_id_type=pl.DeviceIdType.MESH))
            recvs.append(pltpu.make_async_remote_copy(
                src_ref=s_ref, dst_ref=sr_ref.at[me ^ r], send_sem=small_send.at[r - 1],
                recv_sem=small_recv.at[r - 1], device_id=to, device_id_type=pl.DeviceIdType.MESH))
        for cp in sends:
            cp.start()
        for cp in recvs:
            cp.wait_recv()
        for cp in sends:
            cp.wait_send()
        for cp in own:
            cp.wait()

    n_off = len(CHIP_OFFSETS)
    outs = pl.pallas_call(
        body, name=name,
        out_shape=tuple(jax.ShapeDtypeStruct(cs.shape, cs.dtype) for cs in csums)
        + (jax.ShapeDtypeStruct((N_DEV,) + small.shape, small.dtype),),
        in_specs=[ANY_SPEC] * (n + 1), out_specs=tuple([ANY_SPEC] * (n + 1)),
        scratch_shapes=[pltpu.SemaphoreType.DMA((n, n_off)), pltpu.SemaphoreType.DMA((n, n_off)),
                        pltpu.SemaphoreType.DMA((N_DEV - 1,)), pltpu.SemaphoreType.DMA((N_DEV - 1,)),
                        pltpu.SemaphoreType.DMA((n + 1,))],
    )(*csums, small)
    return list(outs[:n]), outs[n]


def _pair_sum(gpack, other, core, name, ts=256):
    n_chips, r, c = other.shape
    ts = _pick(r, ts, 16)

    def body(core_ref, g_ref, o_ref, out_ref):
        del core_ref
        out_ref[...] = (g_ref[...].astype(F32) + o_ref[...].astype(F32)).astype(out_ref.dtype)

    return pl.pallas_call(
        body, name=name,
        out_shape=jax.ShapeDtypeStruct(other.shape, other.dtype),
        grid_spec=pltpu.PrefetchScalarGridSpec(
            num_scalar_prefetch=1, grid=(n_chips, r // ts),
            in_specs=[pl.BlockSpec((None, ts, c), lambda j, i, core_ref: (2 * j + core_ref[0], i, 0)),
                      pl.BlockSpec((None, ts, c), lambda j, i, core_ref: (j, i, 0))],
            out_specs=pl.BlockSpec((None, ts, c), lambda j, i, core_ref: (j, i, 0))),
        compiler_params=_params("parallel", "parallel"),
    )(core, gpack, other)


def _matmul(a, b, out_dtype, name, tm=1024, tn=1024, tk=1024):
    m, kdim = a.shape
    n = b.shape[1]
    tm, tn, tk = _pick(m, tm), _pick(n, tn), _pick(kdim, tk)
    nk = kdim // tk

    if nk == 1:
        def body(a_ref, b_ref, o_ref):
            o_ref[...] = jnp.dot(a_ref[...], b_ref[...], preferred_element_type=F32).astype(o_ref.dtype)
        scratch = []
    else:
        def body(a_ref, b_ref, o_ref, acc_ref):
            kk = pl.program_id(2)

            @pl.when(kk == 0)
            def _():
                acc_ref[...] = jnp.zeros_like(acc_ref)

            acc_ref[...] += jnp.dot(a_ref[...], b_ref[...], preferred_element_type=F32)

            @pl.when(kk == nk - 1)
            def _():
                o_ref[...] = acc_ref[...].astype(o_ref.dtype)
        scratch = [pltpu.VMEM((tm, tn), F32)]

    return pl.pallas_call(
        body, name=name,
        out_shape=jax.ShapeDtypeStruct((m, n), out_dtype),
        grid=(m // tm, n // tn, nk),
        in_specs=[pl.BlockSpec((tm, tk), lambda i, j, k: (i, k)),
                  pl.BlockSpec((tk, tn), lambda i, j, k: (k, j))],
        out_specs=pl.BlockSpec((tm, tn), lambda i, j, k: (i, j)),
        scratch_shapes=scratch,
        compiler_params=_params("parallel", "parallel", "arbitrary"),
    )(a, b)


def _matmul_cat(parts, b, out_dtype, name, tm=1024, tn=1024, tk=1024):
    m = parts[0].shape[0]
    n = b.shape[1]
    tm, tn = _pick(m, tm), _pick(n, tn)
    tk = min(_pick(p.shape[1], tk) for p in parts)
    steps = [p.shape[1] // tk for p in parts]
    assert all(p.shape[1] % tk == 0 for p in parts)
    starts = [sum(steps[:t]) for t in range(len(parts))]
    nk = sum(steps)
    n_parts = len(parts)

    def body(*refs):
        a_refs, b_ref, o_ref, acc_ref = refs[:n_parts], refs[n_parts], refs[n_parts + 1], refs[n_parts + 2]
        kk = pl.program_id(2)

        @pl.when(kk == 0)
        def _():
            acc_ref[...] = jnp.zeros_like(acc_ref)

        for t in range(n_parts):
            @pl.when(jnp.logical_and(kk >= starts[t], kk < starts[t] + steps[t]))
            def _(t=t):
                acc_ref[...] += jnp.dot(a_refs[t][...], b_ref[...], preferred_element_type=F32)

        @pl.when(kk == nk - 1)
        def _():
            o_ref[...] = acc_ref[...].astype(o_ref.dtype)

    def part_map(t):
        return lambda i, j, k: (i, jnp.clip(k - starts[t], 0, steps[t] - 1))

    return pl.pallas_call(
        body, name=name,
        out_shape=jax.ShapeDtypeStruct((m, n), out_dtype),
        grid=(m // tm, n // tn, nk),
        in_specs=[pl.BlockSpec((tm, tk), part_map(t)) for t in range(n_parts)]
        + [pl.BlockSpec((tk, tn), lambda i, j, k: (k, j))],
        out_specs=pl.BlockSpec((tm, tn), lambda i, j, k: (i, j)),
        scratch_shapes=[pltpu.VMEM((tm, tn), F32)],
        compiler_params=_params("parallel", "parallel", "arbitrary"),
    )(*parts, b)


def _matmul_tn(a, b, name, tm=1024, tn=1024, tk=1024, out_dtype=F32):
    kdim, m = a.shape
    n = b.shape[1]
    tm, tn, tk = _pick(m, tm), _pick(n, tn), _pick(kdim, tk)
    nk = kdim // tk

    def body(a_ref, b_ref, o_ref, acc_ref):
        kk = pl.program_id(2)

        @pl.when(kk == 0)
        def _():
            acc_ref[...] = jnp.zeros_like(acc_ref)

        acc_ref[...] += lax.dot_general(a_ref[...], b_ref[...], (((0,), (0,)), ((), ())),
                                        preferred_element_type=F32)

        @pl.when(kk == nk - 1)
        def _():
            o_ref[...] = acc_ref[...].astype(o_ref.dtype)

    return pl.pallas_call(
        body, name=name,
        out_shape=jax.ShapeDtypeStruct((m, n), out_dtype),
        grid=(m // tm, n // tn, nk),
        in_specs=[pl.BlockSpec((tk, tm), lambda i, j, k: (k, i)),
                  pl.BlockSpec((tk, tn), lambda i, j, k: (k, j))],
        out_specs=pl.BlockSpec((tm, tn), lambda i, j, k: (i, j)),
        scratch_shapes=[pltpu.VMEM((tm, tn), F32)],
        compiler_params=_params("parallel", "parallel", "arbitrary"),
    )(a, b)


def _grouped_weight_grad(a, b, ng, name, tk=1024):
    s, e = a.shape
    g = e // ng
    tk = _pick(s, tk)
    nk = s // tk

    def body(a_ref, b_ref, o_ref):
        kk = pl.program_id(1)

        @pl.when(kk == 0)
        def _():
            o_ref[...] = jnp.zeros_like(o_ref)

        o_ref[...] += lax.dot_general(a_ref[...], b_ref[...], (((0,), (0,)), ((), ())),
                                      preferred_element_type=F32)

    return pl.pallas_call(
        body, name=name,
        out_shape=jax.ShapeDtypeStruct((ng, g, g), F32),
        grid=(ng, nk),
        in_specs=[pl.BlockSpec((tk, g), lambda j, k: (k, j)),
                  pl.BlockSpec((tk, g), lambda j, k: (k, j))],
        out_specs=pl.BlockSpec((None, g, g), lambda j, k: (j, 0, 0)),
        compiler_params=_params("parallel", "arbitrary"),
    )(a, b)


def _rms(x):
    r = lax.rsqrt(jnp.mean(x * x, axis=1, keepdims=True) + RMS_EPS)
    return x * r, r


def _rmsnorm_fwd(x, g, name, ts=256):
    s, d = x.shape
    ts = _pick(s, ts, 8)

    def body(x_ref, g_ref, h_ref):
        xhat, _ = _rms(x_ref[...])
        h_ref[...] = (xhat * g_ref[...]).astype(BF16)

    return pl.pallas_call(
        body, name=name,
        out_shape=jax.ShapeDtypeStruct((s, d), BF16),
        grid=(s // ts,),
        in_specs=[pl.BlockSpec((ts, d), lambda i: (i, 0)), pl.BlockSpec((1, d), lambda i: (0, 0))],
        out_specs=pl.BlockSpec((ts, d), lambda i: (i, 0)),
        compiler_params=_params("parallel"),
    )(x, g)


def _outproj_norm(z, w, x, g, name, tm=512):
    s, e = z.shape
    d = w.shape[1]
    tm = _pick(s, tm)

    def body(z_ref, w_ref, x_ref, g_ref, xo_ref, h_ref):
        xn = x_ref[...] + jnp.dot(z_ref[...], w_ref[...], preferred_element_type=F32)
        xo_ref[...] = xn
        xhat, _ = _rms(xn)
        h_ref[...] = (xhat * g_ref[...]).astype(BF16)

    return pl.pallas_call(
        body, name=name,
        out_shape=(jax.ShapeDtypeStruct((s, d), F32), jax.ShapeDtypeStruct((s, d), BF16)),
        grid=(s // tm,),
        in_specs=[pl.BlockSpec((tm, e), lambda i: (i, 0)), pl.BlockSpec((e, d), lambda i: (0, 0)),
                  pl.BlockSpec((tm, d), lambda i: (i, 0)), pl.BlockSpec((1, d), lambda i: (0, 0))],
        out_specs=(pl.BlockSpec((tm, d), lambda i: (i, 0)), pl.BlockSpec((tm, d), lambda i: (i, 0))),
        compiler_params=_params("parallel"),
    )(z, w, x, g)


def _outproj_loss(z, w, x, g, target, name, tm=512):
    s, e = z.shape
    d = w.shape[1]
    tm = _pick(s, tm)

    def body(z_ref, w_ref, x_ref, g_ref, t_ref, dx_ref, dxb_ref, dg_ref, loss_ref):
        i = pl.program_id(0)
        xn = x_ref[...] + jnp.dot(z_ref[...], w_ref[...], preferred_element_type=F32)
        xhat, r = _rms(xn)
        gain = g_ref[...]
        diff = xhat * gain - t_ref[...]
        dout = diff * (1.0 / d)
        dxhat = dout * gain
        dx = r * (dxhat - xhat * jnp.mean(dxhat * xhat, axis=1, keepdims=True))
        dx_ref[...] = dx
        dxb_ref[...] = dx.astype(BF16)

        @pl.when(i == 0)
        def _():
            dg_ref[...] = jnp.zeros_like(dg_ref)
            loss_ref[...] = jnp.zeros_like(loss_ref)

        dg_ref[...] += jnp.sum(dout * xhat, axis=0, keepdims=True)
        loss_ref[...] += jnp.sum(diff * diff, axis=0, keepdims=True)

    row = lambda i: (i, 0)
    fixed = lambda i: (0, 0)
    return pl.pallas_call(
        body, name=name,
        out_shape=(jax.ShapeDtypeStruct((s, d), F32), jax.ShapeDtypeStruct((s, d), BF16),
                   jax.ShapeDtypeStruct((1, d), F32), jax.ShapeDtypeStruct((1, d), F32)),
        grid=(s // tm,),
        in_specs=[pl.BlockSpec((tm, e), row), pl.BlockSpec((e, d), fixed), pl.BlockSpec((tm, d), row),
                  pl.BlockSpec((1, d), fixed), pl.BlockSpec((tm, d), row)],
        out_specs=(pl.BlockSpec((tm, d), row), pl.BlockSpec((tm, d), row),
                   pl.BlockSpec((1, d), fixed), pl.BlockSpec((1, d), fixed)),
        compiler_params=_params("arbitrary"),
    )(z, w, x, g, target)


def _rmsnorm_bwd(dhs, x, g, dx_next, name, ts=256):
    s, d = x.shape
    ts = _pick(s, ts, 8)
    n_dh = len(dhs)

    def body(*refs):
        dh_refs = refs[:n_dh]
        x_ref, g_ref, dn_ref, dx_ref, dxb_ref, dg_ref = refs[n_dh:]
        i = pl.program_id(0)
        xhat, r = _rms(x_ref[...])
        dh_ = dh_refs[0][...]
        for extra in dh_refs[1:]:
            dh_ = dh_ + extra[...]
        dxhat = dh_ * g_ref[...]
        dx = dn_ref[...] + r * (dxhat - xhat * jnp.mean(dxhat * xhat, axis=1, keepdims=True))
        dx_ref[...] = dx
        dxb_ref[...] = dx.astype(BF16)

        @pl.when(i == 0)
        def _():
            dg_ref[...] = jnp.zeros_like(dg_ref)

        dg_ref[...] += jnp.sum(dh_ * xhat, axis=0, keepdims=True)

    row = lambda i: (i, 0)
    fixed = lambda i: (0, 0)
    return pl.pallas_call(
        body, name=name,
        out_shape=(jax.ShapeDtypeStruct((s, d), F32), jax.ShapeDtypeStruct((s, d), BF16),
                   jax.ShapeDtypeStruct((1, d), F32)),
        grid=(s // ts,),
        in_specs=[pl.BlockSpec((ts, d), row)] * n_dh + [pl.BlockSpec((ts, d), row), pl.BlockSpec((1, d), fixed),
                                                        pl.BlockSpec((ts, d), row)],
        out_specs=(pl.BlockSpec((ts, d), row), pl.BlockSpec((ts, d), row), pl.BlockSpec((1, d), fixed)),
        compiler_params=_params("arbitrary"),
    )(*dhs, x, g, dx_next)


def _pool_counts(t0, rows, cols, window):
    t = t0 + lax.broadcasted_iota(jnp.int32, (rows, cols), 0)
    return jnp.minimum(t + 1, window).astype(F32)


def _proj_pool_fwd(h, w, name, ts=1024, tc=512):
    s, dm = h.shape
    e = w.shape[1]
    ng = len(POOL_WINDOWS)
    gdim = e // ng
    ts, tc = _pick(s, ts), _pick(gdim, tc)
    cpg = gdim // tc
    hb = ts // POOL_HALO

    def body(h_ref, halo_ref, w_ref, d_ref):
        i, grp = pl.program_id(0), pl.program_id(1)
        cur = jnp.dot(h_ref[...], w_ref[...], preferred_element_type=F32)
        halo = jnp.dot(halo_ref[...], w_ref[...], preferred_element_type=F32)
        ext = jnp.concatenate([jnp.where(i > 0, halo, 0.0), cur], axis=0)
        for gi, window in enumerate(POOL_WINDOWS):
            @pl.when(grp == gi)
            def _(window=window):
                acc = ext
                k = 1
                while k < window:
                    acc = acc + pltpu.roll(acc, k, 0)
                    k *= 2
                pooled = acc[POOL_HALO:, :] / _pool_counts(i * ts, ts, tc, window)
                d_ref[...] = (pooled - cur).astype(BF16)

    return pl.pallas_call(
        body, name=name,
        out_shape=jax.ShapeDtypeStruct((s, e), BF16),
        grid=(s // ts, ng, cpg),
        in_specs=[pl.BlockSpec((ts, dm), lambda i, g, j: (i, 0)),
                  pl.BlockSpec((POOL_HALO, dm), lambda i, g, j: (jnp.maximum(i * hb - 1, 0), 0)),
                  pl.BlockSpec((dm, tc), lambda i, g, j: (0, g * cpg + j))],
        out_specs=pl.BlockSpec((ts, tc), lambda i, g, j: (i, g * cpg + j)),
        compiler_params=_params("parallel", "parallel", "parallel"),
    )(h, h, w)


def _dz_fused(dxb, w_t, tiles, vecs, n_out, epilogue, name, tm, tn, with_col_sum=False, rows=()):
    s, dm = dxb.shape
    e = w_t.shape[1]
    tm, tn = _pick(s, tm), _pick(e, tn)
    n_t, n_v, n_r = len(tiles), len(vecs), len(rows)

    def body(*refs):
        a_ref, b_ref = refs[:2]
        tile_refs, vec_refs = refs[2:2 + n_t], refs[2 + n_t:2 + n_t + n_v]
        row_refs = refs[2 + n_t + n_v:2 + n_t + n_v + n_r]
        out_refs = refs[2 + n_t + n_v + n_r:]
        i = pl.program_id(1)
        dz = jnp.dot(a_ref[...], b_ref[...], preferred_element_type=F32)
        extra = ([r[...] for r in row_refs], pl.program_id(0) * tn) if n_r else ()
        res = epilogue(dz, [t[...] for t in tile_refs], [v[...] for v in vec_refs], *extra)
        for o_ref, val in zip(out_refs[:n_out], res[:n_out]):
            o_ref[...] = val.astype(o_ref.dtype)
        if with_col_sum:
            sum_ref = out_refs[n_out]

            @pl.when(i == 0)
            def _():
                sum_ref[...] = jnp.zeros_like(sum_ref)

            sum_ref[...] += jnp.sum(res[n_out], axis=0, keepdims=True)

    blk = lambda j, i: (i, j)
    vec = lambda j, i: (0, j)
    out_shape = [jax.ShapeDtypeStruct((s, e), BF16)] * n_out
    out_specs = [pl.BlockSpec((tm, tn), blk)] * n_out
    if with_col_sum:
        out_shape.append(jax.ShapeDtypeStruct((1, e), F32))
        out_specs.append(pl.BlockSpec((1, tn), vec))
    return pl.pallas_call(
        body, name=name,
        out_shape=tuple(out_shape),
        grid=(e // tn, s // tm),
        in_specs=[pl.BlockSpec((tm, dm), lambda j, i: (i, 0)), pl.BlockSpec((dm, tn), lambda j, i: (0, j))]
        + [pl.BlockSpec((tm, tn), blk)] * n_t + [pl.BlockSpec((1, tn), vec)] * n_v
        + [pl.BlockSpec((tm, r.shape[1]), lambda j, i: (i, 0)) for r in rows],
        out_specs=tuple(out_specs),
        compiler_params=_params("parallel", "arbitrary"),
    )(dxb, w_t, *tiles, *vecs, *rows)


def _group_pool_bwd(dyr, w_t, name, ts=1024, tc=512):
    s, e = dyr.shape
    ng = len(POOL_WINDOWS)
    gdim = e // ng
    ts, tc = _pick(s, ts), _pick(gdim, tc)
    cpg = gdim // tc
    hb = ts // POOL_HALO
    n_halo = s // POOL_HALO
    nst = s // ts

    def body(dy_ref, halo_ref, w_ref, du_ref):
        i, grp = pl.program_id(0), pl.program_id(1)
        cur = jnp.dot(dy_ref[...], w_ref[...], preferred_element_type=F32)
        halo = jnp.dot(halo_ref[...], w_ref[...], preferred_element_type=F32)
        ext = jnp.concatenate([cur, jnp.where(i < nst - 1, halo, 0.0)], axis=0)
        rows = ts + POOL_HALO
        for gi, window in enumerate(POOL_WINDOWS):
            @pl.when(grp == gi)
            def _(window=window):
                acc = ext / _pool_counts(i * ts, rows, tc, window)
                k = 1
                while k < window:
                    acc = acc + pltpu.roll(acc, rows - k, 0)
                    k *= 2
                du_ref[...] = (acc[:ts, :] - cur).astype(BF16)

    return pl.pallas_call(
        body, name=name,
        out_shape=jax.ShapeDtypeStruct((s, e), BF16),
        grid=(nst, ng, cpg),
        in_specs=[pl.BlockSpec((ts, gdim), lambda i, g, j: (i, g)),
                  pl.BlockSpec((POOL_HALO, gdim), lambda i, g, j: (jnp.minimum((i + 1) * hb, n_halo - 1), g)),
                  pl.BlockSpec((None, gdim, tc), lambda i, g, j: (g, 0, j))],
        out_specs=pl.BlockSpec((ts, tc), lambda i, g, j: (i, g * cpg + j)),
        compiler_params=_params("parallel", "parallel", "parallel"),
    )(dyr, dyr, w_t)


def _a_group_fwd(d, w, scale, gate, name, tm=1024):
    s, e = d.shape
    ng, g, _ = w.shape
    tm = _pick(s, tm)

    def body(d_ref, w_ref, s_ref, gate_ref, yr_ref, z_ref):
        yr = jnp.dot(d_ref[...], w_ref[...], preferred_element_type=F32)
        yr_ref[...] = yr.astype(yr_ref.dtype)
        gt = gate_ref[...].astype(F32)
        z_ref[...] = ((yr * s_ref[...]) * (gt * _sigmoid(gt))).astype(BF16)

    blk = lambda i, j: (i, j)
    return pl.pallas_call(
        body, name=name,
        out_shape=(jax.ShapeDtypeStruct((s, e), BF16), jax.ShapeDtypeStruct((s, e), BF16)),
        grid=(s // tm, ng),
        in_specs=[pl.BlockSpec((tm, g), blk), pl.BlockSpec((None, g, g), lambda i, j: (j, 0, 0)),
                  pl.BlockSpec((1, g), lambda i, j: (0, j)), pl.BlockSpec((tm, g), blk)],
        out_specs=(pl.BlockSpec((tm, g), blk), pl.BlockSpec((tm, g), blk)),
        compiler_params=_params("parallel", "parallel"),
    )(d, w, scale, gate)


def _silu_and_slope(gt):
    sg = _sigmoid(gt)
    return gt * sg, sg * (1.0 + gt * (1.0 - sg))


def _a_gate_epilogue(dz, tiles, vecs):
    yr, gt = tiles[0].astype(F32), tiles[1].astype(F32)
    sc = vecs[0]
    silu, slope = _silu_and_slope(gt)
    dy = dz * silu
    return dz * (yr * sc) * slope, dy * sc, dy * yr


def _gate_epilogue(dz, tiles, vecs):
    y, gt = tiles[0].astype(F32), tiles[1].astype(F32)
    silu, slope = _silu_and_slope(gt)
    return dz * y * slope, dz * silu


def _merge_gate_epilogue(dz, tiles, vecs, lses, col0):
    y, gt = tiles[0].astype(F32), tiles[1].astype(F32)
    silu, slope = _silu_and_slope(gt)
    dy = dz * silu
    w0, w1, w2 = _merge_weights_expanded(lses, col0, dz.shape[1])
    return dz * y * slope, w0 * dy, w1 * dy, w2 * dy


def _gate_fwd(y, gate, name, ts=512, tc=512):
    s, e = y.shape
    ts, tc = _pick(s, ts), _pick(e, tc)

    def body(y_ref, gate_ref, z_ref):
        gt = gate_ref[...].astype(F32)
        z_ref[...] = (y_ref[...].astype(F32) * (gt * _sigmoid(gt))).astype(BF16)

    blk = lambda i, j: (i, j)
    return pl.pallas_call(
        body, name=name,
        out_shape=jax.ShapeDtypeStruct((s, e), BF16),
        grid=(s // ts, e // tc),
        in_specs=[pl.BlockSpec((ts, tc), blk)] * 2,
        out_specs=pl.BlockSpec((ts, tc), blk),
        compiler_params=_params("parallel", "parallel"),
    )(y, gate)


def _merge_weights(l0, l1, l2):
    m = jnp.maximum(jnp.maximum(l0, l1), l2)
    e0, e1, e2 = jnp.exp(l0 - m), jnp.exp(l1 - m), jnp.exp(l2 - m)
    inv = 1.0 / (e0 + e1 + e2)
    return e0 * inv, e1 * inv, e2 * inv


def _expand_heads(w, col0, width):
    n_heads = w.shape[1]
    head_of_lane = (col0 + lax.broadcasted_iota(jnp.int32, (n_heads, width), 1)) // HEAD_DIM
    pick = jnp.where(head_of_lane == lax.broadcasted_iota(jnp.int32, (n_heads, width), 0), 1.0, 0.0).astype(BF16)
    high = w.astype(BF16)
    rest = (w - high.astype(F32)).astype(BF16)
    return (jnp.dot(high, pick, preferred_element_type=F32) + jnp.dot(rest, pick, preferred_element_type=F32))


def _merge_weights_expanded(lses, col0, width):
    return [_expand_heads(w, col0, width) for w in _merge_weights(*lses)]


def _merge_gate_fwd(outs, lses, gate, name, ts=512, tc=512):
    s, e = gate.shape
    n_heads = lses[0].shape[1]
    ts, tc = _pick(s, ts), _pick(e, tc)

    def body(o0, o1, o2, l0, l1, l2, gate_ref, y_ref, z_ref):
        w0, w1, w2 = _merge_weights_expanded([l0[...], l1[...], l2[...]], pl.program_id(1) * tc, tc)
        y = w0 * o0[...].astype(F32) + w1 * o1[...].astype(F32) + w2 * o2[...].astype(F32)
        y_ref[...] = y.astype(y_ref.dtype)
        gt = gate_ref[...].astype(F32)
        z_ref[...] = (y * (gt * _sigmoid(gt))).astype(BF16)

    blk = lambda i, j: (i, j)
    per_head = pl.BlockSpec((ts, n_heads), lambda i, j: (i, 0))
    return pl.pallas_call(
        body, name=name,
        out_shape=(jax.ShapeDtypeStruct((s, e), BF16), jax.ShapeDtypeStruct((s, e), BF16)),
        grid=(s // ts, e // tc),
        in_specs=[pl.BlockSpec((ts, tc), blk)] * 3 + [per_head] * 3 + [pl.BlockSpec((ts, tc), blk)],
        out_specs=(pl.BlockSpec((ts, tc), blk), pl.BlockSpec((ts, tc), blk)),
        compiler_params=_params("parallel", "parallel"),
    )(*outs, *lses, gate)


def _band(max_dist, width):
    row = lax.broadcasted_iota(jnp.int32, (2 * BLOCK, width), 0) & (BLOCK - 1)
    col = lax.broadcasted_iota(jnp.int32, (2 * BLOCK, width), 1)
    low = row if max_dist == BLOCK else row + 1
    return jnp.logical_and(col >= low, col <= row + BLOCK), col >= BLOCK


def _fill_bias(bias_ref, max_dist):
    band, own = _band(max_dist, 2 * BLOCK)
    bias_ref[0] = jnp.where(band, 0.0, NEG)
    bias_ref[1] = jnp.where(jnp.logical_and(band, own), 0.0, NEG)


def _aligned(v):
    return v if isinstance(v, int) else pl.multiple_of(v, BLOCK)


def _stack_heads(x, lo):
    return jnp.concatenate([jnp.where(lo, x, 0.0), jnp.where(lo, 0.0, x)], axis=0).astype(BF16)


def _unstack_heads(x2, lo):
    return jnp.where(lo, x2[:BLOCK], x2[BLOCK:])


def _head_col(x, hm):
    return jnp.max(jnp.where(hm, x, NEG), axis=1, keepdims=True)


def _dot_nt(a, b):
    return lax.dot_general(a, b, (((1,), (1,)), ((), ())), preferred_element_type=F32)


def _dot_tn(a, b):
    return lax.dot_general(a, b, (((0,), (0,)), ((), ())), preferred_element_type=F32)


def _stream_view(a, dil):
    s, w = a.shape
    return a.reshape(s // (BLOCK * dil), dil, BLOCK, w)


def _fill_window(dst, halo_ref, cur_ref, n):
    dst[0:BLOCK, :] = halo_ref[0]
    for jc in range(n):
        dst[(jc + 1) * BLOCK:(jc + 2) * BLOCK, :] = cur_ref[jc]


def _attn_fwd(q, k, v, sinks, max_dist, rep, dil, out_dtype, name, tq=2048, per_head_lse=False):
    assert max_dist in (BLOCK - 1, BLOCK)
    s, w = q.shape
    l = s // dil
    n_pairs = w // LANES
    n_heads = 2 * n_pairs
    tq = _pick(l, tq)
    n = tq // BLOCK
    has_sink = sinks is not None
    scale = HEAD_DIM ** -0.5

    def body(*refs):
        if has_sink:
            sink_ref, refs = refs[0], refs[1:]
        q_ref, kc_ref, kh_ref, vc_ref, vh_ref, o_ref, lse_ref = refs[:7]
        refs = refs[7:]
        if per_head_lse:
            lseh_ref, refs = refs[0], refs[1:]
        kx, vx, bias_ref = refs
        i, p = pl.program_id(0), pl.program_id(2)
        _fill_window(kx, kh_ref, kc_ref, n)
        _fill_window(vx, vh_ref, vc_ref, n)
        if per_head_lse:
            @pl.when(p == 0)
            def _():
                lseh_ref[...] = jnp.zeros_like(lseh_ref)
            head_lane = lax.broadcasted_iota(jnp.int32, (BLOCK, n_heads), 1)
        lo = lax.broadcasted_iota(jnp.int32, (BLOCK, LANES), 1) < HEAD_DIM
        _fill_bias(bias_ref, max_dist)
        top = lax.broadcasted_iota(jnp.int32, (2 * BLOCK, 1), 0) < BLOCK

        def scores(j):
            r0 = _aligned(j * BLOCK)
            q2 = _stack_heads(q_ref[j].astype(F32) * scale, lo)
            first = jnp.logical_and(i == 0, j == 0).astype(jnp.int32)
            return _dot_nt(q2, kx[pl.ds(r0, 2 * BLOCK), :]) + bias_ref[first]

        per_step = 2 if n % 2 == 0 else 1

        def step(jj, carry):
            nxt = tuple(scores(jnp.minimum((jj + 1) * per_step + t, n - 1)) for t in range(per_step))
            for t in range(per_step):
                finish(jj * per_step + t, carry[t])
            return nxt

        def finish(j, s2):
            r0 = _aligned(j * BLOCK)
            vw = vx[pl.ds(r0, 2 * BLOCK), :]
            m = jnp.max(s2, axis=1, keepdims=True)
            if has_sink:
                sk = jnp.where(top, sink_ref[2 * p], sink_ref[2 * p + 1])
                m = jnp.maximum(m, sk)
            pr = jnp.exp(s2 - m)
            den = jnp.sum(pr, axis=1, keepdims=True)
            if has_sink:
                den = den + jnp.exp(sk - m)
            o2 = jnp.dot(pr.astype(BF16), vw, preferred_element_type=F32) * (1.0 / den)
            lse2 = m + jnp.log(den)
            o_ref[j] = _unstack_heads(o2, lo).astype(o_ref.dtype)
            lse_ref[j] = _unstack_heads(lse2, lo)
            if per_head_lse:
                lseh_ref[j] = jnp.where(head_lane == 2 * p, lse2[:BLOCK],
                                        jnp.where(head_lane == 2 * p + 1, lse2[BLOCK:], lseh_ref[j]))

        lax.fori_loop(0, n // per_step, step, tuple(scores(t) for t in range(per_step)))

    cur = lambda i, r, p: (i, r, 0, p)
    kv_cur = lambda i, r, p: (i, r, 0, p // rep)
    kv_halo = lambda i, r, p: (jnp.maximum(i * n - 1, 0), r, 0, p // rep)
    big, small = (n, None, BLOCK, LANES), (1, None, BLOCK, LANES)
    in_specs = [pl.BlockSpec(big, cur), pl.BlockSpec(big, kv_cur), pl.BlockSpec(small, kv_halo),
                pl.BlockSpec(big, kv_cur), pl.BlockSpec(small, kv_halo)]
    q4, k4, v4 = _stream_view(q, dil), _stream_view(k, dil), _stream_view(v, dil)
    args = [q4, k4, k4, v4, v4]
    if has_sink:
        in_specs = [pl.BlockSpec(memory_space=pltpu.SMEM)] + in_specs
        args = [sinks] + args
    out_shape = [jax.ShapeDtypeStruct(q4.shape, out_dtype), jax.ShapeDtypeStruct(q4.shape, F32)]
    out_specs = [pl.BlockSpec(big, cur), pl.BlockSpec(big, cur)]
    if per_head_lse:
        out_shape.append(jax.ShapeDtypeStruct(q4.shape[:3] + (n_heads,), F32))
        out_specs.append(pl.BlockSpec((n, None, BLOCK, n_heads), lambda i, r, p: (i, r, 0, 0)))
    outs = pl.pallas_call(
        body, name=name,
        out_shape=tuple(out_shape),
        grid=(l // tq, dil, n_pairs),
        in_specs=in_specs,
        out_specs=tuple(out_specs),
        scratch_shapes=[pltpu.VMEM((tq + BLOCK, LANES), BF16), pltpu.VMEM((tq + BLOCK, LANES), BF16),
                        pltpu.VMEM((2, 2 * BLOCK, 2 * BLOCK), F32)],
        compiler_params=_params("parallel", "parallel", "arbitrary"),
    )(*args)
    res = [outs[0].reshape(s, w), outs[1].reshape(s, w)]
    if per_head_lse:
        res.append(outs[2].reshape(s, n_heads))
    return res


def _attn_bwd(q, k, v, do, y, lse, sinks, max_dist, rep, dil, name, tq=2048):
    s, w = q.shape
    l = s // dil
    n_pairs = w // LANES
    tq = _pick(l, tq)
    n = tq // BLOCK
    n_blk = l // BLOCK
    n_sb = l // tq
    has_sink = sinks is not None
    scale = HEAD_DIM ** -0.5
    kv_dtype = BF16
    ext = tq + BLOCK

    def body(*refs):
        if has_sink:
            sink_ref, refs = refs[0], refs[1:]
        (q_ref, qn_ref, kc_ref, kh_ref, vc_ref, vh_ref, do_ref, don_ref, y_ref, yn_ref,
         lse_ref, lsen_ref) = refs[:12]
        refs = refs[12:]
        dq_ref, dk_ref, dv_ref = refs[:3]
        refs = refs[3:]
        if has_sink:
            dsink_ref, refs = refs[0], refs[1:]
        kx, vx, bias_ref = refs[:3]
        if rep > 1:
            dk_acc, dv_acc = refs[3:]
        i, p = pl.program_id(0), pl.program_id(2)
        _fill_bias(bias_ref, max_dist)
        own_rows = (q_ref, do_ref, y_ref, lse_ref)
        next_rows = (qn_ref, don_ref, yn_ref, lsen_ref)
        _fill_window(kx, kh_ref, kc_ref, n)
        _fill_window(vx, vh_ref, vc_ref, n)
        if rep > 1:
            @pl.when(p % rep == 0)
            def _():
                dk_acc[...] = jnp.zeros_like(dk_acc)
                dv_acc[...] = jnp.zeros_like(dv_acc)
        lo = lax.broadcasted_iota(jnp.int32, (BLOCK, LANES), 1) < HEAD_DIM
        hi = jnp.logical_not(lo)
        top = lax.broadcasted_iota(jnp.int32, (2 * BLOCK, 1), 0) < BLOCK

        def rows_of(j):
            if isinstance(j, int) and j == n:
                return next_rows, 0
            return own_rows, j

        def front(j, width):
            (qr, dor, _, _), jb = rows_of(j)
            r0 = _aligned(j * BLOCK)
            first = jnp.logical_and(i == 0, j == 0).astype(jnp.int32)
            q2 = _stack_heads(qr[jb].astype(F32) * scale, lo)
            do2 = _stack_heads(dor[jb].astype(F32), lo)
            s2 = _dot_nt(q2, kx[pl.ds(r0, width), :]) + bias_ref[first, :, pl.ds(0, width)]
            return s2, _dot_nt(do2, vx[pl.ds(r0, width), :])

        row_lo = lax.broadcasted_iota(jnp.int32, (LANES, BLOCK), 0) < HEAD_DIM

        def stack_t(x):
            xt = x.T
            return jnp.concatenate([jnp.where(row_lo, xt, 0.0), jnp.where(row_lo, 0.0, xt)], axis=1).astype(BF16)

        def emit(jk, dk_t, dv_t):
            dk_blk, dv_blk = dk_t.T, dv_t.T
            if rep == 1:
                dk_ref[jk] = dk_blk.astype(dk_ref.dtype)
                dv_ref[jk] = dv_blk.astype(dv_ref.dtype)
            else:
                rows = pl.ds(_aligned(jk * BLOCK), BLOCK)
                dk_acc[rows, :] += dk_blk
                dv_acc[rows, :] += dv_blk

        def back(j, width, q_valid, s2, dp2, state):
            sink_acc, carry_k, carry_v = state
            (qr, dor, yr, lser), jb = rows_of(j)
            r0 = _aligned(j * BLOCK)
            qf, dof = qr[jb].astype(F32) * scale, dor[jb].astype(F32)
            yb, lseb = yr[jb].astype(F32), lser[jb]
            prod = dof * yb
            delta = jnp.concatenate([jnp.sum(jnp.where(lo, prod, 0.0), axis=1, keepdims=True),
                                     jnp.sum(jnp.where(lo, 0.0, prod), axis=1, keepdims=True)], axis=0)
            lse2 = jnp.concatenate([_head_col(lseb, lo), _head_col(lseb, hi)], axis=0)
            pr = jnp.exp(s2 - lse2)
            if q_valid is not True:
                pr = jnp.where(q_valid, pr, 0.0)
            ds = pr * (dp2 - delta)
            dk_t = jnp.dot(stack_t(qf), ds.astype(BF16), preferred_element_type=F32)
            dv_t = jnp.dot(stack_t(dof), pr.astype(BF16), preferred_element_type=F32)
            done_k, done_v = carry_k + dk_t[:, :BLOCK], carry_v + dv_t[:, :BLOCK]
            if isinstance(j, int):
                emit(j - 1, done_k, done_v)
            elif rep == 1:
                emit(jnp.maximum(j - 1, 0), done_k, done_v)
            else:
                keep = j > 0
                emit(jnp.maximum(j - 1, 0), jnp.where(keep, done_k, 0.0), jnp.where(keep, done_v, 0.0))
            if width == 2 * BLOCK:
                dq2 = jnp.dot(ds.astype(BF16), kx[pl.ds(r0, width), :], preferred_element_type=F32) * scale
                dq_ref[jb] = _unstack_heads(dq2, lo).astype(dq_ref.dtype)
                carry_k, carry_v = dk_t[:, BLOCK:], dv_t[:, BLOCK:]
            if has_sink:
                sk = jnp.where(top, sink_ref[2 * p], sink_ref[2 * p + 1])
                sink_acc = sink_acc - jnp.exp(sk - lse2) * delta
            return sink_acc, carry_k, carry_v

        per_step = 2 if n % 2 == 0 else 1

        def step(jj, state):
            fronts = [front(jj * per_step + t, 2 * BLOCK) for t in range(per_step)]
            for t in range(per_step):
                state = back(jj * per_step + t, 2 * BLOCK, True, *fronts[t], state)
            return state

        zero_blk = jnp.zeros((LANES, BLOCK), F32)
        state = lax.fori_loop(0, n // per_step, step, (jnp.zeros((2 * BLOCK, 1), F32), zero_blk, zero_blk))
        sink_acc = state[0]
        if n_sb > 1:
            back(n, BLOCK, i < n_sb - 1, *front(n, BLOCK), state)
        else:
            emit(n - 1, state[1], state[2])

        if rep > 1:
            @pl.when(p % rep == rep - 1)
            def _():
                for jc in range(n):
                    rows = slice(jc * BLOCK, (jc + 1) * BLOCK)
                    dk_ref[jc] = dk_acc[rows, :].astype(dk_ref.dtype)
                    dv_ref[jc] = dv_acc[rows, :].astype(dv_ref.dtype)
        if has_sink:
            rowi = lax.broadcasted_iota(jnp.int32, (8, LANES), 0)
            s0 = jnp.sum(sink_acc[:BLOCK], axis=0, keepdims=True)
            s1 = jnp.sum(sink_acc[BLOCK:], axis=0, keepdims=True)
            dsink_ref[...] = jnp.where(rowi == 0, s0, jnp.where(rowi == 1, s1, 0.0))

    cur = lambda i, r, p: (i, r, 0, p)
    nxt = lambda i, r, p: (jnp.minimum((i + 1) * n, n_blk - 1), r, 0, p)
    kv_cur = lambda i, r, p: (i, r, 0, p // rep)
    kv_halo = lambda i, r, p: (jnp.maximum(i * n - 1, 0), r, 0, p // rep)
    big, small = (n, None, BLOCK, LANES), (1, None, BLOCK, LANES)
    in_specs = [pl.BlockSpec(big, cur), pl.BlockSpec(small, nxt),
                pl.BlockSpec(big, kv_cur), pl.BlockSpec(small, kv_halo),
                pl.BlockSpec(big, kv_cur), pl.BlockSpec(small, kv_halo),
                pl.BlockSpec(big, cur), pl.BlockSpec(small, nxt),
                pl.BlockSpec(big, cur), pl.BlockSpec(small, nxt),
                pl.BlockSpec(big, cur), pl.BlockSpec(small, nxt)]
    q4, k4, v4, do4, y4, lse4 = [_stream_view(a, dil) for a in (q, k, v, do, y, lse)]
    args = [q4, q4, k4, k4, v4, v4, do4, do4, y4, y4, lse4, lse4]
    out_shape = [jax.ShapeDtypeStruct(q4.shape, BF16),
                 jax.ShapeDtypeStruct(k4.shape, kv_dtype), jax.ShapeDtypeStruct(v4.shape, kv_dtype)]
    out_specs = [pl.BlockSpec(big, cur), pl.BlockSpec(big, kv_cur), pl.BlockSpec(big, kv_cur)]
    if has_sink:
        in_specs = [pl.BlockSpec(memory_space=pltpu.SMEM)] + in_specs
        args = [sinks] + args
        out_shape.append(jax.ShapeDtypeStruct((n_sb, dil, n_pairs, 8, LANES), F32))
        out_specs.append(pl.BlockSpec((None, None, None, 8, LANES), lambda i, r, p: (i, r, p, 0, 0)))
    outs = pl.pallas_call(
        body, name=name,
        out_shape=tuple(out_shape),
        grid=(n_sb, dil, n_pairs),
        in_specs=in_specs,
        out_specs=tuple(out_specs),
        scratch_shapes=[pltpu.VMEM((ext, LANES), BF16), pltpu.VMEM((ext, LANES), BF16),
                        pltpu.VMEM((2, 2 * BLOCK, 2 * BLOCK), F32)]
        + ([pltpu.VMEM((tq, LANES), F32), pltpu.VMEM((tq, LANES), F32)] if rep > 1 else []),
        compiler_params=_params("parallel", "parallel", "arbitrary"),
    )(*args)
    grads =[outs[0].reshape(s, w), outs[1].reshape(k.shape), outs[2].reshape(v.shape)]
    if has_sink:
        grads.append(outs[3].sum(axis=(0, 1))[:, 0:2, 0].reshape(1, 2 * n_pairs))
    return grads


def _sum_slots(recv, name, ts=256):
    nd, r, c = recv.shape
    ts = _pick(r, ts, 8)

    def body(r_ref, o_ref):
        acc = r_ref[0].astype(F32)
        for dev in range(1, nd):
            acc = acc + r_ref[dev].astype(F32)
        o_ref[...] = acc

    return pl.pallas_call(
        body, name=name,
        out_shape=jax.ShapeDtypeStruct((r, c), F32),
        grid=(r // ts,),
        in_specs=[pl.BlockSpec((nd, ts, c), lambda i: (0, i, 0))],
        out_specs=pl.BlockSpec((ts, c), lambda i: (i, 0)),
        compiler_params=_params("parallel"),
    )(recv)


def _adamw_math(w, g, m, v):
    c1 = 1.0 - ADAM_B1 ** ADAM_STEP
    c2 = 1.0 - ADAM_B2 ** ADAM_STEP
    m_ = ADAM_B1 * m + (1.0 - ADAM_B1) * g
    v_ = ADAM_B2 * v + (1.0 - ADAM_B2) * (g * g)
    return -ADAM_LR * ((m_ / c1) / (jnp.sqrt(v_ / c2) + ADAM_EPS) + ADAM_WD * w), m_, v_


def _row_tile(r, c, budget=1 << 18):
    return _pick(r, max(8, min(256, budget // c // 8 * 8)), 8)


def _adamw(w, g, m, v, name):
    r, c = w.shape
    ts = _row_tile(r, c)

    def body(w_ref, g_ref, m_ref, v_ref, d_ref, mo_ref, vo_ref):
        d_ref[...], mo_ref[...], vo_ref[...] = _adamw_math(w_ref[...], g_ref[...], m_ref[...], v_ref[...])

    blk = pl.BlockSpec((ts, c), lambda i: (i, 0))
    return pl.pallas_call(
        body, name=name,
        out_shape=tuple([jax.ShapeDtypeStruct((r, c), F32)] * 3),
        grid=(r // ts,),
        in_specs=[blk] * 4,
        out_specs=(blk, blk, blk),
        compiler_params=_params("parallel"),
    )(w, g, m, v)


def _adamw_slots(w, slots, m, v, name):
    r, c = w.shape
    nd = slots.shape[0]
    ts = _row_tile(r, c)

    def body(w_ref, s_ref, m_ref, v_ref, g_ref, d_ref, mo_ref, vo_ref):
        g = s_ref[0].astype(F32)
        for slot in range(1, nd):
            g = g + s_ref[slot].astype(F32)
        g_ref[...] = g
        d_ref[...], mo_ref[...], vo_ref[...] = _adamw_math(w_ref[...], g, m_ref[...], v_ref[...])

    blk = pl.BlockSpec((ts, c), lambda i: (i, 0))
    return pl.pallas_call(
        body, name=name,
        out_shape=tuple([jax.ShapeDtypeStruct((r, c), F32)] * 4),
        grid=(r // ts,),
        in_specs=[blk, pl.BlockSpec((nd, ts, c), lambda i: (0, i, 0)), blk, blk],
        out_specs=(blk, blk, blk, blk),
        compiler_params=_params("parallel"),
    )(w, slots, m, v)


def _rows(a):
    flat = a.reshape(-1)
    pad = (-flat.shape[0]) % PACK_W
    if pad:
        flat = jnp.concatenate([flat, jnp.zeros((pad,), flat.dtype)])
    return flat.reshape(-1, PACK_W)


def _pad_rows(a, mult):
    pad = (-a.shape[-2]) % mult
    if pad:
        widths = [(0, 0)] * (a.ndim - 2) + [(0, pad), (0, 0)]
        a = jnp.pad(a, widths)
    return a


def _to_global(stack, axis):
    moved = jnp.moveaxis(stack, 0, axis)
    shp = list(moved.shape)
    shp[axis:axis + 2] = [shp[axis] * shp[axis + 1]]
    return moved.reshape(shp)


def _to_stack(full, axis):
    shp = list(full.shape)
    shp[axis:axis + 1] = [N_DEV, shp[axis] // N_DEV]
    return jnp.moveaxis(full.reshape(shp), axis, 0)


_BIG = (("w_out", 1), ("a_w_in", 2), ("a_w_group", 2), ("b_w_in", 2), ("c_w_in", 2))


def _dup_heads(wk, n_kv):
    d = wk.shape[0]
    return jnp.tile(wk.reshape(d, n_kv, 1, HEAD_DIM), (1, 1, 2, 1)).reshape(d, n_kv * LANES)


def _fold_heads(dwk, n_kv):
    d = dwk.shape[0]
    folded = dwk.astype(F32).reshape(d, n_kv, 2, HEAD_DIM).sum(axis=2)
    return folded.reshape(d, n_kv * HEAD_DIM).astype(dwk.dtype)


def _perm(a, dil):
    if dil == 1:
        return a
    s, w = a.shape
    return a.reshape(s // (BLOCK * dil), BLOCK, dil, w).transpose(0, 2, 1, 3).reshape(s, w)


def _unperm(a, dil):
    if dil == 1:
        return a
    s, w = a.shape
    return a.reshape(s // (BLOCK * dil), dil, BLOCK, w).transpose(0, 2, 1, 3).reshape(s, w)


def kernel(x, norm_g, final_g, w_out, a_w_in, a_w_group, a_scale, b_w_in, b_sinks, c_w_in, loss_target, m_norm_g, m_final_g, m_w_out, m_a_w_in, m_a_w_group, m_a_scale, m_b_w_in, m_b_sinks, m_c_w_in, v_norm_g, v_final_g, v_w_out, v_a_w_in, v_a_w_group, v_a_scale, v_b_w_in, v_b_sinks, v_c_w_in):
    local = dict(w_out=w_out, a_w_in=a_w_in, a_w_group=a_w_group, b_w_in=b_w_in, c_w_in=c_w_in)
    mom_m = dict(w_out=m_w_out, a_w_in=m_a_w_in, a_w_group=m_a_w_group, b_w_in=m_b_w_in, c_w_in=m_c_w_in)
    mom_v = dict(w_out=v_w_out, a_w_in=v_a_w_in, a_w_group=v_a_w_group, b_w_in=v_b_w_in, c_w_in=v_c_w_in)
    s, d = x.shape[1], x.shape[2]
    depth = norm_g.shape[0]
    e = w_out.shape[1] * N_DEV
    n_heads = e // HEAD_DIM
    n_kv = n_heads // Q_PER_KV
    kv_w = n_kv * HEAD_DIM
    rep = Q_PER_KV // 2
    n_groups = len(POOL_WINDOWS)
    me = 4 * lax.axis_index("x") + 2 * lax.axis_index("y") + lax.axis_index("c")

    flat = {n: local[n].reshape(-1, local[n].shape[-1]) for n, _ in _BIG}
    spack = _pad_rows(_rows(a_scale), 8)
    *walls, sall = _gather([flat[n].astype(BF16) for n, _ in _BIG] + [spack], "gather_weights")
    full = {}
    for k, (name, axis) in enumerate(_BIG):
        full[name] = _to_global(walls[k].reshape((N_DEV,) + local[name].shape), axis)
    scale_full = _to_global(sall.reshape(N_DEV, -1)[:, :a_scale.size].reshape((N_DEV,) + a_scale.shape), 1)

    wout_t = jnp.swapaxes(full["w_out"], 1, 2)
    wa = full["a_w_in"]
    wa_t = jnp.swapaxes(wa, 1, 2)
    wg = full["a_w_group"]
    wg_t = jnp.swapaxes(wg, 2, 3)
    wb = full["b_w_in"][0]
    wb_ext = jnp.concatenate([wb[:, :e], _dup_heads(wb[:, e:e + kv_w], n_kv),
                              _dup_heads(wb[:, e + kv_w:e + 2 * kv_w], n_kv), wb[:, e + 2 * kv_w:]], axis=1)
    wb_ext_t = wb_ext.T
    kd_w = n_kv * LANES
    wc = full["c_w_in"][0]
    wc_t = wc.T

    xs, hs, zs, saved = [x.reshape(s, d)], [], [], []
    hs.append(_rmsnorm_fwd(xs[0], norm_g[0:1], "norm0"))
    loss_vec = dfinal = dx = dxb = None
    for i in range(depth):
        kind, j = i % 3, i // 3
        h = hs[i]
        tag = f"l{i}"
        if kind == 0:
            dpool = _proj_pool_fwd(h, wa[j][:, :e], tag + "_in_pool")
            gate = _matmul(h, wa[j][:, e:], BF16, tag + "_in_gate")
            yr, z = _a_group_fwd(dpool, wg[j], scale_full[j:j + 1], gate, tag + "_group")
            saved.append(dict(dpool=dpool, yr=yr, gate=gate))
        elif kind == 1:
            q = _matmul(h, wb_ext[:, :e], BF16, tag + "_in_q")
            kd = _matmul(h, wb_ext[:, e:e + kd_w], BF16, tag + "_in_k")
            vd = _matmul(h, wb_ext[:, e + kd_w:e + 2 * kd_w], BF16, tag + "_in_v")
            gate = _matmul(h, wb_ext[:, e + 2 * kd_w:], BF16, tag + "_in_gate")
            sinks = b_sinks[j]
            y, lse = _attn_fwd(q, kd, vd, sinks, SWA_MAX_DIST, rep, 1, BF16, tag + "_attn")
            z = _gate_fwd(y, gate, tag + "_gate")
            saved.append(dict(q=q, kd=kd, vd=vd, gate=gate, y=y, lse=lse, sinks=sinks))
        else:
            qkv, outs, lses, lses_tok, h_perm = [], [], [], [], []
            for gi, (window, dil) in enumerate(DILATED_PAIRS):
                hp = _perm(h, dil)
                trio = [_matmul(hp, wc[:, (3 * gi + t) * e:(3 * gi + t + 1) * e], BF16,
                                f"{tag}_in_{'qkv'[t]}{gi}") for t in range(3)]
                o, lse, lse_heads = _attn_fwd(trio[0], trio[1], trio[2], None, window // dil, 1, dil, BF16,
                                              f"{tag}_attn{gi}", per_head_lse=True)
                qkv.append(trio)
                h_perm.append(hp)
                outs.append(_unperm(o, dil))
                lses.append(lse)
                lses_tok.append(_unperm(lse_heads, dil))
            gate = _matmul(h, wc[:, 9 * e:], BF16, tag + "_in_gate")
            y, z = _merge_gate_fwd(outs, lses_tok, gate, tag + "_merge")
            saved.append(dict(qkv=qkv, lses=lses, lses_tok=lses_tok, gate=gate, y=y, h_perm=h_perm))
        zs.append(z)
        if i + 1 < depth:
            x_new, h_new = _outproj_norm(z, full["w_out"][i], xs[i], norm_g[i + 1:i + 2], tag + "_out")
            xs.append(x_new)
            hs.append(h_new)
        else:
            dx, dxb, dfinal, loss_vec = _outproj_loss(z, full["w_out"][i], xs[i], final_g.reshape(1, d),
                                                      loss_target.reshape(s, d), tag + "_out_loss")

    g_full = {"w_out": [None] * depth, "a_w_in": [None] * wa.shape[0], "a_w_group": [None] * wa.shape[0]}
    d_norm = [None] * depth
    d_scale = [None] * wa.shape[0]
    d_sinks = None
    for i in reversed(range(depth)):
        kind, j = i % 3, i // 3
        tag = f"b{i}"
        sv = saved[i]
        g_full["w_out"][i] = _matmul_tn(zs[i], dxb, tag + "_dwout", out_dtype=BF16)
        if kind == 0:
            dgate, dyr, dsc = _dz_fused(dxb, wout_t[i], [sv["yr"], sv["gate"]], [scale_full[j:j + 1]], 2,
                                        _a_gate_epilogue, tag + "_dz_gate", 1024, 1024, with_col_sum=True)
            d_scale[j] = dsc
            du = _group_pool_bwd(dyr, wg_t[j], tag + "_dd_pool")
            g_full["a_w_group"][j] = _grouped_weight_grad(sv["dpool"], dyr, n_groups, tag + "_dwg")
            parts = [du, dgate]
            g_full["a_w_in"][j] = jnp.concatenate(
                [_matmul_tn(hs[i], part, f"{tag}_dwin{t}", out_dtype=BF16) for t, part in enumerate(parts)], axis=1)
            dhs = [_matmul_cat(parts, wa_t[j], F32, tag + "_dh")]
        elif kind == 1:
            dgate, do = _dz_fused(dxb, wout_t[i], [sv["y"], sv["gate"]], [], 2, _gate_epilogue,
                                  tag + "_dz_gate", 1024, 1024)
            dq, dkd, dvd, d_sinks = _attn_bwd(sv["q"], sv["kd"], sv["vd"], do, sv["y"], sv["lse"], sv["sinks"],
                                              SWA_MAX_DIST, rep, 1, tag + "_attn")
            parts = [dq, dkd, dvd, dgate]
            dws = [_matmul_tn(hs[i], part, f"{tag}_dwin{t}", out_dtype=BF16) for t, part in enumerate(parts)]
            g_full["b_w_in"] = jnp.concatenate(
                [dws[0], _fold_heads(dws[1], n_kv), _fold_heads(dws[2], n_kv), dws[3]], axis=1)[None]
            dhs = [_matmul_cat(parts, wb_ext_t, F32, tag + "_dh")]
        else:
            dgate, *dos = _dz_fused(dxb, wout_t[i], [sv["y"], sv["gate"]], [], 4, _merge_gate_epilogue,
                                    tag + "_dz_merge", 1024, 512, rows=sv["lses_tok"])
            y_bf = sv["y"]
            dws, dhs = [], []
            for gi, (window, dil) in enumerate(DILATED_PAIRS):
                qv, kv, vv = sv["qkv"][gi]
                grads = _attn_bwd(qv, kv, vv, _perm(dos[gi], dil), _perm(y_bf, dil), sv["lses"][gi], None,
                                  window // dil, 1, dil, f"{tag}_attn{gi}")
                dws += [_matmul_tn(sv["h_perm"][gi], part, f"{tag}_dwin{gi}{'qkv'[t]}", out_dtype=BF16)
                        for t, part in enumerate(grads)]
                dhs.append(_unperm(_matmul_cat(grads, wc_t[3 * gi * e:3 * (gi + 1) * e], F32, f"{tag}_dh{gi}"),
                                   dil))
            dws.append(_matmul_tn(hs[i], dgate, tag + "_dwin_gate", out_dtype=BF16))
            dhs.append(_matmul(dgate, wc_t[9 * e:], F32, tag + "_dh_gate"))
            g_full["c_w_in"] = jnp.concatenate(dws, axis=1)[None]
        dx, dxb, d_norm[i] = _rmsnorm_bwd(dhs, xs[i], norm_g[i:i + 1], dx, tag + "_norm")
    grad_x = dx.reshape(x.shape)
    for name in ("w_out", "a_w_in", "a_w_group"):
        g_full[name] = jnp.stack(g_full[name], axis=0)

    stacks = [_to_stack(g_full[n], axis).astype(BF16).reshape((N_DEV,) + flat[n].shape) for n, axis in _BIG]
    loss_local = (0.5 / d) * jnp.sum(loss_vec)
    small = [jnp.concatenate(d_norm, axis=0), dfinal, d_sinks, jnp.concatenate(d_scale, axis=0),
             loss_local.reshape(1, 1)]
    small_rows = [_rows(a) for a in small]
    small_offs = [sum(r.shape[0] for r in small_rows[:k]) for k in range(len(small_rows) + 1)]
    small_pack = _pad_rows(jnp.concatenate(small_rows, axis=0), 8)
    core = lax.axis_index("c").astype(jnp.int32).reshape(1)
    from_sibling = _sibling_exchange(stacks, "exchange_sibling")
    chip_sums = [_pair_sum(stacks[k], from_sibling[k], core, "sum_pair_" + n) for k, (n, _) in enumerate(_BIG)]
    grecv, srecv = _chip_exchange(chip_sums, small_pack, "exchange_chips")
    ssum = _sum_slots(srecv, "sum_small")

    def small_part(k, like):
        return ssum[small_offs[k]:small_offs[k + 1]].reshape(-1)[:like.size].reshape(like.shape)

    g_norm = small_part(0, norm_g)
    g_final = small_part(1, final_g)
    g_sinks = small_part(2, b_sinks)
    g_scale_full = small_part(3, scale_full)
    loss = ssum[small_offs[4], 0]
    g_scale = lax.dynamic_slice_in_dim(g_scale_full, me * a_scale.shape[1], a_scale.shape[1], axis=1)

    small_w = [("norm_g", norm_g, m_norm_g, v_norm_g, g_norm), ("final_g", final_g, m_final_g, v_final_g, g_final),
               ("a_scale", a_scale, m_a_scale, v_a_scale, g_scale), ("b_sinks", b_sinks, m_b_sinks, v_b_sinks, g_sinks)]
    tail = lambda idx: _pad_rows(jnp.concatenate([_rows(t[idx]) for t in small_w], axis=0), 8)
    tail_sizes = [_rows(t[1]).shape[0] for t in small_w]
    tail_offs = [sum(tail_sizes[:k]) for k in range(len(tail_sizes) + 1)]
    g_tail = tail(4)
    tails = (g_tail,) + _adamw(tail(1), g_tail, tail(2), tail(3), "adamw_small")
    grads, deltas, new_m, new_v = {}, {}, {}, {}
    for k, (name, w_, _, _, _) in enumerate(small_w):
        for out, packed in zip((grads, deltas, new_m, new_v), tails):
            out[name] = packed[tail_offs[k]:tail_offs[k + 1]].reshape(-1)[:w_.size].reshape(w_.shape)
    for k, (name, _) in enumerate(_BIG):
        shape2d = flat[name].shape
        res = _adamw_slots(flat[name], grecv[k], mom_m[name].reshape(shape2d), mom_v[name].reshape(shape2d),
                           "adamw_" + name)
        for out, val in zip((grads, deltas, new_m, new_v), res):
            out[name] = val.reshape(local[name].shape)

    order = ("norm_g", "final_g", "w_out", "a_w_in", "a_w_group", "a_scale", "b_w_in", "b_sinks", "c_w_in")
    return (loss, grad_x, *[grads[n] for n in order], *[deltas[n] for n in order],
            *[new_m[n] for n in order], *[new_v[n] for n in order])
```

```python
import functools

import jax
import jax.numpy as jnp
from jax import lax
from jax.experimental import pallas as pl
from jax.experimental.pallas import tpu as pltpu

F32 = jnp.float32
BF16 = jnp.bfloat16

N_DEV = 8
HEAD_DIM = 64
LANES = 128
BLOCK = 128
Q_PER_KV = 8
POOL_WINDOWS = (2, 4, 8, 16)
POOL_HALO = 16
DILATED_PAIRS = ((128, 1), (512, 4), (2048, 16))
SWA_MAX_DIST = 127
RMS_EPS = 1e-5
PACK_W = 1024
NEG = -1e30

ADAM_LR = 0.001
ADAM_B1 = 0.9
ADAM_B2 = 0.999
ADAM_EPS = 1e-08
ADAM_WD = 0.01
ADAM_STEP = 10

VMEM_LIMIT = 48 * 1024 * 1024


def _params(*sem):
    return pltpu.CompilerParams(dimension_semantics=sem if sem else None, vmem_limit_bytes=VMEM_LIMIT)


def _pick(dim, target, mult=LANES):
    if dim <= target:
        return dim
    t = target - target % mult
    while dim % t:
        t -= mult
    return t


def _sigmoid(x):
    return 1.0 / (1.0 + jnp.exp(-x))


CHIP_OFFSETS = (2, 4, 6)
ANY_SPEC = pl.BlockSpec(memory_space=pl.ANY)


def _where_am_i():
    x, y, c = lax.axis_index("x"), lax.axis_index("y"), lax.axis_index("c")
    return x, y, c, 4 * x + 2 * y + c


def _peer(x, y, c, r):
    return x ^ ((r >> 2) & 1), y ^ ((r >> 1) & 1), c ^ (r & 1)


GATHER_SEMS = 8


def _gather(blocks, split, name):
    n = len(blocks)
    halves = [b.shape[0] // 2 for b in blocks]

    def body(*refs):
        send, recv = refs[:n], refs[n:2 * n]
        send_sems, recv_sems, local_sems = refs[2 * n:]
        x, y, c, me = _where_am_i()
        sib, xn, yn, dg = (_peer(x, y, c, r) for r in (1, 4, 2, 6))
        sib_id, xn_id, yn_id, dg_id = me ^ 1, me ^ 4, me ^ 2, me ^ 6

        def copy(k, sem, src, dst, to):
            return pltpu.make_async_remote_copy(
                src_ref=src, dst_ref=dst, send_sem=send_sems.at[k, sem], recv_sem=recv_sems.at[k, sem],
                device_id=to, device_id_type=pl.DeviceIdType.MESH)

        def part(k, slot, half):
            return recv[k].at[slot, pl.ds(half * halves[k], halves[k])]

        started = []
        for k in range(n):
            own = pltpu.make_async_copy(send[k], recv[k].at[me], local_sems.at[k])
            own.start()
            started.append(own)
        sends = []
        for k in range(n):
            sends += [copy(k, 0, send[k], recv[k].at[me], sib), copy(k, 1, send[k], recv[k].at[me], xn),
                      copy(k, 2, send[k], recv[k].at[me], yn)]
            if not split[k]:
                sends.append(copy(k, 3, send[k], recv[k].at[me], dg))
        for cp in sends:
            cp.start()

        def after(k, sem, slot, hand_on_sem, half, half_sem, half_to):
            copy(k, sem, send[k], recv[k].at[slot], sib).wait_recv()
            new = [copy(k, hand_on_sem, recv[k].at[slot], recv[k].at[slot], sib)]
            if split[k]:
                new.append(copy(k, half_sem, part(k, slot, half), part(k, slot, half), half_to))
            for cp in new:
                cp.start()
            sends.extend(new)

        for k in range(n):
            after(k, 2, yn_id, 6, 0, 3, xn)
        for k in range(n):
            after(k, 1, xn_id, 5, 1, 4, yn)
        for k in range(n):
            if split[k]:
                copy(k, 3, part(k, dg_id, 0), part(k, dg_id, 0), sib).wait_recv()
                copy(k, 4, part(k, dg_id, 1), part(k, dg_id, 1), sib).wait_recv()
            else:
                copy(k, 3, send[k], recv[k].at[dg_id], sib).wait_recv()
            fwd = copy(k, 7, recv[k].at[dg_id], recv[k].at[dg_id], sib)
            fwd.start()
            sends.append(fwd)
        for k in range(n):
            copy(k, 0, send[k], recv[k].at[sib_id], sib).wait_recv()
            for sem, r in ((5, 4), (6, 2), (7, 6)):
                copy(k, sem, send[k], recv[k].at[sib_id ^ r], sib).wait_recv()
        for cp in sends:
            cp.wait_send()
        for own in started:
            own.wait()

    return pl.pallas_call(
        body, name=name,
        out_shape=tuple(jax.ShapeDtypeStruct((N_DEV,) + b.shape, b.dtype) for b in blocks),
        in_specs=[ANY_SPEC] * n,
        out_specs=tuple([ANY_SPEC] * n),
        scratch_shapes=[pltpu.SemaphoreType.DMA((n, GATHER_SEMS)), pltpu.SemaphoreType.DMA((n, GATHER_SEMS)),
                        pltpu.SemaphoreType.DMA((n,))],
    )(*blocks)


def _sibling_exchange(stacks, name):
    n_chips = N_DEV // 2
    n = len(stacks)

    def body(*refs):
        g_refs, t_refs = refs[:n], refs[n:2 * n]
        send_sems, recv_sems = refs[2 * n:]
        x, y, c, _ = _where_am_i()
        sib = _peer(x, y, c, 1)
        copies = [pltpu.make_async_remote_copy(
            src_ref=g_refs[k].at[2 * chip + (1 - c)], dst_ref=t_refs[k].at[chip], send_sem=send_sems.at[k, chip],
            recv_sem=recv_sems.at[k, chip], device_id=sib, device_id_type=pl.DeviceIdType.MESH)
            for k in range(n) for chip in range(n_chips)]
        for cp in copies:
            cp.start()
        for cp in copies:
            cp.wait_recv()
        for cp in copies:
            cp.wait_send()

    return pl.pallas_call(
        body, name=name,
        out_shape=tuple(jax.ShapeDtypeStruct((n_chips,) + g.shape[1:], g.dtype) for g in stacks),
        in_specs=[ANY_SPEC] * n, out_specs=tuple([ANY_SPEC] * n),
        scratch_shapes=[pltpu.SemaphoreType.DMA((n, n_chips)), pltpu.SemaphoreType.DMA((n, n_chips))],
    )(*stacks)


def _chip_exchange(csums, small, name):
    n = len(csums)

    def body(*refs):
        c_refs, s_ref = refs[:n], refs[n]
        r_refs, sr_ref = refs[n + 1:2 * n + 1], refs[2 * n + 1]
        send_sems, recv_sems, small_send, small_recv, local_sems = refs[2 * n + 2:]
        x, y, c, me = _where_am_i()
        my_chip = 2 * x + y
        own = [pltpu.make_async_copy(c_refs[k].at[my_chip], r_refs[k].at[my_chip], local_sems.at[k])
               for k in range(n)]
        own.append(pltpu.make_async_copy(s_ref, sr_ref.at[me], local_sems.at[n]))
        for cp in own:
            cp.start()
        sends, recvs = [], []
        for j, r in enumerate(CHIP_OFFSETS):
            to = _peer(x, y, c, r)
            chip = my_chip ^ (r >> 1)
            for k in range(n):
                sends.append(pltpu.make_async_remote_copy(
                    src_ref=c_refs[k].at[chip], dst_ref=r_refs[k].at[my_chip], send_sem=send_sems.at[k, j],
                    recv_sem=recv_sems.at[k, j], device_id=to, device_id_type=pl.DeviceIdType.MESH))
                recvs.append(pltpu.make_async_remote_copy(
                    src_ref=c_refs[k].at[chip], dst_ref=r_refs[k].at[chip], send_sem=send_sems.at[k, j],
                    recv_sem=recv_sems.at[k, j], device_id=to, device_id_type=pl.DeviceIdType.MESH))
        for r in range(1, N_DEV):
            to = _peer(x, y, c, r)
            sends.append(pltpu.make_async_remote_copy(
                src_ref=s_ref, dst_ref=sr_ref.at[me], send_sem=small_send.at[r - 1],
                recv_sem=small_recv.at[r - 1], device_id=to, device_id_type=pl.DeviceIdType.MESH))
            recvs.append(pltpu.make_async_remote_copy(
                src_ref=s_ref, dst_ref=sr_ref.at[me ^ r], send_sem=small_send.at[r - 1],
                recv_sem=small_recv.at[r - 1], device_id=to, device_id_type=pl.DeviceIdType.MESH))
        for cp in sends:
            cp.start()
        for cp in recvs:
            cp.wait_recv()
        for cp in sends:
            cp.wait_send()
        for cp in own:
            cp.wait()

    n_off = len(CHIP_OFFSETS)
    outs = pl.pallas_call(
        body, name=name,
        out_shape=tuple(jax.ShapeDtypeStruct(cs.shape, cs.dtype) for cs in csums)
        + (jax.ShapeDtypeStruct((N_DEV,) + small.shape, small.dtype),),
        in_specs=[ANY_SPEC] * (n + 1), out_specs=tuple([ANY_SPEC] * (n + 1)),
        scratch_shapes=[pltpu.SemaphoreType.DMA((n, n_off)), pltpu.SemaphoreType.DMA((n, n_off)),
                        pltpu.SemaphoreType.DMA((N_DEV - 1,)), pltpu.SemaphoreType.DMA((N_DEV - 1,)),
                        pltpu.SemaphoreType.DMA((n + 1,))],
    )(*csums, small)
    return list(outs[:n]), outs[n]


def _pair_sum(gpack, other, core, name, ts=256):
    n_chips, r, c = other.shape
    ts = _pick(r, ts, 16)

    def body(core_ref, g_ref, o_ref, out_ref):
        del core_ref
        out_ref[...] = (g_ref[...].astype(F32) + o_ref[...].astype(F32)).astype(out_ref.dtype)

    return pl.pallas_call(
        body, name=name,
        out_shape=jax.ShapeDtypeStruct(other.shape, other.dtype),
        grid_spec=pltpu.PrefetchScalarGridSpec(
            num_scalar_prefetch=1, grid=(n_chips, r // ts),
            in_specs=[pl.BlockSpec((None, ts, c), lambda j, i, core_ref: (2 * j + core_ref[0], i, 0)),
                      pl.BlockSpec((None, ts, c), lambda j, i, core_ref: (j, i, 0))],
            out_specs=pl.BlockSpec((None, ts, c), lambda j, i, core_ref: (j, i, 0))),
        compiler_params=_params("parallel", "parallel"),
    )(core, gpack, other)


def _matmul(a, b, out_dtype, name, tm=1024, tn=1024, tk=1024):
    m, kdim = a.shape
    n = b.shape[1]
    tm, tn, tk = _pick(m, tm), _pick(n, tn), _pick(kdim, tk)
    nk = kdim // tk

    if nk == 1:
        def body(a_ref, b_ref, o_ref):
            o_ref[...] = jnp.dot(a_ref[...], b_ref[...], preferred_element_type=F32).astype(o_ref.dtype)
        scratch = []
    else:
        def body(a_ref, b_ref, o_ref, acc_ref):
            kk = pl.program_id(2)

            @pl.when(kk == 0)
            def _():
                acc_ref[...] = jnp.zeros_like(acc_ref)

            acc_ref[...] += jnp.dot(a_ref[...], b_ref[...], preferred_element_type=F32)

            @pl.when(kk == nk - 1)
            def _():
                o_ref[...] = acc_ref[...].astype(o_ref.dtype)
        scratch = [pltpu.VMEM((tm, tn), F32)]

    return pl.pallas_call(
        body, name=name,
        out_shape=jax.ShapeDtypeStruct((m, n), out_dtype),
        grid=(m // tm, n // tn, nk),
        in_specs=[pl.BlockSpec((tm, tk), lambda i, j, k: (i, k)),
                  pl.BlockSpec((tk, tn), lambda i, j, k: (k, j))],
        out_specs=pl.BlockSpec((tm, tn), lambda i, j, k: (i, j)),
        scratch_shapes=scratch,
        compiler_params=_params("parallel", "parallel", "arbitrary"),
    )(a, b)


def _matmul_cat(parts, b, out_dtype, name, tm=1024, tn=1024, tk=1024):
    m = parts[0].shape[0]
    n = b.shape[1]
    tm, tn = _pick(m, tm), _pick(n, tn)
    tk = min(_pick(p.shape[1], tk) for p in parts)
    steps = [p.shape[1] // tk for p in parts]
    assert all(p.shape[1] % tk == 0 for p in parts)
    starts = [sum(steps[:t]) for t in range(len(parts))]
    nk = sum(steps)
    n_parts = len(parts)

    def body(*refs):
        a_refs, b_ref, o_ref, acc_ref = refs[:n_parts], refs[n_parts], refs[n_parts + 1], refs[n_parts + 2]
        kk = pl.program_id(2)

        @pl.when(kk == 0)
        def _():
            acc_ref[...] = jnp.zeros_like(acc_ref)

        for t in range(n_parts):
            @pl.when(jnp.logical_and(kk >= starts[t], kk < starts[t] + steps[t]))
            def _(t=t):
                acc_ref[...] += jnp.dot(a_refs[t][...], b_ref[...], preferred_element_type=F32)

        @pl.when(kk == nk - 1)
        def _():
            o_ref[...] = acc_ref[...].astype(o_ref.dtype)

    def part_map(t):
        return lambda i, j, k: (i, jnp.clip(k - starts[t], 0, steps[t] - 1))

    return pl.pallas_call(
        body, name=name,
        out_shape=jax.ShapeDtypeStruct((m, n), out_dtype),
        grid=(m // tm, n // tn, nk),
        in_specs=[pl.BlockSpec((tm, tk), part_map(t)) for t in range(n_parts)]
        + [pl.BlockSpec((tk, tn), lambda i, j, k: (k, j))],
        out_specs=pl.BlockSpec((tm, tn), lambda i, j, k: (i, j)),
        scratch_shapes=[pltpu.VMEM((tm, tn), F32)],
        compiler_params=_params("parallel", "parallel", "arbitrary"),
    )(*parts, b)


def _matmul_tn(a, b, name, tm=1024, tn=1024, tk=1024, out_dtype=F32):
    kdim, m = a.shape
    n = b.shape[1]
    tm, tn, tk = _pick(m, tm), _pick(n, tn), _pick(kdim, tk)
    nk = kdim // tk

    def body(a_ref, b_ref, o_ref, acc_ref):
        kk = pl.program_id(2)

        @pl.when(kk == 0)
        def _():
            acc_ref[...] = jnp.zeros_like(acc_ref)

        acc_ref[...] += lax.dot_general(a_ref[...], b_ref[...], (((0,), (0,)), ((), ())),
                                        preferred_element_type=F32)

        @pl.when(kk == nk - 1)
        def _():
            o_ref[...] = acc_ref[...].astype(o_ref.dtype)

    return pl.pallas_call(
        body, name=name,
        out_shape=jax.ShapeDtypeStruct((m, n), out_dtype),
        grid=(m // tm, n // tn, nk),
        in_specs=[pl.BlockSpec((tk, tm), lambda i, j, k: (k, i)),
                  pl.BlockSpec((tk, tn), lambda i, j, k: (k, j))],
        out_specs=pl.BlockSpec((tm, tn), lambda i, j, k: (i, j)),
        scratch_shapes=[pltpu.VMEM((tm, tn), F32)],
        compiler_params=_params("parallel", "parallel", "arbitrary"),
    )(a, b)


def _grouped_weight_grad(a, b, ng, name, tk=1024):
    s, e = a.shape
    g = e // ng
    tk = _pick(s, tk)
    nk = s // tk

    def body(a_ref, b_ref, o_ref):
        kk = pl.program_id(1)

        @pl.when(kk == 0)
        def _():
            o_ref[...] = jnp.zeros_like(o_ref)

        o_ref[...] += lax.dot_general(a_ref[...], b_ref[...], (((0,), (0,)), ((), ())),
                                      preferred_element_type=F32)

    return pl.pallas_call(
        body, name=name,
        out_shape=jax.ShapeDtypeStruct((ng, g, g), F32),
        grid=(ng, nk),
        in_specs=[pl.BlockSpec((tk, g), lambda j, k: (k, j)),
                  pl.BlockSpec((tk, g), lambda j, k: (k, j))],
        out_specs=pl.BlockSpec((None, g, g), lambda j, k: (j, 0, 0)),
        compiler_params=_params("parallel", "arbitrary"),
    )(a, b)


def _rms(x):
    r = lax.rsqrt(jnp.mean(x * x, axis=1, keepdims=True) + RMS_EPS)
    return x * r, r


def _rmsnorm_fwd(x, g, name, ts=256):
    s, d = x.shape
    ts = _pick(s, ts, 8)

    def body(x_ref, g_ref, h_ref):
        xhat, _ = _rms(x_ref[...])
        h_ref[...] = (xhat * g_ref[...]).astype(BF16)

    return pl.pallas_call(
        body, name=name,
        out_shape=jax.ShapeDtypeStruct((s, d), BF16),
        grid=(s // ts,),
        in_specs=[pl.BlockSpec((ts, d), lambda i: (i, 0)), pl.BlockSpec((1, d), lambda i: (0, 0))],
        out_specs=pl.BlockSpec((ts, d), lambda i: (i, 0)),
        compiler_params=_params("parallel"),
    )(x, g)


def _outproj_norm(z, w, x, g, name, tm=512):
    s, e = z.shape
    d = w.shape[1]
    tm = _pick(s, tm)

    def body(z_ref, w_ref, x_ref, g_ref, xo_ref, h_ref):
        xn = x_ref[...] + jnp.dot(z_ref[...], w_ref[...], preferred_element_type=F32)
        xo_ref[...] = xn
        xhat, _ = _rms(xn)
        h_ref[...] = (xhat * g_ref[...]).astype(BF16)

    return pl.pallas_call(
        body, name=name,
        out_shape=(jax.ShapeDtypeStruct((s, d), F32), jax.ShapeDtypeStruct((s, d), BF16)),
        grid=(s // tm,),
        in_specs=[pl.BlockSpec((tm, e), lambda i: (i, 0)), pl.BlockSpec((e, d), lambda i: (0, 0)),
                  pl.BlockSpec((tm, d), lambda i: (i, 0)), pl.BlockSpec((1, d), lambda i: (0, 0))],
        out_specs=(pl.BlockSpec((tm, d), lambda i: (i, 0)), pl.BlockSpec((tm, d), lambda i: (i, 0))),
        compiler_params=_params("parallel"),
    )(z, w, x, g)


def _outproj_loss(z, w, x, g, target, name, tm=512):
    s, e = z.shape
    d = w.shape[1]
    tm = _pick(s, tm)

    def body(z_ref, w_ref, x_ref, g_ref, t_ref, dx_ref, dxb_ref, dg_ref, loss_ref):
        i = pl.program_id(0)
        xn = x_ref[...] + jnp.dot(z_ref[...], w_ref[...], preferred_element_type=F32)
        xhat, r = _rms(xn)
        gain = g_ref[...]
        diff = xhat * gain - t_ref[...]
        dout = diff * (1.0 / d)
        dxhat = dout * gain
        dx = r * (dxhat - xhat * jnp.mean(dxhat * xhat, axis=1, keepdims=True))
        dx_ref[...] = dx
        dxb_ref[...] = dx.astype(BF16)

        @pl.when(i == 0)
        def _():
            dg_ref[...] = jnp.zeros_like(dg_ref)
            loss_ref[...] = jnp.zeros_like(loss_ref)

        dg_ref[...] += jnp.sum(dout * xhat, axis=0, keepdims=True)
        loss_ref[...] += jnp.sum(diff * diff, axis=0, keepdims=True)

    row = lambda i: (i, 0)
    fixed = lambda i: (0, 0)
    return pl.pallas_call(
        body, name=name,
        out_shape=(jax.ShapeDtypeStruct((s, d), F32), jax.ShapeDtypeStruct((s, d), BF16),
                   jax.ShapeDtypeStruct((1, d), F32), jax.ShapeDtypeStruct((1, d), F32)),
        grid=(s // tm,),
        in_specs=[pl.BlockSpec((tm, e), row), pl.BlockSpec((e, d), fixed), pl.BlockSpec((tm, d), row),
                  pl.BlockSpec((1, d), fixed), pl.BlockSpec((tm, d), row)],
        out_specs=(pl.BlockSpec((tm, d), row), pl.BlockSpec((tm, d), row),
                   pl.BlockSpec((1, d), fixed), pl.BlockSpec((1, d), fixed)),
        compiler_params=_params("arbitrary"),
    )(z, w, x, g, target)


def _rmsnorm_bwd(dhs, x, g, dx_next, name, ts=256):
    s, d = x.shape
    ts = _pick(s, ts, 8)
    n_dh = len(dhs)

    def body(*refs):
        dh_refs = refs[:n_dh]
        x_ref, g_ref, dn_ref, dx_ref, dxb_ref, dg_ref = refs[n_dh:]
        i = pl.program_id(0)
        xhat, r = _rms(x_ref[...])
        dh_ = dh_refs[0][...]
        for extra in dh_refs[1:]:
            dh_ = dh_ + extra[...]
        dxhat = dh_ * g_ref[...]
        dx = dn_ref[...] + r * (dxhat - xhat * jnp.mean(dxhat * xhat, axis=1, keepdims=True))
        dx_ref[...] = dx
        dxb_ref[...] = dx.astype(BF16)

        @pl.when(i == 0)
        def _():
            dg_ref[...] = jnp.zeros_like(dg_ref)

        dg_ref[...] += jnp.sum(dh_ * xhat, axis=0, keepdims=True)

    row = lambda i: (i, 0)
    fixed = lambda i: (0, 0)
    return pl.pallas_call(
        body, name=name,
        out_shape=(jax.ShapeDtypeStruct((s, d), F32), jax.ShapeDtypeStruct((s, d), BF16),
                   jax.ShapeDtypeStruct((1, d), F32)),
        grid=(s // ts,),
        in_specs=[pl.BlockSpec((ts, d), row)] * n_dh + [pl.BlockSpec((ts, d), row), pl.BlockSpec((1, d), fixed),
                                                        pl.BlockSpec((ts, d), row)],
        out_specs=(pl.BlockSpec((ts, d), row), pl.BlockSpec((ts, d), row), pl.BlockSpec((1, d), fixed)),
        compiler_params=_params("arbitrary"),
    )(*dhs, x, g, dx_next)


def _pool_counts(t0, rows, cols, window):
    t = t0 + lax.broadcasted_iota(jnp.int32, (rows, cols), 0)
    return jnp.minimum(t + 1, window).astype(F32)


def _proj_pool_fwd(h, w, name, ts=1024, tc=512):
    s, dm = h.shape
    e = w.shape[1]
    ng = len(POOL_WINDOWS)
    gdim = e // ng
    ts, tc = _pick(s, ts), _pick(gdim, tc)
    cpg = gdim // tc
    hb = ts // POOL_HALO

    def body(h_ref, halo_ref, w_ref, d_ref):
        i, grp = pl.program_id(0), pl.program_id(1)
        cur = jnp.dot(h_ref[...], w_ref[...], preferred_element_type=F32)
        halo = jnp.dot(halo_ref[...], w_ref[...], preferred_element_type=F32)
        ext = jnp.concatenate([jnp.where(i > 0, halo, 0.0), cur], axis=0)
        for gi, window in enumerate(POOL_WINDOWS):
            @pl.when(grp == gi)
            def _(window=window):
                acc = ext
                k = 1
                while k < window:
                    acc = acc + pltpu.roll(acc, k, 0)
                    k *= 2
                pooled = acc[POOL_HALO:, :] / _pool_counts(i * ts, ts, tc, window)
                d_ref[...] = (pooled - cur).astype(BF16)

    return pl.pallas_call(
        body, name=name,
        out_shape=jax.ShapeDtypeStruct((s, e), BF16),
        grid=(s // ts, ng, cpg),
        in_specs=[pl.BlockSpec((ts, dm), lambda i, g, j: (i, 0)),
                  pl.BlockSpec((POOL_HALO, dm), lambda i, g, j: (jnp.maximum(i * hb - 1, 0), 0)),
                  pl.BlockSpec((dm, tc), lambda i, g, j: (0, g * cpg + j))],
        out_specs=pl.BlockSpec((ts, tc), lambda i, g, j: (i, g * cpg + j)),
        compiler_params=_params("parallel", "parallel", "parallel"),
    )(h, h, w)


def _dz_fused(dxb, w_t, tiles, vecs, n_out, epilogue, name, tm, tn, with_col_sum=False, rows=()):
    s, dm = dxb.shape
    e = w_t.shape[1]
    tm, tn = _pick(s, tm), _pick(e, tn)
    n_t, n_v, n_r = len(tiles), len(vecs), len(rows)

    def body(*refs):
        a_ref, b_ref = refs[:2]
        tile_refs, vec_refs = refs[2:2 + n_t], refs[2 + n_t:2 + n_t + n_v]
        row_refs = refs[2 + n_t + n_v:2 + n_t + n_v + n_r]
        out_refs = refs[2 + n_t + n_v + n_r:]
        i = pl.program_id(1)
        dz = jnp.dot(a_ref[...], b_ref[...], preferred_element_type=F32)
        extra = ([r[...] for r in row_refs], pl.program_id(0) * tn) if n_r else ()
        res = epilogue(dz, [t[...] for t in tile_refs], [v[...] for v in vec_refs], *extra)
        for o_ref, val in zip(out_refs[:n_out], res[:n_out]):
            o_ref[...] = val.astype(o_ref.dtype)
        if with_col_sum:
            sum_ref = out_refs[n_out]

            @pl.when(i == 0)
            def _():
                sum_ref[...] = jnp.zeros_like(sum_ref)

            sum_ref[...] += jnp.sum(res[n_out], axis=0, keepdims=True)

    blk = lambda j, i: (i, j)
    vec = lambda j, i: (0, j)
    out_shape = [jax.ShapeDtypeStruct((s, e), BF16)] * n_out
    out_specs = [pl.BlockSpec((tm, tn), blk)] * n_out
    if with_col_sum:
        out_shape.append(jax.ShapeDtypeStruct((1, e), F32))
        out_specs.append(pl.BlockSpec((1, tn), vec))
    return pl.pallas_call(
        body, name=name,
        out_shape=tuple(out_shape),
        grid=(e // tn, s // tm),
        in_specs=[pl.BlockSpec((tm, dm), lambda j, i: (i, 0)), pl.BlockSpec((dm, tn), lambda j, i: (0, j))]
        + [pl.BlockSpec((tm, tn), blk)] * n_t + [pl.BlockSpec((1, tn), vec)] * n_v
        + [pl.BlockSpec((tm, r.shape[1]), lambda j, i: (i, 0)) for r in rows],
        out_specs=tuple(out_specs),
        compiler_params=_params("parallel", "arbitrary"),
    )(dxb, w_t, *tiles, *vecs, *rows)


def _group_pool_bwd(dyr, w_t, name, ts=1024, tc=512):
    s, e = dyr.shape
    ng = len(POOL_WINDOWS)
    gdim = e // ng
    ts, tc = _pick(s, ts), _pick(gdim, tc)
    cpg = gdim // tc
    hb = ts // POOL_HALO
    n_halo = s // POOL_HALO
    nst = s // ts

    def body(dy_ref, halo_ref, w_ref, du_ref):
        i, grp = pl.program_id(0), pl.program_id(1)
        cur = jnp.dot(dy_ref[...], w_ref[...], preferred_element_type=F32)
        halo = jnp.dot(halo_ref[...], w_ref[...], preferred_element_type=F32)
        ext = jnp.concatenate([cur, jnp.where(i < nst - 1, halo, 0.0)], axis=0)
        rows = ts + POOL_HALO
        for gi, window in enumerate(POOL_WINDOWS):
            @pl.when(grp == gi)
            def _(window=window):
                acc = ext / _pool_counts(i * ts, rows, tc, window)
                k = 1
                while k < window:
                    acc = acc + pltpu.roll(acc, rows - k, 0)
                    k *= 2
                du_ref[...] = (acc[:ts, :] - cur).astype(BF16)

    return pl.pallas_call(
        body, name=name,
        out_shape=jax.ShapeDtypeStruct((s, e), BF16),
        grid=(nst, ng, cpg),
        in_specs=[pl.BlockSpec((ts, gdim), lambda i, g, j: (i, g)),
                  pl.BlockSpec((POOL_HALO, gdim), lambda i, g, j: (jnp.minimum((i + 1) * hb, n_halo - 1), g)),
                  pl.BlockSpec((None, gdim, tc), lambda i, g, j: (g, 0, j))],
        out_specs=pl.BlockSpec((ts, tc), lambda i, g, j: (i, g * cpg + j)),
        compiler_params=_params("parallel", "parallel", "parallel"),
    )(dyr, dyr, w_t)


def _a_group_fwd(d, w, scale, gate, name, tm=1024):
    s, e = d.shape
    ng, g, _ = w.shape
    tm = _pick(s, tm)

    def body(d_ref, w_ref, s_ref, gate_ref, yr_ref, z_ref):
        yr = jnp.dot(d_ref[...], w_ref[...], preferred_element_type=F32)
        yr_ref[...] = yr.astype(yr_ref.dtype)
        gt = gate_ref[...].astype(F32)
        z_ref[...] = ((yr * s_ref[...]) * (gt * _sigmoid(gt))).astype(BF16)

    blk = lambda i, j: (i, j)
    return pl.pallas_call(
        body, name=name,
        out_shape=(jax.ShapeDtypeStruct((s, e), BF16), jax.ShapeDtypeStruct((s, e), BF16)),
        grid=(s // tm, ng),
        in_specs=[pl.BlockSpec((tm, g), blk), pl.BlockSpec((None, g, g), lambda i, j: (j, 0, 0)),
                  pl.BlockSpec((1, g), lambda i, j: (0, j)), pl.BlockSpec((tm, g), blk)],
        out_specs=(pl.BlockSpec((tm, g), blk), pl.BlockSpec((tm, g), blk)),
        compiler_params=_params("parallel", "parallel"),
    )(d, w, scale, gate)


def _silu_and_slope(gt):
    sg = _sigmoid(gt)
    return gt * sg, sg * (1.0 + gt * (1.0 - sg))


def _a_gate_epilogue(dz, tiles, vecs):
    yr, gt = tiles[0].astype(F32), tiles[1].astype(F32)
    sc = vecs[0]
    silu, slope = _silu_and_slope(gt)
    dy = dz * silu
    return dz * (yr * sc) * slope, dy * sc, dy * yr


def _gate_epilogue(dz, tiles, vecs):
    y, gt = tiles[0].astype(F32), tiles[1].astype(F32)
    silu, slope = _silu_and_slope(gt)
    return dz * y * slope, dz * silu


def _merge_gate_epilogue(dz, tiles, vecs, lses, col0):
    y, gt = tiles[0].astype(F32), tiles[1].astype(F32)
    silu, slope = _silu_and_slope(gt)
    dy = dz * silu
    w0, w1, w2 = _merge_weights_expanded(lses, col0, dz.shape[1])
    return dz * y * slope, w0 * dy, w1 * dy, w2 * dy


def _gate_fwd(y, gate, name, ts=512, tc=512):
    s, e = y.shape
    ts, tc = _pick(s, ts), _pick(e, tc)

    def body(y_ref, gate_ref, z_ref):
        gt = gate_ref[...].astype(F32)
        z_ref[...] = (y_ref[...].astype(F32) * (gt * _sigmoid(gt))).astype(BF16)

    blk = lambda i, j: (i, j)
    return pl.pallas_call(
        body, name=name,
        out_shape=jax.ShapeDtypeStruct((s, e), BF16),
        grid=(s // ts, e // tc),
        in_specs=[pl.BlockSpec((ts, tc), blk)] * 2,
        out_specs=pl.BlockSpec((ts, tc), blk),
        compiler_params=_params("parallel", "parallel"),
    )(y, gate)


def _merge_weights(l0, l1, l2):
    m = jnp.maximum(jnp.maximum(l0, l1), l2)
    e0, e1, e2 = jnp.exp(l0 - m), jnp.exp(l1 - m), jnp.exp(l2 - m)
    inv = 1.0 / (e0 + e1 + e2)
    return e0 * inv, e1 * inv, e2 * inv


def _expand_heads(w, col0, width):
    n_heads = w.shape[1]
    head_of_lane = (col0 + lax.broadcasted_iota(jnp.int32, (n_heads, width), 1)) // HEAD_DIM
    pick = jnp.where(head_of_lane == lax.broadcasted_iota(jnp.int32, (n_heads, width), 0), 1.0, 0.0).astype(BF16)
    high = w.astype(BF16)
    rest = (w - high.astype(F32)).astype(BF16)
    return (jnp.dot(high, pick, preferred_element_type=F32) + jnp.dot(rest, pick, preferred_element_type=F32))


def _merge_weights_expanded(lses, col0, width):
    return [_expand_heads(w, col0, width) for w in _merge_weights(*lses)]


def _merge_gate_fwd(outs, lses, gate, name, ts=512, tc=512):
    s, e = gate.shape
    n_heads = lses[0].shape[1]
    ts, tc = _pick(s, ts), _pick(e, tc)

    def body(o0, o1, o2, l0, l1, l2, gate_ref, y_ref, z_ref):
        w0, w1, w2 = _merge_weights_expanded([l0[...], l1[...], l2[...]], pl.program_id(1) * tc, tc)
        y = w0 * o0[...].astype(F32) + w1 * o1[...].astype(F32) + w2 * o2[...].astype(F32)
        y_ref[...] = y.astype(y_ref.dtype)
        gt = gate_ref[...].astype(F32)
        z_ref[...] = (y * (gt * _sigmoid(gt))).astype(BF16)

    blk = lambda i, j: (i, j)
    per_head = pl.BlockSpec((ts, n_heads), lambda i, j: (i, 0))
    return pl.pallas_call(
        body, name=name,
        out_shape=(jax.ShapeDtypeStruct((s, e), BF16), jax.ShapeDtypeStruct((s, e), BF16)),
        grid=(s // ts, e // tc),
        in_specs=[pl.BlockSpec((ts, tc), blk)] * 3 + [per_head] * 3 + [pl.BlockSpec((ts, tc), blk)],
        out_specs=(pl.BlockSpec((ts, tc), blk), pl.BlockSpec((ts, tc), blk)),
        compiler_params=_params("parallel", "parallel"),
    )(*outs, *lses, gate)


def _band(max_dist, width):
    row = lax.broadcasted_iota(jnp.int32, (2 * BLOCK, width), 0) & (BLOCK - 1)
    col = lax.broadcasted_iota(jnp.int32, (2 * BLOCK, width), 1)
    low = row if max_dist == BLOCK else row + 1
    return jnp.logical_and(col >= low, col <= row + BLOCK), col >= BLOCK


def _fill_bias(bias_ref, max_dist):
    @pl.when(jnp.logical_and(pl.program_id(0) == 0, jnp.logical_and(pl.program_id(1) == 0, pl.program_id(2) == 0)))
    def _():
        band, own = _band(max_dist, 2 * BLOCK)
        bias_ref[0] = jnp.where(band, 0.0, NEG)
        bias_ref[1] = jnp.where(jnp.logical_and(band, own), 0.0, NEG)


def _aligned(v):
    return v if isinstance(v, int) else pl.multiple_of(v, BLOCK)


def _stack_heads(x, lo):
    return jnp.concatenate([jnp.where(lo, x, 0.0), jnp.where(lo, 0.0, x)], axis=0).astype(BF16)


def _unstack_heads(x2, lo):
    return jnp.where(lo, x2[:BLOCK], x2[BLOCK:])


def _head_col(x, hm):
    return jnp.max(jnp.where(hm, x, NEG), axis=1, keepdims=True)


def _dot_nt(a, b):
    return lax.dot_general(a, b, (((1,), (1,)), ((), ())), preferred_element_type=F32)


def _dot_tn(a, b):
    return lax.dot_general(a, b, (((0,), (0,)), ((), ())), preferred_element_type=F32)


def _stream_view(a, dil):
    s, w = a.shape
    return a.reshape(s // (BLOCK * dil), dil, BLOCK, w)


def _fill_window(dst, halo_ref, cur_ref, n):
    dst[0:BLOCK, :] = halo_ref[0]
    for jc in range(n):
        dst[(jc + 1) * BLOCK:(jc + 2) * BLOCK, :] = cur_ref[jc]


def _attn_fwd(q, k, v, sinks, max_dist, rep, dil, out_dtype, name, tq=2048, per_head_lse=False):
    assert max_dist in (BLOCK - 1, BLOCK)
    s, w = q.shape
    l = s // dil
    n_pairs = w // LANES
    n_heads = 2 * n_pairs
    tq = _pick(l, tq)
    n = tq // BLOCK
    has_sink = sinks is not None
    scale = HEAD_DIM ** -0.5

    def body(*refs):
        if has_sink:
            sink_ref, refs = refs[0], refs[1:]
        q_ref, kc_ref, kh_ref, vc_ref, vh_ref, o_ref, lse_ref = refs[:7]
        refs = refs[7:]
        if per_head_lse:
            lseh_ref, refs = refs[0], refs[1:]
        kx, vx, bias_ref = refs
        i, p = pl.program_id(0), pl.program_id(2)
        _fill_window(kx, kh_ref, kc_ref, n)
        _fill_window(vx, vh_ref, vc_ref, n)
        if per_head_lse:
            @pl.when(p == 0)
            def _():
                lseh_ref[...] = jnp.zeros_like(lseh_ref)
            head_lane = lax.broadcasted_iota(jnp.int32, (BLOCK, n_heads), 1)
        lo = lax.broadcasted_iota(jnp.int32, (BLOCK, LANES), 1) < HEAD_DIM
        _fill_bias(bias_ref, max_dist)
        top = lax.broadcasted_iota(jnp.int32, (2 * BLOCK, 1), 0) < BLOCK

        def scores(j):
            r0 = _aligned(j * BLOCK)
            q2 = _stack_heads(q_ref[j].astype(F32) * scale, lo)
            first = jnp.logical_and(i == 0, j == 0).astype(jnp.int32)
            return _dot_nt(q2, kx[pl.ds(r0, 2 * BLOCK), :]) + bias_ref[first]

        per_step = 2 if n % 2 == 0 else 1

        def step(jj, carry):
            nxt = tuple(scores(jnp.minimum((jj + 1) * per_step + t, n - 1)) for t in range(per_step))
            for t in range(per_step):
                finish(jj * per_step + t, carry[t])
            return nxt

        def finish(j, s2):
            r0 = _aligned(j * BLOCK)
            vw = vx[pl.ds(r0, 2 * BLOCK), :]
            m = jnp.max(s2, axis=1, keepdims=True)
            if has_sink:
                sk = jnp.where(top, sink_ref[2 * p], sink_ref[2 * p + 1])
                m = jnp.maximum(m, sk)
            pr = jnp.exp(s2 - m)
            den = jnp.sum(pr, axis=1, keepdims=True)
            if has_sink:
                den = den + jnp.exp(sk - m)
            o2 = jnp.dot(pr.astype(BF16), vw, preferred_element_type=F32) * (1.0 / den)
            lse2 = m + jnp.log(den)
            o_ref[j] = _unstack_heads(o2, lo).astype(o_ref.dtype)
            lse_ref[j] = _unstack_heads(lse2, lo)
            if per_head_lse:
                lseh_ref[j] = jnp.where(head_lane == 2 * p, lse2[:BLOCK],
                                        jnp.where(head_lane == 2 * p + 1, lse2[BLOCK:], lseh_ref[j]))

        lax.fori_loop(0, n // per_step, step, tuple(scores(t) for t in range(per_step)))

    cur = lambda i, r, p: (i, r, 0, p)
    kv_cur = lambda i, r, p: (i, r, 0, p // rep)
    kv_halo = lambda i, r, p: (jnp.maximum(i * n - 1, 0), r, 0, p // rep)
    big, small = (n, None, BLOCK, LANES), (1, None, BLOCK, LANES)
    in_specs = [pl.BlockSpec(big, cur), pl.BlockSpec(big, kv_cur), pl.BlockSpec(small, kv_halo),
                pl.BlockSpec(big, kv_cur), pl.BlockSpec(small, kv_halo)]
    q4, k4, v4 = _stream_view(q, dil), _stream_view(k, dil), _stream_view(v, dil)
    args = [q4, k4, k4, v4, v4]
    if has_sink:
        in_specs = [pl.BlockSpec(memory_space=pltpu.SMEM)] + in_specs
        args = [sinks] + args
    out_shape = [jax.ShapeDtypeStruct(q4.shape, out_dtype), jax.ShapeDtypeStruct(q4.shape, F32)]
    out_specs = [pl.BlockSpec(big, cur), pl.BlockSpec(big, cur)]
    if per_head_lse:
        out_shape.append(jax.ShapeDtypeStruct(q4.shape[:3] + (n_heads,), F32))
        out_specs.append(pl.BlockSpec((n, None, BLOCK, n_heads), lambda i, r, p: (i, r, 0, 0)))
    outs = pl.pallas_call(
        body, name=name,
        out_shape=tuple(out_shape),
        grid=(l // tq, dil, n_pairs),
        in_specs=in_specs,
        out_specs=tuple(out_specs),
        scratch_shapes=[pltpu.VMEM((tq + BLOCK, LANES), BF16), pltpu.VMEM((tq + BLOCK, LANES), BF16),
                        pltpu.VMEM((2, 2 * BLOCK, 2 * BLOCK), F32)],
        compiler_params=_params("arbitrary", "arbitrary", "arbitrary"),
    )(*args)
    res = [outs[0].reshape(s, w), outs[1].reshape(s, w)]
    if per_head_lse:
        res.append(outs[2].reshape(s, n_heads))
    return res


def _attn_bwd(q, k, v, do, y, lse, sinks, max_dist, rep, dil, name, tq=2048):
    s, w = q.shape
    l = s // dil
    n_pairs = w // LANES
    tq = _pick(l, tq)
    n = tq // BLOCK
    n_blk = l // BLOCK
    n_sb = l // tq
    has_sink = sinks is not None
    scale = HEAD_DIM ** -0.5
    kv_dtype = BF16
    ext = tq + BLOCK

    def body(*refs):
        if has_sink:
            sink_ref, refs = refs[0], refs[1:]
        (q_ref, qn_ref, kc_ref, kh_ref, vc_ref, vh_ref, do_ref, don_ref, y_ref, yn_ref,
         lse_ref, lsen_ref) = refs[:12]
        refs = refs[12:]
        dq_ref, dk_ref, dv_ref = refs[:3]
        refs = refs[3:]
        if has_sink:
            dsink_ref, refs = refs[0], refs[1:]
        kx, vx, bias_ref = refs[:3]
        if rep > 1:
            dk_acc, dv_acc = refs[3:]
        i, p = pl.program_id(0), pl.program_id(2)
        _fill_bias(bias_ref, max_dist)
        own_rows = (q_ref, do_ref, y_ref, lse_ref)
        next_rows = (qn_ref, don_ref, yn_ref, lsen_ref)
        _fill_window(kx, kh_ref, kc_ref, n)
        _fill_window(vx, vh_ref, vc_ref, n)
        if rep > 1:
            @pl.when(p % rep == 0)
            def _():
                dk_acc[...] = jnp.zeros_like(dk_acc)
                dv_acc[...] = jnp.zeros_like(dv_acc)
        lo = lax.broadcasted_iota(jnp.int32, (BLOCK, LANES), 1) < HEAD_DIM
        hi = jnp.logical_not(lo)
        top = lax.broadcasted_iota(jnp.int32, (2 * BLOCK, 1), 0) < BLOCK

        def rows_of(j):
            if isinstance(j, int) and j == n:
                return next_rows, 0
            return own_rows, j

        def front(j, width):
            (qr, dor, _, _), jb = rows_of(j)
            r0 = _aligned(j * BLOCK)
            first = jnp.logical_and(i == 0, j == 0).astype(jnp.int32)
            q2 = _stack_heads(qr[jb].astype(F32) * scale, lo)
            do2 = _stack_heads(dor[jb].astype(F32), lo)
            s2 = _dot_nt(q2, kx[pl.ds(r0, width), :]) + bias_ref[first, :, pl.ds(0, width)]
            return s2, _dot_nt(do2, vx[pl.ds(r0, width), :])

        row_lo = lax.broadcasted_iota(jnp.int32, (LANES, BLOCK), 0) < HEAD_DIM

        def stack_t(x):
            xt = x.T
            return jnp.concatenate([jnp.where(row_lo, xt, 0.0), jnp.where(row_lo, 0.0, xt)], axis=1).astype(BF16)

        def emit(jk, dk_t, dv_t):
            dk_blk, dv_blk = dk_t.T, dv_t.T
            if rep == 1:
                dk_ref[jk] = dk_blk.astype(dk_ref.dtype)
                dv_ref[jk] = dv_blk.astype(dv_ref.dtype)
            else:
                rows = pl.ds(_aligned(jk * BLOCK), BLOCK)
                dk_acc[rows, :] += dk_blk
                dv_acc[rows, :] += dv_blk

        def back(j, width, q_valid, s2, dp2, state):
            sink_acc, carry_k, carry_v = state
            (qr, dor, yr, lser), jb = rows_of(j)
            r0 = _aligned(j * BLOCK)
            qf, dof = qr[jb].astype(F32) * scale, dor[jb].astype(F32)
            yb, lseb = yr[jb].astype(F32), lser[jb]
            prod = dof * yb
            delta = jnp.concatenate([jnp.sum(jnp.where(lo, prod, 0.0), axis=1, keepdims=True),
                                     jnp.sum(jnp.where(lo, 0.0, prod), axis=1, keepdims=True)], axis=0)
            lse2 = jnp.concatenate([_head_col(lseb, lo), _head_col(lseb, hi)], axis=0)
            pr = jnp.exp(s2 - lse2)
            if q_valid is not True:
                pr = jnp.where(q_valid, pr, 0.0)
            ds = pr * (dp2 - delta)
            dk_t = jnp.dot(stack_t(qf), ds.astype(BF16), preferred_element_type=F32)
            dv_t = jnp.dot(stack_t(dof), pr.astype(BF16), preferred_element_type=F32)
            done_k, done_v = carry_k + dk_t[:, :BLOCK], carry_v + dv_t[:, :BLOCK]
            if isinstance(j, int):
                emit(j - 1, done_k, done_v)
            elif rep == 1:
                emit(jnp.maximum(j - 1, 0), done_k, done_v)
            else:
                keep = j > 0
                emit(jnp.maximum(j - 1, 0), jnp.where(keep, done_k, 0.0), jnp.where(keep, done_v, 0.0))
            if width == 2 * BLOCK:
                dq2 = jnp.dot(ds.astype(BF16), kx[pl.ds(r0, width), :], preferred_element_type=F32) * scale
                dq_ref[jb] = _unstack_heads(dq2, lo).astype(dq_ref.dtype)
                carry_k, carry_v = dk_t[:, BLOCK:], dv_t[:, BLOCK:]
            if has_sink:
                sk = jnp.where(top, sink_ref[2 * p], sink_ref[2 * p + 1])
                sink_acc = sink_acc - jnp.exp(sk - lse2) * delta
            return sink_acc, carry_k, carry_v

        per_step = 2 if n % 2 == 0 else 1

        def step(jj, state):
            fronts = [front(jj * per_step + t, 2 * BLOCK) for t in range(per_step)]
            for t in range(per_step):
                state = back(jj * per_step + t, 2 * BLOCK, True, *fronts[t], state)
            return state

        zero_blk = jnp.zeros((LANES, BLOCK), F32)
        state = lax.fori_loop(0, n // per_step, step, (jnp.zeros((2 * BLOCK, 1), F32), zero_blk, zero_blk))
        sink_acc = state[0]
        if n_sb > 1:
            back(n, BLOCK, i < n_sb - 1, *front(n, BLOCK), state)
        else:
            emit(n - 1, state[1], state[2])

        if rep > 1:
            @pl.when(p % rep == rep - 1)
            def _():
                for jc in range(n):
                    rows = slice(jc * BLOCK, (jc + 1) * BLOCK)
                    dk_ref[jc] = dk_acc[rows, :].astype(dk_ref.dtype)
                    dv_ref[jc] = dv_acc[rows, :].astype(dv_ref.dtype)
        if has_sink:
            rowi = lax.broadcasted_iota(jnp.int32, (8, LANES), 0)
            s0 = jnp.sum(sink_acc[:BLOCK], axis=0, keepdims=True)
            s1 = jnp.sum(sink_acc[BLOCK:], axis=0, keepdims=True)
            dsink_ref[...] = jnp.where(rowi == 0, s0, jnp.where(rowi == 1, s1, 0.0))

    cur = lambda i, r, p: (i, r, 0, p)
    nxt = lambda i, r, p: (jnp.minimum((i + 1) * n, n_blk - 1), r, 0, p)
    kv_cur = lambda i, r, p: (i, r, 0, p // rep)
    kv_halo = lambda i, r, p: (jnp.maximum(i * n - 1, 0), r, 0, p // rep)
    big, small = (n, None, BLOCK, LANES), (1, None, BLOCK, LANES)
    in_specs = [pl.BlockSpec(big, cur), pl.BlockSpec(small, nxt),
                pl.BlockSpec(big, kv_cur), pl.BlockSpec(small, kv_halo),
                pl.BlockSpec(big, kv_cur), pl.BlockSpec(small, kv_halo),
                pl.BlockSpec(big, cur), pl.BlockSpec(small, nxt),
                pl.BlockSpec(big, cur), pl.BlockSpec(small, nxt),
                pl.BlockSpec(big, cur), pl.BlockSpec(small, nxt)]
    q4, k4, v4, do4, y4, lse4 = [_stream_view(a, dil) for a in (q, k, v, do, y, lse)]
    args = [q4, q4, k4, k4, v4, v4, do4, do4, y4, y4, lse4, lse4]
    out_shape = [jax.ShapeDtypeStruct(q4.shape, BF16),
                 jax.ShapeDtypeStruct(k4.shape, kv_dtype), jax.ShapeDtypeStruct(v4.shape, kv_dtype)]
    out_specs = [pl.BlockSpec(big, cur), pl.BlockSpec(big, kv_cur), pl.BlockSpec(big, kv_cur)]
    if has_sink:
        in_specs = [pl.BlockSpec(memory_space=pltpu.SMEM)] + in_specs
        args = [sinks] + args
        out_shape.append(jax.ShapeDtypeStruct((n_sb, dil, n_pairs, 8, LANES), F32))
        out_specs.append(pl.BlockSpec((None, None, None, 8, LANES), lambda i, r, p: (i, r, p, 0, 0)))
    outs = pl.pallas_call(
        body, name=name,
        out_shape=tuple(out_shape),
        grid=(n_sb, dil, n_pairs),
        in_specs=in_specs,
        out_specs=tuple(out_specs),
        scratch_shapes=[pltpu.VMEM((ext, LANES), BF16), pltpu.VMEM((ext, LANES), BF16),
                        pltpu.VMEM((2, 2 * BLOCK, 2 * BLOCK), F32)]
        + ([pltpu.VMEM((tq, LANES), F32), pltpu.VMEM((tq, LANES), F32)] if rep > 1 else []),
        compiler_params=_params("arbitrary", "arbitrary", "arbitrary"),
    )(*args)
    grads =[outs[0].reshape(s, w), outs[1].reshape(k.shape), outs[2].reshape(v.shape)]
    if has_sink:
        grads.append(outs[3].sum(axis=(0, 1))[:, 0:2, 0].reshape(1, 2 * n_pairs))
    return grads


def _sum_slots(recv, name, ts=256):
    nd, r, c = recv.shape
    ts = _pick(r, ts, 8)

    def body(r_ref, o_ref):
        acc = r_ref[0].astype(F32)
        for dev in range(1, nd):
            acc = acc + r_ref[dev].astype(F32)
        o_ref[...] = acc

    return pl.pallas_call(
        body, name=name,
        out_shape=jax.ShapeDtypeStruct((r, c), F32),
        grid=(r // ts,),
        in_specs=[pl.BlockSpec((nd, ts, c), lambda i: (0, i, 0))],
        out_specs=pl.BlockSpec((ts, c), lambda i: (i, 0)),
        compiler_params=_params("parallel"),
    )(recv)


def _adamw_math(w, g, m, v):
    c1 = 1.0 - ADAM_B1 ** ADAM_STEP
    c2 = 1.0 - ADAM_B2 ** ADAM_STEP
    m_ = ADAM_B1 * m + (1.0 - ADAM_B1) * g
    v_ = ADAM_B2 * v + (1.0 - ADAM_B2) * (g * g)
    return -ADAM_LR * ((m_ / c1) / (jnp.sqrt(v_ / c2) + ADAM_EPS) + ADAM_WD * w), m_, v_


def _row_tile(r, c, budget=1 << 18):
    return _pick(r, max(8, min(256, budget // c // 8 * 8)), 8)


def _adamw(w, g, m, v, name):
    r, c = w.shape
    ts = _row_tile(r, c)

    def body(w_ref, g_ref, m_ref, v_ref, d_ref, mo_ref, vo_ref):
        d_ref[...], mo_ref[...], vo_ref[...] = _adamw_math(w_ref[...], g_ref[...], m_ref[...], v_ref[...])

    blk = pl.BlockSpec((ts, c), lambda i: (i, 0))
    return pl.pallas_call(
        body, name=name,
        out_shape=tuple([jax.ShapeDtypeStruct((r, c), F32)] * 3),
        grid=(r // ts,),
        in_specs=[blk] * 4,
        out_specs=(blk, blk, blk),
        compiler_params=_params("parallel"),
    )(w, g, m, v)


def _adamw_slots(w, slots, m, v, name):
    r, c = w.shape
    nd = slots.shape[0]
    ts = _row_tile(r, c)

    def body(w_ref, s_ref, m_ref, v_ref, g_ref, d_ref, mo_ref, vo_ref):
        g = s_ref[0].astype(F32)
        for slot in range(1, nd):
            g = g + s_ref[slot].astype(F32)
        g_ref[...] = g
        d_ref[...], mo_ref[...], vo_ref[...] = _adamw_math(w_ref[...], g, m_ref[...], v_ref[...])

    blk = pl.BlockSpec((ts, c), lambda i: (i, 0))
    return pl.pallas_call(
        body, name=name,
        out_shape=tuple([jax.ShapeDtypeStruct((r, c), F32)] * 4),
        grid=(r // ts,),
        in_specs=[blk, pl.BlockSpec((nd, ts, c), lambda i: (0, i, 0)), blk, blk],
        out_specs=(blk, blk, blk, blk),
        compiler_params=_params("parallel"),
    )(w, slots, m, v)


def _rows(a):
    flat = a.reshape(-1)
    pad = (-flat.shape[0]) % PACK_W
    if pad:
        flat = jnp.concatenate([flat, jnp.zeros((pad,), flat.dtype)])
    return flat.reshape(-1, PACK_W)


def _pad_rows(a, mult):
    pad = (-a.shape[-2]) % mult
    if pad:
        widths = [(0, 0)] * (a.ndim - 2) + [(0, pad), (0, 0)]
        a = jnp.pad(a, widths)
    return a


def _to_global(stack, axis):
    moved = jnp.moveaxis(stack, 0, axis)
    shp = list(moved.shape)
    shp[axis:axis + 2] = [shp[axis] * shp[axis + 1]]
    return moved.reshape(shp)


def _to_stack(full, axis):
    shp = list(full.shape)
    shp[axis:axis + 1] = [N_DEV, shp[axis] // N_DEV]
    return jnp.moveaxis(full.reshape(shp), axis, 0)


_BIG = (("w_out", 1), ("a_w_in", 2), ("a_w_group", 2), ("b_w_in", 2), ("c_w_in", 2))


def _dup_heads(wk, n_kv):
    d = wk.shape[0]
    return jnp.tile(wk.reshape(d, n_kv, 1, HEAD_DIM), (1, 1, 2, 1)).reshape(d, n_kv * LANES)


def _fold_heads(dwk, n_kv):
    d = dwk.shape[0]
    folded = dwk.astype(F32).reshape(d, n_kv, 2, HEAD_DIM).sum(axis=2)
    return folded.reshape(d, n_kv * HEAD_DIM).astype(dwk.dtype)


def _perm(a, dil):
    if dil == 1:
        return a
    s, w = a.shape
    return a.reshape(s // (BLOCK * dil), BLOCK, dil, w).transpose(0, 2, 1, 3).reshape(s, w)


def _unperm(a, dil):
    if dil == 1:
        return a
    s, w = a.shape
    return a.reshape(s // (BLOCK * dil), dil, BLOCK, w).transpose(0, 2, 1, 3).reshape(s, w)


def kernel(x, norm_g, final_g, w_out, a_w_in, a_w_group, a_scale, b_w_in, b_sinks, c_w_in, loss_target, m_norm_g, m_final_g, m_w_out, m_a_w_in, m_a_w_group, m_a_scale, m_b_w_in, m_b_sinks, m_c_w_in, v_norm_g, v_final_g, v_w_out, v_a_w_in, v_a_w_group, v_a_scale, v_b_w_in, v_b_sinks, v_c_w_in):
    local = dict(w_out=w_out, a_w_in=a_w_in, a_w_group=a_w_group, b_w_in=b_w_in, c_w_in=c_w_in)
    mom_m = dict(w_out=m_w_out, a_w_in=m_a_w_in, a_w_group=m_a_w_group, b_w_in=m_b_w_in, c_w_in=m_c_w_in)
    mom_v = dict(w_out=v_w_out, a_w_in=v_a_w_in, a_w_group=v_a_w_group, b_w_in=v_b_w_in, c_w_in=v_c_w_in)
    s, d = x.shape[1], x.shape[2]
    depth = norm_g.shape[0]
    e = w_out.shape[1] * N_DEV
    n_heads = e // HEAD_DIM
    n_kv = n_heads // Q_PER_KV
    kv_w = n_kv * HEAD_DIM
    rep = Q_PER_KV // 2
    n_groups = len(POOL_WINDOWS)
    me = 4 * lax.axis_index("x") + 2 * lax.axis_index("y") + lax.axis_index("c")

    flat = {n: local[n].reshape(-1, local[n].shape[-1]) for n, _ in _BIG}
    spack = _pad_rows(_rows(a_scale), 8)
    *walls, sall = _gather([flat[n].astype(BF16) for n, _ in _BIG] + [spack], [True] * len(_BIG) + [False],
                           "gather_weights")
    full = {}
    for k, (name, axis) in enumerate(_BIG):
        full[name] = _to_global(walls[k].reshape((N_DEV,) + local[name].shape), axis)
    scale_full = _to_global(sall.reshape(N_DEV, -1)[:, :a_scale.size].reshape((N_DEV,) + a_scale.shape), 1)

    wout_t = jnp.swapaxes(full["w_out"], 1, 2)
    wa = full["a_w_in"]
    wa_t = jnp.swapaxes(wa, 1, 2)
    wg = full["a_w_group"]
    wg_t = jnp.swapaxes(wg, 2, 3)
    wb = full["b_w_in"][0]
    wb_ext = jnp.concatenate([wb[:, :e], _dup_heads(wb[:, e:e + kv_w], n_kv),
                              _dup_heads(wb[:, e + kv_w:e + 2 * kv_w], n_kv), wb[:, e + 2 * kv_w:]], axis=1)
    wb_ext_t = wb_ext.T
    kd_w = n_kv * LANES
    wc = full["c_w_in"][0]
    wc_t = wc.T

    xs, hs, zs, saved = [x.reshape(s, d)], [], [], []
    hs.append(_rmsnorm_fwd(xs[0], norm_g[0:1], "norm0"))
    loss_vec = dfinal = dx = dxb = None
    for i in range(depth):
        kind, j = i % 3, i // 3
        h = hs[i]
        tag = f"l{i}"
        if kind == 0:
            dpool = _proj_pool_fwd(h, wa[j][:, :e], tag + "_in_pool")
            gate = _matmul(h, wa[j][:, e:], BF16, tag + "_in_gate")
            yr, z = _a_group_fwd(dpool, wg[j], scale_full[j:j + 1], gate, tag + "_group")
            saved.append(dict(dpool=dpool, yr=yr, gate=gate))
        elif kind == 1:
            q = _matmul(h, wb_ext[:, :e], BF16, tag + "_in_q")
            kd = _matmul(h, wb_ext[:, e:e + kd_w], BF16, tag + "_in_k")
            vd = _matmul(h, wb_ext[:, e + kd_w:e + 2 * kd_w], BF16, tag + "_in_v")
            gate = _matmul(h, wb_ext[:, e + 2 * kd_w:], BF16, tag + "_in_gate")
            sinks = b_sinks[j]
            y, lse = _attn_fwd(q, kd, vd, sinks, SWA_MAX_DIST, rep, 1, BF16, tag + "_attn")
            z = _gate_fwd(y, gate, tag + "_gate")
            saved.append(dict(q=q, kd=kd, vd=vd, gate=gate, y=y, lse=lse, sinks=sinks))
        else:
            qkv, outs, lses, lses_tok, h_perm = [], [], [], [], []
            for gi, (window, dil) in enumerate(DILATED_PAIRS):
                hp = _perm(h, dil)
                trio = [_matmul(hp, wc[:, (3 * gi + t) * e:(3 * gi + t + 1) * e], BF16,
                                f"{tag}_in_{'qkv'[t]}{gi}") for t in range(3)]
                o, lse, lse_heads = _attn_fwd(trio[0], trio[1], trio[2], None, window // dil, 1, dil, BF16,
                                              f"{tag}_attn{gi}", per_head_lse=True)
                qkv.append(trio)
                h_perm.append(hp)
                outs.append(_unperm(o, dil))
                lses.append(lse)
                lses_tok.append(_unperm(lse_heads, dil))
            gate = _matmul(h, wc[:, 9 * e:], BF16, tag + "_in_gate")
            y, z = _merge_gate_fwd(outs, lses_tok, gate, tag + "_merge")
            saved.append(dict(qkv=qkv, lses=lses, lses_tok=lses_tok, gate=gate, y=y, h_perm=h_perm))
        zs.append(z)
        if i + 1 < depth:
            x_new, h_new = _outproj_norm(z, full["w_out"][i], xs[i], norm_g[i + 1:i + 2], tag + "_out")
            xs.append(x_new)
            hs.append(h_new)
        else:
            dx, dxb, dfinal, loss_vec = _outproj_loss(z, full["w_out"][i], xs[i], final_g.reshape(1, d),
                                                      loss_target.reshape(s, d), tag + "_out_loss")

    g_full = {"w_out": [None] * depth, "a_w_in": [None] * wa.shape[0], "a_w_group": [None] * wa.shape[0]}
    d_norm = [None] * depth
    d_scale = [None] * wa.shape[0]
    d_sinks = None
    for i in reversed(range(depth)):
        kind, j = i % 3, i // 3
        tag = f"b{i}"
        sv = saved[i]
        g_full["w_out"][i] = _matmul_tn(zs[i], dxb, tag + "_dwout", out_dtype=BF16)
        if kind == 0:
            dgate, dyr, dsc = _dz_fused(dxb, wout_t[i], [sv["yr"], sv["gate"]], [scale_full[j:j + 1]], 2,
                                        _a_gate_epilogue, tag + "_dz_gate", 1024, 1024, with_col_sum=True)
            d_scale[j] = dsc
            du = _group_pool_bwd(dyr, wg_t[j], tag + "_dd_pool")
            g_full["a_w_group"][j] = _grouped_weight_grad(sv["dpool"], dyr, n_groups, tag + "_dwg")
            parts = [du, dgate]
            g_full["a_w_in"][j] = jnp.concatenate(
                [_matmul_tn(hs[i], part, f"{tag}_dwin{t}", out_dtype=BF16) for t, part in enumerate(parts)], axis=1)
            dhs = [_matmul_cat(parts, wa_t[j], F32, tag + "_dh")]
        elif kind == 1:
            dgate, do = _dz_fused(dxb, wout_t[i], [sv["y"], sv["gate"]], [], 2, _gate_epilogue,
                                  tag + "_dz_gate", 1024, 1024)
            dq, dkd, dvd, d_sinks = _attn_bwd(sv["q"], sv["kd"], sv["vd"], do, sv["y"], sv["lse"], sv["sinks"],
                                              SWA_MAX_DIST, rep, 1, tag + "_attn")
            parts = [dq, dkd, dvd, dgate]
            dws = [_matmul_tn(hs[i], part, f"{tag}_dwin{t}", out_dtype=BF16) for t, part in enumerate(parts)]
            g_full["b_w_in"] = jnp.concatenate(
                [dws[0], _fold_heads(dws[1], n_kv), _fold_heads(dws[2], n_kv), dws[3]], axis=1)[None]
            dhs = [_matmul_cat(parts, wb_ext_t, F32, tag + "_dh")]
        else:
            dgate, *dos = _dz_fused(dxb, wout_t[i], [sv["y"], sv["gate"]], [], 4, _merge_gate_epilogue,
                                    tag + "_dz_merge", 1024, 512, rows=sv["lses_tok"])
            y_bf = sv["y"]
            dws, dhs = [], []
            for gi, (window, dil) in enumerate(DILATED_PAIRS):
                qv, kv, vv = sv["qkv"][gi]
                grads = _attn_bwd(qv, kv, vv, _perm(dos[gi], dil), _perm(y_bf, dil), sv["lses"][gi], None,
                                  window // dil, 1, dil, f"{tag}_attn{gi}")
                dws += [_matmul_tn(sv["h_perm"][gi], part, f"{tag}_dwin{gi}{'qkv'[t]}", out_dtype=BF16)
                        for t, part in enumerate(grads)]
                dhs.append(_unperm(_matmul_cat(grads, wc_t[3 * gi * e:3 * (gi + 1) * e], F32, f"{tag}_dh{gi}"),
                                   dil))
            dws.append(_matmul_tn(hs[i], dgate, tag + "_dwin_gate", out_dtype=BF16))
            dhs.append(_matmul(dgate, wc_t[9 * e:], F32, tag + "_dh_gate"))
            g_full["c_w_in"] = jnp.concatenate(dws, axis=1)[None]
        dx, dxb, d_norm[i] = _rmsnorm_bwd(dhs, xs[i], norm_g[i:i + 1], dx, tag + "_norm")
    grad_x = dx.reshape(x.shape)
    for name in ("w_out", "a_w_in", "a_w_group"):
        g_full[name] = jnp.stack(g_full[name], axis=0)

    stacks = [_to_stack(g_full[n], axis).astype(BF16).reshape((N_DEV,) + flat[n].shape) for n, axis in _BIG]
    loss_local = (0.5 / d) * jnp.sum(loss_vec)
    small = [jnp.concatenate(d_norm, axis=0), dfinal, d_sinks, jnp.concatenate(d_scale, axis=0),
             loss_local.reshape(1, 1)]
    small_rows = [_rows(a) for a in small]
    small_offs = [sum(r.shape[0] for r in small_rows[:k]) for k in range(len(small_rows) + 1)]
    small_pack = _pad_rows(jnp.concatenate(small_rows, axis=0), 8)
    core = lax.axis_index("c").astype(jnp.int32).reshape(1)
    from_sibling = _sibling_exchange(stacks, "exchange_sibling")
    chip_sums = [_pair_sum(stacks[k], from_sibling[k], core, "sum_pair_" + n) for k, (n, _) in enumerate(_BIG)]
    grecv, srecv = _chip_exchange(chip_sums, small_pack, "exchange_chips")
    ssum = _sum_slots(srecv, "sum_small")

    def small_part(k, like):
        return ssum[small_offs[k]:small_offs[k + 1]].reshape(-1)[:like.size].reshape(like.shape)

    g_norm = small_part(0, norm_g)
    g_final = small_part(1, final_g)
    g_sinks = small_part(2, b_sinks)
    g_scale_full = small_part(3, scale_full)
    loss = ssum[small_offs[4], 0]
    g_scale = lax.dynamic_slice_in_dim(g_scale_full, me * a_scale.shape[1], a_scale.shape[1], axis=1)

    small_w = [("norm_g", norm_g, m_norm_g, v_norm_g, g_norm), ("final_g", final_g, m_final_g, v_final_g, g_final),
               ("a_scale", a_scale, m_a_scale, v_a_scale, g_scale), ("b_sinks", b_sinks, m_b_sinks, v_b_sinks, g_sinks)]
    tail = lambda idx: _pad_rows(jnp.concatenate([_rows(t[idx]) for t in small_w], axis=0), 8)
    tail_sizes = [_rows(t[1]).shape[0] for t in small_w]
    tail_offs = [sum(tail_sizes[:k]) for k in range(len(tail_sizes) + 1)]
    g_tail = tail(4)
    tails = (g_tail,) + _adamw(tail(1), g_tail, tail(2), tail(3), "adamw_small")
    grads, deltas, new_m, new_v = {}, {}, {}, {}
    for k, (name, w_, _, _, _) in enumerate(small_w):
        for out, packed in zip((grads, deltas, new_m, new_v), tails):
            out[name] = packed[tail_offs[k]:tail_offs[k + 1]].reshape(-1)[:w_.size].reshape(w_.shape)
    for k, (name, _) in enumerate(_BIG):
        shape2d = flat[name].shape
        res = _adamw_slots(flat[name], grecv[k], mom_m[name].reshape(shape2d), mom_v[name].reshape(shape2d),
                           "adamw_" + name)
        for out, val in zip((grads, deltas, new_m, new_v), res):
            out[name] = val.reshape(local[name].shape)

    order = ("norm_g", "final_g", "w_out", "a_w_in", "a_w_group", "a_scale", "b_w_in", "b_sinks", "c_w_in")
    return (loss, grad_x, *[grads[n] for n in order], *[deltas[n] for n in order],
            *[new_m[n] for n in order], *[new_v[n] for n in order])
```

```python
import functools

import jax
import jax.numpy as jnp
from jax import lax
from jax.experimental import pallas as pl
from jax.experimental.pallas import tpu as pltpu

F32 = jnp.float32
BF16 = jnp.bfloat16

N_DEV = 8
HEAD_DIM = 64
LANES = 128
BLOCK = 128
Q_PER_KV = 8
POOL_WINDOWS = (2, 4, 8, 16)
POOL_HALO = 16
DILATED_PAIRS = ((128, 1), (512, 4), (2048, 16))
SWA_MAX_DIST = 127
RMS_EPS = 1e-5
PACK_W = 1024
NEG = -1e30

ADAM_LR = 0.001
ADAM_B1 = 0.9
ADAM_B2 = 0.999
ADAM_EPS = 1e-08
ADAM_WD = 0.01
ADAM_STEP = 10

VMEM_LIMIT = 48 * 1024 * 1024


def _params(*sem):
    return pltpu.CompilerParams(dimension_semantics=sem if sem else None, vmem_limit_bytes=VMEM_LIMIT)


def _pick(dim, target, mult=LANES):
    if dim <= target:
        return dim
    t = target - target % mult
    while dim % t:
        t -= mult
    return t


def _sigmoid(x):
    return 1.0 / (1.0 + jnp.exp(-x))


ANY_SPEC = pl.BlockSpec(memory_space=pl.ANY)


def _where_am_i():
    x, y, c = lax.axis_index("x"), lax.axis_index("y"), lax.axis_index("c")
    return x, y, c, 4 * x + 2 * y + c


def _peer(x, y, c, r):
    return x ^ ((r >> 2) & 1), y ^ ((r >> 1) & 1), c ^ (r & 1)


GATHER_SEMS = 8


def _gather(blocks, split, name):
    n = len(blocks)
    halves = [b.shape[0] // 2 for b in blocks]

    def body(*refs):
        send, recv = refs[:n], refs[n:2 * n]
        send_sems, recv_sems, local_sems = refs[2 * n:]
        x, y, c, me = _where_am_i()
        sib, xn, yn, dg = (_peer(x, y, c, r) for r in (1, 4, 2, 6))
        sib_id, xn_id, yn_id, dg_id = me ^ 1, me ^ 4, me ^ 2, me ^ 6

        def copy(k, sem, src, dst, to):
            return pltpu.make_async_remote_copy(
                src_ref=src, dst_ref=dst, send_sem=send_sems.at[k, sem], recv_sem=recv_sems.at[k, sem],
                device_id=to, device_id_type=pl.DeviceIdType.MESH)

        def part(k, slot, half):
            return recv[k].at[slot, pl.ds(half * halves[k], halves[k])]

        started = []
        for k in range(n):
            own = pltpu.make_async_copy(send[k], recv[k].at[me], local_sems.at[k])
            own.start()
            started.append(own)
        sends = []
        for k in range(n):
            sends += [copy(k, 0, send[k], recv[k].at[me], sib), copy(k, 1, send[k], recv[k].at[me], xn),
                      copy(k, 2, send[k], recv[k].at[me], yn)]
            if not split[k]:
                sends.append(copy(k, 3, send[k], recv[k].at[me], dg))
        for cp in sends:
            cp.start()

        def after(k, sem, slot, hand_on_sem, half, half_sem, half_to):
            copy(k, sem, send[k], recv[k].at[slot], sib).wait_recv()
            new = [copy(k, hand_on_sem, recv[k].at[slot], recv[k].at[slot], sib)]
            if split[k]:
                new.append(copy(k, half_sem, part(k, slot, half), part(k, slot, half), half_to))
            for cp in new:
                cp.start()
            sends.extend(new)

        for k in range(n):
            after(k, 2, yn_id, 6, 0, 3, xn)
        for k in range(n):
            after(k, 1, xn_id, 5, 1, 4, yn)
        for k in range(n):
            if split[k]:
                copy(k, 3, part(k, dg_id, 0), part(k, dg_id, 0), sib).wait_recv()
                copy(k, 4, part(k, dg_id, 1), part(k, dg_id, 1), sib).wait_recv()
            else:
                copy(k, 3, send[k], recv[k].at[dg_id], sib).wait_recv()
            fwd = copy(k, 7, recv[k].at[dg_id], recv[k].at[dg_id], sib)
            fwd.start()
            sends.append(fwd)
        for k in range(n):
            copy(k, 0, send[k], recv[k].at[sib_id], sib).wait_recv()
            for sem, r in ((5, 4), (6, 2), (7, 6)):
                copy(k, sem, send[k], recv[k].at[sib_id ^ r], sib).wait_recv()
        for cp in sends:
            cp.wait_send()
        for own in started:
            own.wait()

    return pl.pallas_call(
        body, name=name,
        out_shape=tuple(jax.ShapeDtypeStruct((N_DEV,) + b.shape, b.dtype) for b in blocks),
        in_specs=[ANY_SPEC] * n,
        out_specs=tuple([ANY_SPEC] * n),
        scratch_shapes=[pltpu.SemaphoreType.DMA((n, GATHER_SEMS)), pltpu.SemaphoreType.DMA((n, GATHER_SEMS)),
                        pltpu.SemaphoreType.DMA((n,))],
    )(*blocks)


def _sibling_exchange(stacks, name):
    n_chips = N_DEV // 2
    n = len(stacks)

    def body(*refs):
        g_refs, t_refs = refs[:n], refs[n:2 * n]
        send_sems, recv_sems = refs[2 * n:]
        x, y, c, _ = _where_am_i()
        sib = _peer(x, y, c, 1)
        copies = [pltpu.make_async_remote_copy(
            src_ref=g_refs[k].at[2 * chip + (1 - c)], dst_ref=t_refs[k].at[chip], send_sem=send_sems.at[k, chip],
            recv_sem=recv_sems.at[k, chip], device_id=sib, device_id_type=pl.DeviceIdType.MESH)
            for k in range(n) for chip in range(n_chips)]
        for cp in copies:
            cp.start()
        for cp in copies:
            cp.wait_recv()
        for cp in copies:
            cp.wait_send()

    return pl.pallas_call(
        body, name=name,
        out_shape=tuple(jax.ShapeDtypeStruct((n_chips,) + g.shape[1:], g.dtype) for g in stacks),
        in_specs=[ANY_SPEC] * n, out_specs=tuple([ANY_SPEC] * n),
        scratch_shapes=[pltpu.SemaphoreType.DMA((n, n_chips)), pltpu.SemaphoreType.DMA((n, n_chips))],
    )(*stacks)


CHIP_SEMS = 6


def _chip_exchange(csums, small, name):
    n = len(csums)
    halves = [cs.shape[1] // 2 for cs in csums]

    def body(*refs):
        c_refs, s_ref = refs[:n], refs[n]
        r_refs, sr_ref = refs[n + 1:2 * n + 1], refs[2 * n + 1]
        stage_x, stage_y = refs[2 * n + 2:3 * n + 2], refs[3 * n + 2:4 * n + 2]
        send_sems, recv_sems, small_send, small_recv, local_sems = refs[4 * n + 2:]
        x, y, c, me = _where_am_i()
        my_chip = 2 * x + y
        xn, yn = _peer(x, y, c, 4), _peer(x, y, c, 2)
        xn_chip, yn_chip, dg_chip = my_chip ^ 2, my_chip ^ 1, my_chip ^ 3
        own = [pltpu.make_async_copy(c_refs[k].at[my_chip], r_refs[k].at[my_chip], local_sems.at[k])
               for k in range(n)]
        own.append(pltpu.make_async_copy(s_ref, sr_ref.at[me], local_sems.at[n]))
        for cp in own:
            cp.start()

        def copy(k, sem, src, dst, to):
            return pltpu.make_async_remote_copy(
                src_ref=src, dst_ref=dst, send_sem=send_sems.at[k, sem], recv_sem=recv_sems.at[k, sem],
                device_id=to, device_id_type=pl.DeviceIdType.MESH)

        def half(ref, k, slot, which):
            return ref.at[slot, pl.ds(which * halves[k], halves[k])]

        sends, recvs = [], []
        for k in range(n):
            sends += [copy(k, 0, c_refs[k].at[xn_chip], r_refs[k].at[my_chip], xn),
                      copy(k, 1, c_refs[k].at[yn_chip], r_refs[k].at[my_chip], yn),
                      copy(k, 2, half(c_refs[k], k, dg_chip, 0), stage_x[k], xn),
                      copy(k, 3, half(c_refs[k], k, dg_chip, 1), stage_y[k], yn)]
            recvs += [copy(k, 0, c_refs[k].at[xn_chip], r_refs[k].at[xn_chip], xn),
                      copy(k, 1, c_refs[k].at[yn_chip], r_refs[k].at[yn_chip], yn),
                      copy(k, 4, stage_x[k], half(r_refs[k], k, dg_chip, 0), yn),
                      copy(k, 5, stage_y[k], half(r_refs[k], k, dg_chip, 1), xn)]
        for r in range(1, N_DEV):
            to = _peer(x, y, c, r)
            sends.append(pltpu.make_async_remote_copy(
                src_ref=s_ref, dst_ref=sr_ref.at[me], send_sem=small_send.at[r - 1],
                recv_sem=small_recv.at[r - 1], device_id=to, device_id_type=pl.DeviceIdType.MESH))
            recvs.append(pltpu.make_async_remote_copy(
                src_ref=s_ref, dst_ref=sr_ref.at[me ^ r], send_sem=small_send.at[r - 1],
                recv_sem=small_recv.at[r - 1], device_id=to, device_id_type=pl.DeviceIdType.MESH))
        for cp in sends:
            cp.start()
        for k in range(n):
            copy(k, 2, stage_x[k], stage_x[k], xn).wait_recv()
            fwd = copy(k, 4, stage_x[k], half(r_refs[k], k, xn_chip, 0), yn)
            fwd.start()
            sends.append(fwd)
        for k in range(n):
            copy(k, 3, stage_y[k], stage_y[k], yn).wait_recv()
            fwd = copy(k, 5, stage_y[k], half(r_refs[k], k, yn_chip, 1), xn)
            fwd.start()
            sends.append(fwd)
        for cp in recvs:
            cp.wait_recv()
        for cp in sends:
            cp.wait_send()
        for cp in own:
            cp.wait()

    stages = tuple(jax.ShapeDtypeStruct((h, cs.shape[2]), cs.dtype) for h, cs in zip(halves, csums))
    outs = pl.pallas_call(
        body, name=name,
        out_shape=tuple(jax.ShapeDtypeStruct(cs.shape, cs.dtype) for cs in csums)
        + (jax.ShapeDtypeStruct((N_DEV,) + small.shape, small.dtype),) + stages + stages,
        in_specs=[ANY_SPEC] * (n + 1), out_specs=tuple([ANY_SPEC] * (3 * n + 1)),
        scratch_shapes=[pltpu.SemaphoreType.DMA((n, CHIP_SEMS)), pltpu.SemaphoreType.DMA((n, CHIP_SEMS)),
                        pltpu.SemaphoreType.DMA((N_DEV - 1,)), pltpu.SemaphoreType.DMA((N_DEV - 1,)),
                        pltpu.SemaphoreType.DMA((n + 1,))],
    )(*csums, small)
    return list(outs[:n]), outs[n]


def _pair_sum(gpack, other, core, name, ts=256):
    n_chips, r, c = other.shape
    ts = _pick(r, ts, 16)

    def body(core_ref, g_ref, o_ref, out_ref):
        del core_ref
        out_ref[...] = (g_ref[...].astype(F32) + o_ref[...].astype(F32)).astype(out_ref.dtype)

    return pl.pallas_call(
        body, name=name,
        out_shape=jax.ShapeDtypeStruct(other.shape, other.dtype),
        grid_spec=pltpu.PrefetchScalarGridSpec(
            num_scalar_prefetch=1, grid=(n_chips, r // ts),
            in_specs=[pl.BlockSpec((None, ts, c), lambda j, i, core_ref: (2 * j + core_ref[0], i, 0)),
                      pl.BlockSpec((None, ts, c), lambda j, i, core_ref: (j, i, 0))],
            out_specs=pl.BlockSpec((None, ts, c), lambda j, i, core_ref: (j, i, 0))),
        compiler_params=_params("parallel", "parallel"),
    )(core, gpack, other)


def _matmul(a, b, out_dtype, name, tm=1024, tn=1024, tk=1024):
    m, kdim = a.shape
    n = b.shape[1]
    tm, tn, tk = _pick(m, tm), _pick(n, tn), _pick(kdim, tk)
    nk = kdim // tk

    if nk == 1:
        def body(a_ref, b_ref, o_ref):
            o_ref[...] = jnp.dot(a_ref[...], b_ref[...], preferred_element_type=F32).astype(o_ref.dtype)
        scratch = []
    else:
        def body(a_ref, b_ref, o_ref, acc_ref):
            kk = pl.program_id(2)

            @pl.when(kk == 0)
            def _():
                acc_ref[...] = jnp.zeros_like(acc_ref)

            acc_ref[...] += jnp.dot(a_ref[...], b_ref[...], preferred_element_type=F32)

            @pl.when(kk == nk - 1)
            def _():
                o_ref[...] = acc_ref[...].astype(o_ref.dtype)
        scratch = [pltpu.VMEM((tm, tn), F32)]

    return pl.pallas_call(
        body, name=name,
        out_shape=jax.ShapeDtypeStruct((m, n), out_dtype),
        grid=(m // tm, n // tn, nk),
        in_specs=[pl.BlockSpec((tm, tk), lambda i, j, k: (i, k)),
                  pl.BlockSpec((tk, tn), lambda i, j, k: (k, j))],
        out_specs=pl.BlockSpec((tm, tn), lambda i, j, k: (i, j)),
        scratch_shapes=scratch,
        compiler_params=_params("parallel", "parallel", "arbitrary"),
    )(a, b)


def _matmul_cat(parts, b, out_dtype, name, tm=1024, tn=1024, tk=1024):
    m = parts[0].shape[0]
    n = b.shape[1]
    tm, tn = _pick(m, tm), _pick(n, tn)
    tk = min(_pick(p.shape[1], tk) for p in parts)
    steps = [p.shape[1] // tk for p in parts]
    assert all(p.shape[1] % tk == 0 for p in parts)
    starts = [sum(steps[:t]) for t in range(len(parts))]
    nk = sum(steps)
    n_parts = len(parts)

    def body(*refs):
        a_refs, b_ref, o_ref, acc_ref = refs[:n_parts], refs[n_parts], refs[n_parts + 1], refs[n_parts + 2]
        kk = pl.program_id(2)

        @pl.when(kk == 0)
        def _():
            acc_ref[...] = jnp.zeros_like(acc_ref)

        for t in range(n_parts):
            @pl.when(jnp.logical_and(kk >= starts[t], kk < starts[t] + steps[t]))
            def _(t=t):
                acc_ref[...] += jnp.dot(a_refs[t][...], b_ref[...], preferred_element_type=F32)

        @pl.when(kk == nk - 1)
        def _():
            o_ref[...] = acc_ref[...].astype(o_ref.dtype)

    def part_map(t):
        return lambda i, j, k: (i, jnp.clip(k - starts[t], 0, steps[t] - 1))

    return pl.pallas_call(
        body, name=name,
        out_shape=jax.ShapeDtypeStruct((m, n), out_dtype),
        grid=(m // tm, n // tn, nk),
        in_specs=[pl.BlockSpec((tm, tk), part_map(t)) for t in range(n_parts)]
        + [pl.BlockSpec((tk, tn), lambda i, j, k: (k, j))],
        out_specs=pl.BlockSpec((tm, tn), lambda i, j, k: (i, j)),
        scratch_shapes=[pltpu.VMEM((tm, tn), F32)],
        compiler_params=_params("parallel", "parallel", "arbitrary"),
    )(*parts, b)


def _matmul_tn(a, b, name, tm=1024, tn=1024, tk=1024, out_dtype=F32):
    kdim, m = a.shape
    n = b.shape[1]
    tm, tn, tk = _pick(m, tm), _pick(n, tn), _pick(kdim, tk)
    nk = kdim // tk

    def body(a_ref, b_ref, o_ref, acc_ref):
        kk = pl.program_id(2)

        @pl.when(kk == 0)
        def _():
            acc_ref[...] = jnp.zeros_like(acc_ref)

        acc_ref[...] += lax.dot_general(a_ref[...], b_ref[...], (((0,), (0,)), ((), ())),
                                        preferred_element_type=F32)

        @pl.when(kk == nk - 1)
        def _():
            o_ref[...] = acc_ref[...].astype(o_ref.dtype)

    return pl.pallas_call(
        body, name=name,
        out_shape=jax.ShapeDtypeStruct((m, n), out_dtype),
        grid=(m // tm, n // tn, nk),
        in_specs=[pl.BlockSpec((tk, tm), lambda i, j, k: (k, i)),
                  pl.BlockSpec((tk, tn), lambda i, j, k: (k, j))],
        out_specs=pl.BlockSpec((tm, tn), lambda i, j, k: (i, j)),
        scratch_shapes=[pltpu.VMEM((tm, tn), F32)],
        compiler_params=_params("parallel", "parallel", "arbitrary"),
    )(a, b)


def _grouped_weight_grad(a, b, ng, name, tk=1024):
    s, e = a.shape
    g = e // ng
    tk = _pick(s, tk)
    nk = s // tk

    def body(a_ref, b_ref, o_ref):
        kk = pl.program_id(1)

        @pl.when(kk == 0)
        def _():
            o_ref[...] = jnp.zeros_like(o_ref)

        o_ref[...] += lax.dot_general(a_ref[...], b_ref[...], (((0,), (0,)), ((), ())),
                                      preferred_element_type=F32)

    return pl.pallas_call(
        body, name=name,
        out_shape=jax.ShapeDtypeStruct((ng, g, g), F32),
        grid=(ng, nk),
        in_specs=[pl.BlockSpec((tk, g), lambda j, k: (k, j)),
                  pl.BlockSpec((tk, g), lambda j, k: (k, j))],
        out_specs=pl.BlockSpec((None, g, g), lambda j, k: (j, 0, 0)),
        compiler_params=_params("parallel", "arbitrary"),
    )(a, b)


def _rms(x):
    r = lax.rsqrt(jnp.mean(x * x, axis=1, keepdims=True) + RMS_EPS)
    return x * r, r


def _rmsnorm_fwd(x, g, name, ts=256):
    s, d = x.shape
    ts = _pick(s, ts, 8)

    def body(x_ref, g_ref, h_ref):
        xhat, _ = _rms(x_ref[...])
        h_ref[...] = (xhat * g_ref[...]).astype(BF16)

    return pl.pallas_call(
        body, name=name,
        out_shape=jax.ShapeDtypeStruct((s, d), BF16),
        grid=(s // ts,),
        in_specs=[pl.BlockSpec((ts, d), lambda i: (i, 0)), pl.BlockSpec((1, d), lambda i: (0, 0))],
        out_specs=pl.BlockSpec((ts, d), lambda i: (i, 0)),
        compiler_params=_params("parallel"),
    )(x, g)


def _outproj_norm(z, w, x, g, name, tm=512):
    s, e = z.shape
    d = w.shape[1]
    tm = _pick(s, tm)

    def body(z_ref, w_ref, x_ref, g_ref, xo_ref, h_ref):
        xn = x_ref[...] + jnp.dot(z_ref[...], w_ref[...], preferred_element_type=F32)
        xo_ref[...] = xn
        xhat, _ = _rms(xn)
        h_ref[...] = (xhat * g_ref[...]).astype(BF16)

    return pl.pallas_call(
        body, name=name,
        out_shape=(jax.ShapeDtypeStruct((s, d), F32), jax.ShapeDtypeStruct((s, d), BF16)),
        grid=(s // tm,),
        in_specs=[pl.BlockSpec((tm, e), lambda i: (i, 0)), pl.BlockSpec((e, d), lambda i: (0, 0)),
                  pl.BlockSpec((tm, d), lambda i: (i, 0)), pl.BlockSpec((1, d), lambda i: (0, 0))],
        out_specs=(pl.BlockSpec((tm, d), lambda i: (i, 0)), pl.BlockSpec((tm, d), lambda i: (i, 0))),
        compiler_params=_params("parallel"),
    )(z, w, x, g)


def _outproj_loss(z, w, x, g, target, name, tm=512):
    s, e = z.shape
    d = w.shape[1]
    tm = _pick(s, tm)

    def body(z_ref, w_ref, x_ref, g_ref, t_ref, dx_ref, dxb_ref, dg_ref, loss_ref):
        i = pl.program_id(0)
        xn = x_ref[...] + jnp.dot(z_ref[...], w_ref[...], preferred_element_type=F32)
        xhat, r = _rms(xn)
        gain = g_ref[...]
        diff = xhat * gain - t_ref[...]
        dout = diff * (1.0 / d)
        dxhat = dout * gain
        dx = r * (dxhat - xhat * jnp.mean(dxhat * xhat, axis=1, keepdims=True))
        dx_ref[...] = dx
        dxb_ref[...] = dx.astype(BF16)

        @pl.when(i == 0)
        def _():
            dg_ref[...] = jnp.zeros_like(dg_ref)
            loss_ref[...] = jnp.zeros_like(loss_ref)

        dg_ref[...] += jnp.sum(dout * xhat, axis=0, keepdims=True)
        loss_ref[...] += jnp.sum(diff * diff, axis=0, keepdims=True)

    row = lambda i: (i, 0)
    fixed = lambda i: (0, 0)
    return pl.pallas_call(
        body, name=name,
        out_shape=(jax.ShapeDtypeStruct((s, d), F32), jax.ShapeDtypeStruct((s, d), BF16),
                   jax.ShapeDtypeStruct((1, d), F32), jax.ShapeDtypeStruct((1, d), F32)),
        grid=(s // tm,),
        in_specs=[pl.BlockSpec((tm, e), row), pl.BlockSpec((e, d), fixed), pl.BlockSpec((tm, d), row),
                  pl.BlockSpec((1, d), fixed), pl.BlockSpec((tm, d), row)],
        out_specs=(pl.BlockSpec((tm, d), row), pl.BlockSpec((tm, d), row),
                   pl.BlockSpec((1, d), fixed), pl.BlockSpec((1, d), fixed)),
        compiler_params=_params("arbitrary"),
    )(z, w, x, g, target)


def _rmsnorm_bwd(dhs, x, g, dx_next, name, ts=256):
    s, d = x.shape
    ts = _pick(s, ts, 8)
    n_dh = len(dhs)

    def body(*refs):
        dh_refs = refs[:n_dh]
        x_ref, g_ref, dn_ref, dx_ref, dxb_ref, dg_ref = refs[n_dh:]
        i = pl.program_id(0)
        xhat, r = _rms(x_ref[...])
        dh_ = dh_refs[0][...]
        for extra in dh_refs[1:]:
            dh_ = dh_ + extra[...]
        dxhat = dh_ * g_ref[...]
        dx = dn_ref[...] + r * (dxhat - xhat * jnp.mean(dxhat * xhat, axis=1, keepdims=True))
        dx_ref[...] = dx
        dxb_ref[...] = dx.astype(BF16)

        @pl.when(i == 0)
        def _():
            dg_ref[...] = jnp.zeros_like(dg_ref)

        dg_ref[...] += jnp.sum(dh_ * xhat, axis=0, keepdims=True)

    row = lambda i: (i, 0)
    fixed = lambda i: (0, 0)
    return pl.pallas_call(
        body, name=name,
        out_shape=(jax.ShapeDtypeStruct((s, d), F32), jax.ShapeDtypeStruct((s, d), BF16),
                   jax.ShapeDtypeStruct((1, d), F32)),
        grid=(s // ts,),
        in_specs=[pl.BlockSpec((ts, d), row)] * n_dh + [pl.BlockSpec((ts, d), row), pl.BlockSpec((1, d), fixed),
                                                        pl.BlockSpec((ts, d), row)],
        out_specs=(pl.BlockSpec((ts, d), row), pl.BlockSpec((ts, d), row), pl.BlockSpec((1, d), fixed)),
        compiler_params=_params("arbitrary"),
    )(*dhs, x, g, dx_next)


def _pool_counts(t0, rows, cols, window):
    t = t0 + lax.broadcasted_iota(jnp.int32, (rows, cols), 0)
    return jnp.minimum(t + 1, window).astype(F32)


def _proj_pool_fwd(h, w, name, ts=1024, tc=512):
    s, dm = h.shape
    e = w.shape[1]
    ng = len(POOL_WINDOWS)
    gdim = e // ng
    ts, tc = _pick(s, ts), _pick(gdim, tc)
    cpg = gdim // tc
    hb = ts // POOL_HALO

    def body(h_ref, halo_ref, w_ref, d_ref):
        i, grp = pl.program_id(0), pl.program_id(1)
        cur = jnp.dot(h_ref[...], w_ref[...], preferred_element_type=F32)
        halo = jnp.dot(halo_ref[...], w_ref[...], preferred_element_type=F32)
        ext = jnp.concatenate([jnp.where(i > 0, halo, 0.0), cur], axis=0)
        for gi, window in enumerate(POOL_WINDOWS):
            @pl.when(grp == gi)
            def _(window=window):
                acc = ext
                k = 1
                while k < window:
                    acc = acc + pltpu.roll(acc, k, 0)
                    k *= 2
                pooled = acc[POOL_HALO:, :] / _pool_counts(i * ts, ts, tc, window)
                d_ref[...] = (pooled - cur).astype(BF16)

    return pl.pallas_call(
        body, name=name,
        out_shape=jax.ShapeDtypeStruct((s, e), BF16),
        grid=(s // ts, ng, cpg),
        in_specs=[pl.BlockSpec((ts, dm), lambda i, g, j: (i, 0)),
                  pl.BlockSpec((POOL_HALO, dm), lambda i, g, j: (jnp.maximum(i * hb - 1, 0), 0)),
                  pl.BlockSpec((dm, tc), lambda i, g, j: (0, g * cpg + j))],
        out_specs=pl.BlockSpec((ts, tc), lambda i, g, j: (i, g * cpg + j)),
        compiler_params=_params("parallel", "parallel", "parallel"),
    )(h, h, w)


def _dz_fused(dxb, w_t, tiles, vecs, n_out, epilogue, name, tm, tn, with_col_sum=False, rows=()):
    s, dm = dxb.shape
    e = w_t.shape[1]
    tm, tn = _pick(s, tm), _pick(e, tn)
    n_t, n_v, n_r = len(tiles), len(vecs), len(rows)

    def body(*refs):
        a_ref, b_ref = refs[:2]
        tile_refs, vec_refs = refs[2:2 + n_t], refs[2 + n_t:2 + n_t + n_v]
        row_refs = refs[2 + n_t + n_v:2 + n_t + n_v + n_r]
        out_refs = refs[2 + n_t + n_v + n_r:]
        i = pl.program_id(1)
        dz = jnp.dot(a_ref[...], b_ref[...], preferred_element_type=F32)
        extra = ([r[...] for r in row_refs], pl.program_id(0) * tn) if n_r else ()
        res = epilogue(dz, [t[...] for t in tile_refs], [v[...] for v in vec_refs], *extra)
        for o_ref, val in zip(out_refs[:n_out], res[:n_out]):
            o_ref[...] = val.astype(o_ref.dtype)
        if with_col_sum:
            sum_ref = out_refs[n_out]

            @pl.when(i == 0)
            def _():
                sum_ref[...] = jnp.zeros_like(sum_ref)

            sum_ref[...] += jnp.sum(res[n_out], axis=0, keepdims=True)

    blk = lambda j, i: (i, j)
    vec = lambda j, i: (0, j)
    out_shape = [jax.ShapeDtypeStruct((s, e), BF16)] * n_out
    out_specs = [pl.BlockSpec((tm, tn), blk)] * n_out
    if with_col_sum:
        out_shape.append(jax.ShapeDtypeStruct((1, e), F32))
        out_specs.append(pl.BlockSpec((1, tn), vec))
    return pl.pallas_call(
        body, name=name,
        out_shape=tuple(out_shape),
        grid=(e // tn, s // tm),
        in_specs=[pl.BlockSpec((tm, dm), lambda j, i: (i, 0)), pl.BlockSpec((dm, tn), lambda j, i: (0, j))]
        + [pl.BlockSpec((tm, tn), blk)] * n_t + [pl.BlockSpec((1, tn), vec)] * n_v
        + [pl.BlockSpec((tm, r.shape[1]), lambda j, i: (i, 0)) for r in rows],
        out_specs=tuple(out_specs),
        compiler_params=_params("parallel", "arbitrary"),
    )(dxb, w_t, *tiles, *vecs, *rows)


def _group_pool_bwd(dyr, w_t, name, ts=1024, tc=512):
    s, e = dyr.shape
    ng = len(POOL_WINDOWS)
    gdim = e // ng
    ts, tc = _pick(s, ts), _pick(gdim, tc)
    cpg = gdim // tc
    hb = ts // POOL_HALO
    n_halo = s // POOL_HALO
    nst = s // ts

    def body(dy_ref, halo_ref, w_ref, du_ref):
        i, grp = pl.program_id(0), pl.program_id(1)
        cur = jnp.dot(dy_ref[...], w_ref[...], preferred_element_type=F32)
        halo = jnp.dot(halo_ref[...], w_ref[...], preferred_element_type=F32)
        ext = jnp.concatenate([cur, jnp.where(i < nst - 1, halo, 0.0)], axis=0)
        rows = ts + POOL_HALO
        for gi, window in enumerate(POOL_WINDOWS):
            @pl.when(grp == gi)
            def _(window=window):
                acc = ext / _pool_counts(i * ts, rows, tc, window)
                k = 1
                while k < window:
                    acc = acc + pltpu.roll(acc, rows - k, 0)
                    k *= 2
                du_ref[...] = (acc[:ts, :] - cur).astype(BF16)

    return pl.pallas_call(
        body, name=name,
        out_shape=jax.ShapeDtypeStruct((s, e), BF16),
        grid=(nst, ng, cpg),
        in_specs=[pl.BlockSpec((ts, gdim), lambda i, g, j: (i, g)),
                  pl.BlockSpec((POOL_HALO, gdim), lambda i, g, j: (jnp.minimum((i + 1) * hb, n_halo - 1), g)),
                  pl.BlockSpec((None, gdim, tc), lambda i, g, j: (g, 0, j))],
        out_specs=pl.BlockSpec((ts, tc), lambda i, g, j: (i, g * cpg + j)),
        compiler_params=_params("parallel", "parallel", "parallel"),
    )(dyr, dyr, w_t)


def _a_group_fwd(d, w, scale, gate, name, tm=1024):
    s, e = d.shape
    ng, g, _ = w.shape
    tm = _pick(s, tm)

    def body(d_ref, w_ref, s_ref, gate_ref, yr_ref, z_ref):
        yr = jnp.dot(d_ref[...], w_ref[...], preferred_element_type=F32)
        yr_ref[...] = yr.astype(yr_ref.dtype)
        gt = gate_ref[...].astype(F32)
        z_ref[...] = ((yr * s_ref[...]) * (gt * _sigmoid(gt))).astype(BF16)

    blk = lambda i, j: (i, j)
    return pl.pallas_call(
        body, name=name,
        out_shape=(jax.ShapeDtypeStruct((s, e), BF16), jax.ShapeDtypeStruct((s, e), BF16)),
        grid=(s // tm, ng),
        in_specs=[pl.BlockSpec((tm, g), blk), pl.BlockSpec((None, g, g), lambda i, j: (j, 0, 0)),
                  pl.BlockSpec((1, g), lambda i, j: (0, j)), pl.BlockSpec((tm, g), blk)],
        out_specs=(pl.BlockSpec((tm, g), blk), pl.BlockSpec((tm, g), blk)),
        compiler_params=_params("parallel", "parallel"),
    )(d, w, scale, gate)


def _silu_and_slope(gt):
    sg = _sigmoid(gt)
    return gt * sg, sg * (1.0 + gt * (1.0 - sg))


def _a_gate_epilogue(dz, tiles, vecs):
    yr, gt = tiles[0].astype(F32), tiles[1].astype(F32)
    sc = vecs[0]
    silu, slope = _silu_and_slope(gt)
    dy = dz * silu
    return dz * (yr * sc) * slope, dy * sc, dy * yr


def _gate_epilogue(dz, tiles, vecs):
    y, gt = tiles[0].astype(F32), tiles[1].astype(F32)
    silu, slope = _silu_and_slope(gt)
    return dz * y * slope, dz * silu


def _merge_gate_epilogue(dz, tiles, vecs, lses, col0):
    y, gt = tiles[0].astype(F32), tiles[1].astype(F32)
    silu, slope = _silu_and_slope(gt)
    dy = dz * silu
    w0, w1, w2 = _merge_weights_expanded(lses, col0, dz.shape[1])
    return dz * y * slope, w0 * dy, w1 * dy, w2 * dy


def _gate_fwd(y, gate, name, ts=512, tc=512):
    s, e = y.shape
    ts, tc = _pick(s, ts), _pick(e, tc)

    def body(y_ref, gate_ref, z_ref):
        gt = gate_ref[...].astype(F32)
        z_ref[...] = (y_ref[...].astype(F32) * (gt * _sigmoid(gt))).astype(BF16)

    blk = lambda i, j: (i, j)
    return pl.pallas_call(
        body, name=name,
        out_shape=jax.ShapeDtypeStruct((s, e), BF16),
        grid=(s // ts, e // tc),
        in_specs=[pl.BlockSpec((ts, tc), blk)] * 2,
        out_specs=pl.BlockSpec((ts, tc), blk),
        compiler_params=_params("parallel", "parallel"),
    )(y, gate)


def _merge_weights(l0, l1, l2):
    m = jnp.maximum(jnp.maximum(l0, l1), l2)
    e0, e1, e2 = jnp.exp(l0 - m), jnp.exp(l1 - m), jnp.exp(l2 - m)
    inv = 1.0 / (e0 + e1 + e2)
    return e0 * inv, e1 * inv, e2 * inv


def _expand_heads(w, col0, width):
    n_heads = w.shape[1]
    head_of_lane = (col0 + lax.broadcasted_iota(jnp.int32, (n_heads, width), 1)) // HEAD_DIM
    pick = jnp.where(head_of_lane == lax.broadcasted_iota(jnp.int32, (n_heads, width), 0), 1.0, 0.0).astype(BF16)
    high = w.astype(BF16)
    rest = (w - high.astype(F32)).astype(BF16)
    return (jnp.dot(high, pick, preferred_element_type=F32) + jnp.dot(rest, pick, preferred_element_type=F32))


def _merge_weights_expanded(lses, col0, width):
    return [_expand_heads(w, col0, width) for w in _merge_weights(*lses)]


def _merge_gate_fwd(outs, lses, gate, name, ts=512, tc=512):
    s, e = gate.shape
    n_heads = lses[0].shape[1]
    ts, tc = _pick(s, ts), _pick(e, tc)

    def body(o0, o1, o2, l0, l1, l2, gate_ref, y_ref, z_ref):
        w0, w1, w2 = _merge_weights_expanded([l0[...], l1[...], l2[...]], pl.program_id(1) * tc, tc)
        y = w0 * o0[...].astype(F32) + w1 * o1[...].astype(F32) + w2 * o2[...].astype(F32)
        y_ref[...] = y.astype(y_ref.dtype)
        gt = gate_ref[...].astype(F32)
        z_ref[...] = (y * (gt * _sigmoid(gt))).astype(BF16)

    blk = lambda i, j: (i, j)
    per_head = pl.BlockSpec((ts, n_heads), lambda i, j: (i, 0))
    return pl.pallas_call(
        body, name=name,
        out_shape=(jax.ShapeDtypeStruct((s, e), BF16), jax.ShapeDtypeStruct((s, e), BF16)),
        grid=(s // ts, e // tc),
        in_specs=[pl.BlockSpec((ts, tc), blk)] * 3 + [per_head] * 3 + [pl.BlockSpec((ts, tc), blk)],
        out_specs=(pl.BlockSpec((ts, tc), blk), pl.BlockSpec((ts, tc), blk)),
        compiler_params=_params("parallel", "parallel"),
    )(*outs, *lses, gate)


def _band(max_dist, width):
    row = lax.broadcasted_iota(jnp.int32, (2 * BLOCK, width), 0) & (BLOCK - 1)
    col = lax.broadcasted_iota(jnp.int32, (2 * BLOCK, width), 1)
    low = row if max_dist == BLOCK else row + 1
    return jnp.logical_and(col >= low, col <= row + BLOCK), col >= BLOCK


def _fill_bias(bias_ref, max_dist):
    @pl.when(jnp.logical_and(pl.program_id(0) == 0, jnp.logical_and(pl.program_id(1) == 0, pl.program_id(2) == 0)))
    def _():
        band, own = _band(max_dist, 2 * BLOCK)
        bias_ref[0] = jnp.where(band, 0.0, NEG)
        bias_ref[1] = jnp.where(jnp.logical_and(band, own), 0.0, NEG)


def _aligned(v):
    return v if isinstance(v, int) else pl.multiple_of(v, BLOCK)


def _stack_heads(x, lo):
    return jnp.concatenate([jnp.where(lo, x, 0.0), jnp.where(lo, 0.0, x)], axis=0).astype(BF16)


def _unstack_heads(x2, lo):
    return jnp.where(lo, x2[:BLOCK], x2[BLOCK:])


def _head_col(x, hm):
    return jnp.max(jnp.where(hm, x, NEG), axis=1, keepdims=True)


def _dot_nt(a, b):
    return lax.dot_general(a, b, (((1,), (1,)), ((), ())), preferred_element_type=F32)


def _dot_tn(a, b):
    return lax.dot_general(a, b, (((0,), (0,)), ((), ())), preferred_element_type=F32)


def _stream_view(a, dil):
    s, w = a.shape
    return a.reshape(s // (BLOCK * dil), dil, BLOCK, w)


def _fill_window(dst, halo_ref, cur_ref, n):
    dst[0:BLOCK, :] = halo_ref[0]
    for jc in range(n):
        dst[(jc + 1) * BLOCK:(jc + 2) * BLOCK, :] = cur_ref[jc]


def _attn_fwd(q, k, v, sinks, max_dist, rep, dil, out_dtype, name, tq=2048, per_head_lse=False):
    assert max_dist in (BLOCK - 1, BLOCK)
    s, w = q.shape
    l = s // dil
    n_pairs = w // LANES
    n_heads = 2 * n_pairs
    tq = _pick(l, tq)
    n = tq // BLOCK
    has_sink = sinks is not None
    scale = HEAD_DIM ** -0.5

    def body(*refs):
        if has_sink:
            sink_ref, refs = refs[0], refs[1:]
        q_ref, kc_ref, kh_ref, vc_ref, vh_ref, o_ref, lse_ref = refs[:7]
        refs = refs[7:]
        if per_head_lse:
            lseh_ref, refs = refs[0], refs[1:]
        kx, vx, bias_ref = refs
        i, p = pl.program_id(0), pl.program_id(2)
        _fill_window(kx, kh_ref, kc_ref, n)
        _fill_window(vx, vh_ref, vc_ref, n)
        if per_head_lse:
            @pl.when(p == 0)
            def _():
                lseh_ref[...] = jnp.zeros_like(lseh_ref)
            head_lane = lax.broadcasted_iota(jnp.int32, (BLOCK, n_heads), 1)
        lo = lax.broadcasted_iota(jnp.int32, (BLOCK, LANES), 1) < HEAD_DIM
        _fill_bias(bias_ref, max_dist)
        top = lax.broadcasted_iota(jnp.int32, (2 * BLOCK, 1), 0) < BLOCK

        def scores(j):
            r0 = _aligned(j * BLOCK)
            q2 = _stack_heads(q_ref[j].astype(F32) * scale, lo)
            first = jnp.logical_and(i == 0, j == 0).astype(jnp.int32)
            return _dot_nt(q2, kx[pl.ds(r0, 2 * BLOCK), :]) + bias_ref[first]

        per_step = 2 if n % 2 == 0 else 1

        def step(jj, carry):
            nxt = tuple(scores(jnp.minimum((jj + 1) * per_step + t, n - 1)) for t in range(per_step))
            for t in range(per_step):
                finish(jj * per_step + t, carry[t])
            return nxt

        def finish(j, s2):
            r0 = _aligned(j * BLOCK)
            vw = vx[pl.ds(r0, 2 * BLOCK), :]
            m = jnp.max(s2, axis=1, keepdims=True)
            if has_sink:
                sk = jnp.where(top, sink_ref[2 * p], sink_ref[2 * p + 1])
                m = jnp.maximum(m, sk)
            pr = jnp.exp(s2 - m)
            den = jnp.sum(pr, axis=1, keepdims=True)
            if has_sink:
                den = den + jnp.exp(sk - m)
            o2 = jnp.dot(pr.astype(BF16), vw, preferred_element_type=F32) * (1.0 / den)
            lse2 = m + jnp.log(den)
            o_ref[j] = _unstack_heads(o2, lo).astype(o_ref.dtype)
            lse_ref[j] = _unstack_heads(lse2, lo)
            if per_head_lse:
                lseh_ref[j] = jnp.where(head_lane == 2 * p, lse2[:BLOCK],
                                        jnp.where(head_lane == 2 * p + 1, lse2[BLOCK:], lseh_ref[j]))

        lax.fori_loop(0, n // per_step, step, tuple(scores(t) for t in range(per_step)))

    cur = lambda i, r, p: (i, r, 0, p)
    kv_cur = lambda i, r, p: (i, r, 0, p // rep)
    kv_halo = lambda i, r, p: (jnp.maximum(i * n - 1, 0), r, 0, p // rep)
    big, small = (n, None, BLOCK, LANES), (1, None, BLOCK, LANES)
    in_specs = [pl.BlockSpec(big, cur), pl.BlockSpec(big, kv_cur), pl.BlockSpec(small, kv_halo),
                pl.BlockSpec(big, kv_cur), pl.BlockSpec(small, kv_halo)]
    q4, k4, v4 = _stream_view(q, dil), _stream_view(k, dil), _stream_view(v, dil)
    args = [q4, k4, k4, v4, v4]
    if has_sink:
        in_specs = [pl.BlockSpec(memory_space=pltpu.SMEM)] + in_specs
        args = [sinks] + args
    out_shape = [jax.ShapeDtypeStruct(q4.shape, out_dtype), jax.ShapeDtypeStruct(q4.shape, F32)]
    out_specs = [pl.BlockSpec(big, cur), pl.BlockSpec(big, cur)]
    if per_head_lse:
        out_shape.append(jax.ShapeDtypeStruct(q4.shape[:3] + (n_heads,), F32))
        out_specs.append(pl.BlockSpec((n, None, BLOCK, n_heads), lambda i, r, p: (i, r, 0, 0)))
    outs = pl.pallas_call(
        body, name=name,
        out_shape=tuple(out_shape),
        grid=(l // tq, dil, n_pairs),
        in_specs=in_specs,
        out_specs=tuple(out_specs),
        scratch_shapes=[pltpu.VMEM((tq + BLOCK, LANES), BF16), pltpu.VMEM((tq + BLOCK, LANES), BF16),
                        pltpu.VMEM((2, 2 * BLOCK, 2 * BLOCK), F32)],
        compiler_params=_params("arbitrary", "arbitrary", "arbitrary"),
    )(*args)
    res = [outs[0].reshape(s, w), outs[1].reshape(s, w)]
    if per_head_lse:
        res.append(outs[2].reshape(s, n_heads))
    return res


def _attn_bwd(q, k, v, do, y, lse, sinks, max_dist, rep, dil, name, tq=2048):
    s, w = q.shape
    l = s // dil
    n_pairs = w // LANES
    tq = _pick(l, tq)
    n = tq // BLOCK
    n_blk = l // BLOCK
    n_sb = l // tq
    has_sink = sinks is not None
    scale = HEAD_DIM ** -0.5
    kv_dtype = BF16
    ext = tq + BLOCK

    def body(*refs):
        if has_sink:
            sink_ref, refs = refs[0], refs[1:]
        (q_ref, qn_ref, kc_ref, kh_ref, vc_ref, vh_ref, do_ref, don_ref, y_ref, yn_ref,
         lse_ref, lsen_ref) = refs[:12]
        refs = refs[12:]
        dq_ref, dk_ref, dv_ref = refs[:3]
        refs = refs[3:]
        if has_sink:
            dsink_ref, refs = refs[0], refs[1:]
        kx, vx, bias_ref = refs[:3]
        if rep > 1:
            dk_acc, dv_acc = refs[3:]
        i, p = pl.program_id(0), pl.program_id(2)
        _fill_bias(bias_ref, max_dist)
        own_rows = (q_ref, do_ref, y_ref, lse_ref)
        next_rows = (qn_ref, don_ref, yn_ref, lsen_ref)
        _fill_window(kx, kh_ref, kc_ref, n)
        _fill_window(vx, vh_ref, vc_ref, n)
        if rep > 1:
            @pl.when(p % rep == 0)
            def _():
                dk_acc[...] = jnp.zeros_like(dk_acc)
                dv_acc[...] = jnp.zeros_like(dv_acc)
        lo = lax.broadcasted_iota(jnp.int32, (BLOCK, LANES), 1) < HEAD_DIM
        hi = jnp.logical_not(lo)
        top = lax.broadcasted_iota(jnp.int32, (2 * BLOCK, 1), 0) < BLOCK

        def rows_of(j):
            if isinstance(j, int) and j == n:
                return next_rows, 0
            return own_rows, j

        def front(j, width):
            (qr, dor, _, _), jb = rows_of(j)
            r0 = _aligned(j * BLOCK)
            first = jnp.logical_and(i == 0, j == 0).astype(jnp.int32)
            q2 = _stack_heads(qr[jb].astype(F32) * scale, lo)
            do2 = _stack_heads(dor[jb].astype(F32), lo)
            s2 = _dot_nt(q2, kx[pl.ds(r0, width), :]) + bias_ref[first, :, pl.ds(0, width)]
            return s2, _dot_nt(do2, vx[pl.ds(r0, width), :])

        row_lo = lax.broadcasted_iota(jnp.int32, (LANES, BLOCK), 0) < HEAD_DIM

        def stack_t(x):
            xt = x.T
            return jnp.concatenate([jnp.where(row_lo, xt, 0.0), jnp.where(row_lo, 0.0, xt)], axis=1).astype(BF16)

        def emit(jk, dk_t, dv_t):
            dk_blk, dv_blk = dk_t.T, dv_t.T
            if rep == 1:
                dk_ref[jk] = dk_blk.astype(dk_ref.dtype)
                dv_ref[jk] = dv_blk.astype(dv_ref.dtype)
            else:
                rows = pl.ds(_aligned(jk * BLOCK), BLOCK)
                dk_acc[rows, :] += dk_blk
                dv_acc[rows, :] += dv_blk

        def back(j, width, q_valid, s2, dp2, state):
            sink_acc, carry_k, carry_v = state
            (qr, dor, yr, lser), jb = rows_of(j)
            r0 = _aligned(j * BLOCK)
            qf, dof = qr[jb].astype(F32) * scale, dor[jb].astype(F32)
            yb, lseb = yr[jb].astype(F32), lser[jb]
            prod = dof * yb
            delta = jnp.concatenate([jnp.sum(jnp.where(lo, prod, 0.0), axis=1, keepdims=True),
                                     jnp.sum(jnp.where(lo, 0.0, prod), axis=1, keepdims=True)], axis=0)
            lse2 = jnp.concatenate([_head_col(lseb, lo), _head_col(lseb, hi)], axis=0)
            pr = jnp.exp(s2 - lse2)
            if q_valid is not True:
                pr = jnp.where(q_valid, pr, 0.0)
            ds = pr * (dp2 - delta)
            dk_t = jnp.dot(stack_t(qf), ds.astype(BF16), preferred_element_type=F32)
            dv_t = jnp.dot(stack_t(dof), pr.astype(BF16), preferred_element_type=F32)
            done_k, done_v = carry_k + dk_t[:, :BLOCK], carry_v + dv_t[:, :BLOCK]
            if isinstance(j, int):
                emit(j - 1, done_k, done_v)
            elif rep == 1:
                emit(jnp.maximum(j - 1, 0), done_k, done_v)
            else:
                keep = j > 0
                emit(jnp.maximum(j - 1, 0), jnp.where(keep, done_k, 0.0), jnp.where(keep, done_v, 0.0))
            if width == 2 * BLOCK:
                dq2 = jnp.dot(ds.astype(BF16), kx[pl.ds(r0, width), :], preferred_element_type=F32) * scale
                dq_ref[jb] = _unstack_heads(dq2, lo).astype(dq_ref.dtype)
                carry_k, carry_v = dk_t[:, BLOCK:], dv_t[:, BLOCK:]
            if has_sink:
                sk = jnp.where(top, sink_ref[2 * p], sink_ref[2 * p + 1])
                sink_acc = sink_acc - jnp.exp(sk - lse2) * delta
            return sink_acc, carry_k, carry_v

        per_step = 2 if n % 2 == 0 else 1

        def step(jj, state):
            fronts = [front(jj * per_step + t, 2 * BLOCK) for t in range(per_step)]
            for t in range(per_step):
                state = back(jj * per_step + t, 2 * BLOCK, True, *fronts[t], state)
            return state

        zero_blk = jnp.zeros((LANES, BLOCK), F32)
        state = lax.fori_loop(0, n // per_step, step, (jnp.zeros((2 * BLOCK, 1), F32), zero_blk, zero_blk))
        sink_acc = state[0]
        if n_sb > 1:
            back(n, BLOCK, i < n_sb - 1, *front(n, BLOCK), state)
        else:
            emit(n - 1, state[1], state[2])

        if rep > 1:
            @pl.when(p % rep == rep - 1)
            def _():
                for jc in range(n):
                    rows = slice(jc * BLOCK, (jc + 1) * BLOCK)
                    dk_ref[jc] = dk_acc[rows, :].astype(dk_ref.dtype)
                    dv_ref[jc] = dv_acc[rows, :].astype(dv_ref.dtype)
        if has_sink:
            rowi = lax.broadcasted_iota(jnp.int32, (8, LANES), 0)
            s0 = jnp.sum(sink_acc[:BLOCK], axis=0, keepdims=True)
            s1 = jnp.sum(sink_acc[BLOCK:], axis=0, keepdims=True)
            dsink_ref[...] = jnp.where(rowi == 0, s0, jnp.where(rowi == 1, s1, 0.0))

    cur = lambda i, r, p: (i, r, 0, p)
    nxt = lambda i, r, p: (jnp.minimum((i + 1) * n, n_blk - 1), r, 0, p)
    kv_cur = lambda i, r, p: (i, r, 0, p // rep)
    kv_halo = lambda i, r, p: (jnp.maximum(i * n - 1, 0), r, 0, p // rep)
    big, small = (n, None, BLOCK, LANES), (1, None, BLOCK, LANES)
    in_specs = [pl.BlockSpec(big, cur), pl.BlockSpec(small, nxt),
                pl.BlockSpec(big, kv_cur), pl.BlockSpec(small, kv_halo),
                pl.BlockSpec(big, kv_cur), pl.BlockSpec(small, kv_halo),
                pl.BlockSpec(big, cur), pl.BlockSpec(small, nxt),
                pl.BlockSpec(big, cur), pl.BlockSpec(small, nxt),
                pl.BlockSpec(big, cur), pl.BlockSpec(small, nxt)]
    q4, k4, v4, do4, y4, lse4 = [_stream_view(a, dil) for a in (q, k, v, do, y, lse)]
    args = [q4, q4, k4, k4, v4, v4, do4, do4, y4, y4, lse4, lse4]
    out_shape = [jax.ShapeDtypeStruct(q4.shape, BF16),
                 jax.ShapeDtypeStruct(k4.shape, kv_dtype), jax.ShapeDtypeStruct(v4.shape, kv_dtype)]
    out_specs = [pl.BlockSpec(big, cur), pl.BlockSpec(big, kv_cur), pl.BlockSpec(big, kv_cur)]
    if has_sink:
        in_specs = [pl.BlockSpec(memory_space=pltpu.SMEM)] + in_specs
        args = [sinks] + args
        out_shape.append(jax.ShapeDtypeStruct((n_sb, dil, n_pairs, 8, LANES), F32))
        out_specs.append(pl.BlockSpec((None, None, None, 8, LANES), lambda i, r, p: (i, r, p, 0, 0)))
    outs = pl.pallas_call(
        body, name=name,
        out_shape=tuple(out_shape),
        grid=(n_sb, dil, n_pairs),
        in_specs=in_specs,
        out_specs=tuple(out_specs),
        scratch_shapes=[pltpu.VMEM((ext, LANES), BF16), pltpu.VMEM((ext, LANES), BF16),
                        pltpu.VMEM((2, 2 * BLOCK, 2 * BLOCK), F32)]
        + ([pltpu.VMEM((tq, LANES), F32), pltpu.VMEM((tq, LANES), F32)] if rep > 1 else []),
        compiler_params=_params("arbitrary", "arbitrary", "arbitrary"),
    )(*args)
    grads =[outs[0].reshape(s, w), outs[1].reshape(k.shape), outs[2].reshape(v.shape)]
    if has_sink:
        grads.append(outs[3].sum(axis=(0, 1))[:, 0:2, 0].reshape(1, 2 * n_pairs))
    return grads


def _sum_slots(recv, name, ts=256):
    nd, r, c = recv.shape
    ts = _pick(r, ts, 8)

    def body(r_ref, o_ref):
        acc = r_ref[0].astype(F32)
        for dev in range(1, nd):
            acc = acc + r_ref[dev].astype(F32)
        o_ref[...] = acc

    return pl.pallas_call(
        body, name=name,
        out_shape=jax.ShapeDtypeStruct((r, c), F32),
        grid=(r // ts,),
        in_specs=[pl.BlockSpec((nd, ts, c), lambda i: (0, i, 0))],
        out_specs=pl.BlockSpec((ts, c), lambda i: (i, 0)),
        compiler_params=_params("parallel"),
    )(recv)


def _adamw_math(w, g, m, v):
    c1 = 1.0 - ADAM_B1 ** ADAM_STEP
    c2 = 1.0 - ADAM_B2 ** ADAM_STEP
    m_ = ADAM_B1 * m + (1.0 - ADAM_B1) * g
    v_ = ADAM_B2 * v + (1.0 - ADAM_B2) * (g * g)
    return -ADAM_LR * ((m_ / c1) / (jnp.sqrt(v_ / c2) + ADAM_EPS) + ADAM_WD * w), m_, v_


def _row_tile(r, c, budget=1 << 18):
    return _pick(r, max(8, min(256, budget // c // 8 * 8)), 8)


def _adamw(w, g, m, v, name):
    r, c = w.shape
    ts = _row_tile(r, c)

    def body(w_ref, g_ref, m_ref, v_ref, d_ref, mo_ref, vo_ref):
        d_ref[...], mo_ref[...], vo_ref[...] = _adamw_math(w_ref[...], g_ref[...], m_ref[...], v_ref[...])

    blk = pl.BlockSpec((ts, c), lambda i: (i, 0))
    return pl.pallas_call(
        body, name=name,
        out_shape=tuple([jax.ShapeDtypeStruct((r, c), F32)] * 3),
        grid=(r // ts,),
        in_specs=[blk] * 4,
        out_specs=(blk, blk, blk),
        compiler_params=_params("parallel"),
    )(w, g, m, v)


def _adamw_slots(w, slots, m, v, name):
    r, c = w.shape
    nd = slots.shape[0]
    ts = _row_tile(r, c)

    def body(w_ref, s_ref, m_ref, v_ref, g_ref, d_ref, mo_ref, vo_ref):
        g = s_ref[0].astype(F32)
        for slot in range(1, nd):
            g = g + s_ref[slot].astype(F32)
        g_ref[...] = g
        d_ref[...], mo_ref[...], vo_ref[...] = _adamw_math(w_ref[...], g, m_ref[...], v_ref[...])

    blk = pl.BlockSpec((ts, c), lambda i: (i, 0))
    return pl.pallas_call(
        body, name=name,
        out_shape=tuple([jax.ShapeDtypeStruct((r, c), F32)] * 4),
        grid=(r // ts,),
        in_specs=[blk, pl.BlockSpec((nd, ts, c), lambda i: (0, i, 0)), blk, blk],
        out_specs=(blk, blk, blk, blk),
        compiler_params=_params("parallel"),
    )(w, slots, m, v)


def _rows(a):
    flat = a.reshape(-1)
    pad = (-flat.shape[0]) % PACK_W
    if pad:
        flat = jnp.concatenate([flat, jnp.zeros((pad,), flat.dtype)])
    return flat.reshape(-1, PACK_W)


def _pad_rows(a, mult):
    pad = (-a.shape[-2]) % mult
    if pad:
        widths = [(0, 0)] * (a.ndim - 2) + [(0, pad), (0, 0)]
        a = jnp.pad(a, widths)
    return a


def _to_global(stack, axis):
    moved = jnp.moveaxis(stack, 0, axis)
    shp = list(moved.shape)
    shp[axis:axis + 2] = [shp[axis] * shp[axis + 1]]
    return moved.reshape(shp)


def _to_stack(full, axis):
    shp = list(full.shape)
    shp[axis:axis + 1] = [N_DEV, shp[axis] // N_DEV]
    return jnp.moveaxis(full.reshape(shp), axis, 0)


_BIG = (("w_out", 1), ("a_w_in", 2), ("a_w_group", 2), ("b_w_in", 2), ("c_w_in", 2))


def _dup_heads(wk, n_kv):
    d = wk.shape[0]
    return jnp.tile(wk.reshape(d, n_kv, 1, HEAD_DIM), (1, 1, 2, 1)).reshape(d, n_kv * LANES)


def _fold_heads(dwk, n_kv):
    d = dwk.shape[0]
    folded = dwk.astype(F32).reshape(d, n_kv, 2, HEAD_DIM).sum(axis=2)
    return folded.reshape(d, n_kv * HEAD_DIM).astype(dwk.dtype)


def _perm(a, dil):
    if dil == 1:
        return a
    s, w = a.shape
    return a.reshape(s // (BLOCK * dil), BLOCK, dil, w).transpose(0, 2, 1, 3).reshape(s, w)


def _unperm(a, dil):
    if dil == 1:
        return a
    s, w = a.shape
    return a.reshape(s // (BLOCK * dil), dil, BLOCK, w).transpose(0, 2, 1, 3).reshape(s, w)


def kernel(x, norm_g, final_g, w_out, a_w_in, a_w_group, a_scale, b_w_in, b_sinks, c_w_in, loss_target, m_norm_g, m_final_g, m_w_out, m_a_w_in, m_a_w_group, m_a_scale, m_b_w_in, m_b_sinks, m_c_w_in, v_norm_g, v_final_g, v_w_out, v_a_w_in, v_a_w_group, v_a_scale, v_b_w_in, v_b_sinks, v_c_w_in):
    local = dict(w_out=w_out, a_w_in=a_w_in, a_w_group=a_w_group, b_w_in=b_w_in, c_w_in=c_w_in)
    mom_m = dict(w_out=m_w_out, a_w_in=m_a_w_in, a_w_group=m_a_w_group, b_w_in=m_b_w_in, c_w_in=m_c_w_in)
    mom_v = dict(w_out=v_w_out, a_w_in=v_a_w_in, a_w_group=v_a_w_group, b_w_in=v_b_w_in, c_w_in=v_c_w_in)
    s, d = x.shape[1], x.shape[2]
    depth = norm_g.shape[0]
    e = w_out.shape[1] * N_DEV
    n_heads = e // HEAD_DIM
    n_kv = n_heads // Q_PER_KV
    kv_w = n_kv * HEAD_DIM
    rep = Q_PER_KV // 2
    n_groups = len(POOL_WINDOWS)
    me = 4 * lax.axis_index("x") + 2 * lax.axis_index("y") + lax.axis_index("c")

    flat = {n: local[n].reshape(-1, local[n].shape[-1]) for n, _ in _BIG}
    spack = _pad_rows(_rows(a_scale), 8)
    *walls, sall = _gather([flat[n].astype(BF16) for n, _ in _BIG] + [spack], [True] * len(_BIG) + [False],
                           "gather_weights")
    full = {}
    for k, (name, axis) in enumerate(_BIG):
        full[name] = _to_global(walls[k].reshape((N_DEV,) + local[name].shape), axis)
    scale_full = _to_global(sall.reshape(N_DEV, -1)[:, :a_scale.size].reshape((N_DEV,) + a_scale.shape), 1)

    wout_t = jnp.swapaxes(full["w_out"], 1, 2)
    wa = full["a_w_in"]
    wa_t = jnp.swapaxes(wa, 1, 2)
    wg = full["a_w_group"]
    wg_t = jnp.swapaxes(wg, 2, 3)
    wb = full["b_w_in"][0]
    wb_ext = jnp.concatenate([wb[:, :e], _dup_heads(wb[:, e:e + kv_w], n_kv),
                              _dup_heads(wb[:, e + kv_w:e + 2 * kv_w], n_kv), wb[:, e + 2 * kv_w:]], axis=1)
    wb_ext_t = wb_ext.T
    kd_w = n_kv * LANES
    wc = full["c_w_in"][0]
    wc_t = wc.T

    xs, hs, zs, saved = [x.reshape(s, d)], [], [], []
    hs.append(_rmsnorm_fwd(xs[0], norm_g[0:1], "norm0"))
    loss_vec = dfinal = dx = dxb = None
    for i in range(depth):
        kind, j = i % 3, i // 3
        h = hs[i]
        tag = f"l{i}"
        if kind == 0:
            dpool = _proj_pool_fwd(h, wa[j][:, :e], tag + "_in_pool")
            gate = _matmul(h, wa[j][:, e:], BF16, tag + "_in_gate")
            yr, z = _a_group_fwd(dpool, wg[j], scale_full[j:j + 1], gate, tag + "_group")
            saved.append(dict(dpool=dpool, yr=yr, gate=gate))
        elif kind == 1:
            q = _matmul(h, wb_ext[:, :e], BF16, tag + "_in_q")
            kd = _matmul(h, wb_ext[:, e:e + kd_w], BF16, tag + "_in_k")
            vd = _matmul(h, wb_ext[:, e + kd_w:e + 2 * kd_w], BF16, tag + "_in_v")
            gate = _matmul(h, wb_ext[:, e + 2 * kd_w:], BF16, tag + "_in_gate")
            sinks = b_sinks[j]
            y, lse = _attn_fwd(q, kd, vd, sinks, SWA_MAX_DIST, rep, 1, BF16, tag + "_attn")
            z = _gate_fwd(y, gate, tag + "_gate")
            saved.append(dict(q=q, kd=kd, vd=vd, gate=gate, y=y, lse=lse, sinks=sinks))
        else:
            qkv, outs, lses, lses_tok, h_perm = [], [], [], [], []
            for gi, (window, dil) in enumerate(DILATED_PAIRS):
                hp = _perm(h, dil)
                trio = [_matmul(hp, wc[:, (3 * gi + t) * e:(3 * gi + t + 1) * e], BF16,
                                f"{tag}_in_{'qkv'[t]}{gi}") for t in range(3)]
                o, lse, lse_heads = _attn_fwd(trio[0], trio[1], trio[2], None, window // dil, 1, dil, BF16,
                                              f"{tag}_attn{gi}", per_head_lse=True)
                qkv.append(trio)
                h_perm.append(hp)
                outs.append(_unperm(o, dil))
                lses.append(lse)
                lses_tok.append(_unperm(lse_heads, dil))
            gate = _matmul(h, wc[:, 9 * e:], BF16, tag + "_in_gate")
            y, z = _merge_gate_fwd(outs, lses_tok, gate, tag + "_merge")
            saved.append(dict(qkv=qkv, lses=lses, lses_tok=lses_tok, gate=gate, y=y, h_perm=h_perm))
        zs.append(z)
        if i + 1 < depth:
            x_new, h_new = _outproj_norm(z, full["w_out"][i], xs[i], norm_g[i + 1:i + 2], tag + "_out")
            xs.append(x_new)
            hs.append(h_new)
        else:
            dx, dxb, dfinal, loss_vec = _outproj_loss(z, full["w_out"][i], xs[i], final_g.reshape(1, d),
                                                      loss_target.reshape(s, d), tag + "_out_loss")

    g_full = {"w_out": [None] * depth, "a_w_in": [None] * wa.shape[0], "a_w_group": [None] * wa.shape[0]}
    d_norm = [None] * depth
    d_scale = [None] * wa.shape[0]
    d_sinks = None
    for i in reversed(range(depth)):
        kind, j = i % 3, i // 3
        tag = f"b{i}"
        sv = saved[i]
        g_full["w_out"][i] = _matmul_tn(zs[i], dxb, tag + "_dwout", out_dtype=BF16)
        if kind == 0:
            dgate, dyr, dsc = _dz_fused(dxb, wout_t[i], [sv["yr"], sv["gate"]], [scale_full[j:j + 1]], 2,
                                        _a_gate_epilogue, tag + "_dz_gate", 1024, 1024, with_col_sum=True)
            d_scale[j] = dsc
            du = _group_pool_bwd(dyr, wg_t[j], tag + "_dd_pool")
            g_full["a_w_group"][j] = _grouped_weight_grad(sv["dpool"], dyr, n_groups, tag + "_dwg")
            parts = [du, dgate]
            g_full["a_w_in"][j] = jnp.concatenate(
                [_matmul_tn(hs[i], part, f"{tag}_dwin{t}", out_dtype=BF16) for t, part in enumerate(parts)], axis=1)
            dhs = [_matmul_cat(parts, wa_t[j], F32, tag + "_dh")]
        elif kind == 1:
            dgate, do = _dz_fused(dxb, wout_t[i], [sv["y"], sv["gate"]], [], 2, _gate_epilogue,
                                  tag + "_dz_gate", 1024, 1024)
            dq, dkd, dvd, d_sinks = _attn_bwd(sv["q"], sv["kd"], sv["vd"], do, sv["y"], sv["lse"], sv["sinks"],
                                              SWA_MAX_DIST, rep, 1, tag + "_attn")
            parts = [dq, dkd, dvd, dgate]
            dws = [_matmul_tn(hs[i], part, f"{tag}_dwin{t}", out_dtype=BF16) for t, part in enumerate(parts)]
            g_full["b_w_in"] = jnp.concatenate(
                [dws[0], _fold_heads(dws[1], n_kv), _fold_heads(dws[2], n_kv), dws[3]], axis=1)[None]
            dhs = [_matmul_cat(parts, wb_ext_t, F32, tag + "_dh")]
        else:
            dgate, *dos = _dz_fused(dxb, wout_t[i], [sv["y"], sv["gate"]], [], 4, _merge_gate_epilogue,
                                    tag + "_dz_merge", 1024, 512, rows=sv["lses_tok"])
            y_bf = sv["y"]
            dws, dhs = [], []
            for gi, (window, dil) in enumerate(DILATED_PAIRS):
                qv, kv, vv = sv["qkv"][gi]
                grads = _attn_bwd(qv, kv, vv, _perm(dos[gi], dil), _perm(y_bf, dil), sv["lses"][gi], None,
                                  window // dil, 1, dil, f"{tag}_attn{gi}")
                dws += [_matmul_tn(sv["h_perm"][gi], part, f"{tag}_dwin{gi}{'qkv'[t]}", out_dtype=BF16)
                        for t, part in enumerate(grads)]
                dhs.append(_unperm(_matmul_cat(grads, wc_t[3 * gi * e:3 * (gi + 1) * e], F32, f"{tag}_dh{gi}"),
                                   dil))
            dws.append(_matmul_tn(hs[i], dgate, tag + "_dwin_gate", out_dtype=BF16))
            dhs.append(_matmul(dgate, wc_t[9 * e:], F32, tag + "_dh_gate"))
            g_full["c_w_in"] = jnp.concatenate(dws, axis=1)[None]
        dx, dxb, d_norm[i] = _rmsnorm_bwd(dhs, xs[i], norm_g[i:i + 1], dx, tag + "_norm")
    grad_x = dx.reshape(x.shape)
    for name in ("w_out", "a_w_in", "a_w_group"):
        g_full[name] = jnp.stack(g_full[name], axis=0)

    stacks = [_to_stack(g_full[n], axis).astype(BF16).reshape((N_DEV,) + flat[n].shape) for n, axis in _BIG]
    loss_local = (0.5 / d) * jnp.sum(loss_vec)
    small = [jnp.concatenate(d_norm, axis=0), dfinal, d_sinks, jnp.concatenate(d_scale, axis=0),
             loss_local.reshape(1, 1)]
    small_rows = [_rows(a) for a in small]
    small_offs = [sum(r.shape[0] for r in small_rows[:k]) for k in range(len(small_rows) + 1)]
    small_pack = _pad_rows(jnp.concatenate(small_rows, axis=0), 8)
    core = lax.axis_index("c").astype(jnp.int32).reshape(1)
    from_sibling = _sibling_exchange(stacks, "exchange_sibling")
    chip_sums = [_pair_sum(stacks[k], from_sibling[k], core, "sum_pair_" + n) for k, (n, _) in enumerate(_BIG)]
    grecv, srecv = _chip_exchange(chip_sums, small_pack, "exchange_chips")
    ssum = _sum_slots(srecv, "sum_small")

    def small_part(k, like):
        return ssum[small_offs[k]:small_offs[k + 1]].reshape(-1)[:like.size].reshape(like.shape)

    g_norm = small_part(0, norm_g)
    g_final = small_part(1, final_g)
    g_sinks = small_part(2, b_sinks)
    g_scale_full = small_part(3, scale_full)
    loss = ssum[small_offs[4], 0]
    g_scale = lax.dynamic_slice_in_dim(g_scale_full, me * a_scale.shape[1], a_scale.shape[1], axis=1)

    small_w = [("norm_g", norm_g, m_norm_g, v_norm_g, g_norm), ("final_g", final_g, m_final_g, v_final_g, g_final),
               ("a_scale", a_scale, m_a_scale, v_a_scale, g_scale), ("b_sinks", b_sinks, m_b_sinks, v_b_sinks, g_sinks)]
    tail = lambda idx: _pad_rows(jnp.concatenate([_rows(t[idx]) for t in small_w], axis=0), 8)
    tail_sizes = [_rows(t[1]).shape[0] for t in small_w]
    tail_offs = [sum(tail_sizes[:k]) for k in range(len(tail_sizes) + 1)]
    g_tail = tail(4)
    tails = (g_tail,) + _adamw(tail(1), g_tail, tail(2), tail(3), "adamw_small")
    grads, deltas, new_m, new_v = {}, {}, {}, {}
    for k, (name, w_, _, _, _) in enumerate(small_w):
        for out, packed in zip((grads, deltas, new_m, new_v), tails):
            out[name] = packed[tail_offs[k]:tail_offs[k + 1]].reshape(-1)[:w_.size].reshape(w_.shape)
    for k, (name, _) in enumerate(_BIG):
        shape2d = flat[name].shape
        res = _adamw_slots(flat[name], grecv[k], mom_m[name].reshape(shape2d), mom_v[name].reshape(shape2d),
                           "adamw_" + name)
        for out, val in zip((grads, deltas, new_m, new_v), res):
            out[name] = val.reshape(local[name].shape)

    order = ("norm_g", "final_g", "w_out", "a_w_in", "a_w_group", "a_scale", "b_w_in", "b_sinks", "c_w_in")
    return (loss, grad_x, *[grads[n] for n in order], *[deltas[n] for n in order],
            *[new_m[n] for n in order], *[new_v[n] for n in order])
```

```python
import functools

import jax
import jax.numpy as jnp
from jax import lax
from jax.experimental import pallas as pl
from jax.experimental.pallas import tpu as pltpu

F32 = jnp.float32
BF16 = jnp.bfloat16

N_DEV = 8
HEAD_DIM = 64
LANES = 128
BLOCK = 128
Q_PER_KV = 8
POOL_WINDOWS = (2, 4, 8, 16)
POOL_HALO = 16
DILATED_PAIRS = ((128, 1), (512, 4), (2048, 16))
SWA_MAX_DIST = 127
RMS_EPS = 1e-5
PACK_W = 1024
NEG = -1e30

ADAM_LR = 0.001
ADAM_B1 = 0.9
ADAM_B2 = 0.999
ADAM_EPS = 1e-08
ADAM_WD = 0.01
ADAM_STEP = 10

VMEM_LIMIT = 48 * 1024 * 1024


def _params(*sem):
    return pltpu.CompilerParams(dimension_semantics=sem if sem else None, vmem_limit_bytes=VMEM_LIMIT)


def _pick(dim, target, mult=LANES):
    if dim <= target:
        return dim
    t = target - target % mult
    while dim % t:
        t -= mult
    return t


def _sigmoid(x):
    return 1.0 / (1.0 + jnp.exp(-x))


ANY_SPEC = pl.BlockSpec(memory_space=pl.ANY)


def _where_am_i():
    x, y, c = lax.axis_index("x"), lax.axis_index("y"), lax.axis_index("c")
    return x, y, c, 4 * x + 2 * y + c


def _peer(x, y, c, r):
    return x ^ ((r >> 2) & 1), y ^ ((r >> 1) & 1), c ^ (r & 1)


GATHER_SEMS = 8


def _gather(blocks, split, name):
    n = len(blocks)
    halves = [b.shape[0] // 2 for b in blocks]

    def body(*refs):
        send, recv = refs[:n], refs[n:2 * n]
        send_sems, recv_sems, local_sems = refs[2 * n:]
        x, y, c, me = _where_am_i()
        sib, xn, yn, dg = (_peer(x, y, c, r) for r in (1, 4, 2, 6))
        sib_id, xn_id, yn_id, dg_id = me ^ 1, me ^ 4, me ^ 2, me ^ 6

        def copy(k, sem, src, dst, to):
            return pltpu.make_async_remote_copy(
                src_ref=src, dst_ref=dst, send_sem=send_sems.at[k, sem], recv_sem=recv_sems.at[k, sem],
                device_id=to, device_id_type=pl.DeviceIdType.MESH)

        def part(k, slot, half):
            return recv[k].at[slot, pl.ds(half * halves[k], halves[k])]

        started = []
        for k in range(n):
            own = pltpu.make_async_copy(send[k], recv[k].at[me], local_sems.at[k])
            own.start()
            started.append(own)
        sends = []
        for k in range(n):
            sends += [copy(k, 0, send[k], recv[k].at[me], sib), copy(k, 1, send[k], recv[k].at[me], xn),
                      copy(k, 2, send[k], recv[k].at[me], yn)]
            if not split[k]:
                sends.append(copy(k, 3, send[k], recv[k].at[me], dg))
        for cp in sends:
            cp.start()

        def after(k, sem, slot, hand_on_sem, half, half_sem, half_to):
            copy(k, sem, send[k], recv[k].at[slot], sib).wait_recv()
            new = [copy(k, hand_on_sem, recv[k].at[slot], recv[k].at[slot], sib)]
            if split[k]:
                new.append(copy(k, half_sem, part(k, slot, half), part(k, slot, half), half_to))
            for cp in new:
                cp.start()
            sends.extend(new)

        for k in range(n):
            after(k, 2, yn_id, 6, 0, 3, xn)
        for k in range(n):
            after(k, 1, xn_id, 5, 1, 4, yn)
        for k in range(n):
            if split[k]:
                copy(k, 3, part(k, dg_id, 0), part(k, dg_id, 0), sib).wait_recv()
                copy(k, 4, part(k, dg_id, 1), part(k, dg_id, 1), sib).wait_recv()
            else:
                copy(k, 3, send[k], recv[k].at[dg_id], sib).wait_recv()
            fwd = copy(k, 7, recv[k].at[dg_id], recv[k].at[dg_id], sib)
            fwd.start()
            sends.append(fwd)
        for k in range(n):
            copy(k, 0, send[k], recv[k].at[sib_id], sib).wait_recv()
            for sem, r in ((5, 4), (6, 2), (7, 6)):
                copy(k, sem, send[k], recv[k].at[sib_id ^ r], sib).wait_recv()
        for cp in sends:
            cp.wait_send()
        for own in started:
            own.wait()

    return pl.pallas_call(
        body, name=name,
        out_shape=tuple(jax.ShapeDtypeStruct((N_DEV,) + b.shape, b.dtype) for b in blocks),
        in_specs=[ANY_SPEC] * n,
        out_specs=tuple([ANY_SPEC] * n),
        scratch_shapes=[pltpu.SemaphoreType.DMA((n, GATHER_SEMS)), pltpu.SemaphoreType.DMA((n, GATHER_SEMS)),
                        pltpu.SemaphoreType.DMA((n,))],
    )(*blocks)


def _sibling_exchange(stacks, name):
    n_chips = N_DEV // 2
    n = len(stacks)

    def body(*refs):
        g_refs, t_refs = refs[:n], refs[n:2 * n]
        send_sems, recv_sems = refs[2 * n:]
        x, y, c, _ = _where_am_i()
        sib = _peer(x, y, c, 1)
        copies = [pltpu.make_async_remote_copy(
            src_ref=g_refs[k].at[2 * chip + (1 - c)], dst_ref=t_refs[k].at[chip], send_sem=send_sems.at[k, chip],
            recv_sem=recv_sems.at[k, chip], device_id=sib, device_id_type=pl.DeviceIdType.MESH)
            for k in range(n) for chip in range(n_chips)]
        for cp in copies:
            cp.start()
        for cp in copies:
            cp.wait_recv()
        for cp in copies:
            cp.wait_send()

    return pl.pallas_call(
        body, name=name,
        out_shape=tuple(jax.ShapeDtypeStruct((n_chips,) + g.shape[1:], g.dtype) for g in stacks),
        in_specs=[ANY_SPEC] * n, out_specs=tuple([ANY_SPEC] * n),
        scratch_shapes=[pltpu.SemaphoreType.DMA((n, n_chips)), pltpu.SemaphoreType.DMA((n, n_chips))],
    )(*stacks)


CHIP_SEMS = 6


def _chip_exchange(csums, small, name):
    n = len(csums)
    halves = [cs.shape[1] // 2 for cs in csums]

    def body(*refs):
        c_refs, s_ref = refs[:n], refs[n]
        r_refs, sr_ref = refs[n + 1:2 * n + 1], refs[2 * n + 1]
        stage_x, stage_y = refs[2 * n + 2:3 * n + 2], refs[3 * n + 2:4 * n + 2]
        send_sems, recv_sems, small_send, small_recv, local_sems = refs[4 * n + 2:]
        x, y, c, me = _where_am_i()
        my_chip = 2 * x + y
        xn, yn = _peer(x, y, c, 4), _peer(x, y, c, 2)
        xn_chip, yn_chip, dg_chip = my_chip ^ 2, my_chip ^ 1, my_chip ^ 3
        own = [pltpu.make_async_copy(c_refs[k].at[my_chip], r_refs[k].at[my_chip], local_sems.at[k])
               for k in range(n)]
        own.append(pltpu.make_async_copy(s_ref, sr_ref.at[me], local_sems.at[n]))
        for cp in own:
            cp.start()

        def copy(k, sem, src, dst, to):
            return pltpu.make_async_remote_copy(
                src_ref=src, dst_ref=dst, send_sem=send_sems.at[k, sem], recv_sem=recv_sems.at[k, sem],
                device_id=to, device_id_type=pl.DeviceIdType.MESH)

        def half(ref, k, slot, which):
            return ref.at[slot, pl.ds(which * halves[k], halves[k])]

        sends, recvs = [], []
        for k in range(n):
            sends += [copy(k, 0, c_refs[k].at[xn_chip], r_refs[k].at[my_chip], xn),
                      copy(k, 1, c_refs[k].at[yn_chip], r_refs[k].at[my_chip], yn),
                      copy(k, 2, half(c_refs[k], k, dg_chip, 0), stage_x[k], xn),
                      copy(k, 3, half(c_refs[k], k, dg_chip, 1), stage_y[k], yn)]
            recvs += [copy(k, 0, c_refs[k].at[xn_chip], r_refs[k].at[xn_chip], xn),
                      copy(k, 1, c_refs[k].at[yn_chip], r_refs[k].at[yn_chip], yn),
                      copy(k, 4, stage_x[k], half(r_refs[k], k, dg_chip, 0), yn),
                      copy(k, 5, stage_y[k], half(r_refs[k], k, dg_chip, 1), xn)]
        for r in range(1, N_DEV):
            to = _peer(x, y, c, r)
            sends.append(pltpu.make_async_remote_copy(
                src_ref=s_ref, dst_ref=sr_ref.at[me], send_sem=small_send.at[r - 1],
                recv_sem=small_recv.at[r - 1], device_id=to, device_id_type=pl.DeviceIdType.MESH))
            recvs.append(pltpu.make_async_remote_copy(
                src_ref=s_ref, dst_ref=sr_ref.at[me ^ r], send_sem=small_send.at[r - 1],
                recv_sem=small_recv.at[r - 1], device_id=to, device_id_type=pl.DeviceIdType.MESH))
        for cp in sends:
            cp.start()
        for k in range(n):
            copy(k, 2, stage_x[k], stage_x[k], xn).wait_recv()
            fwd = copy(k, 4, stage_x[k], half(r_refs[k], k, xn_chip, 0), yn)
            fwd.start()
            sends.append(fwd)
        for k in range(n):
            copy(k, 3, stage_y[k], stage_y[k], yn).wait_recv()
            fwd = copy(k, 5, stage_y[k], half(r_refs[k], k, yn_chip, 1), xn)
            fwd.start()
            sends.append(fwd)
        for cp in recvs:
            cp.wait_recv()
        for cp in sends:
            cp.wait_send()
        for cp in own:
            cp.wait()

    stages = tuple(jax.ShapeDtypeStruct((h, cs.shape[2]), cs.dtype) for h, cs in zip(halves, csums))
    outs = pl.pallas_call(
        body, name=name,
        out_shape=tuple(jax.ShapeDtypeStruct(cs.shape, cs.dtype) for cs in csums)
        + (jax.ShapeDtypeStruct((N_DEV,) + small.shape, small.dtype),) + stages + stages,
        in_specs=[ANY_SPEC] * (n + 1), out_specs=tuple([ANY_SPEC] * (3 * n + 1)),
        scratch_shapes=[pltpu.SemaphoreType.DMA((n, CHIP_SEMS)), pltpu.SemaphoreType.DMA((n, CHIP_SEMS)),
                        pltpu.SemaphoreType.DMA((N_DEV - 1,)), pltpu.SemaphoreType.DMA((N_DEV - 1,)),
                        pltpu.SemaphoreType.DMA((n + 1,))],
    )(*csums, small)
    return list(outs[:n]), outs[n]


def _pair_sum(gpack, other, core, name, ts=256):
    n_chips, r, c = other.shape
    ts = _pick(r, ts, 16)

    def body(core_ref, g_ref, o_ref, out_ref):
        del core_ref
        out_ref[...] = (g_ref[...].astype(F32) + o_ref[...].astype(F32)).astype(out_ref.dtype)

    return pl.pallas_call(
        body, name=name,
        out_shape=jax.ShapeDtypeStruct(other.shape, other.dtype),
        grid_spec=pltpu.PrefetchScalarGridSpec(
            num_scalar_prefetch=1, grid=(n_chips, r // ts),
            in_specs=[pl.BlockSpec((None, ts, c), lambda j, i, core_ref: (2 * j + core_ref[0], i, 0)),
                      pl.BlockSpec((None, ts, c), lambda j, i, core_ref: (j, i, 0))],
            out_specs=pl.BlockSpec((None, ts, c), lambda j, i, core_ref: (j, i, 0))),
        compiler_params=_params("parallel", "parallel"),
    )(core, gpack, other)


def _matmul(a, b, out_dtype, name, tm=2048, tn=1024, tk=1024):
    m, kdim = a.shape
    n = b.shape[1]
    tm, tn, tk = _pick(m, tm), _pick(n, tn), _pick(kdim, tk)
    nk = kdim // tk

    if nk == 1:
        def body(a_ref, b_ref, o_ref):
            o_ref[...] = jnp.dot(a_ref[...], b_ref[...], preferred_element_type=F32).astype(o_ref.dtype)
        scratch = []
    else:
        def body(a_ref, b_ref, o_ref, acc_ref):
            kk = pl.program_id(2)

            @pl.when(kk == 0)
            def _():
                acc_ref[...] = jnp.zeros_like(acc_ref)

            acc_ref[...] += jnp.dot(a_ref[...], b_ref[...], preferred_element_type=F32)

            @pl.when(kk == nk - 1)
            def _():
                o_ref[...] = acc_ref[...].astype(o_ref.dtype)
        scratch = [pltpu.VMEM((tm, tn), F32)]

    return pl.pallas_call(
        body, name=name,
        out_shape=jax.ShapeDtypeStruct((m, n), out_dtype),
        grid=(m // tm, n // tn, nk),
        in_specs=[pl.BlockSpec((tm, tk), lambda i, j, k: (i, k)),
                  pl.BlockSpec((tk, tn), lambda i, j, k: (k, j))],
        out_specs=pl.BlockSpec((tm, tn), lambda i, j, k: (i, j)),
        scratch_shapes=scratch,
        compiler_params=_params("parallel", "parallel", "arbitrary"),
    )(a, b)


def _matmul_cat(parts, b, out_dtype, name, tm=1024, tn=1024, tk=1024):
    m = parts[0].shape[0]
    n = b.shape[1]
    tm, tn = _pick(m, tm), _pick(n, tn)
    tk = min(_pick(p.shape[1], tk) for p in parts)
    steps = [p.shape[1] // tk for p in parts]
    assert all(p.shape[1] % tk == 0 for p in parts)
    starts = [sum(steps[:t]) for t in range(len(parts))]
    nk = sum(steps)
    n_parts = len(parts)

    def body(*refs):
        a_refs, b_ref, o_ref, acc_ref = refs[:n_parts], refs[n_parts], refs[n_parts + 1], refs[n_parts + 2]
        kk = pl.program_id(2)

        @pl.when(kk == 0)
        def _():
            acc_ref[...] = jnp.zeros_like(acc_ref)

        for t in range(n_parts):
            @pl.when(jnp.logical_and(kk >= starts[t], kk < starts[t] + steps[t]))
            def _(t=t):
                acc_ref[...] += jnp.dot(a_refs[t][...], b_ref[...], preferred_element_type=F32)

        @pl.when(kk == nk - 1)
        def _():
            o_ref[...] = acc_ref[...].astype(o_ref.dtype)

    def part_map(t):
        return lambda i, j, k: (i, jnp.clip(k - starts[t], 0, steps[t] - 1))

    return pl.pallas_call(
        body, name=name,
        out_shape=jax.ShapeDtypeStruct((m, n), out_dtype),
        grid=(m // tm, n // tn, nk),
        in_specs=[pl.BlockSpec((tm, tk), part_map(t)) for t in range(n_parts)]
        + [pl.BlockSpec((tk, tn), lambda i, j, k: (k, j))],
        out_specs=pl.BlockSpec((tm, tn), lambda i, j, k: (i, j)),
        scratch_shapes=[pltpu.VMEM((tm, tn), F32)],
        compiler_params=_params("parallel", "parallel", "arbitrary"),
    )(*parts, b)


def _matmul_tn(a, b, name, tm=1024, tn=1024, tk=2048, out_dtype=F32):
    kdim, m = a.shape
    n = b.shape[1]
    tm, tn, tk = _pick(m, tm), _pick(n, tn), _pick(kdim, tk)
    nk = kdim // tk

    def body(a_ref, b_ref, o_ref, acc_ref):
        kk = pl.program_id(2)

        @pl.when(kk == 0)
        def _():
            acc_ref[...] = jnp.zeros_like(acc_ref)

        acc_ref[...] += lax.dot_general(a_ref[...], b_ref[...], (((0,), (0,)), ((), ())),
                                        preferred_element_type=F32)

        @pl.when(kk == nk - 1)
        def _():
            o_ref[...] = acc_ref[...].astype(o_ref.dtype)

    return pl.pallas_call(
        body, name=name,
        out_shape=jax.ShapeDtypeStruct((m, n), out_dtype),
        grid=(m // tm, n // tn, nk),
        in_specs=[pl.BlockSpec((tk, tm), lambda i, j, k: (k, i)),
                  pl.BlockSpec((tk, tn), lambda i, j, k: (k, j))],
        out_specs=pl.BlockSpec((tm, tn), lambda i, j, k: (i, j)),
        scratch_shapes=[pltpu.VMEM((tm, tn), F32)],
        compiler_params=_params("parallel", "parallel", "arbitrary"),
    )(a, b)


def _grouped_weight_grad(a, b, ng, name, tk=1024):
    s, e = a.shape
    g = e // ng
    tk = _pick(s, tk)
    nk = s // tk

    def body(a_ref, b_ref, o_ref):
        kk = pl.program_id(1)

        @pl.when(kk == 0)
        def _():
            o_ref[...] = jnp.zeros_like(o_ref)

        o_ref[...] += lax.dot_general(a_ref[...], b_ref[...], (((0,), (0,)), ((), ())),
                                      preferred_element_type=F32)

    return pl.pallas_call(
        body, name=name,
        out_shape=jax.ShapeDtypeStruct((ng, g, g), F32),
        grid=(ng, nk),
        in_specs=[pl.BlockSpec((tk, g), lambda j, k: (k, j)),
                  pl.BlockSpec((tk, g), lambda j, k: (k, j))],
        out_specs=pl.BlockSpec((None, g, g), lambda j, k: (j, 0, 0)),
        compiler_params=_params("parallel", "arbitrary"),
    )(a, b)


def _rms(x):
    r = lax.rsqrt(jnp.mean(x * x, axis=1, keepdims=True) + RMS_EPS)
    return x * r, r


def _rmsnorm_fwd(x, g, name, ts=256):
    s, d = x.shape
    ts = _pick(s, ts, 8)

    def body(x_ref, g_ref, h_ref):
        xhat, _ = _rms(x_ref[...])
        h_ref[...] = (xhat * g_ref[...]).astype(BF16)

    return pl.pallas_call(
        body, name=name,
        out_shape=jax.ShapeDtypeStruct((s, d), BF16),
        grid=(s // ts,),
        in_specs=[pl.BlockSpec((ts, d), lambda i: (i, 0)), pl.BlockSpec((1, d), lambda i: (0, 0))],
        out_specs=pl.BlockSpec((ts, d), lambda i: (i, 0)),
        compiler_params=_params("parallel"),
    )(x, g)


def _outproj_norm(z, w, x, g, name, tm=512):
    s, e = z.shape
    d = w.shape[1]
    tm = _pick(s, tm)

    def body(z_ref, w_ref, x_ref, g_ref, xo_ref, h_ref):
        xn = x_ref[...] + jnp.dot(z_ref[...], w_ref[...], preferred_element_type=F32)
        xo_ref[...] = xn
        xhat, _ = _rms(xn)
        h_ref[...] = (xhat * g_ref[...]).astype(BF16)

    return pl.pallas_call(
        body, name=name,
        out_shape=(jax.ShapeDtypeStruct((s, d), F32), jax.ShapeDtypeStruct((s, d), BF16)),
        grid=(s // tm,),
        in_specs=[pl.BlockSpec((tm, e), lambda i: (i, 0)), pl.BlockSpec((e, d), lambda i: (0, 0)),
                  pl.BlockSpec((tm, d), lambda i: (i, 0)), pl.BlockSpec((1, d), lambda i: (0, 0))],
        out_specs=(pl.BlockSpec((tm, d), lambda i: (i, 0)), pl.BlockSpec((tm, d), lambda i: (i, 0))),
        compiler_params=_params("parallel"),
    )(z, w, x, g)


def _outproj_loss(z, w, x, g, target, name, tm=512):
    s, e = z.shape
    d = w.shape[1]
    tm = _pick(s, tm)

    def body(z_ref, w_ref, x_ref, g_ref, t_ref, dx_ref, dxb_ref, dg_ref, loss_ref):
        i = pl.program_id(0)
        xn = x_ref[...] + jnp.dot(z_ref[...], w_ref[...], preferred_element_type=F32)
        xhat, r = _rms(xn)
        gain = g_ref[...]
        diff = xhat * gain - t_ref[...]
        dout = diff * (1.0 / d)
        dxhat = dout * gain
        dx = r * (dxhat - xhat * jnp.mean(dxhat * xhat, axis=1, keepdims=True))
        dx_ref[...] = dx
        dxb_ref[...] = dx.astype(BF16)

        @pl.when(i == 0)
        def _():
            dg_ref[...] = jnp.zeros_like(dg_ref)
            loss_ref[...] = jnp.zeros_like(loss_ref)

        dg_ref[...] += jnp.sum(dout * xhat, axis=0, keepdims=True)
        loss_ref[...] += jnp.sum(diff * diff, axis=0, keepdims=True)

    row = lambda i: (i, 0)
    fixed = lambda i: (0, 0)
    return pl.pallas_call(
        body, name=name,
        out_shape=(jax.ShapeDtypeStruct((s, d), F32), jax.ShapeDtypeStruct((s, d), BF16),
                   jax.ShapeDtypeStruct((1, d), F32), jax.ShapeDtypeStruct((1, d), F32)),
        grid=(s // tm,),
        in_specs=[pl.BlockSpec((tm, e), row), pl.BlockSpec((e, d), fixed), pl.BlockSpec((tm, d), row),
                  pl.BlockSpec((1, d), fixed), pl.BlockSpec((tm, d), row)],
        out_specs=(pl.BlockSpec((tm, d), row), pl.BlockSpec((tm, d), row),
                   pl.BlockSpec((1, d), fixed), pl.BlockSpec((1, d), fixed)),
        compiler_params=_params("arbitrary"),
    )(z, w, x, g, target)


def _rmsnorm_bwd(dhs, x, g, dx_next, name, ts=256):
    s, d = x.shape
    ts = _pick(s, ts, 8)
    n_dh = len(dhs)

    def body(*refs):
        dh_refs = refs[:n_dh]
        x_ref, g_ref, dn_ref, dx_ref, dxb_ref, dg_ref = refs[n_dh:]
        i = pl.program_id(0)
        xhat, r = _rms(x_ref[...])
        dh_ = dh_refs[0][...]
        for extra in dh_refs[1:]:
            dh_ = dh_ + extra[...]
        dxhat = dh_ * g_ref[...]
        dx = dn_ref[...] + r * (dxhat - xhat * jnp.mean(dxhat * xhat, axis=1, keepdims=True))
        dx_ref[...] = dx
        dxb_ref[...] = dx.astype(BF16)

        @pl.when(i == 0)
        def _():
            dg_ref[...] = jnp.zeros_like(dg_ref)

        dg_ref[...] += jnp.sum(dh_ * xhat, axis=0, keepdims=True)

    row = lambda i: (i, 0)
    fixed = lambda i: (0, 0)
    return pl.pallas_call(
        body, name=name,
        out_shape=(jax.ShapeDtypeStruct((s, d), F32), jax.ShapeDtypeStruct((s, d), BF16),
                   jax.ShapeDtypeStruct((1, d), F32)),
        grid=(s // ts,),
        in_specs=[pl.BlockSpec((ts, d), row)] * n_dh + [pl.BlockSpec((ts, d), row), pl.BlockSpec((1, d), fixed),
                                                        pl.BlockSpec((ts, d), row)],
        out_specs=(pl.BlockSpec((ts, d), row), pl.BlockSpec((ts, d), row), pl.BlockSpec((1, d), fixed)),
        compiler_params=_params("arbitrary"),
    )(*dhs, x, g, dx_next)


def _pool_counts(t0, rows, cols, window):
    t = t0 + lax.broadcasted_iota(jnp.int32, (rows, cols), 0)
    return jnp.minimum(t + 1, window).astype(F32)


def _proj_pool_fwd(h, w, name, ts=1024, tc=512):
    s, dm = h.shape
    e = w.shape[1]
    ng = len(POOL_WINDOWS)
    gdim = e // ng
    ts, tc = _pick(s, ts), _pick(gdim, tc)
    cpg = gdim // tc
    hb = ts // POOL_HALO

    def body(h_ref, halo_ref, w_ref, d_ref):
        i, grp = pl.program_id(0), pl.program_id(1)
        cur = jnp.dot(h_ref[...], w_ref[...], preferred_element_type=F32)
        halo = jnp.dot(halo_ref[...], w_ref[...], preferred_element_type=F32)
        ext = jnp.concatenate([jnp.where(i > 0, halo, 0.0), cur], axis=0)
        for gi, window in enumerate(POOL_WINDOWS):
            @pl.when(grp == gi)
            def _(window=window):
                acc = ext
                k = 1
                while k < window:
                    acc = acc + pltpu.roll(acc, k, 0)
                    k *= 2
                pooled = acc[POOL_HALO:, :] / _pool_counts(i * ts, ts, tc, window)
                d_ref[...] = (pooled - cur).astype(BF16)

    return pl.pallas_call(
        body, name=name,
        out_shape=jax.ShapeDtypeStruct((s, e), BF16),
        grid=(s // ts, ng, cpg),
        in_specs=[pl.BlockSpec((ts, dm), lambda i, g, j: (i, 0)),
                  pl.BlockSpec((POOL_HALO, dm), lambda i, g, j: (jnp.maximum(i * hb - 1, 0), 0)),
                  pl.BlockSpec((dm, tc), lambda i, g, j: (0, g * cpg + j))],
        out_specs=pl.BlockSpec((ts, tc), lambda i, g, j: (i, g * cpg + j)),
        compiler_params=_params("parallel", "parallel", "parallel"),
    )(h, h, w)


def _dz_fused(dxb, w_t, tiles, vecs, n_out, epilogue, name, tm, tn, with_col_sum=False, rows=()):
    s, dm = dxb.shape
    e = w_t.shape[1]
    tm, tn = _pick(s, tm), _pick(e, tn)
    n_t, n_v, n_r = len(tiles), len(vecs), len(rows)

    def body(*refs):
        a_ref, b_ref = refs[:2]
        tile_refs, vec_refs = refs[2:2 + n_t], refs[2 + n_t:2 + n_t + n_v]
        row_refs = refs[2 + n_t + n_v:2 + n_t + n_v + n_r]
        out_refs = refs[2 + n_t + n_v + n_r:]
        i = pl.program_id(1)
        dz = jnp.dot(a_ref[...], b_ref[...], preferred_element_type=F32)
        extra = ([r[...] for r in row_refs], pl.program_id(0) * tn) if n_r else ()
        res = epilogue(dz, [t[...] for t in tile_refs], [v[...] for v in vec_refs], *extra)
        for o_ref, val in zip(out_refs[:n_out], res[:n_out]):
            o_ref[...] = val.astype(o_ref.dtype)
        if with_col_sum:
            sum_ref = out_refs[n_out]

            @pl.when(i == 0)
            def _():
                sum_ref[...] = jnp.zeros_like(sum_ref)

            sum_ref[...] += jnp.sum(res[n_out], axis=0, keepdims=True)

    blk = lambda j, i: (i, j)
    vec = lambda j, i: (0, j)
    out_shape = [jax.ShapeDtypeStruct((s, e), BF16)] * n_out
    out_specs = [pl.BlockSpec((tm, tn), blk)] * n_out
    if with_col_sum:
        out_shape.append(jax.ShapeDtypeStruct((1, e), F32))
        out_specs.append(pl.BlockSpec((1, tn), vec))
    return pl.pallas_call(
        body, name=name,
        out_shape=tuple(out_shape),
        grid=(e // tn, s // tm),
        in_specs=[pl.BlockSpec((tm, dm), lambda j, i: (i, 0)), pl.BlockSpec((dm, tn), lambda j, i: (0, j))]
        + [pl.BlockSpec((tm, tn), blk)] * n_t + [pl.BlockSpec((1, tn), vec)] * n_v
        + [pl.BlockSpec((tm, r.shape[1]), lambda j, i: (i, 0)) for r in rows],
        out_specs=tuple(out_specs),
        compiler_params=_params("parallel", "arbitrary"),
    )(dxb, w_t, *tiles, *vecs, *rows)


def _group_pool_bwd(dyr, w_t, name, ts=1024, tc=512):
    s, e = dyr.shape
    ng = len(POOL_WINDOWS)
    gdim = e // ng
    ts, tc = _pick(s, ts), _pick(gdim, tc)
    cpg = gdim // tc
    hb = ts // POOL_HALO
    n_halo = s // POOL_HALO
    nst = s // ts

    def body(dy_ref, halo_ref, w_ref, du_ref):
        i, grp = pl.program_id(0), pl.program_id(1)
        cur = jnp.dot(dy_ref[...], w_ref[...], preferred_element_type=F32)
        halo = jnp.dot(halo_ref[...], w_ref[...], preferred_element_type=F32)
        ext = jnp.concatenate([cur, jnp.where(i < nst - 1, halo, 0.0)], axis=0)
        rows = ts + POOL_HALO
        for gi, window in enumerate(POOL_WINDOWS):
            @pl.when(grp == gi)
            def _(window=window):
                acc = ext / _pool_counts(i * ts, rows, tc, window)
                k = 1
                while k < window:
                    acc = acc + pltpu.roll(acc, rows - k, 0)
                    k *= 2
                du_ref[...] = (acc[:ts, :] - cur).astype(BF16)

    return pl.pallas_call(
        body, name=name,
        out_shape=jax.ShapeDtypeStruct((s, e), BF16),
        grid=(nst, ng, cpg),
        in_specs=[pl.BlockSpec((ts, gdim), lambda i, g, j: (i, g)),
                  pl.BlockSpec((POOL_HALO, gdim), lambda i, g, j: (jnp.minimum((i + 1) * hb, n_halo - 1), g)),
                  pl.BlockSpec((None, gdim, tc), lambda i, g, j: (g, 0, j))],
        out_specs=pl.BlockSpec((ts, tc), lambda i, g, j: (i, g * cpg + j)),
        compiler_params=_params("parallel", "parallel", "parallel"),
    )(dyr, dyr, w_t)


def _a_group_fwd(d, w, scale, gate, name, tm=1024):
    s, e = d.shape
    ng, g, _ = w.shape
    tm = _pick(s, tm)

    def body(d_ref, w_ref, s_ref, gate_ref, yr_ref, z_ref):
        yr = jnp.dot(d_ref[...], w_ref[...], preferred_element_type=F32)
        yr_ref[...] = yr.astype(yr_ref.dtype)
        gt = gate_ref[...].astype(F32)
        z_ref[...] = ((yr * s_ref[...]) * (gt * _sigmoid(gt))).astype(BF16)

    blk = lambda i, j: (i, j)
    return pl.pallas_call(
        body, name=name,
        out_shape=(jax.ShapeDtypeStruct((s, e), BF16), jax.ShapeDtypeStruct((s, e), BF16)),
        grid=(s // tm, ng),
        in_specs=[pl.BlockSpec((tm, g), blk), pl.BlockSpec((None, g, g), lambda i, j: (j, 0, 0)),
                  pl.BlockSpec((1, g), lambda i, j: (0, j)), pl.BlockSpec((tm, g), blk)],
        out_specs=(pl.BlockSpec((tm, g), blk), pl.BlockSpec((tm, g), blk)),
        compiler_params=_params("parallel", "parallel"),
    )(d, w, scale, gate)


def _silu_and_slope(gt):
    sg = _sigmoid(gt)
    return gt * sg, sg * (1.0 + gt * (1.0 - sg))


def _a_gate_epilogue(dz, tiles, vecs):
    yr, gt = tiles[0].astype(F32), tiles[1].astype(F32)
    sc = vecs[0]
    silu, slope = _silu_and_slope(gt)
    dy = dz * silu
    return dz * (yr * sc) * slope, dy * sc, dy * yr


def _gate_epilogue(dz, tiles, vecs):
    y, gt = tiles[0].astype(F32), tiles[1].astype(F32)
    silu, slope = _silu_and_slope(gt)
    return dz * y * slope, dz * silu


def _merge_gate_epilogue(dz, tiles, vecs, lses, col0):
    y, gt = tiles[0].astype(F32), tiles[1].astype(F32)
    silu, slope = _silu_and_slope(gt)
    dy = dz * silu
    w0, w1, w2 = _merge_weights_expanded(lses, col0, dz.shape[1])
    return dz * y * slope, w0 * dy, w1 * dy, w2 * dy


def _gate_fwd(y, gate, name, ts=512, tc=512):
    s, e = y.shape
    ts, tc = _pick(s, ts), _pick(e, tc)

    def body(y_ref, gate_ref, z_ref):
        gt = gate_ref[...].astype(F32)
        z_ref[...] = (y_ref[...].astype(F32) * (gt * _sigmoid(gt))).astype(BF16)

    blk = lambda i, j: (i, j)
    return pl.pallas_call(
        body, name=name,
        out_shape=jax.ShapeDtypeStruct((s, e), BF16),
        grid=(s // ts, e // tc),
        in_specs=[pl.BlockSpec((ts, tc), blk)] * 2,
        out_specs=pl.BlockSpec((ts, tc), blk),
        compiler_params=_params("parallel", "parallel"),
    )(y, gate)


def _merge_weights(l0, l1, l2):
    m = jnp.maximum(jnp.maximum(l0, l1), l2)
    e0, e1, e2 = jnp.exp(l0 - m), jnp.exp(l1 - m), jnp.exp(l2 - m)
    inv = 1.0 / (e0 + e1 + e2)
    return e0 * inv, e1 * inv, e2 * inv


def _expand_heads(w, col0, width):
    n_heads = w.shape[1]
    head_of_lane = (col0 + lax.broadcasted_iota(jnp.int32, (n_heads, width), 1)) // HEAD_DIM
    pick = jnp.where(head_of_lane == lax.broadcasted_iota(jnp.int32, (n_heads, width), 0), 1.0, 0.0).astype(BF16)
    high = w.astype(BF16)
    rest = (w - high.astype(F32)).astype(BF16)
    return (jnp.dot(high, pick, preferred_element_type=F32) + jnp.dot(rest, pick, preferred_element_type=F32))


def _merge_weights_expanded(lses, col0, width):
    return [_expand_heads(w, col0, width) for w in _merge_weights(*lses)]


def _merge_gate_fwd(outs, lses, gate, name, ts=512, tc=512):
    s, e = gate.shape
    n_heads = lses[0].shape[1]
    ts, tc = _pick(s, ts), _pick(e, tc)

    def body(o0, o1, o2, l0, l1, l2, gate_ref, y_ref, z_ref):
        w0, w1, w2 = _merge_weights_expanded([l0[...], l1[...], l2[...]], pl.program_id(1) * tc, tc)
        y = w0 * o0[...].astype(F32) + w1 * o1[...].astype(F32) + w2 * o2[...].astype(F32)
        y_ref[...] = y.astype(y_ref.dtype)
        gt = gate_ref[...].astype(F32)
        z_ref[...] = (y * (gt * _sigmoid(gt))).astype(BF16)

    blk = lambda i, j: (i, j)
    per_head = pl.BlockSpec((ts, n_heads), lambda i, j: (i, 0))
    return pl.pallas_call(
        body, name=name,
        out_shape=(jax.ShapeDtypeStruct((s, e), BF16), jax.ShapeDtypeStruct((s, e), BF16)),
        grid=(s // ts, e // tc),
        in_specs=[pl.BlockSpec((ts, tc), blk)] * 3 + [per_head] * 3 + [pl.BlockSpec((ts, tc), blk)],
        out_specs=(pl.BlockSpec((ts, tc), blk), pl.BlockSpec((ts, tc), blk)),
        compiler_params=_params("parallel", "parallel"),
    )(*outs, *lses, gate)


def _band(max_dist, width):
    row = lax.broadcasted_iota(jnp.int32, (2 * BLOCK, width), 0) & (BLOCK - 1)
    col = lax.broadcasted_iota(jnp.int32, (2 * BLOCK, width), 1)
    low = row if max_dist == BLOCK else row + 1
    return jnp.logical_and(col >= low, col <= row + BLOCK), col >= BLOCK


def _fill_bias(bias_ref, max_dist):
    @pl.when(jnp.logical_and(pl.program_id(0) == 0, jnp.logical_and(pl.program_id(1) == 0, pl.program_id(2) == 0)))
    def _():
        band, own = _band(max_dist, 2 * BLOCK)
        bias_ref[0] = jnp.where(band, 0.0, NEG)
        bias_ref[1] = jnp.where(jnp.logical_and(band, own), 0.0, NEG)


def _aligned(v):
    return v if isinstance(v, int) else pl.multiple_of(v, BLOCK)


def _stack_heads(x, lo):
    return jnp.concatenate([jnp.where(lo, x, 0.0), jnp.where(lo, 0.0, x)], axis=0).astype(BF16)


def _unstack_heads(x2, lo):
    return jnp.where(lo, x2[:BLOCK], x2[BLOCK:])


def _head_col(x, hm):
    return jnp.max(jnp.where(hm, x, NEG), axis=1, keepdims=True)


def _dot_nt(a, b):
    return lax.dot_general(a, b, (((1,), (1,)), ((), ())), preferred_element_type=F32)


def _dot_tn(a, b):
    return lax.dot_general(a, b, (((0,), (0,)), ((), ())), preferred_element_type=F32)


def _stream_view(a, dil):
    s, w = a.shape
    return a.reshape(s // (BLOCK * dil), dil, BLOCK, w)


def _fill_window(dst, halo_ref, cur_ref, n):
    dst[0:BLOCK, :] = halo_ref[0]
    for jc in range(n):
        dst[(jc + 1) * BLOCK:(jc + 2) * BLOCK, :] = cur_ref[jc]


def _attn_fwd(q, k, v, sinks, max_dist, rep, dil, out_dtype, name, tq=2048, per_head_lse=False):
    assert max_dist in (BLOCK - 1, BLOCK)
    s, w = q.shape
    l = s // dil
    n_pairs = w // LANES
    n_heads = 2 * n_pairs
    tq = _pick(l, tq)
    n = tq // BLOCK
    has_sink = sinks is not None
    scale = HEAD_DIM ** -0.5

    def body(*refs):
        if has_sink:
            sink_ref, refs = refs[0], refs[1:]
        q_ref, kc_ref, kh_ref, vc_ref, vh_ref, o_ref, lse_ref = refs[:7]
        refs = refs[7:]
        if per_head_lse:
            lseh_ref, refs = refs[0], refs[1:]
        kx, vx, bias_ref = refs
        i, p = pl.program_id(0), pl.program_id(2)
        _fill_window(kx, kh_ref, kc_ref, n)
        _fill_window(vx, vh_ref, vc_ref, n)
        if per_head_lse:
            @pl.when(p == 0)
            def _():
                lseh_ref[...] = jnp.zeros_like(lseh_ref)
            head_lane = lax.broadcasted_iota(jnp.int32, (BLOCK, n_heads), 1)
        lo = lax.broadcasted_iota(jnp.int32, (BLOCK, LANES), 1) < HEAD_DIM
        _fill_bias(bias_ref, max_dist)
        top = lax.broadcasted_iota(jnp.int32, (2 * BLOCK, 1), 0) < BLOCK

        def scores(j):
            r0 = _aligned(j * BLOCK)
            q2 = _stack_heads(q_ref[j].astype(F32) * scale, lo)
            first = jnp.logical_and(i == 0, j == 0).astype(jnp.int32)
            return _dot_nt(q2, kx[pl.ds(r0, 2 * BLOCK), :]) + bias_ref[first]

        per_step = 2 if n % 2 == 0 else 1

        def step(jj, carry):
            nxt = tuple(scores(jnp.minimum((jj + 1) * per_step + t, n - 1)) for t in range(per_step))
            for t in range(per_step):
                finish(jj * per_step + t, carry[t])
            return nxt

        def finish(j, s2):
            r0 = _aligned(j * BLOCK)
            vw = vx[pl.ds(r0, 2 * BLOCK), :]
            m = jnp.max(s2, axis=1, keepdims=True)
            if has_sink:
                sk = jnp.where(top, sink_ref[2 * p], sink_ref[2 * p + 1])
                m = jnp.maximum(m, sk)
            pr = jnp.exp(s2 - m)
            den = jnp.sum(pr, axis=1, keepdims=True)
            if has_sink:
                den = den + jnp.exp(sk - m)
            o2 = jnp.dot(pr.astype(BF16), vw, preferred_element_type=F32) * (1.0 / den)
            lse2 = m + jnp.log(den)
            o_ref[j] = _unstack_heads(o2, lo).astype(o_ref.dtype)
            lse_ref[j] = _unstack_heads(lse2, lo)
            if per_head_lse:
                lseh_ref[j] = jnp.where(head_lane == 2 * p, lse2[:BLOCK],
                                        jnp.where(head_lane == 2 * p + 1, lse2[BLOCK:], lseh_ref[j]))

        lax.fori_loop(0, n // per_step, step, tuple(scores(t) for t in range(per_step)))

    cur = lambda i, r, p: (i, r, 0, p)
    kv_cur = lambda i, r, p: (i, r, 0, p // rep)
    kv_halo = lambda i, r, p: (jnp.maximum(i * n - 1, 0), r, 0, p // rep)
    big, small = (n, None, BLOCK, LANES), (1, None, BLOCK, LANES)
    in_specs = [pl.BlockSpec(big, cur), pl.BlockSpec(big, kv_cur), pl.BlockSpec(small, kv_halo),
                pl.BlockSpec(big, kv_cur), pl.BlockSpec(small, kv_halo)]
    q4, k4, v4 = _stream_view(q, dil), _stream_view(k, dil), _stream_view(v, dil)
    args = [q4, k4, k4, v4, v4]
    if has_sink:
        in_specs = [pl.BlockSpec(memory_space=pltpu.SMEM)] + in_specs
        args = [sinks] + args
    out_shape = [jax.ShapeDtypeStruct(q4.shape, out_dtype), jax.ShapeDtypeStruct(q4.shape, F32)]
    out_specs = [pl.BlockSpec(big, cur), pl.BlockSpec(big, cur)]
    if per_head_lse:
        out_shape.append(jax.ShapeDtypeStruct(q4.shape[:3] + (n_heads,), F32))
        out_specs.append(pl.BlockSpec((n, None, BLOCK, n_heads), lambda i, r, p: (i, r, 0, 0)))
    outs = pl.pallas_call(
        body, name=name,
        out_shape=tuple(out_shape),
        grid=(l // tq, dil, n_pairs),
        in_specs=in_specs,
        out_specs=tuple(out_specs),
        scratch_shapes=[pltpu.VMEM((tq + BLOCK, LANES), BF16), pltpu.VMEM((tq + BLOCK, LANES), BF16),
                        pltpu.VMEM((2, 2 * BLOCK, 2 * BLOCK), F32)],
        compiler_params=_params("arbitrary", "arbitrary", "arbitrary"),
    )(*args)
    res = [outs[0].reshape(s, w), outs[1].reshape(s, w)]
    if per_head_lse:
        res.append(outs[2].reshape(s, n_heads))
    return res


def _attn_bwd(q, k, v, do, y, lse, sinks, max_dist, rep, dil, name, tq=2048):
    s, w = q.shape
    l = s // dil
    n_pairs = w // LANES
    tq = _pick(l, tq)
    n = tq // BLOCK
    n_blk = l // BLOCK
    n_sb = l // tq
    has_sink = sinks is not None
    scale = HEAD_DIM ** -0.5
    kv_dtype = BF16
    ext = tq + BLOCK

    def body(*refs):
        if has_sink:
            sink_ref, refs = refs[0], refs[1:]
        (q_ref, qn_ref, kc_ref, kh_ref, vc_ref, vh_ref, do_ref, don_ref, y_ref, yn_ref,
         lse_ref, lsen_ref) = refs[:12]
        refs = refs[12:]
        dq_ref, dk_ref, dv_ref = refs[:3]
        refs = refs[3:]
        if has_sink:
            dsink_ref, refs = refs[0], refs[1:]
        kx, vx, bias_ref = refs[:3]
        if rep > 1:
            dk_acc, dv_acc = refs[3:]
        i, p = pl.program_id(0), pl.program_id(2)
        _fill_bias(bias_ref, max_dist)
        own_rows = (q_ref, do_ref, y_ref, lse_ref)
        next_rows = (qn_ref, don_ref, yn_ref, lsen_ref)
        _fill_window(kx, kh_ref, kc_ref, n)
        _fill_window(vx, vh_ref, vc_ref, n)
        if rep > 1:
            @pl.when(p % rep == 0)
            def _():
                dk_acc[...] = jnp.zeros_like(dk_acc)
                dv_acc[...] = jnp.zeros_like(dv_acc)
        lo = lax.broadcasted_iota(jnp.int32, (BLOCK, LANES), 1) < HEAD_DIM
        hi = jnp.logical_not(lo)
        top = lax.broadcasted_iota(jnp.int32, (2 * BLOCK, 1), 0) < BLOCK

        def rows_of(j):
            if isinstance(j, int) and j == n:
                return next_rows, 0
            return own_rows, j

        def front(j, width):
            (qr, dor, _, _), jb = rows_of(j)
            r0 = _aligned(j * BLOCK)
            first = jnp.logical_and(i == 0, j == 0).astype(jnp.int32)
            q2 = _stack_heads(qr[jb].astype(F32) * scale, lo)
            do2 = _stack_heads(dor[jb].astype(F32), lo)
            s2 = _dot_nt(q2, kx[pl.ds(r0, width), :]) + bias_ref[first, :, pl.ds(0, width)]
            return s2, _dot_nt(do2, vx[pl.ds(r0, width), :])

        row_lo = lax.broadcasted_iota(jnp.int32, (LANES, BLOCK), 0) < HEAD_DIM

        def stack_t(x):
            xt = x.T
            return jnp.concatenate([jnp.where(row_lo, xt, 0.0), jnp.where(row_lo, 0.0, xt)], axis=1).astype(BF16)

        def emit(jk, dk_t, dv_t):
            dk_blk, dv_blk = dk_t.T, dv_t.T
            if rep == 1:
                dk_ref[jk] = dk_blk.astype(dk_ref.dtype)
                dv_ref[jk] = dv_blk.astype(dv_ref.dtype)
            else:
                rows = pl.ds(_aligned(jk * BLOCK), BLOCK)
                dk_acc[rows, :] += dk_blk
                dv_acc[rows, :] += dv_blk

        def back(j, width, q_valid, s2, dp2, state):
            sink_acc, carry_k, carry_v = state
            (qr, dor, yr, lser), jb = rows_of(j)
            r0 = _aligned(j * BLOCK)
            qf, dof = qr[jb].astype(F32) * scale, dor[jb].astype(F32)
            yb, lseb = yr[jb].astype(F32), lser[jb]
            prod = dof * yb
            delta = jnp.concatenate([jnp.sum(jnp.where(lo, prod, 0.0), axis=1, keepdims=True),
                                     jnp.sum(jnp.where(lo, 0.0, prod), axis=1, keepdims=True)], axis=0)
            lse2 = jnp.concatenate([_head_col(lseb, lo), _head_col(lseb, hi)], axis=0)
            pr = jnp.exp(s2 - lse2)
            if q_valid is not True:
                pr = jnp.where(q_valid, pr, 0.0)
            ds = pr * (dp2 - delta)
            dk_t = jnp.dot(stack_t(qf), ds.astype(BF16), preferred_element_type=F32)
            dv_t = jnp.dot(stack_t(dof), pr.astype(BF16), preferred_element_type=F32)
            done_k, done_v = carry_k + dk_t[:, :BLOCK], carry_v + dv_t[:, :BLOCK]
            if isinstance(j, int):
                emit(j - 1, done_k, done_v)
            elif rep == 1:
                emit(jnp.maximum(j - 1, 0), done_k, done_v)
            else:
                keep = j > 0
                emit(jnp.maximum(j - 1, 0), jnp.where(keep, done_k, 0.0), jnp.where(keep, done_v, 0.0))
            if width == 2 * BLOCK:
                dq2 = jnp.dot(ds.astype(BF16), kx[pl.ds(r0, width), :], preferred_element_type=F32) * scale
                dq_ref[jb] = _unstack_heads(dq2, lo).astype(dq_ref.dtype)
                carry_k, carry_v = dk_t[:, BLOCK:], dv_t[:, BLOCK:]
            if has_sink:
                sk = jnp.where(top, sink_ref[2 * p], sink_ref[2 * p + 1])
                sink_acc = sink_acc - jnp.exp(sk - lse2) * delta
            return sink_acc, carry_k, carry_v

        per_step = 2 if n % 2 == 0 else 1

        def step(jj, state):
            fronts = [front(jj * per_step + t, 2 * BLOCK) for t in range(per_step)]
            for t in range(per_step):
                state = back(jj * per_step + t, 2 * BLOCK, True, *fronts[t], state)
            return state

        zero_blk = jnp.zeros((LANES, BLOCK), F32)
        state = lax.fori_loop(0, n // per_step, step, (jnp.zeros((2 * BLOCK, 1), F32), zero_blk, zero_blk))
        sink_acc = state[0]
        if n_sb > 1:
            back(n, BLOCK, i < n_sb - 1, *front(n, BLOCK), state)
        else:
            emit(n - 1, state[1], state[2])

        if rep > 1:
            @pl.when(p % rep == rep - 1)
            def _():
                for jc in range(n):
                    rows = slice(jc * BLOCK, (jc + 1) * BLOCK)
                    dk_ref[jc] = dk_acc[rows, :].astype(dk_ref.dtype)
                    dv_ref[jc] = dv_acc[rows, :].astype(dv_ref.dtype)
        if has_sink:
            rowi = lax.broadcasted_iota(jnp.int32, (8, LANES), 0)
            s0 = jnp.sum(sink_acc[:BLOCK], axis=0, keepdims=True)
            s1 = jnp.sum(sink_acc[BLOCK:], axis=0, keepdims=True)
            dsink_ref[...] = jnp.where(rowi == 0, s0, jnp.where(rowi == 1, s1, 0.0))

    cur = lambda i, r, p: (i, r, 0, p)
    nxt = lambda i, r, p: (jnp.minimum((i + 1) * n, n_blk - 1), r, 0, p)
    kv_cur = lambda i, r, p: (i, r, 0, p // rep)
    kv_halo = lambda i, r, p: (jnp.maximum(i * n - 1, 0), r, 0, p // rep)
    big, small = (n, None, BLOCK, LANES), (1, None, BLOCK, LANES)
    in_specs = [pl.BlockSpec(big, cur), pl.BlockSpec(small, nxt),
                pl.BlockSpec(big, kv_cur), pl.BlockSpec(small, kv_halo),
                pl.BlockSpec(big, kv_cur), pl.BlockSpec(small, kv_halo),
                pl.BlockSpec(big, cur), pl.BlockSpec(small, nxt),
                pl.BlockSpec(big, cur), pl.BlockSpec(small, nxt),
                pl.BlockSpec(big, cur), pl.BlockSpec(small, nxt)]
    q4, k4, v4, do4, y4, lse4 = [_stream_view(a, dil) for a in (q, k, v, do, y, lse)]
    args = [q4, q4, k4, k4, v4, v4, do4, do4, y4, y4, lse4, lse4]
    out_shape = [jax.ShapeDtypeStruct(q4.shape, BF16),
                 jax.ShapeDtypeStruct(k4.shape, kv_dtype), jax.ShapeDtypeStruct(v4.shape, kv_dtype)]
    out_specs = [pl.BlockSpec(big, cur), pl.BlockSpec(big, kv_cur), pl.BlockSpec(big, kv_cur)]
    if has_sink:
        in_specs = [pl.BlockSpec(memory_space=pltpu.SMEM)] + in_specs
        args = [sinks] + args
        out_shape.append(jax.ShapeDtypeStruct((n_sb, dil, n_pairs, 8, LANES), F32))
        out_specs.append(pl.BlockSpec((None, None, None, 8, LANES), lambda i, r, p: (i, r, p, 0, 0)))
    outs = pl.pallas_call(
        body, name=name,
        out_shape=tuple(out_shape),
        grid=(n_sb, dil, n_pairs),
        in_specs=in_specs,
        out_specs=tuple(out_specs),
        scratch_shapes=[pltpu.VMEM((ext, LANES), BF16), pltpu.VMEM((ext, LANES), BF16),
                        pltpu.VMEM((2, 2 * BLOCK, 2 * BLOCK), F32)]
        + ([pltpu.VMEM((tq, LANES), F32), pltpu.VMEM((tq, LANES), F32)] if rep > 1 else []),
        compiler_params=_params("arbitrary", "arbitrary", "arbitrary"),
    )(*args)
    grads =[outs[0].reshape(s, w), outs[1].reshape(k.shape), outs[2].reshape(v.shape)]
    if has_sink:
        grads.append(outs[3].sum(axis=(0, 1))[:, 0:2, 0].reshape(1, 2 * n_pairs))
    return grads


def _sum_slots(recv, name, ts=256):
    nd, r, c = recv.shape
    ts = _pick(r, ts, 8)

    def body(r_ref, o_ref):
        acc = r_ref[0].astype(F32)
        for dev in range(1, nd):
            acc = acc + r_ref[dev].astype(F32)
        o_ref[...] = acc

    return pl.pallas_call(
        body, name=name,
        out_shape=jax.ShapeDtypeStruct((r, c), F32),
        grid=(r // ts,),
        in_specs=[pl.BlockSpec((nd, ts, c), lambda i: (0, i, 0))],
        out_specs=pl.BlockSpec((ts, c), lambda i: (i, 0)),
        compiler_params=_params("parallel"),
    )(recv)


def _adamw_math(w, g, m, v):
    c1 = 1.0 - ADAM_B1 ** ADAM_STEP
    c2 = 1.0 - ADAM_B2 ** ADAM_STEP
    m_ = ADAM_B1 * m + (1.0 - ADAM_B1) * g
    v_ = ADAM_B2 * v + (1.0 - ADAM_B2) * (g * g)
    return -ADAM_LR * ((m_ / c1) / (jnp.sqrt(v_ / c2) + ADAM_EPS) + ADAM_WD * w), m_, v_


def _row_tile(r, c, budget=1 << 18):
    return _pick(r, max(8, min(256, budget // c // 8 * 8)), 8)


def _adamw(w, g, m, v, name):
    r, c = w.shape
    ts = _row_tile(r, c)

    def body(w_ref, g_ref, m_ref, v_ref, d_ref, mo_ref, vo_ref):
        d_ref[...], mo_ref[...], vo_ref[...] = _adamw_math(w_ref[...], g_ref[...], m_ref[...], v_ref[...])

    blk = pl.BlockSpec((ts, c), lambda i: (i, 0))
    return pl.pallas_call(
        body, name=name,
        out_shape=tuple([jax.ShapeDtypeStruct((r, c), F32)] * 3),
        grid=(r // ts,),
        in_specs=[blk] * 4,
        out_specs=(blk, blk, blk),
        compiler_params=_params("parallel"),
    )(w, g, m, v)


def _adamw_slots(w, slots, m, v, name):
    r, c = w.shape
    nd = slots.shape[0]
    ts = _row_tile(r, c)

    def body(w_ref, s_ref, m_ref, v_ref, g_ref, d_ref, mo_ref, vo_ref):
        g = s_ref[0].astype(F32)
        for slot in range(1, nd):
            g = g + s_ref[slot].astype(F32)
        g_ref[...] = g
        d_ref[...], mo_ref[...], vo_ref[...] = _adamw_math(w_ref[...], g, m_ref[...], v_ref[...])

    blk = pl.BlockSpec((ts, c), lambda i: (i, 0))
    return pl.pallas_call(
        body, name=name,
        out_shape=tuple([jax.ShapeDtypeStruct((r, c), F32)] * 4),
        grid=(r // ts,),
        in_specs=[blk, pl.BlockSpec((nd, ts, c), lambda i: (0, i, 0)), blk, blk],
        out_specs=(blk, blk, blk, blk),
        compiler_params=_params("parallel"),
    )(w, slots, m, v)


def _rows(a):
    flat = a.reshape(-1)
    pad = (-flat.shape[0]) % PACK_W
    if pad:
        flat = jnp.concatenate([flat, jnp.zeros((pad,), flat.dtype)])
    return flat.reshape(-1, PACK_W)


def _pad_rows(a, mult):
    pad = (-a.shape[-2]) % mult
    if pad:
        widths = [(0, 0)] * (a.ndim - 2) + [(0, pad), (0, 0)]
        a = jnp.pad(a, widths)
    return a


def _to_global(stack, axis):
    moved = jnp.moveaxis(stack, 0, axis)
    shp = list(moved.shape)
    shp[axis:axis + 2] = [shp[axis] * shp[axis + 1]]
    return moved.reshape(shp)


def _to_stack(full, axis):
    shp = list(full.shape)
    shp[axis:axis + 1] = [N_DEV, shp[axis] // N_DEV]
    return jnp.moveaxis(full.reshape(shp), axis, 0)


_BIG = (("w_out", 1), ("a_w_in", 2), ("a_w_group", 2), ("b_w_in", 2), ("c_w_in", 2))


def _dup_heads(wk, n_kv):
    d = wk.shape[0]
    return jnp.tile(wk.reshape(d, n_kv, 1, HEAD_DIM), (1, 1, 2, 1)).reshape(d, n_kv * LANES)


def _fold_heads(dwk, n_kv):
    d = dwk.shape[0]
    folded = dwk.astype(F32).reshape(d, n_kv, 2, HEAD_DIM).sum(axis=2)
    return folded.reshape(d, n_kv * HEAD_DIM).astype(dwk.dtype)


def _perm(a, dil):
    if dil == 1:
        return a
    s, w = a.shape
    return a.reshape(s // (BLOCK * dil), BLOCK, dil, w).transpose(0, 2, 1, 3).reshape(s, w)


def _unperm(a, dil):
    if dil == 1:
        return a
    s, w = a.shape
    return a.reshape(s // (BLOCK * dil), dil, BLOCK, w).transpose(0, 2, 1, 3).reshape(s, w)


def kernel(x, norm_g, final_g, w_out, a_w_in, a_w_group, a_scale, b_w_in, b_sinks, c_w_in, loss_target, m_norm_g, m_final_g, m_w_out, m_a_w_in, m_a_w_group, m_a_scale, m_b_w_in, m_b_sinks, m_c_w_in, v_norm_g, v_final_g, v_w_out, v_a_w_in, v_a_w_group, v_a_scale, v_b_w_in, v_b_sinks, v_c_w_in):
    local = dict(w_out=w_out, a_w_in=a_w_in, a_w_group=a_w_group, b_w_in=b_w_in, c_w_in=c_w_in)
    mom_m = dict(w_out=m_w_out, a_w_in=m_a_w_in, a_w_group=m_a_w_group, b_w_in=m_b_w_in, c_w_in=m_c_w_in)
    mom_v = dict(w_out=v_w_out, a_w_in=v_a_w_in, a_w_group=v_a_w_group, b_w_in=v_b_w_in, c_w_in=v_c_w_in)
    s, d = x.shape[1], x.shape[2]
    depth = norm_g.shape[0]
    e = w_out.shape[1] * N_DEV
    n_heads = e // HEAD_DIM
    n_kv = n_heads // Q_PER_KV
    kv_w = n_kv * HEAD_DIM
    rep = Q_PER_KV // 2
    n_groups = len(POOL_WINDOWS)
    me = 4 * lax.axis_index("x") + 2 * lax.axis_index("y") + lax.axis_index("c")

    flat = {n: local[n].reshape(-1, local[n].shape[-1]) for n, _ in _BIG}
    spack = _pad_rows(_rows(a_scale), 8)
    *walls, sall = _gather([flat[n].astype(BF16) for n, _ in _BIG] + [spack], [True] * len(_BIG) + [False],
                           "gather_weights")
    full = {}
    for k, (name, axis) in enumerate(_BIG):
        full[name] = _to_global(walls[k].reshape((N_DEV,) + local[name].shape), axis)
    scale_full = _to_global(sall.reshape(N_DEV, -1)[:, :a_scale.size].reshape((N_DEV,) + a_scale.shape), 1)

    wout_t = jnp.swapaxes(full["w_out"], 1, 2)
    wa = full["a_w_in"]
    wa_t = jnp.swapaxes(wa, 1, 2)
    wg = full["a_w_group"]
    wg_t = jnp.swapaxes(wg, 2, 3)
    wb = full["b_w_in"][0]
    wb_ext = jnp.concatenate([wb[:, :e], _dup_heads(wb[:, e:e + kv_w], n_kv),
                              _dup_heads(wb[:, e + kv_w:e + 2 * kv_w], n_kv), wb[:, e + 2 * kv_w:]], axis=1)
    wb_ext_t = wb_ext.T
    kd_w = n_kv * LANES
    wc = full["c_w_in"][0]
    wc_t = wc.T

    xs, hs, zs, saved = [x.reshape(s, d)], [], [], []
    hs.append(_rmsnorm_fwd(xs[0], norm_g[0:1], "norm0"))
    loss_vec = dfinal = dx = dxb = None
    for i in range(depth):
        kind, j = i % 3, i // 3
        h = hs[i]
        tag = f"l{i}"
        if kind == 0:
            dpool = _proj_pool_fwd(h, wa[j][:, :e], tag + "_in_pool")
            gate = _matmul(h, wa[j][:, e:], BF16, tag + "_in_gate")
            yr, z = _a_group_fwd(dpool, wg[j], scale_full[j:j + 1], gate, tag + "_group")
            saved.append(dict(dpool=dpool, yr=yr, gate=gate))
        elif kind == 1:
            q = _matmul(h, wb_ext[:, :e], BF16, tag + "_in_q")
            kd = _matmul(h, wb_ext[:, e:e + kd_w], BF16, tag + "_in_k")
            vd = _matmul(h, wb_ext[:, e + kd_w:e + 2 * kd_w], BF16, tag + "_in_v")
            gate = _matmul(h, wb_ext[:, e + 2 * kd_w:], BF16, tag + "_in_gate")
            sinks = b_sinks[j]
            y, lse = _attn_fwd(q, kd, vd, sinks, SWA_MAX_DIST, rep, 1, BF16, tag + "_attn")
            z = _gate_fwd(y, gate, tag + "_gate")
            saved.append(dict(q=q, kd=kd, vd=vd, gate=gate, y=y, lse=lse, sinks=sinks))
        else:
            qkv, outs, lses, lses_tok, h_perm = [], [], [], [], []
            for gi, (window, dil) in enumerate(DILATED_PAIRS):
                hp = _perm(h, dil)
                trio = [_matmul(hp, wc[:, (3 * gi + t) * e:(3 * gi + t + 1) * e], BF16,
                                f"{tag}_in_{'qkv'[t]}{gi}") for t in range(3)]
                o, lse, lse_heads = _attn_fwd(trio[0], trio[1], trio[2], None, window // dil, 1, dil, BF16,
                                              f"{tag}_attn{gi}", per_head_lse=True)
                qkv.append(trio)
                h_perm.append(hp)
                outs.append(_unperm(o, dil))
                lses.append(lse)
                lses_tok.append(_unperm(lse_heads, dil))
            gate = _matmul(h, wc[:, 9 * e:], BF16, tag + "_in_gate")
            y, z = _merge_gate_fwd(outs, lses_tok, gate, tag + "_merge")
            saved.append(dict(qkv=qkv, lses=lses, lses_tok=lses_tok, gate=gate, y=y, h_perm=h_perm))
        zs.append(z)
        if i + 1 < depth:
            x_new, h_new = _outproj_norm(z, full["w_out"][i], xs[i], norm_g[i + 1:i + 2], tag + "_out")
            xs.append(x_new)
            hs.append(h_new)
        else:
            dx, dxb, dfinal, loss_vec = _outproj_loss(z, full["w_out"][i], xs[i], final_g.reshape(1, d),
                                                      loss_target.reshape(s, d), tag + "_out_loss")

    g_full = {"w_out": [None] * depth, "a_w_in": [None] * wa.shape[0], "a_w_group": [None] * wa.shape[0]}
    d_norm = [None] * depth
    d_scale = [None] * wa.shape[0]
    d_sinks = None
    for i in reversed(range(depth)):
        kind, j = i % 3, i // 3
        tag = f"b{i}"
        sv = saved[i]
        g_full["w_out"][i] = _matmul_tn(zs[i], dxb, tag + "_dwout", out_dtype=BF16)
        if kind == 0:
            dgate, dyr, dsc = _dz_fused(dxb, wout_t[i], [sv["yr"], sv["gate"]], [scale_full[j:j + 1]], 2,
                                        _a_gate_epilogue, tag + "_dz_gate", 1024, 1024, with_col_sum=True)
            d_scale[j] = dsc
            du = _group_pool_bwd(dyr, wg_t[j], tag + "_dd_pool")
            g_full["a_w_group"][j] = _grouped_weight_grad(sv["dpool"], dyr, n_groups, tag + "_dwg")
            parts = [du, dgate]
            g_full["a_w_in"][j] = jnp.concatenate(
                [_matmul_tn(hs[i], part, f"{tag}_dwin{t}", out_dtype=BF16) for t, part in enumerate(parts)], axis=1)
            dhs = [_matmul_cat(parts, wa_t[j], F32, tag + "_dh")]
        elif kind == 1:
            dgate, do = _dz_fused(dxb, wout_t[i], [sv["y"], sv["gate"]], [], 2, _gate_epilogue,
                                  tag + "_dz_gate", 1024, 1024)
            dq, dkd, dvd, d_sinks = _attn_bwd(sv["q"], sv["kd"], sv["vd"], do, sv["y"], sv["lse"], sv["sinks"],
                                              SWA_MAX_DIST, rep, 1, tag + "_attn")
            parts = [dq, dkd, dvd, dgate]
            dws = [_matmul_tn(hs[i], part, f"{tag}_dwin{t}", out_dtype=BF16) for t, part in enumerate(parts)]
            g_full["b_w_in"] = jnp.concatenate(
                [dws[0], _fold_heads(dws[1], n_kv), _fold_heads(dws[2], n_kv), dws[3]], axis=1)[None]
            dhs = [_matmul_cat(parts, wb_ext_t, F32, tag + "_dh")]
        else:
            dgate, *dos = _dz_fused(dxb, wout_t[i], [sv["y"], sv["gate"]], [], 4, _merge_gate_epilogue,
                                    tag + "_dz_merge", 1024, 512, rows=sv["lses_tok"])
            y_bf = sv["y"]
            dws, dhs = [], []
            for gi, (window, dil) in enumerate(DILATED_PAIRS):
                qv, kv, vv = sv["qkv"][gi]
                grads = _attn_bwd(qv, kv, vv, _perm(dos[gi], dil), _perm(y_bf, dil), sv["lses"][gi], None,
                                  window // dil, 1, dil, f"{tag}_attn{gi}")
                dws += [_matmul_tn(sv["h_perm"][gi], part, f"{tag}_dwin{gi}{'qkv'[t]}", out_dtype=BF16)
                        for t, part in enumerate(grads)]
                dhs.append(_unperm(_matmul_cat(grads, wc_t[3 * gi * e:3 * (gi + 1) * e], F32, f"{tag}_dh{gi}"),
                                   dil))
            dws.append(_matmul_tn(hs[i], dgate, tag + "_dwin_gate", out_dtype=BF16))
            dhs.append(_matmul(dgate, wc_t[9 * e:], F32, tag + "_dh_gate"))
            g_full["c_w_in"] = jnp.concatenate(dws, axis=1)[None]
        dx, dxb, d_norm[i] = _rmsnorm_bwd(dhs, xs[i], norm_g[i:i + 1], dx, tag + "_norm")
    grad_x = dx.reshape(x.shape)
    for name in ("w_out", "a_w_in", "a_w_group"):
        g_full[name] = jnp.stack(g_full[name], axis=0)

    stacks = [_to_stack(g_full[n], axis).astype(BF16).reshape((N_DEV,) + flat[n].shape) for n, axis in _BIG]
    loss_local = (0.5 / d) * jnp.sum(loss_vec)
    small = [jnp.concatenate(d_norm, axis=0), dfinal, d_sinks, jnp.concatenate(d_scale, axis=0),
             loss_local.reshape(1, 1)]
    small_rows = [_rows(a) for a in small]
    small_offs = [sum(r.shape[0] for r in small_rows[:k]) for k in range(len(small_rows) + 1)]
    small_pack = _pad_rows(jnp.concatenate(small_rows, axis=0), 8)
    core = lax.axis_index("c").astype(jnp.int32).reshape(1)
    from_sibling = _sibling_exchange(stacks, "exchange_sibling")
    chip_sums = [_pair_sum(stacks[k], from_sibling[k], core, "sum_pair_" + n) for k, (n, _) in enumerate(_BIG)]
    grecv, srecv = _chip_exchange(chip_sums, small_pack, "exchange_chips")
    ssum = _sum_slots(srecv, "sum_small")

    def small_part(k, like):
        return ssum[small_offs[k]:small_offs[k + 1]].reshape(-1)[:like.size].reshape(like.shape)

    g_norm = small_part(0, norm_g)
    g_final = small_part(1, final_g)
    g_sinks = small_part(2, b_sinks)
    g_scale_full = small_part(3, scale_full)
    loss = ssum[small_offs[4], 0]
    g_scale = lax.dynamic_slice_in_dim(g_scale_full, me * a_scale.shape[1], a_scale.shape[1], axis=1)

    small_w = [("norm_g", norm_g, m_norm_g, v_norm_g, g_norm), ("final_g", final_g, m_final_g, v_final_g, g_final),
               ("a_scale", a_scale, m_a_scale, v_a_scale, g_scale), ("b_sinks", b_sinks, m_b_sinks, v_b_sinks, g_sinks)]
    tail = lambda idx: _pad_rows(jnp.concatenate([_rows(t[idx]) for t in small_w], axis=0), 8)
    tail_sizes = [_rows(t[1]).shape[0] for t in small_w]
    tail_offs = [sum(tail_sizes[:k]) for k in range(len(tail_sizes) + 1)]
    g_tail = tail(4)
    tails = (g_tail,) + _adamw(tail(1), g_tail, tail(2), tail(3), "adamw_small")
    grads, deltas, new_m, new_v = {}, {}, {}, {}
    for k, (name, w_, _, _, _) in enumerate(small_w):
        for out, packed in zip((grads, deltas, new_m, new_v), tails):
            out[name] = packed[tail_offs[k]:tail_offs[k + 1]].reshape(-1)[:w_.size].reshape(w_.shape)
    for k, (name, _) in enumerate(_BIG):
        shape2d = flat[name].shape
        res = _adamw_slots(flat[name], grecv[k], mom_m[name].reshape(shape2d), mom_v[name].reshape(shape2d),
                           "adamw_" + name)
        for out, val in zip((grads, deltas, new_m, new_v), res):
            out[name] = val.reshape(local[name].shape)

    order = ("norm_g", "final_g", "w_out", "a_w_in", "a_w_group", "a_scale", "b_w_in", "b_sinks", "c_w_in")
    return (loss, grad_x, *[grads[n] for n in order], *[deltas[n] for n in order],
            *[new_m[n] for n in order], *[new_v[n] for n in order])
```

```python
import functools

import jax
import jax.numpy as jnp
from jax import lax
from jax.experimental import pallas as pl
from jax.experimental.pallas import tpu as pltpu

F32 = jnp.float32
BF16 = jnp.bfloat16

N_DEV = 8
HEAD_DIM = 64
LANES = 128
BLOCK = 128
Q_PER_KV = 8
POOL_WINDOWS = (2, 4, 8, 16)
POOL_HALO = 16
DILATED_PAIRS = ((128, 1), (512, 4), (2048, 16))
SWA_MAX_DIST = 127
RMS_EPS = 1e-5
PACK_W = 1024
NEG = -1e30

ADAM_LR = 0.001
ADAM_B1 = 0.9
ADAM_B2 = 0.999
ADAM_EPS = 1e-08
ADAM_WD = 0.01
ADAM_STEP = 10

VMEM_LIMIT = 48 * 1024 * 1024


def _params(*sem):
    return pltpu.CompilerParams(dimension_semantics=sem if sem else None, vmem_limit_bytes=VMEM_LIMIT)


def _pick(dim, target, mult=LANES):
    if dim <= target:
        return dim
    t = target - target % mult
    while dim % t:
        t -= mult
    return t


def _sigmoid(x):
    return 1.0 / (1.0 + jnp.exp(-x))


ANY_SPEC = pl.BlockSpec(memory_space=pl.ANY)


def _where_am_i():
    x, y, c = lax.axis_index("x"), lax.axis_index("y"), lax.axis_index("c")
    return x, y, c, 4 * x + 2 * y + c


def _peer(x, y, c, r):
    return x ^ ((r >> 2) & 1), y ^ ((r >> 1) & 1), c ^ (r & 1)


GATHER_SEMS = 8


def _gather(blocks, split, name):
    n = len(blocks)
    halves = [b.shape[0] // 2 for b in blocks]

    def body(*refs):
        send, recv = refs[:n], refs[n:2 * n]
        send_sems, recv_sems, local_sems = refs[2 * n:]
        x, y, c, me = _where_am_i()
        sib, xn, yn, dg = (_peer(x, y, c, r) for r in (1, 4, 2, 6))
        sib_id, xn_id, yn_id, dg_id = me ^ 1, me ^ 4, me ^ 2, me ^ 6

        def copy(k, sem, src, dst, to):
            return pltpu.make_async_remote_copy(
                src_ref=src, dst_ref=dst, send_sem=send_sems.at[k, sem], recv_sem=recv_sems.at[k, sem],
                device_id=to, device_id_type=pl.DeviceIdType.MESH)

        def part(k, slot, half):
            return recv[k].at[slot, pl.ds(half * halves[k], halves[k])]

        started = []
        for k in range(n):
            own = pltpu.make_async_copy(send[k], recv[k].at[me], local_sems.at[k])
            own.start()
            started.append(own)
        sends = []
        for k in range(n):
            sends += [copy(k, 0, send[k], recv[k].at[me], sib), copy(k, 1, send[k], recv[k].at[me], xn),
                      copy(k, 2, send[k], recv[k].at[me], yn)]
            if not split[k]:
                sends.append(copy(k, 3, send[k], recv[k].at[me], dg))
        for cp in sends:
            cp.start()

        def after(k, sem, slot, hand_on_sem, half, half_sem, half_to):
            copy(k, sem, send[k], recv[k].at[slot], sib).wait_recv()
            new = [copy(k, hand_on_sem, recv[k].at[slot], recv[k].at[slot], sib)]
            if split[k]:
                new.append(copy(k, half_sem, part(k, slot, half), part(k, slot, half), half_to))
            for cp in new:
                cp.start()
            sends.extend(new)

        for k in range(n):
            after(k, 2, yn_id, 6, 0, 3, xn)
        for k in range(n):
            after(k, 1, xn_id, 5, 1, 4, yn)
        for k in range(n):
            if split[k]:
                copy(k, 3, part(k, dg_id, 0), part(k, dg_id, 0), sib).wait_recv()
                copy(k, 4, part(k, dg_id, 1), part(k, dg_id, 1), sib).wait_recv()
            else:
                copy(k, 3, send[k], recv[k].at[dg_id], sib).wait_recv()
            fwd = copy(k, 7, recv[k].at[dg_id], recv[k].at[dg_id], sib)
            fwd.start()
            sends.append(fwd)
        for k in range(n):
            copy(k, 0, send[k], recv[k].at[sib_id], sib).wait_recv()
            for sem, r in ((5, 4), (6, 2), (7, 6)):
                copy(k, sem, send[k], recv[k].at[sib_id ^ r], sib).wait_recv()
        for cp in sends:
            cp.wait_send()
        for own in started:
            own.wait()

    return pl.pallas_call(
        body, name=name,
        out_shape=tuple(jax.ShapeDtypeStruct((N_DEV,) + b.shape, b.dtype) for b in blocks),
        in_specs=[ANY_SPEC] * n,
        out_specs=tuple([ANY_SPEC] * n),
        scratch_shapes=[pltpu.SemaphoreType.DMA((n, GATHER_SEMS)), pltpu.SemaphoreType.DMA((n, GATHER_SEMS)),
                        pltpu.SemaphoreType.DMA((n,))],
    )(*blocks)


def _sibling_exchange(stacks, name):
    n_chips = N_DEV // 2
    n = len(stacks)

    def body(*refs):
        g_refs, t_refs = refs[:n], refs[n:2 * n]
        send_sems, recv_sems = refs[2 * n:]
        x, y, c, _ = _where_am_i()
        sib = _peer(x, y, c, 1)
        copies = [pltpu.make_async_remote_copy(
            src_ref=g_refs[k].at[2 * chip + (1 - c)], dst_ref=t_refs[k].at[chip], send_sem=send_sems.at[k, chip],
            recv_sem=recv_sems.at[k, chip], device_id=sib, device_id_type=pl.DeviceIdType.MESH)
            for k in range(n) for chip in range(n_chips)]
        for cp in copies:
            cp.start()
        for cp in copies:
            cp.wait_recv()
        for cp in copies:
            cp.wait_send()

    return pl.pallas_call(
        body, name=name,
        out_shape=tuple(jax.ShapeDtypeStruct((n_chips,) + g.shape[1:], g.dtype) for g in stacks),
        in_specs=[ANY_SPEC] * n, out_specs=tuple([ANY_SPEC] * n),
        scratch_shapes=[pltpu.SemaphoreType.DMA((n, n_chips)), pltpu.SemaphoreType.DMA((n, n_chips))],
    )(*stacks)


CHIP_SEMS = 6


def _chip_exchange(csums, small, name):
    n = len(csums)
    halves = [cs.shape[1] // 2 for cs in csums]

    def body(*refs):
        c_refs, s_ref = refs[:n], refs[n]
        r_refs, sr_ref = refs[n + 1:2 * n + 1], refs[2 * n + 1]
        stage_x, stage_y = refs[2 * n + 2:3 * n + 2], refs[3 * n + 2:4 * n + 2]
        send_sems, recv_sems, small_send, small_recv, local_sems = refs[4 * n + 2:]
        x, y, c, me = _where_am_i()
        my_chip = 2 * x + y
        xn, yn = _peer(x, y, c, 4), _peer(x, y, c, 2)
        xn_chip, yn_chip, dg_chip = my_chip ^ 2, my_chip ^ 1, my_chip ^ 3
        own = [pltpu.make_async_copy(c_refs[k].at[my_chip], r_refs[k].at[my_chip], local_sems.at[k])
               for k in range(n)]
        own.append(pltpu.make_async_copy(s_ref, sr_ref.at[me], local_sems.at[n]))
        for cp in own:
            cp.start()

        def copy(k, sem, src, dst, to):
            return pltpu.make_async_remote_copy(
                src_ref=src, dst_ref=dst, send_sem=send_sems.at[k, sem], recv_sem=recv_sems.at[k, sem],
                device_id=to, device_id_type=pl.DeviceIdType.MESH)

        def half(ref, k, slot, which):
            return ref.at[slot, pl.ds(which * halves[k], halves[k])]

        sends, recvs = [], []
        for k in range(n):
            sends += [copy(k, 0, c_refs[k].at[xn_chip], r_refs[k].at[my_chip], xn),
                      copy(k, 1, c_refs[k].at[yn_chip], r_refs[k].at[my_chip], yn),
                      copy(k, 2, half(c_refs[k], k, dg_chip, 0), stage_x[k], xn),
                      copy(k, 3, half(c_refs[k], k, dg_chip, 1), stage_y[k], yn)]
            recvs += [copy(k, 0, c_refs[k].at[xn_chip], r_refs[k].at[xn_chip], xn),
                      copy(k, 1, c_refs[k].at[yn_chip], r_refs[k].at[yn_chip], yn),
                      copy(k, 4, stage_x[k], half(r_refs[k], k, dg_chip, 0), yn),
                      copy(k, 5, stage_y[k], half(r_refs[k], k, dg_chip, 1), xn)]
        for r in range(1, N_DEV):
            to = _peer(x, y, c, r)
            sends.append(pltpu.make_async_remote_copy(
                src_ref=s_ref, dst_ref=sr_ref.at[me], send_sem=small_send.at[r - 1],
                recv_sem=small_recv.at[r - 1], device_id=to, device_id_type=pl.DeviceIdType.MESH))
            recvs.append(pltpu.make_async_remote_copy(
                src_ref=s_ref, dst_ref=sr_ref.at[me ^ r], send_sem=small_send.at[r - 1],
                recv_sem=small_recv.at[r - 1], device_id=to, device_id_type=pl.DeviceIdType.MESH))
        for cp in sends:
            cp.start()
        for k in range(n):
            copy(k, 2, stage_x[k], stage_x[k], xn).wait_recv()
            fwd = copy(k, 4, stage_x[k], half(r_refs[k], k, xn_chip, 0), yn)
            fwd.start()
            sends.append(fwd)
        for k in range(n):
            copy(k, 3, stage_y[k], stage_y[k], yn).wait_recv()
            fwd = copy(k, 5, stage_y[k], half(r_refs[k], k, yn_chip, 1), xn)
            fwd.start()
            sends.append(fwd)
        for cp in recvs:
            cp.wait_recv()
        for cp in sends:
            cp.wait_send()
        for cp in own:
            cp.wait()

    stages = tuple(jax.ShapeDtypeStruct((h, cs.shape[2]), cs.dtype) for h, cs in zip(halves, csums))
    outs = pl.pallas_call(
        body, name=name,
        out_shape=tuple(jax.ShapeDtypeStruct(cs.shape, cs.dtype) for cs in csums)
        + (jax.ShapeDtypeStruct((N_DEV,) + small.shape, small.dtype),) + stages + stages,
        in_specs=[ANY_SPEC] * (n + 1), out_specs=tuple([ANY_SPEC] * (3 * n + 1)),
        scratch_shapes=[pltpu.SemaphoreType.DMA((n, CHIP_SEMS)), pltpu.SemaphoreType.DMA((n, CHIP_SEMS)),
                        pltpu.SemaphoreType.DMA((N_DEV - 1,)), pltpu.SemaphoreType.DMA((N_DEV - 1,)),
                        pltpu.SemaphoreType.DMA((n + 1,))],
    )(*csums, small)
    return list(outs[:n]), outs[n]


def _pair_sum(gpack, other, core, name, ts=256):
    n_chips, r, c = other.shape
    ts = _pick(r, ts, 16)

    def body(core_ref, g_ref, o_ref, out_ref):
        del core_ref
        out_ref[...] = (g_ref[...].astype(F32) + o_ref[...].astype(F32)).astype(out_ref.dtype)

    return pl.pallas_call(
        body, name=name,
        out_shape=jax.ShapeDtypeStruct(other.shape, other.dtype),
        grid_spec=pltpu.PrefetchScalarGridSpec(
            num_scalar_prefetch=1, grid=(n_chips, r // ts),
            in_specs=[pl.BlockSpec((None, ts, c), lambda j, i, core_ref: (2 * j + core_ref[0], i, 0)),
                      pl.BlockSpec((None, ts, c), lambda j, i, core_ref: (j, i, 0))],
            out_specs=pl.BlockSpec((None, ts, c), lambda j, i, core_ref: (j, i, 0))),
        compiler_params=_params("parallel", "parallel"),
    )(core, gpack, other)


def _matmul(a, b, out_dtype, name, tm=2048, tn=1024, tk=1024):
    m, kdim = a.shape
    n = b.shape[1]
    tm, tn, tk = _pick(m, tm), _pick(n, tn), _pick(kdim, tk)
    nk = kdim // tk

    if nk == 1:
        def body(a_ref, b_ref, o_ref):
            o_ref[...] = jnp.dot(a_ref[...], b_ref[...], preferred_element_type=F32).astype(o_ref.dtype)
        scratch = []
    else:
        def body(a_ref, b_ref, o_ref, acc_ref):
            kk = pl.program_id(2)

            @pl.when(kk == 0)
            def _():
                acc_ref[...] = jnp.zeros_like(acc_ref)

            acc_ref[...] += jnp.dot(a_ref[...], b_ref[...], preferred_element_type=F32)

            @pl.when(kk == nk - 1)
            def _():
                o_ref[...] = acc_ref[...].astype(o_ref.dtype)
        scratch = [pltpu.VMEM((tm, tn), F32)]

    return pl.pallas_call(
        body, name=name,
        out_shape=jax.ShapeDtypeStruct((m, n), out_dtype),
        grid=(m // tm, n // tn, nk),
        in_specs=[pl.BlockSpec((tm, tk), lambda i, j, k: (i, k)),
                  pl.BlockSpec((tk, tn), lambda i, j, k: (k, j))],
        out_specs=pl.BlockSpec((tm, tn), lambda i, j, k: (i, j)),
        scratch_shapes=scratch,
        compiler_params=_params("parallel", "parallel", "arbitrary"),
    )(a, b)


def _matmul_cat(parts, b, out_dtype, name, tm=1024, tn=1024, tk=1024):
    m = parts[0].shape[0]
    n = b.shape[1]
    tm, tn = _pick(m, tm), _pick(n, tn)
    tk = min(_pick(p.shape[1], tk) for p in parts)
    steps = [p.shape[1] // tk for p in parts]
    assert all(p.shape[1] % tk == 0 for p in parts)
    starts = [sum(steps[:t]) for t in range(len(parts))]
    nk = sum(steps)
    n_parts = len(parts)

    def body(*refs):
        a_refs, b_ref, o_ref, acc_ref = refs[:n_parts], refs[n_parts], refs[n_parts + 1], refs[n_parts + 2]
        kk = pl.program_id(2)

        @pl.when(kk == 0)
        def _():
            acc_ref[...] = jnp.zeros_like(acc_ref)

        for t in range(n_parts):
            @pl.when(jnp.logical_and(kk >= starts[t], kk < starts[t] + steps[t]))
            def _(t=t):
                acc_ref[...] += jnp.dot(a_refs[t][...], b_ref[...], preferred_element_type=F32)

        @pl.when(kk == nk - 1)
        def _():
            o_ref[...] = acc_ref[...].astype(o_ref.dtype)

    def part_map(t):
        return lambda i, j, k: (i, jnp.clip(k - starts[t], 0, steps[t] - 1))

    return pl.pallas_call(
        body, name=name,
        out_shape=jax.ShapeDtypeStruct((m, n), out_dtype),
        grid=(m // tm, n // tn, nk),
        in_specs=[pl.BlockSpec((tm, tk), part_map(t)) for t in range(n_parts)]
        + [pl.BlockSpec((tk, tn), lambda i, j, k: (k, j))],
        out_specs=pl.BlockSpec((tm, tn), lambda i, j, k: (i, j)),
        scratch_shapes=[pltpu.VMEM((tm, tn), F32)],
        compiler_params=_params("parallel", "parallel", "arbitrary"),
    )(*parts, b)


def _matmul_tn(a, b, name, tm=1024, tn=1024, tk=2048, out_dtype=F32):
    kdim, m = a.shape
    n = b.shape[1]
    tm, tn, tk = _pick(m, tm), _pick(n, tn), _pick(kdim, tk)
    nk = kdim // tk

    def body(a_ref, b_ref, o_ref, acc_ref):
        kk = pl.program_id(2)

        @pl.when(kk == 0)
        def _():
            acc_ref[...] = jnp.zeros_like(acc_ref)

        acc_ref[...] += lax.dot_general(a_ref[...], b_ref[...], (((0,), (0,)), ((), ())),
                                        preferred_element_type=F32)

        @pl.when(kk == nk - 1)
        def _():
            o_ref[...] = acc_ref[...].astype(o_ref.dtype)

    return pl.pallas_call(
        body, name=name,
        out_shape=jax.ShapeDtypeStruct((m, n), out_dtype),
        grid=(m // tm, n // tn, nk),
        in_specs=[pl.BlockSpec((tk, tm), lambda i, j, k: (k, i)),
                  pl.BlockSpec((tk, tn), lambda i, j, k: (k, j))],
        out_specs=pl.BlockSpec((tm, tn), lambda i, j, k: (i, j)),
        scratch_shapes=[pltpu.VMEM((tm, tn), F32)],
        compiler_params=_params("parallel", "parallel", "arbitrary"),
    )(a, b)


def _grouped_weight_grad(a, b, ng, name, tk=1024):
    s, e = a.shape
    g = e // ng
    tk = _pick(s, tk)
    nk = s // tk

    def body(a_ref, b_ref, o_ref):
        kk = pl.program_id(1)

        @pl.when(kk == 0)
        def _():
            o_ref[...] = jnp.zeros_like(o_ref)

        o_ref[...] += lax.dot_general(a_ref[...], b_ref[...], (((0,), (0,)), ((), ())),
                                      preferred_element_type=F32)

    return pl.pallas_call(
        body, name=name,
        out_shape=jax.ShapeDtypeStruct((ng, g, g), F32),
        grid=(ng, nk),
        in_specs=[pl.BlockSpec((tk, g), lambda j, k: (k, j)),
                  pl.BlockSpec((tk, g), lambda j, k: (k, j))],
        out_specs=pl.BlockSpec((None, g, g), lambda j, k: (j, 0, 0)),
        compiler_params=_params("parallel", "arbitrary"),
    )(a, b)


def _rms(x):
    r = lax.rsqrt(jnp.mean(x * x, axis=1, keepdims=True) + RMS_EPS)
    return x * r, r


def _rmsnorm_fwd(x, g, name, ts=256):
    s, d = x.shape
    ts = _pick(s, ts, 8)

    def body(x_ref, g_ref, h_ref):
        xhat, _ = _rms(x_ref[...])
        h_ref[...] = (xhat * g_ref[...]).astype(BF16)

    return pl.pallas_call(
        body, name=name,
        out_shape=jax.ShapeDtypeStruct((s, d), BF16),
        grid=(s // ts,),
        in_specs=[pl.BlockSpec((ts, d), lambda i: (i, 0)), pl.BlockSpec((1, d), lambda i: (0, 0))],
        out_specs=pl.BlockSpec((ts, d), lambda i: (i, 0)),
        compiler_params=_params("parallel"),
    )(x, g)


def _outproj_norm(z, w, x, g, name, tm=512):
    s, e = z.shape
    d = w.shape[1]
    tm = _pick(s, tm)

    def body(z_ref, w_ref, x_ref, g_ref, xo_ref, h_ref):
        xn = x_ref[...] + jnp.dot(z_ref[...], w_ref[...], preferred_element_type=F32)
        xo_ref[...] = xn
        xhat, _ = _rms(xn)
        h_ref[...] = (xhat * g_ref[...]).astype(BF16)

    return pl.pallas_call(
        body, name=name,
        out_shape=(jax.ShapeDtypeStruct((s, d), F32), jax.ShapeDtypeStruct((s, d), BF16)),
        grid=(s // tm,),
        in_specs=[pl.BlockSpec((tm, e), lambda i: (i, 0)), pl.BlockSpec((e, d), lambda i: (0, 0)),
                  pl.BlockSpec((tm, d), lambda i: (i, 0)), pl.BlockSpec((1, d), lambda i: (0, 0))],
        out_specs=(pl.BlockSpec((tm, d), lambda i: (i, 0)), pl.BlockSpec((tm, d), lambda i: (i, 0))),
        compiler_params=_params("parallel"),
    )(z, w, x, g)


def _outproj_loss(z, w, x, g, target, name, tm=512):
    s, e = z.shape
    d = w.shape[1]
    tm = _pick(s, tm)

    def body(z_ref, w_ref, x_ref, g_ref, t_ref, dx_ref, dxb_ref, dg_ref, loss_ref):
        i = pl.program_id(0)
        xn = x_ref[...] + jnp.dot(z_ref[...], w_ref[...], preferred_element_type=F32)
        xhat, r = _rms(xn)
        gain = g_ref[...]
        diff = xhat * gain - t_ref[...]
        dout = diff * (1.0 / d)
        dxhat = dout * gain
        dx = r * (dxhat - xhat * jnp.mean(dxhat * xhat, axis=1, keepdims=True))
        dx_ref[...] = dx
        dxb_ref[...] = dx.astype(BF16)

        @pl.when(i == 0)
        def _():
            dg_ref[...] = jnp.zeros_like(dg_ref)
            loss_ref[...] = jnp.zeros_like(loss_ref)

        dg_ref[...] += jnp.sum(dout * xhat, axis=0, keepdims=True)
        loss_ref[...] += jnp.sum(diff * diff, axis=0, keepdims=True)

    row = lambda i: (i, 0)
    fixed = lambda i: (0, 0)
    return pl.pallas_call(
        body, name=name,
        out_shape=(jax.ShapeDtypeStruct((s, d), F32), jax.ShapeDtypeStruct((s, d), BF16),
                   jax.ShapeDtypeStruct((1, d), F32), jax.ShapeDtypeStruct((1, d), F32)),
        grid=(s // tm,),
        in_specs=[pl.BlockSpec((tm, e), row), pl.BlockSpec((e, d), fixed), pl.BlockSpec((tm, d), row),
                  pl.BlockSpec((1, d), fixed), pl.BlockSpec((tm, d), row)],
        out_specs=(pl.BlockSpec((tm, d), row), pl.BlockSpec((tm, d), row),
                   pl.BlockSpec((1, d), fixed), pl.BlockSpec((1, d), fixed)),
        compiler_params=_params("arbitrary"),
    )(z, w, x, g, target)


def _rmsnorm_bwd(dhs, x, g, dx_next, name, ts=256):
    s, d = x.shape
    ts = _pick(s, ts, 8)
    n_dh = len(dhs)

    def body(*refs):
        dh_refs = refs[:n_dh]
        x_ref, g_ref, dn_ref, dx_ref, dxb_ref, dg_ref = refs[n_dh:]
        i = pl.program_id(0)
        xhat, r = _rms(x_ref[...])
        dh_ = dh_refs[0][...]
        for extra in dh_refs[1:]:
            dh_ = dh_ + extra[...]
        dxhat = dh_ * g_ref[...]
        dx = dn_ref[...] + r * (dxhat - xhat * jnp.mean(dxhat * xhat, axis=1, keepdims=True))
        dx_ref[...] = dx
        dxb_ref[...] = dx.astype(BF16)

        @pl.when(i == 0)
        def _():
            dg_ref[...] = jnp.zeros_like(dg_ref)

        dg_ref[...] += jnp.sum(dh_ * xhat, axis=0, keepdims=True)

    row = lambda i: (i, 0)
    fixed = lambda i: (0, 0)
    return pl.pallas_call(
        body, name=name,
        out_shape=(jax.ShapeDtypeStruct((s, d), F32), jax.ShapeDtypeStruct((s, d), BF16),
                   jax.ShapeDtypeStruct((1, d), F32)),
        grid=(s // ts,),
        in_specs=[pl.BlockSpec((ts, d), row)] * n_dh + [pl.BlockSpec((ts, d), row), pl.BlockSpec((1, d), fixed),
                                                        pl.BlockSpec((ts, d), row)],
        out_specs=(pl.BlockSpec((ts, d), row), pl.BlockSpec((ts, d), row), pl.BlockSpec((1, d), fixed)),
        compiler_params=_params("arbitrary"),
    )(*dhs, x, g, dx_next)


def _pool_counts(t0, rows, cols, window):
    t = t0 + lax.broadcasted_iota(jnp.int32, (rows, cols), 0)
    return jnp.minimum(t + 1, window).astype(F32)


def _proj_pool_fwd(h, w, name, ts=1024, tc=512):
    s, dm = h.shape
    e = w.shape[1]
    ng = len(POOL_WINDOWS)
    gdim = e // ng
    ts, tc = _pick(s, ts), _pick(gdim, tc)
    cpg = gdim // tc
    hb = ts // POOL_HALO

    def body(h_ref, halo_ref, w_ref, d_ref):
        i, grp = pl.program_id(0), pl.program_id(1)
        cur = jnp.dot(h_ref[...], w_ref[...], preferred_element_type=F32)
        halo = jnp.dot(halo_ref[...], w_ref[...], preferred_element_type=F32)
        ext = jnp.concatenate([jnp.where(i > 0, halo, 0.0), cur], axis=0)
        for gi, window in enumerate(POOL_WINDOWS):
            @pl.when(grp == gi)
            def _(window=window):
                acc = ext
                k = 1
                while k < window:
                    acc = acc + pltpu.roll(acc, k, 0)
                    k *= 2
                pooled = acc[POOL_HALO:, :] / _pool_counts(i * ts, ts, tc, window)
                d_ref[...] = (pooled - cur).astype(BF16)

    return pl.pallas_call(
        body, name=name,
        out_shape=jax.ShapeDtypeStruct((s, e), BF16),
        grid=(s // ts, ng, cpg),
        in_specs=[pl.BlockSpec((ts, dm), lambda i, g, j: (i, 0)),
                  pl.BlockSpec((POOL_HALO, dm), lambda i, g, j: (jnp.maximum(i * hb - 1, 0), 0)),
                  pl.BlockSpec((dm, tc), lambda i, g, j: (0, g * cpg + j))],
        out_specs=pl.BlockSpec((ts, tc), lambda i, g, j: (i, g * cpg + j)),
        compiler_params=_params("parallel", "parallel", "parallel"),
    )(h, h, w)


def _dz_fused(dxb, w_t, tiles, vecs, n_out, epilogue, name, tm, tn, with_col_sum=False, rows=()):
    s, dm = dxb.shape
    e = w_t.shape[1]
    tm, tn = _pick(s, tm), _pick(e, tn)
    n_t, n_v, n_r = len(tiles), len(vecs), len(rows)

    def body(*refs):
        a_ref, b_ref = refs[:2]
        tile_refs, vec_refs = refs[2:2 + n_t], refs[2 + n_t:2 + n_t + n_v]
        row_refs = refs[2 + n_t + n_v:2 + n_t + n_v + n_r]
        out_refs = refs[2 + n_t + n_v + n_r:]
        i = pl.program_id(1)
        dz = jnp.dot(a_ref[...], b_ref[...], preferred_element_type=F32)
        extra = ([r[...] for r in row_refs], pl.program_id(0) * tn) if n_r else ()
        res = epilogue(dz, [t[...] for t in tile_refs], [v[...] for v in vec_refs], *extra)
        for o_ref, val in zip(out_refs[:n_out], res[:n_out]):
            o_ref[...] = val.astype(o_ref.dtype)
        if with_col_sum:
            sum_ref = out_refs[n_out]

            @pl.when(i == 0)
            def _():
                sum_ref[...] = jnp.zeros_like(sum_ref)

            sum_ref[...] += jnp.sum(res[n_out], axis=0, keepdims=True)

    blk = lambda j, i: (i, j)
    vec = lambda j, i: (0, j)
    out_shape = [jax.ShapeDtypeStruct((s, e), BF16)] * n_out
    out_specs = [pl.BlockSpec((tm, tn), blk)] * n_out
    if with_col_sum:
        out_shape.append(jax.ShapeDtypeStruct((1, e), F32))
        out_specs.append(pl.BlockSpec((1, tn), vec))
    return pl.pallas_call(
        body, name=name,
        out_shape=tuple(out_shape),
        grid=(e // tn, s // tm),
        in_specs=[pl.BlockSpec((tm, dm), lambda j, i: (i, 0)), pl.BlockSpec((dm, tn), lambda j, i: (0, j))]
        + [pl.BlockSpec((tm, tn), blk)] * n_t + [pl.BlockSpec((1, tn), vec)] * n_v
        + [pl.BlockSpec((tm, r.shape[1]), lambda j, i: (i, 0)) for r in rows],
        out_specs=tuple(out_specs),
        compiler_params=_params("parallel", "arbitrary"),
    )(dxb, w_t, *tiles, *vecs, *rows)


def _group_pool_bwd(dyr, w_t, name, ts=1024, tc=512):
    s, e = dyr.shape
    ng = len(POOL_WINDOWS)
    gdim = e // ng
    ts, tc = _pick(s, ts), _pick(gdim, tc)
    cpg = gdim // tc
    hb = ts // POOL_HALO
    n_halo = s // POOL_HALO
    nst = s // ts

    def body(dy_ref, halo_ref, w_ref, du_ref):
        i, grp = pl.program_id(0), pl.program_id(1)
        cur = jnp.dot(dy_ref[...], w_ref[...], preferred_element_type=F32)
        halo = jnp.dot(halo_ref[...], w_ref[...], preferred_element_type=F32)
        ext = jnp.concatenate([cur, jnp.where(i < nst - 1, halo, 0.0)], axis=0)
        rows = ts + POOL_HALO
        for gi, window in enumerate(POOL_WINDOWS):
            @pl.when(grp == gi)
            def _(window=window):
                acc = ext / _pool_counts(i * ts, rows, tc, window)
                k = 1
                while k < window:
                    acc = acc + pltpu.roll(acc, rows - k, 0)
                    k *= 2
                du_ref[...] = (acc[:ts, :] - cur).astype(BF16)

    return pl.pallas_call(
        body, name=name,
        out_shape=jax.ShapeDtypeStruct((s, e), BF16),
        grid=(nst, ng, cpg),
        in_specs=[pl.BlockSpec((ts, gdim), lambda i, g, j: (i, g)),
                  pl.BlockSpec((POOL_HALO, gdim), lambda i, g, j: (jnp.minimum((i + 1) * hb, n_halo - 1), g)),
                  pl.BlockSpec((None, gdim, tc), lambda i, g, j: (g, 0, j))],
        out_specs=pl.BlockSpec((ts, tc), lambda i, g, j: (i, g * cpg + j)),
        compiler_params=_params("parallel", "parallel", "parallel"),
    )(dyr, dyr, w_t)


def _a_group_fwd(d, w, scale, gate, name, tm=1024):
    s, e = d.shape
    ng, g, _ = w.shape
    tm = _pick(s, tm)

    def body(d_ref, w_ref, s_ref, gate_ref, yr_ref, z_ref):
        yr = jnp.dot(d_ref[...], w_ref[...], preferred_element_type=F32)
        yr_ref[...] = yr.astype(yr_ref.dtype)
        gt = gate_ref[...].astype(F32)
        z_ref[...] = ((yr * s_ref[...]) * (gt * _sigmoid(gt))).astype(BF16)

    blk = lambda i, j: (i, j)
    return pl.pallas_call(
        body, name=name,
        out_shape=(jax.ShapeDtypeStruct((s, e), BF16), jax.ShapeDtypeStruct((s, e), BF16)),
        grid=(s // tm, ng),
        in_specs=[pl.BlockSpec((tm, g), blk), pl.BlockSpec((None, g, g), lambda i, j: (j, 0, 0)),
                  pl.BlockSpec((1, g), lambda i, j: (0, j)), pl.BlockSpec((tm, g), blk)],
        out_specs=(pl.BlockSpec((tm, g), blk), pl.BlockSpec((tm, g), blk)),
        compiler_params=_params("parallel", "parallel"),
    )(d, w, scale, gate)


def _silu_and_slope(gt):
    sg = _sigmoid(gt)
    return gt * sg, sg * (1.0 + gt * (1.0 - sg))


def _a_gate_epilogue(dz, tiles, vecs):
    yr, gt = tiles[0].astype(F32), tiles[1].astype(F32)
    sc = vecs[0]
    silu, slope = _silu_and_slope(gt)
    dy = dz * silu
    return dz * (yr * sc) * slope, dy * sc, dy * yr


def _gate_epilogue(dz, tiles, vecs):
    y, gt = tiles[0].astype(F32), tiles[1].astype(F32)
    silu, slope = _silu_and_slope(gt)
    return dz * y * slope, dz * silu


def _merge_gate_epilogue(dz, tiles, vecs, lses, col0):
    y, gt = tiles[0].astype(F32), tiles[1].astype(F32)
    silu, slope = _silu_and_slope(gt)
    dy = dz * silu
    w0, w1, w2 = _merge_weights_expanded(lses, col0, dz.shape[1])
    return dz * y * slope, w0 * dy, w1 * dy, w2 * dy


def _gate_fwd(y, gate, name, ts=512, tc=512):
    s, e = y.shape
    ts, tc = _pick(s, ts), _pick(e, tc)

    def body(y_ref, gate_ref, z_ref):
        gt = gate_ref[...].astype(F32)
        z_ref[...] = (y_ref[...].astype(F32) * (gt * _sigmoid(gt))).astype(BF16)

    blk = lambda i, j: (i, j)
    return pl.pallas_call(
        body, name=name,
        out_shape=jax.ShapeDtypeStruct((s, e), BF16),
        grid=(s // ts, e // tc),
        in_specs=[pl.BlockSpec((ts, tc), blk)] * 2,
        out_specs=pl.BlockSpec((ts, tc), blk),
        compiler_params=_params("parallel", "parallel"),
    )(y, gate)


def _merge_weights(l0, l1, l2):
    m = jnp.maximum(jnp.maximum(l0, l1), l2)
    e0, e1, e2 = jnp.exp(l0 - m), jnp.exp(l1 - m), jnp.exp(l2 - m)
    inv = 1.0 / (e0 + e1 + e2)
    return e0 * inv, e1 * inv, e2 * inv


def _expand_heads(w, col0, width):
    n_heads = w.shape[1]
    head_of_lane = (col0 + lax.broadcasted_iota(jnp.int32, (n_heads, width), 1)) // HEAD_DIM
    pick = jnp.where(head_of_lane == lax.broadcasted_iota(jnp.int32, (n_heads, width), 0), 1.0, 0.0).astype(BF16)
    high = w.astype(BF16)
    rest = (w - high.astype(F32)).astype(BF16)
    return (jnp.dot(high, pick, preferred_element_type=F32) + jnp.dot(rest, pick, preferred_element_type=F32))


def _merge_weights_expanded(lses, col0, width):
    return [_expand_heads(w, col0, width) for w in _merge_weights(*lses)]


def _merge_gate_fwd(outs, lses, gate, name, ts=512, tc=512):
    s, e = gate.shape
    n_heads = lses[0].shape[1]
    ts, tc = _pick(s, ts), _pick(e, tc)

    def body(o0, o1, o2, l0, l1, l2, gate_ref, y_ref, z_ref):
        w0, w1, w2 = _merge_weights_expanded([l0[...], l1[...], l2[...]], pl.program_id(1) * tc, tc)
        y = w0 * o0[...].astype(F32) + w1 * o1[...].astype(F32) + w2 * o2[...].astype(F32)
        y_ref[...] = y.astype(y_ref.dtype)
        gt = gate_ref[...].astype(F32)
        z_ref[...] = (y * (gt * _sigmoid(gt))).astype(BF16)

    blk = lambda i, j: (i, j)
    per_head = pl.BlockSpec((ts, n_heads), lambda i, j: (i, 0))
    return pl.pallas_call(
        body, name=name,
        out_shape=(jax.ShapeDtypeStruct((s, e), BF16), jax.ShapeDtypeStruct((s, e), BF16)),
        grid=(s // ts, e // tc),
        in_specs=[pl.BlockSpec((ts, tc), blk)] * 3 + [per_head] * 3 + [pl.BlockSpec((ts, tc), blk)],
        out_specs=(pl.BlockSpec((ts, tc), blk), pl.BlockSpec((ts, tc), blk)),
        compiler_params=_params("parallel", "parallel"),
    )(*outs, *lses, gate)


def _band(max_dist, width):
    row = lax.broadcasted_iota(jnp.int32, (2 * BLOCK, width), 0) & (BLOCK - 1)
    col = lax.broadcasted_iota(jnp.int32, (2 * BLOCK, width), 1)
    low = row if max_dist == BLOCK else row + 1
    return jnp.logical_and(col >= low, col <= row + BLOCK), col >= BLOCK


def _fill_bias(bias_ref, max_dist):
    @pl.when(jnp.logical_and(pl.program_id(0) == 0, jnp.logical_and(pl.program_id(1) == 0, pl.program_id(2) == 0)))
    def _():
        band, own = _band(max_dist, 2 * BLOCK)
        bias_ref[0] = jnp.where(band, 0.0, NEG)
        bias_ref[1] = jnp.where(jnp.logical_and(band, own), 0.0, NEG)


def _aligned(v):
    return v if isinstance(v, int) else pl.multiple_of(v, BLOCK)


def _stack_heads(x, lo):
    return jnp.concatenate([jnp.where(lo, x, 0.0), jnp.where(lo, 0.0, x)], axis=0).astype(BF16)


def _unstack_heads(x2, lo):
    return jnp.where(lo, x2[:BLOCK], x2[BLOCK:])


def _head_col(x, hm):
    return jnp.max(jnp.where(hm, x, NEG), axis=1, keepdims=True)


def _dot_nt(a, b):
    return lax.dot_general(a, b, (((1,), (1,)), ((), ())), preferred_element_type=F32)


def _dot_tn(a, b):
    return lax.dot_general(a, b, (((0,), (0,)), ((), ())), preferred_element_type=F32)


def _stream_view(a, dil):
    s, w = a.shape
    return a.reshape(s // (BLOCK * dil), dil, BLOCK, w)


def _fill_window(dst, halo_ref, cur_ref, n):
    dst[0:BLOCK, :] = halo_ref[0]
    for jc in range(n):
        dst[(jc + 1) * BLOCK:(jc + 2) * BLOCK, :] = cur_ref[jc]


def _attn_fwd(q, k, v, sinks, max_dist, rep, dil, out_dtype, name, tq=2048, per_head_lse=False):
    assert max_dist in (BLOCK - 1, BLOCK)
    s, w = q.shape
    l = s // dil
    n_pairs = w // LANES
    n_heads = 2 * n_pairs
    tq = _pick(l, tq)
    n = tq // BLOCK
    has_sink = sinks is not None
    scale = HEAD_DIM ** -0.5

    def body(*refs):
        if has_sink:
            sink_ref, refs = refs[0], refs[1:]
        q_ref, kc_ref, kh_ref, vc_ref, vh_ref, o_ref, lse_ref = refs[:7]
        refs = refs[7:]
        if per_head_lse:
            lseh_ref, refs = refs[0], refs[1:]
        kx, vx, bias_ref = refs
        i, p = pl.program_id(0), pl.program_id(2)
        _fill_window(kx, kh_ref, kc_ref, n)
        _fill_window(vx, vh_ref, vc_ref, n)
        if per_head_lse:
            @pl.when(p == 0)
            def _():
                lseh_ref[...] = jnp.zeros_like(lseh_ref)
            head_lane = lax.broadcasted_iota(jnp.int32, (BLOCK, n_heads), 1)
        lo = lax.broadcasted_iota(jnp.int32, (BLOCK, LANES), 1) < HEAD_DIM
        _fill_bias(bias_ref, max_dist)
        top = lax.broadcasted_iota(jnp.int32, (2 * BLOCK, 1), 0) < BLOCK

        per_step = 2 if n % 2 == 0 else 1

        def scores(j):
            r0 = _aligned(j * BLOCK)
            q2 = _stack_heads(q_ref[j].astype(F32) * scale, lo)
            first = jnp.logical_and(i == 0, j == 0).astype(jnp.int32)
            return _dot_nt(q2, kx[pl.ds(r0, 2 * BLOCK), :]) + bias_ref[first]

        def step(jj, carry):
            nxt = tuple(scores((jj + 1) * per_step + t) for t in range(per_step))
            for t in range(per_step):
                finish(jj * per_step + t, carry[t])
            return nxt

        def finish(j, s2):
            r0 = _aligned(j * BLOCK)
            vw = vx[pl.ds(r0, 2 * BLOCK), :]
            outs, lses = [], []
            for hh in range(2):
                s1 = s2[hh * BLOCK:(hh + 1) * BLOCK, :]
                m = jnp.max(s1, axis=1, keepdims=True)
                if has_sink:
                    sk = sink_ref[2 * p + hh]
                    m = jnp.maximum(m, sk)
                pr = jnp.exp(s1 - m)
                den = jnp.sum(pr, axis=1, keepdims=True)
                if has_sink:
                    den = den + jnp.exp(sk - m)
                outs.append(jnp.dot(pr.astype(BF16), vw, preferred_element_type=F32) * (1.0 / den))
                lses.append(m + jnp.log(den))
            o_ref[j] = jnp.where(lo, outs[0], outs[1]).astype(o_ref.dtype)
            lse_ref[j] = jnp.where(lo, lses[0], lses[1])
            if per_head_lse:
                lseh_ref[j] = jnp.where(head_lane == 2 * p, lses[0],
                                        jnp.where(head_lane == 2 * p + 1, lses[1], lseh_ref[j]))

        trips = n // per_step
        last = lax.fori_loop(0, trips - 1, step, tuple(scores(t) for t in range(per_step)))
        for t in range(per_step):
            finish((trips - 1) * per_step + t, last[t])

    cur = lambda i, r, p: (i, r, 0, p)
    kv_cur = lambda i, r, p: (i, r, 0, p // rep)
    kv_halo = lambda i, r, p: (jnp.maximum(i * n - 1, 0), r, 0, p // rep)
    big, small = (n, None, BLOCK, LANES), (1, None, BLOCK, LANES)
    in_specs = [pl.BlockSpec(big, cur), pl.BlockSpec(big, kv_cur), pl.BlockSpec(small, kv_halo),
                pl.BlockSpec(big, kv_cur), pl.BlockSpec(small, kv_halo)]
    q4, k4, v4 = _stream_view(q, dil), _stream_view(k, dil), _stream_view(v, dil)
    args = [q4, k4, k4, v4, v4]
    if has_sink:
        in_specs = [pl.BlockSpec(memory_space=pltpu.SMEM)] + in_specs
        args = [sinks] + args
    out_shape = [jax.ShapeDtypeStruct(q4.shape, out_dtype), jax.ShapeDtypeStruct(q4.shape, F32)]
    out_specs = [pl.BlockSpec(big, cur), pl.BlockSpec(big, cur)]
    if per_head_lse:
        out_shape.append(jax.ShapeDtypeStruct(q4.shape[:3] + (n_heads,), F32))
        out_specs.append(pl.BlockSpec((n, None, BLOCK, n_heads), lambda i, r, p: (i, r, 0, 0)))
    outs = pl.pallas_call(
        body, name=name,
        out_shape=tuple(out_shape),
        grid=(l // tq, dil, n_pairs),
        in_specs=in_specs,
        out_specs=tuple(out_specs),
        scratch_shapes=[pltpu.VMEM((tq + BLOCK, LANES), BF16), pltpu.VMEM((tq + BLOCK, LANES), BF16),
                        pltpu.VMEM((2, 2 * BLOCK, 2 * BLOCK), F32)],
        compiler_params=_params("arbitrary", "arbitrary", "arbitrary"),
    )(*args)
    res = [outs[0].reshape(s, w), outs[1].reshape(s, w)]
    if per_head_lse:
        res.append(outs[2].reshape(s, n_heads))
    return res


def _attn_bwd(q, k, v, do, y, lse, sinks, max_dist, rep, dil, name, tq=2048):
    s, w = q.shape
    l = s // dil
    n_pairs = w // LANES
    tq = _pick(l, tq)
    n = tq // BLOCK
    n_blk = l // BLOCK
    n_sb = l // tq
    has_sink = sinks is not None
    scale = HEAD_DIM ** -0.5
    kv_dtype = BF16
    ext = tq + BLOCK

    def body(*refs):
        if has_sink:
            sink_ref, refs = refs[0], refs[1:]
        (q_ref, qn_ref, kc_ref, kh_ref, vc_ref, vh_ref, do_ref, don_ref, y_ref, yn_ref,
         lse_ref, lsen_ref) = refs[:12]
        refs = refs[12:]
        dq_ref, dk_ref, dv_ref = refs[:3]
        refs = refs[3:]
        if has_sink:
            dsink_ref, refs = refs[0], refs[1:]
        kx, vx, bias_ref = refs[:3]
        if rep > 1:
            dk_acc, dv_acc = refs[3:]
        i, p = pl.program_id(0), pl.program_id(2)
        _fill_bias(bias_ref, max_dist)
        own_rows = (q_ref, do_ref, y_ref, lse_ref)
        next_rows = (qn_ref, don_ref, yn_ref, lsen_ref)
        _fill_window(kx, kh_ref, kc_ref, n)
        _fill_window(vx, vh_ref, vc_ref, n)
        if rep > 1:
            @pl.when(p % rep == 0)
            def _():
                dk_acc[...] = jnp.zeros_like(dk_acc)
                dv_acc[...] = jnp.zeros_like(dv_acc)
        lo = lax.broadcasted_iota(jnp.int32, (BLOCK, LANES), 1) < HEAD_DIM
        hi = jnp.logical_not(lo)
        top = lax.broadcasted_iota(jnp.int32, (2 * BLOCK, 1), 0) < BLOCK

        def rows_of(j):
            if isinstance(j, int) and j == n:
                return next_rows, 0
            return own_rows, j

        def front(j, width):
            (qr, dor, _, _), jb = rows_of(j)
            r0 = _aligned(j * BLOCK)
            first = jnp.logical_and(i == 0, j == 0).astype(jnp.int32)
            q2 = _stack_heads(qr[jb].astype(F32) * scale, lo)
            do2 = _stack_heads(dor[jb].astype(F32), lo)
            s2 = _dot_nt(q2, kx[pl.ds(r0, width), :]) + bias_ref[first, :, pl.ds(0, width)]
            return s2, _dot_nt(do2, vx[pl.ds(r0, width), :])

        row_lo = lax.broadcasted_iota(jnp.int32, (LANES, BLOCK), 0) < HEAD_DIM

        def stack_t(x):
            xt = x.T
            return jnp.concatenate([jnp.where(row_lo, xt, 0.0), jnp.where(row_lo, 0.0, xt)], axis=1).astype(BF16)

        def emit(jk, dk_t, dv_t):
            dk_blk, dv_blk = dk_t.T, dv_t.T
            if rep == 1:
                dk_ref[jk] = dk_blk.astype(dk_ref.dtype)
                dv_ref[jk] = dv_blk.astype(dv_ref.dtype)
            else:
                rows = pl.ds(_aligned(jk * BLOCK), BLOCK)
                dk_acc[rows, :] += dk_blk
                dv_acc[rows, :] += dv_blk

        def back(j, width, q_valid, s2, dp2, state):
            sink_acc, carry_k, carry_v = state
            (qr, dor, yr, lser), jb = rows_of(j)
            r0 = _aligned(j * BLOCK)
            qf, dof = qr[jb].astype(F32) * scale, dor[jb].astype(F32)
            yb, lseb = yr[jb].astype(F32), lser[jb]
            prod = dof * yb
            delta = jnp.concatenate([jnp.sum(jnp.where(lo, prod, 0.0), axis=1, keepdims=True),
                                     jnp.sum(jnp.where(lo, 0.0, prod), axis=1, keepdims=True)], axis=0)
            lse2 = jnp.concatenate([_head_col(lseb, lo), _head_col(lseb, hi)], axis=0)
            pr = jnp.exp(s2 - lse2)
            if q_valid is not True:
                pr = jnp.where(q_valid, pr, 0.0)
            ds = pr * (dp2 - delta)
            dk_t = jnp.dot(stack_t(qf), ds.astype(BF16), preferred_element_type=F32)
            dv_t = jnp.dot(stack_t(dof), pr.astype(BF16), preferred_element_type=F32)
            done_k, done_v = carry_k + dk_t[:, :BLOCK], carry_v + dv_t[:, :BLOCK]
            if isinstance(j, int):
                emit(j - 1, done_k, done_v)
            elif rep == 1:
                emit(jnp.maximum(j - 1, 0), done_k, done_v)
            else:
                keep = j > 0
                emit(jnp.maximum(j - 1, 0), jnp.where(keep, done_k, 0.0), jnp.where(keep, done_v, 0.0))
            if width == 2 * BLOCK:
                dq2 = jnp.dot(ds.astype(BF16), kx[pl.ds(r0, width), :], preferred_element_type=F32) * scale
                dq_ref[jb] = _unstack_heads(dq2, lo).astype(dq_ref.dtype)
                carry_k, carry_v = dk_t[:, BLOCK:], dv_t[:, BLOCK:]
            if has_sink:
                sk = jnp.where(top, sink_ref[2 * p], sink_ref[2 * p + 1])
                sink_acc = sink_acc - jnp.exp(sk - lse2) * delta
            return sink_acc, carry_k, carry_v

        per_step = 2 if n % 2 == 0 else 1

        def step(jj, state):
            fronts = [front(jj * per_step + t, 2 * BLOCK) for t in range(per_step)]
            for t in range(per_step):
                state = back(jj * per_step + t, 2 * BLOCK, True, *fronts[t], state)
            return state

        zero_blk = jnp.zeros((LANES, BLOCK), F32)
        state = lax.fori_loop(0, n // per_step, step, (jnp.zeros((2 * BLOCK, 1), F32), zero_blk, zero_blk))
        sink_acc = state[0]
        if n_sb > 1:
            back(n, BLOCK, i < n_sb - 1, *front(n, BLOCK), state)
        else:
            emit(n - 1, state[1], state[2])

        if rep > 1:
            @pl.when(p % rep == rep - 1)
            def _():
                for jc in range(n):
                    rows = slice(jc * BLOCK, (jc + 1) * BLOCK)
                    dk_ref[jc] = dk_acc[rows, :].astype(dk_ref.dtype)
                    dv_ref[jc] = dv_acc[rows, :].astype(dv_ref.dtype)
        if has_sink:
            rowi = lax.broadcasted_iota(jnp.int32, (8, LANES), 0)
            s0 = jnp.sum(sink_acc[:BLOCK], axis=0, keepdims=True)
            s1 = jnp.sum(sink_acc[BLOCK:], axis=0, keepdims=True)
            dsink_ref[...] = jnp.where(rowi == 0, s0, jnp.where(rowi == 1, s1, 0.0))

    cur = lambda i, r, p: (i, r, 0, p)
    nxt = lambda i, r, p: (jnp.minimum((i + 1) * n, n_blk - 1), r, 0, p)
    kv_cur = lambda i, r, p: (i, r, 0, p // rep)
    kv_halo = lambda i, r, p: (jnp.maximum(i * n - 1, 0), r, 0, p // rep)
    big, small = (n, None, BLOCK, LANES), (1, None, BLOCK, LANES)
    in_specs = [pl.BlockSpec(big, cur), pl.BlockSpec(small, nxt),
                pl.BlockSpec(big, kv_cur), pl.BlockSpec(small, kv_halo),
                pl.BlockSpec(big, kv_cur), pl.BlockSpec(small, kv_halo),
                pl.BlockSpec(big, cur), pl.BlockSpec(small, nxt),
                pl.BlockSpec(big, cur), pl.BlockSpec(small, nxt),
                pl.BlockSpec(big, cur), pl.BlockSpec(small, nxt)]
    q4, k4, v4, do4, y4, lse4 = [_stream_view(a, dil) for a in (q, k, v, do, y, lse)]
    args = [q4, q4, k4, k4, v4, v4, do4, do4, y4, y4, lse4, lse4]
    out_shape = [jax.ShapeDtypeStruct(q4.shape, BF16),
                 jax.ShapeDtypeStruct(k4.shape, kv_dtype), jax.ShapeDtypeStruct(v4.shape, kv_dtype)]
    out_specs = [pl.BlockSpec(big, cur), pl.BlockSpec(big, kv_cur), pl.BlockSpec(big, kv_cur)]
    if has_sink:
        in_specs = [pl.BlockSpec(memory_space=pltpu.SMEM)] + in_specs
        args = [sinks] + args
        out_shape.append(jax.ShapeDtypeStruct((n_sb, dil, n_pairs, 8, LANES), F32))
        out_specs.append(pl.BlockSpec((None, None, None, 8, LANES), lambda i, r, p: (i, r, p, 0, 0)))
    outs = pl.pallas_call(
        body, name=name,
        out_shape=tuple(out_shape),
        grid=(n_sb, dil, n_pairs),
        in_specs=in_specs,
        out_specs=tuple(out_specs),
        scratch_shapes=[pltpu.VMEM((ext, LANES), BF16), pltpu.VMEM((ext, LANES), BF16),
                        pltpu.VMEM((2, 2 * BLOCK, 2 * BLOCK), F32)]
        + ([pltpu.VMEM((tq, LANES), F32), pltpu.VMEM((tq, LANES), F32)] if rep > 1 else []),
        compiler_params=_params("arbitrary", "arbitrary", "arbitrary"),
    )(*args)
    grads =[outs[0].reshape(s, w), outs[1].reshape(k.shape), outs[2].reshape(v.shape)]
    if has_sink:
        grads.append(outs[3].sum(axis=(0, 1))[:, 0:2, 0].reshape(1, 2 * n_pairs))
    return grads


def _sum_slots(recv, name, ts=256):
    nd, r, c = recv.shape
    ts = _pick(r, ts, 8)

    def body(r_ref, o_ref):
        acc = r_ref[0].astype(F32)
        for dev in range(1, nd):
            acc = acc + r_ref[dev].astype(F32)
        o_ref[...] = acc

    return pl.pallas_call(
        body, name=name,
        out_shape=jax.ShapeDtypeStruct((r, c), F32),
        grid=(r // ts,),
        in_specs=[pl.BlockSpec((nd, ts, c), lambda i: (0, i, 0))],
        out_specs=pl.BlockSpec((ts, c), lambda i: (i, 0)),
        compiler_params=_params("parallel"),
    )(recv)


def _adamw_math(w, g, m, v):
    c1 = 1.0 - ADAM_B1 ** ADAM_STEP
    c2 = 1.0 - ADAM_B2 ** ADAM_STEP
    m_ = ADAM_B1 * m + (1.0 - ADAM_B1) * g
    v_ = ADAM_B2 * v + (1.0 - ADAM_B2) * (g * g)
    return -ADAM_LR * ((m_ / c1) / (jnp.sqrt(v_ / c2) + ADAM_EPS) + ADAM_WD * w), m_, v_


def _row_tile(r, c, budget=1 << 18):
    return _pick(r, max(8, min(256, budget // c // 8 * 8)), 8)


def _adamw(w, g, m, v, name):
    r, c = w.shape
    ts = _row_tile(r, c)

    def body(w_ref, g_ref, m_ref, v_ref, d_ref, mo_ref, vo_ref):
        d_ref[...], mo_ref[...], vo_ref[...] = _adamw_math(w_ref[...], g_ref[...], m_ref[...], v_ref[...])

    blk = pl.BlockSpec((ts, c), lambda i: (i, 0))
    return pl.pallas_call(
        body, name=name,
        out_shape=tuple([jax.ShapeDtypeStruct((r, c), F32)] * 3),
        grid=(r // ts,),
        in_specs=[blk] * 4,
        out_specs=(blk, blk, blk),
        compiler_params=_params("parallel"),
    )(w, g, m, v)


def _adamw_slots(w, slots, m, v, name):
    r, c = w.shape
    nd = slots.shape[0]
    ts = _row_tile(r, c)

    def body(w_ref, s_ref, m_ref, v_ref, g_ref, d_ref, mo_ref, vo_ref):
        g = s_ref[0].astype(F32)
        for slot in range(1, nd):
            g = g + s_ref[slot].astype(F32)
        g_ref[...] = g
        d_ref[...], mo_ref[...], vo_ref[...] = _adamw_math(w_ref[...], g, m_ref[...], v_ref[...])

    blk = pl.BlockSpec((ts, c), lambda i: (i, 0))
    return pl.pallas_call(
        body, name=name,
        out_shape=tuple([jax.ShapeDtypeStruct((r, c), F32)] * 4),
        grid=(r // ts,),
        in_specs=[blk, pl.BlockSpec((nd, ts, c), lambda i: (0, i, 0)), blk, blk],
        out_specs=(blk, blk, blk, blk),
        compiler_params=_params("parallel"),
    )(w, slots, m, v)


def _rows(a):
    flat = a.reshape(-1)
    pad = (-flat.shape[0]) % PACK_W
    if pad:
        flat = jnp.concatenate([flat, jnp.zeros((pad,), flat.dtype)])
    return flat.reshape(-1, PACK_W)


def _pad_rows(a, mult):
    pad = (-a.shape[-2]) % mult
    if pad:
        widths = [(0, 0)] * (a.ndim - 2) + [(0, pad), (0, 0)]
        a = jnp.pad(a, widths)
    return a


def _to_global(stack, axis):
    moved = jnp.moveaxis(stack, 0, axis)
    shp = list(moved.shape)
    shp[axis:axis + 2] = [shp[axis] * shp[axis + 1]]
    return moved.reshape(shp)


def _to_stack(full, axis):
    shp = list(full.shape)
    shp[axis:axis + 1] = [N_DEV, shp[axis] // N_DEV]
    return jnp.moveaxis(full.reshape(shp), axis, 0)


_BIG = (("w_out", 1), ("a_w_in", 2), ("a_w_group", 2), ("b_w_in", 2), ("c_w_in", 2))


def _dup_heads(wk, n_kv):
    d = wk.shape[0]
    return jnp.tile(wk.reshape(d, n_kv, 1, HEAD_DIM), (1, 1, 2, 1)).reshape(d, n_kv * LANES)


def _fold_heads(dwk, n_kv):
    d = dwk.shape[0]
    folded = dwk.astype(F32).reshape(d, n_kv, 2, HEAD_DIM).sum(axis=2)
    return folded.reshape(d, n_kv * HEAD_DIM).astype(dwk.dtype)


def _perm(a, dil):
    if dil == 1:
        return a
    s, w = a.shape
    return a.reshape(s // (BLOCK * dil), BLOCK, dil, w).transpose(0, 2, 1, 3).reshape(s, w)


def _unperm(a, dil):
    if dil == 1:
        return a
    s, w = a.shape
    return a.reshape(s // (BLOCK * dil), dil, BLOCK, w).transpose(0, 2, 1, 3).reshape(s, w)


def kernel(x, norm_g, final_g, w_out, a_w_in, a_w_group, a_scale, b_w_in, b_sinks, c_w_in, loss_target, m_norm_g, m_final_g, m_w_out, m_a_w_in, m_a_w_group, m_a_scale, m_b_w_in, m_b_sinks, m_c_w_in, v_norm_g, v_final_g, v_w_out, v_a_w_in, v_a_w_group, v_a_scale, v_b_w_in, v_b_sinks, v_c_w_in):
    local = dict(w_out=w_out, a_w_in=a_w_in, a_w_group=a_w_group, b_w_in=b_w_in, c_w_in=c_w_in)
    mom_m = dict(w_out=m_w_out, a_w_in=m_a_w_in, a_w_group=m_a_w_group, b_w_in=m_b_w_in, c_w_in=m_c_w_in)
    mom_v = dict(w_out=v_w_out, a_w_in=v_a_w_in, a_w_group=v_a_w_group, b_w_in=v_b_w_in, c_w_in=v_c_w_in)
    s, d = x.shape[1], x.shape[2]
    depth = norm_g.shape[0]
    e = w_out.shape[1] * N_DEV
    n_heads = e // HEAD_DIM
    n_kv = n_heads // Q_PER_KV
    kv_w = n_kv * HEAD_DIM
    rep = Q_PER_KV // 2
    n_groups = len(POOL_WINDOWS)
    me = 4 * lax.axis_index("x") + 2 * lax.axis_index("y") + lax.axis_index("c")

    flat = {n: local[n].reshape(-1, local[n].shape[-1]) for n, _ in _BIG}
    spack = _pad_rows(_rows(a_scale), 8)
    *walls, sall = _gather([flat[n].astype(BF16) for n, _ in _BIG] + [spack], [True] * len(_BIG) + [False],
                           "gather_weights")
    full = {}
    for k, (name, axis) in enumerate(_BIG):
        full[name] = _to_global(walls[k].reshape((N_DEV,) + local[name].shape), axis)
    scale_full = _to_global(sall.reshape(N_DEV, -1)[:, :a_scale.size].reshape((N_DEV,) + a_scale.shape), 1)

    wout_t = jnp.swapaxes(full["w_out"], 1, 2)
    wa = full["a_w_in"]
    wa_t = jnp.swapaxes(wa, 1, 2)
    wg = full["a_w_group"]
    wg_t = jnp.swapaxes(wg, 2, 3)
    wb = full["b_w_in"][0]
    wb_ext = jnp.concatenate([wb[:, :e], _dup_heads(wb[:, e:e + kv_w], n_kv),
                              _dup_heads(wb[:, e + kv_w:e + 2 * kv_w], n_kv), wb[:, e + 2 * kv_w:]], axis=1)
    wb_ext_t = wb_ext.T
    kd_w = n_kv * LANES
    wc = full["c_w_in"][0]
    wc_t = wc.T

    xs, hs, zs, saved = [x.reshape(s, d)], [], [], []
    hs.append(_rmsnorm_fwd(xs[0], norm_g[0:1], "norm0"))
    loss_vec = dfinal = dx = dxb = None
    for i in range(depth):
        kind, j = i % 3, i // 3
        h = hs[i]
        tag = f"l{i}"
        if kind == 0:
            dpool = _proj_pool_fwd(h, wa[j][:, :e], tag + "_in_pool")
            gate = _matmul(h, wa[j][:, e:], BF16, tag + "_in_gate")
            yr, z = _a_group_fwd(dpool, wg[j], scale_full[j:j + 1], gate, tag + "_group")
            saved.append(dict(dpool=dpool, yr=yr, gate=gate))
        elif kind == 1:
            q = _matmul(h, wb_ext[:, :e], BF16, tag + "_in_q")
            kd = _matmul(h, wb_ext[:, e:e + kd_w], BF16, tag + "_in_k")
            vd = _matmul(h, wb_ext[:, e + kd_w:e + 2 * kd_w], BF16, tag + "_in_v")
            gate = _matmul(h, wb_ext[:, e + 2 * kd_w:], BF16, tag + "_in_gate")
            sinks = b_sinks[j]
            y, lse = _attn_fwd(q, kd, vd, sinks, SWA_MAX_DIST, rep, 1, BF16, tag + "_attn")
            z = _gate_fwd(y, gate, tag + "_gate")
            saved.append(dict(q=q, kd=kd, vd=vd, gate=gate, y=y, lse=lse, sinks=sinks))
        else:
            qkv, outs, lses, lses_tok, h_perm = [], [], [], [], []
            for gi, (window, dil) in enumerate(DILATED_PAIRS):
                hp = _perm(h, dil)
                trio = [_matmul(hp, wc[:, (3 * gi + t) * e:(3 * gi + t + 1) * e], BF16,
                                f"{tag}_in_{'qkv'[t]}{gi}") for t in range(3)]
                o, lse, lse_heads = _attn_fwd(trio[0], trio[1], trio[2], None, window // dil, 1, dil, BF16,
                                              f"{tag}_attn{gi}", per_head_lse=True)
                qkv.append(trio)
                h_perm.append(hp)
                outs.append(_unperm(o, dil))
                lses.append(lse)
                lses_tok.append(_unperm(lse_heads, dil))
            gate = _matmul(h, wc[:, 9 * e:], BF16, tag + "_in_gate")
            y, z = _merge_gate_fwd(outs, lses_tok, gate, tag + "_merge")
            saved.append(dict(qkv=qkv, lses=lses, lses_tok=lses_tok, gate=gate, y=y, h_perm=h_perm))
        zs.append(z)
        if i + 1 < depth:
            x_new, h_new = _outproj_norm(z, full["w_out"][i], xs[i], norm_g[i + 1:i + 2], tag + "_out")
            xs.append(x_new)
            hs.append(h_new)
        else:
            dx, dxb, dfinal, loss_vec = _outproj_loss(z, full["w_out"][i], xs[i], final_g.reshape(1, d),
                                                      loss_target.reshape(s, d), tag + "_out_loss")

    g_full = {"w_out": [None] * depth, "a_w_in": [None] * wa.shape[0], "a_w_group": [None] * wa.shape[0]}
    d_norm = [None] * depth
    d_scale = [None] * wa.shape[0]
    d_sinks = None
    for i in reversed(range(depth)):
        kind, j = i % 3, i // 3
        tag = f"b{i}"
        sv = saved[i]
        g_full["w_out"][i] = _matmul_tn(zs[i], dxb, tag + "_dwout", out_dtype=BF16)
        if kind == 0:
            dgate, dyr, dsc = _dz_fused(dxb, wout_t[i], [sv["yr"], sv["gate"]], [scale_full[j:j + 1]], 2,
                                        _a_gate_epilogue, tag + "_dz_gate", 1024, 1024, with_col_sum=True)
            d_scale[j] = dsc
            du = _group_pool_bwd(dyr, wg_t[j], tag + "_dd_pool")
            g_full["a_w_group"][j] = _grouped_weight_grad(sv["dpool"], dyr, n_groups, tag + "_dwg")
            parts = [du, dgate]
            g_full["a_w_in"][j] = jnp.concatenate(
                [_matmul_tn(hs[i], part, f"{tag}_dwin{t}", out_dtype=BF16) for t, part in enumerate(parts)], axis=1)
            dhs = [_matmul_cat(parts, wa_t[j], F32, tag + "_dh")]
        elif kind == 1:
            dgate, do = _dz_fused(dxb, wout_t[i], [sv["y"], sv["gate"]], [], 2, _gate_epilogue,
                                  tag + "_dz_gate", 1024, 1024)
            dq, dkd, dvd, d_sinks = _attn_bwd(sv["q"], sv["kd"], sv["vd"], do, sv["y"], sv["lse"], sv["sinks"],
                                              SWA_MAX_DIST, rep, 1, tag + "_attn")
            parts = [dq, dkd, dvd, dgate]
            dws = [_matmul_tn(hs[i], part, f"{tag}_dwin{t}", out_dtype=BF16) for t, part in enumerate(parts)]
            g_full["b_w_in"] = jnp.concatenate(
                [dws[0], _fold_heads(dws[1], n_kv), _fold_heads(dws[2], n_kv), dws[3]], axis=1)[None]
            dhs = [_matmul_cat(parts, wb_ext_t, F32, tag + "_dh")]
        else:
            dgate, *dos = _dz_fused(dxb, wout_t[i], [sv["y"], sv["gate"]], [], 4, _merge_gate_epilogue,
                                    tag + "_dz_merge", 1024, 512, rows=sv["lses_tok"])
            y_bf = sv["y"]
            dws, dhs = [], []
            for gi, (window, dil) in enumerate(DILATED_PAIRS):
                qv, kv, vv = sv["qkv"][gi]
                grads = _attn_bwd(qv, kv, vv, _perm(dos[gi], dil), _perm(y_bf, dil), sv["lses"][gi], None,
                                  window // dil, 1, dil, f"{tag}_attn{gi}")
                dws += [_matmul_tn(sv["h_perm"][gi], part, f"{tag}_dwin{gi}{'qkv'[t]}", out_dtype=BF16)
                        for t, part in enumerate(grads)]
                dhs.append(_unperm(_matmul_cat(grads, wc_t[3 * gi * e:3 * (gi + 1) * e], F32, f"{tag}_dh{gi}"),
                                   dil))
            dws.append(_matmul_tn(hs[i], dgate, tag + "_dwin_gate", out_dtype=BF16))
            dhs.append(_matmul(dgate, wc_t[9 * e:], F32, tag + "_dh_gate"))
            g_full["c_w_in"] = jnp.concatenate(dws, axis=1)[None]
        dx, dxb, d_norm[i] = _rmsnorm_bwd(dhs, xs[i], norm_g[i:i + 1], dx, tag + "_norm")
    grad_x = dx.reshape(x.shape)
    for name in ("w_out", "a_w_in", "a_w_group"):
        g_full[name] = jnp.stack(g_full[name], axis=0)

    stacks = [_to_stack(g_full[n], axis).astype(BF16).reshape((N_DEV,) + flat[n].shape) for n, axis in _BIG]
    loss_local = (0.5 / d) * jnp.sum(loss_vec)
    small = [jnp.concatenate(d_norm, axis=0), dfinal, d_sinks, jnp.concatenate(d_scale, axis=0),
             loss_local.reshape(1, 1)]
    small_rows = [_rows(a) for a in small]
    small_offs = [sum(r.shape[0] for r in small_rows[:k]) for k in range(len(small_rows) + 1)]
    small_pack = _pad_rows(jnp.concatenate(small_rows, axis=0), 8)
    core = lax.axis_index("c").astype(jnp.int32).reshape(1)
    from_sibling = _sibling_exchange(stacks, "exchange_sibling")
    chip_sums = [_pair_sum(stacks[k], from_sibling[k], core, "sum_pair_" + n) for k, (n, _) in enumerate(_BIG)]
    grecv, srecv = _chip_exchange(chip_sums, small_pack, "exchange_chips")
    ssum = _sum_slots(srecv, "sum_small")

    def small_part(k, like):
        return ssum[small_offs[k]:small_offs[k + 1]].reshape(-1)[:like.size].reshape(like.shape)

    g_norm = small_part(0, norm_g)
    g_final = small_part(1, final_g)
    g_sinks = small_part(2, b_sinks)
    g_scale_full = small_part(3, scale_full)
    loss = ssum[small_offs[4], 0]
    g_scale = lax.dynamic_slice_in_dim(g_scale_full, me * a_scale.shape[1], a_scale.shape[1], axis=1)

    small_w = [("norm_g", norm_g, m_norm_g, v_norm_g, g_norm), ("final_g", final_g, m_final_g, v_final_g, g_final),
               ("a_scale", a_scale, m_a_scale, v_a_scale, g_scale), ("b_sinks", b_sinks, m_b_sinks, v_b_sinks, g_sinks)]
    tail = lambda idx: _pad_rows(jnp.concatenate([_rows(t[idx]) for t in small_w], axis=0), 8)
    tail_sizes = [_rows(t[1]).shape[0] for t in small_w]
    tail_offs = [sum(tail_sizes[:k]) for k in range(len(tail_sizes) + 1)]
    g_tail = tail(4)
    tails = (g_tail,) + _adamw(tail(1), g_tail, tail(2), tail(3), "adamw_small")
    grads, deltas, new_m, new_v = {}, {}, {}, {}
    for k, (name, w_, _, _, _) in enumerate(small_w):
        for out, packed in zip((grads, deltas, new_m, new_v), tails):
            out[name] = packed[tail_offs[k]:tail_offs[k + 1]].reshape(-1)[:w_.size].reshape(w_.shape)
    for k, (name, _) in enumerate(_BIG):
        shape2d = flat[name].shape
        res = _adamw_slots(flat[name], grecv[k], mom_m[name].reshape(shape2d), mom_v[name].reshape(shape2d),
                           "adamw_" + name)
        for out, val in zip((grads, deltas, new_m, new_v), res):
            out[name] = val.reshape(local[name].shape)

    order = ("norm_g", "final_g", "w_out", "a_w_in", "a_w_group", "a_scale", "b_w_in", "b_sinks", "c_w_in")
    return (loss, grad_x, *[grads[n] for n in order], *[deltas[n] for n in order],
            *[new_m[n] for n in order], *[new_v[n] for n in order])
```

```python
import functools

import jax
import jax.numpy as jnp
from jax import lax
from jax.experimental import pallas as pl
from jax.experimental.pallas import tpu as pltpu

F32 = jnp.float32
BF16 = jnp.bfloat16

N_DEV = 8
HEAD_DIM = 64
LANES = 128
BLOCK = 128
Q_PER_KV = 8
POOL_WINDOWS = (2, 4, 8, 16)
POOL_HALO = 16
DILATED_PAIRS = ((128, 1), (512, 4), (2048, 16))
SWA_MAX_DIST = 127
RMS_EPS = 1e-5
PACK_W = 1024
NEG = -1e30

ADAM_LR = 0.001
ADAM_B1 = 0.9
ADAM_B2 = 0.999
ADAM_EPS = 1e-08
ADAM_WD = 0.01
ADAM_STEP = 10

VMEM_LIMIT = 48 * 1024 * 1024


def _params(*sem):
    return pltpu.CompilerParams(dimension_semantics=sem if sem else None, vmem_limit_bytes=VMEM_LIMIT)


def _pick(dim, target, mult=LANES):
    if dim <= target:
        return dim
    t = target - target % mult
    while dim % t:
        t -= mult
    return t


def _sigmoid(x):
    return 1.0 / (1.0 + jnp.exp(-x))


ANY_SPEC = pl.BlockSpec(memory_space=pl.ANY)


def _where_am_i():
    x, y, c = lax.axis_index("x"), lax.axis_index("y"), lax.axis_index("c")
    return x, y, c, 4 * x + 2 * y + c


def _peer(x, y, c, r):
    return x ^ ((r >> 2) & 1), y ^ ((r >> 1) & 1), c ^ (r & 1)


GATHER_SEMS = 8


def _gather(blocks, split, name):
    n = len(blocks)
    halves = [b.shape[0] // 2 for b in blocks]

    def body(*refs):
        send, recv = refs[:n], refs[n:2 * n]
        send_sems, recv_sems, local_sems = refs[2 * n:]
        x, y, c, me = _where_am_i()
        sib, xn, yn, dg = (_peer(x, y, c, r) for r in (1, 4, 2, 6))
        sib_id, xn_id, yn_id, dg_id = me ^ 1, me ^ 4, me ^ 2, me ^ 6

        def copy(k, sem, src, dst, to):
            return pltpu.make_async_remote_copy(
                src_ref=src, dst_ref=dst, send_sem=send_sems.at[k, sem], recv_sem=recv_sems.at[k, sem],
                device_id=to, device_id_type=pl.DeviceIdType.MESH)

        def part(k, slot, half):
            return recv[k].at[slot, pl.ds(half * halves[k], halves[k])]

        started = []
        for k in range(n):
            own = pltpu.make_async_copy(send[k], recv[k].at[me], local_sems.at[k])
            own.start()
            started.append(own)
        sends = []
        for k in range(n):
            sends += [copy(k, 0, send[k], recv[k].at[me], sib), copy(k, 1, send[k], recv[k].at[me], xn),
                      copy(k, 2, send[k], recv[k].at[me], yn)]
            if not split[k]:
                sends.append(copy(k, 3, send[k], recv[k].at[me], dg))
        for cp in sends:
            cp.start()

        def after(k, sem, slot, hand_on_sem, half, half_sem, half_to):
            copy(k, sem, send[k], recv[k].at[slot], sib).wait_recv()
            new = [copy(k, hand_on_sem, recv[k].at[slot], recv[k].at[slot], sib)]
            if split[k]:
                new.append(copy(k, half_sem, part(k, slot, half), part(k, slot, half), half_to))
            for cp in new:
                cp.start()
            sends.extend(new)

        for k in range(n):
            after(k, 2, yn_id, 6, 0, 3, xn)
        for k in range(n):
            after(k, 1, xn_id, 5, 1, 4, yn)
        for k in range(n):
            if split[k]:
                copy(k, 3, part(k, dg_id, 0), part(k, dg_id, 0), sib).wait_recv()
                copy(k, 4, part(k, dg_id, 1), part(k, dg_id, 1), sib).wait_recv()
            else:
                copy(k, 3, send[k], recv[k].at[dg_id], sib).wait_recv()
            fwd = copy(k, 7, recv[k].at[dg_id], recv[k].at[dg_id], sib)
            fwd.start()
            sends.append(fwd)
        for k in range(n):
            copy(k, 0, send[k], recv[k].at[sib_id], sib).wait_recv()
            for sem, r in ((5, 4), (6, 2), (7, 6)):
                copy(k, sem, send[k], recv[k].at[sib_id ^ r], sib).wait_recv()
        for cp in sends:
            cp.wait_send()
        for own in started:
            own.wait()

    return pl.pallas_call(
        body, name=name,
        out_shape=tuple(jax.ShapeDtypeStruct((N_DEV,) + b.shape, b.dtype) for b in blocks),
        in_specs=[ANY_SPEC] * n,
        out_specs=tuple([ANY_SPEC] * n),
        scratch_shapes=[pltpu.SemaphoreType.DMA((n, GATHER_SEMS)), pltpu.SemaphoreType.DMA((n, GATHER_SEMS)),
                        pltpu.SemaphoreType.DMA((n,))],
    )(*blocks)


def _sibling_exchange(stacks, name):
    n_chips = N_DEV // 2
    n = len(stacks)

    def body(*refs):
        g_refs, t_refs = refs[:n], refs[n:2 * n]
        send_sems, recv_sems = refs[2 * n:]
        x, y, c, _ = _where_am_i()
        sib = _peer(x, y, c, 1)
        copies = [pltpu.make_async_remote_copy(
            src_ref=g_refs[k].at[2 * chip + (1 - c)], dst_ref=t_refs[k].at[chip], send_sem=send_sems.at[k, chip],
            recv_sem=recv_sems.at[k, chip], device_id=sib, device_id_type=pl.DeviceIdType.MESH)
            for k in range(n) for chip in range(n_chips)]
        for cp in copies:
            cp.start()
        for cp in copies:
            cp.wait_recv()
        for cp in copies:
            cp.wait_send()

    return pl.pallas_call(
        body, name=name,
        out_shape=tuple(jax.ShapeDtypeStruct((n_chips,) + g.shape[1:], g.dtype) for g in stacks),
        in_specs=[ANY_SPEC] * n, out_specs=tuple([ANY_SPEC] * n),
        scratch_shapes=[pltpu.SemaphoreType.DMA((n, n_chips)), pltpu.SemaphoreType.DMA((n, n_chips))],
    )(*stacks)


CHIP_SEMS = 6


def _chip_exchange(csums, small, name):
    n = len(csums)
    halves = [cs.shape[1] // 2 for cs in csums]

    def body(*refs):
        c_refs, s_ref = refs[:n], refs[n]
        r_refs, sr_ref = refs[n + 1:2 * n + 1], refs[2 * n + 1]
        stage_x, stage_y = refs[2 * n + 2:3 * n + 2], refs[3 * n + 2:4 * n + 2]
        send_sems, recv_sems, small_send, small_recv, local_sems = refs[4 * n + 2:]
        x, y, c, me = _where_am_i()
        my_chip = 2 * x + y
        xn, yn = _peer(x, y, c, 4), _peer(x, y, c, 2)
        xn_chip, yn_chip, dg_chip = my_chip ^ 2, my_chip ^ 1, my_chip ^ 3
        own = [pltpu.make_async_copy(c_refs[k].at[my_chip], r_refs[k].at[my_chip], local_sems.at[k])
               for k in range(n)]
        own.append(pltpu.make_async_copy(s_ref, sr_ref.at[me], local_sems.at[n]))
        for cp in own:
            cp.start()

        def copy(k, sem, src, dst, to):
            return pltpu.make_async_remote_copy(
                src_ref=src, dst_ref=dst, send_sem=send_sems.at[k, sem], recv_sem=recv_sems.at[k, sem],
                device_id=to, device_id_type=pl.DeviceIdType.MESH)

        def half(ref, k, slot, which):
            return ref.at[slot, pl.ds(which * halves[k], halves[k])]

        sends, recvs = [], []
        for k in range(n):
            sends += [copy(k, 0, c_refs[k].at[xn_chip], r_refs[k].at[my_chip], xn),
                      copy(k, 1, c_refs[k].at[yn_chip], r_refs[k].at[my_chip], yn),
                      copy(k, 2, half(c_refs[k], k, dg_chip, 0), stage_x[k], xn),
                      copy(k, 3, half(c_refs[k], k, dg_chip, 1), stage_y[k], yn)]
            recvs += [copy(k, 0, c_refs[k].at[xn_chip], r_refs[k].at[xn_chip], xn),
                      copy(k, 1, c_refs[k].at[yn_chip], r_refs[k].at[yn_chip], yn),
                      copy(k, 4, stage_x[k], half(r_refs[k], k, dg_chip, 0), yn),
                      copy(k, 5, stage_y[k], half(r_refs[k], k, dg_chip, 1), xn)]
        for r in range(1, N_DEV):
            to = _peer(x, y, c, r)
            sends.append(pltpu.make_async_remote_copy(
                src_ref=s_ref, dst_ref=sr_ref.at[me], send_sem=small_send.at[r - 1],
                recv_sem=small_recv.at[r - 1], device_id=to, device_id_type=pl.DeviceIdType.MESH))
            recvs.append(pltpu.make_async_remote_copy(
                src_ref=s_ref, dst_ref=sr_ref.at[me ^ r], send_sem=small_send.at[r - 1],
                recv_sem=small_recv.at[r - 1], device_id=to, device_id_type=pl.DeviceIdType.MESH))
        for cp in sends:
            cp.start()
        for k in range(n):
            copy(k, 2, stage_x[k], stage_x[k], xn).wait_recv()
            fwd = copy(k, 4, stage_x[k], half(r_refs[k], k, xn_chip, 0), yn)
            fwd.start()
            sends.append(fwd)
        for k in range(n):
            copy(k, 3, stage_y[k], stage_y[k], yn).wait_recv()
            fwd = copy(k, 5, stage_y[k], half(r_refs[k], k, yn_chip, 1), xn)
            fwd.start()
            sends.append(fwd)
        for cp in recvs:
            cp.wait_recv()
        for cp in sends:
            cp.wait_send()
        for cp in own:
            cp.wait()

    stages = tuple(jax.ShapeDtypeStruct((h, cs.shape[2]), cs.dtype) for h, cs in zip(halves, csums))
    outs = pl.pallas_call(
        body, name=name,
        out_shape=tuple(jax.ShapeDtypeStruct(cs.shape, cs.dtype) for cs in csums)
        + (jax.ShapeDtypeStruct((N_DEV,) + small.shape, small.dtype),) + stages + stages,
        in_specs=[ANY_SPEC] * (n + 1), out_specs=tuple([ANY_SPEC] * (3 * n + 1)),
        scratch_shapes=[pltpu.SemaphoreType.DMA((n, CHIP_SEMS)), pltpu.SemaphoreType.DMA((n, CHIP_SEMS)),
                        pltpu.SemaphoreType.DMA((N_DEV - 1,)), pltpu.SemaphoreType.DMA((N_DEV - 1,)),
                        pltpu.SemaphoreType.DMA((n + 1,))],
    )(*csums, small)
    return list(outs[:n]), outs[n]


def _pair_sum(gpack, other, core, name, ts=256):
    n_chips, r, c = other.shape
    ts = _pick(r, ts, 16)

    def body(core_ref, g_ref, o_ref, out_ref):
        del core_ref
        out_ref[...] = (g_ref[...].astype(F32) + o_ref[...].astype(F32)).astype(out_ref.dtype)

    return pl.pallas_call(
        body, name=name,
        out_shape=jax.ShapeDtypeStruct(other.shape, other.dtype),
        grid_spec=pltpu.PrefetchScalarGridSpec(
            num_scalar_prefetch=1, grid=(n_chips, r // ts),
            in_specs=[pl.BlockSpec((None, ts, c), lambda j, i, core_ref: (2 * j + core_ref[0], i, 0)),
                      pl.BlockSpec((None, ts, c), lambda j, i, core_ref: (j, i, 0))],
            out_specs=pl.BlockSpec((None, ts, c), lambda j, i, core_ref: (j, i, 0))),
        compiler_params=_params("parallel", "parallel"),
    )(core, gpack, other)


def _matmul(a, b, out_dtype, name, tm=2048, tn=1024, tk=1024):
    m, kdim = a.shape
    n = b.shape[1]
    tm, tn, tk = _pick(m, tm), _pick(n, tn), _pick(kdim, tk)
    nk = kdim // tk

    if nk == 1:
        def body(a_ref, b_ref, o_ref):
            o_ref[...] = jnp.dot(a_ref[...], b_ref[...], preferred_element_type=F32).astype(o_ref.dtype)
        scratch = []
    else:
        def body(a_ref, b_ref, o_ref, acc_ref):
            kk = pl.program_id(2)

            @pl.when(kk == 0)
            def _():
                acc_ref[...] = jnp.zeros_like(acc_ref)

            acc_ref[...] += jnp.dot(a_ref[...], b_ref[...], preferred_element_type=F32)

            @pl.when(kk == nk - 1)
            def _():
                o_ref[...] = acc_ref[...].astype(o_ref.dtype)
        scratch = [pltpu.VMEM((tm, tn), F32)]

    return pl.pallas_call(
        body, name=name,
        out_shape=jax.ShapeDtypeStruct((m, n), out_dtype),
        grid=(m // tm, n // tn, nk),
        in_specs=[pl.BlockSpec((tm, tk), lambda i, j, k: (i, k)),
                  pl.BlockSpec((tk, tn), lambda i, j, k: (k, j))],
        out_specs=pl.BlockSpec((tm, tn), lambda i, j, k: (i, j)),
        scratch_shapes=scratch,
        compiler_params=_params("parallel", "parallel", "arbitrary"),
    )(a, b)


def _matmul_cat(parts, b, out_dtype, name, tm=1024, tn=1024, tk=1024):
    m = parts[0].shape[0]
    n = b.shape[1]
    tm, tn = _pick(m, tm), _pick(n, tn)
    tk = min(_pick(p.shape[1], tk) for p in parts)
    steps = [p.shape[1] // tk for p in parts]
    assert all(p.shape[1] % tk == 0 for p in parts)
    starts = [sum(steps[:t]) for t in range(len(parts))]
    nk = sum(steps)
    n_parts = len(parts)

    def body(*refs):
        a_refs, b_ref, o_ref, acc_ref = refs[:n_parts], refs[n_parts], refs[n_parts + 1], refs[n_parts + 2]
        kk = pl.program_id(2)

        @pl.when(kk == 0)
        def _():
            acc_ref[...] = jnp.zeros_like(acc_ref)

        for t in range(n_parts):
            @pl.when(jnp.logical_and(kk >= starts[t], kk < starts[t] + steps[t]))
            def _(t=t):
                acc_ref[...] += jnp.dot(a_refs[t][...], b_ref[...], preferred_element_type=F32)

        @pl.when(kk == nk - 1)
        def _():
            o_ref[...] = acc_ref[...].astype(o_ref.dtype)

    def part_map(t):
        return lambda i, j, k: (i, jnp.clip(k - starts[t], 0, steps[t] - 1))

    return pl.pallas_call(
        body, name=name,
        out_shape=jax.ShapeDtypeStruct((m, n), out_dtype),
        grid=(m // tm, n // tn, nk),
        in_specs=[pl.BlockSpec((tm, tk), part_map(t)) for t in range(n_parts)]
        + [pl.BlockSpec((tk, tn), lambda i, j, k: (k, j))],
        out_specs=pl.BlockSpec((tm, tn), lambda i, j, k: (i, j)),
        scratch_shapes=[pltpu.VMEM((tm, tn), F32)],
        compiler_params=_params("parallel", "parallel", "arbitrary"),
    )(*parts, b)


def _matmul_tn(a, b, name, tm=1024, tn=1024, tk=2048, out_dtype=F32):
    kdim, m = a.shape
    n = b.shape[1]
    tm, tn, tk = _pick(m, tm), _pick(n, tn), _pick(kdim, tk)
    nk = kdim // tk

    def body(a_ref, b_ref, o_ref, acc_ref):
        kk = pl.program_id(2)

        @pl.when(kk == 0)
        def _():
            acc_ref[...] = jnp.zeros_like(acc_ref)

        acc_ref[...] += lax.dot_general(a_ref[...], b_ref[...], (((0,), (0,)), ((), ())),
                                        preferred_element_type=F32)

        @pl.when(kk == nk - 1)
        def _():
            o_ref[...] = acc_ref[...].astype(o_ref.dtype)

    return pl.pallas_call(
        body, name=name,
        out_shape=jax.ShapeDtypeStruct((m, n), out_dtype),
        grid=(m // tm, n // tn, nk),
        in_specs=[pl.BlockSpec((tk, tm), lambda i, j, k: (k, i)),
                  pl.BlockSpec((tk, tn), lambda i, j, k: (k, j))],
        out_specs=pl.BlockSpec((tm, tn), lambda i, j, k: (i, j)),
        scratch_shapes=[pltpu.VMEM((tm, tn), F32)],
        compiler_params=_params("parallel", "parallel", "arbitrary"),
    )(a, b)


def _grouped_weight_grad(a, b, ng, name, tk=1024):
    s, e = a.shape
    g = e // ng
    tk = _pick(s, tk)
    nk = s // tk

    def body(a_ref, b_ref, o_ref):
        kk = pl.program_id(1)

        @pl.when(kk == 0)
        def _():
            o_ref[...] = jnp.zeros_like(o_ref)

        o_ref[...] += lax.dot_general(a_ref[...], b_ref[...], (((0,), (0,)), ((), ())),
                                      preferred_element_type=F32)

    return pl.pallas_call(
        body, name=name,
        out_shape=jax.ShapeDtypeStruct((ng, g, g), F32),
        grid=(ng, nk),
        in_specs=[pl.BlockSpec((tk, g), lambda j, k: (k, j)),
                  pl.BlockSpec((tk, g), lambda j, k: (k, j))],
        out_specs=pl.BlockSpec((None, g, g), lambda j, k: (j, 0, 0)),
        compiler_params=_params("parallel", "arbitrary"),
    )(a, b)


def _rms(x):
    r = lax.rsqrt(jnp.mean(x * x, axis=1, keepdims=True) + RMS_EPS)
    return x * r, r


def _rmsnorm_fwd(x, g, name, ts=256):
    s, d = x.shape
    ts = _pick(s, ts, 8)

    def body(x_ref, g_ref, h_ref):
        xhat, _ = _rms(x_ref[...])
        h_ref[...] = (xhat * g_ref[...]).astype(BF16)

    return pl.pallas_call(
        body, name=name,
        out_shape=jax.ShapeDtypeStruct((s, d), BF16),
        grid=(s // ts,),
        in_specs=[pl.BlockSpec((ts, d), lambda i: (i, 0)), pl.BlockSpec((1, d), lambda i: (0, 0))],
        out_specs=pl.BlockSpec((ts, d), lambda i: (i, 0)),
        compiler_params=_params("parallel"),
    )(x, g)


def _outproj_norm(z, w, x, g, name, tm=512):
    s, e = z.shape
    d = w.shape[1]
    tm = _pick(s, tm)

    def body(z_ref, w_ref, x_ref, g_ref, xo_ref, h_ref):
        xn = x_ref[...] + jnp.dot(z_ref[...], w_ref[...], preferred_element_type=F32)
        xo_ref[...] = xn
        xhat, _ = _rms(xn)
        h_ref[...] = (xhat * g_ref[...]).astype(BF16)

    return pl.pallas_call(
        body, name=name,
        out_shape=(jax.ShapeDtypeStruct((s, d), F32), jax.ShapeDtypeStruct((s, d), BF16)),
        grid=(s // tm,),
        in_specs=[pl.BlockSpec((tm, e), lambda i: (i, 0)), pl.BlockSpec((e, d), lambda i: (0, 0)),
                  pl.BlockSpec((tm, d), lambda i: (i, 0)), pl.BlockSpec((1, d), lambda i: (0, 0))],
        out_specs=(pl.BlockSpec((tm, d), lambda i: (i, 0)), pl.BlockSpec((tm, d), lambda i: (i, 0))),
        compiler_params=_params("parallel"),
    )(z, w, x, g)


def _outproj_loss(z, w, x, g, target, name, tm=512):
    s, e = z.shape
    d = w.shape[1]
    tm = _pick(s, tm)

    def body(z_ref, w_ref, x_ref, g_ref, t_ref, dx_ref, dxb_ref, dg_ref, loss_ref):
        i = pl.program_id(0)
        xn = x_ref[...] + jnp.dot(z_ref[...], w_ref[...], preferred_element_type=F32)
        xhat, r = _rms(xn)
        gain = g_ref[...]
        diff = xhat * gain - t_ref[...]
        dout = diff * (1.0 / d)
        dxhat = dout * gain
        dx = r * (dxhat - xhat * jnp.mean(dxhat * xhat, axis=1, keepdims=True))
        dx_ref[...] = dx
        dxb_ref[...] = dx.astype(BF16)

        @pl.when(i == 0)
        def _():
            dg_ref[...] = jnp.zeros_like(dg_ref)
            loss_ref[...] = jnp.zeros_like(loss_ref)

        dg_ref[...] += jnp.sum(dout * xhat, axis=0, keepdims=True)
        loss_ref[...] += jnp.sum(diff * diff, axis=0, keepdims=True)

    row = lambda i: (i, 0)
    fixed = lambda i: (0, 0)
    return pl.pallas_call(
        body, name=name,
        out_shape=(jax.ShapeDtypeStruct((s, d), F32), jax.ShapeDtypeStruct((s, d), BF16),
                   jax.ShapeDtypeStruct((1, d), F32), jax.ShapeDtypeStruct((1, d), F32)),
        grid=(s // tm,),
        in_specs=[pl.BlockSpec((tm, e), row), pl.BlockSpec((e, d), fixed), pl.BlockSpec((tm, d), row),
                  pl.BlockSpec((1, d), fixed), pl.BlockSpec((tm, d), row)],
        out_specs=(pl.BlockSpec((tm, d), row), pl.BlockSpec((tm, d), row),
                   pl.BlockSpec((1, d), fixed), pl.BlockSpec((1, d), fixed)),
        compiler_params=_params("arbitrary"),
    )(z, w, x, g, target)


def _rmsnorm_bwd(dhs, x, g, dx_next, name, ts=256):
    s, d = x.shape
    ts = _pick(s, ts, 8)
    n_dh = len(dhs)

    def body(*refs):
        dh_refs = refs[:n_dh]
        x_ref, g_ref, dn_ref, dx_ref, dxb_ref, dg_ref = refs[n_dh:]
        i = pl.program_id(0)
        xhat, r = _rms(x_ref[...])
        dh_ = dh_refs[0][...]
        for extra in dh_refs[1:]:
            dh_ = dh_ + extra[...]
        dxhat = dh_ * g_ref[...]
        dx = dn_ref[...] + r * (dxhat - xhat * jnp.mean(dxhat * xhat, axis=1, keepdims=True))
        dx_ref[...] = dx
        dxb_ref[...] = dx.astype(BF16)

        @pl.when(i == 0)
        def _():
            dg_ref[...] = jnp.zeros_like(dg_ref)

        dg_ref[...] += jnp.sum(dh_ * xhat, axis=0, keepdims=True)

    row = lambda i: (i, 0)
    fixed = lambda i: (0, 0)
    return pl.pallas_call(
        body, name=name,
        out_shape=(jax.ShapeDtypeStruct((s, d), F32), jax.ShapeDtypeStruct((s, d), BF16),
                   jax.ShapeDtypeStruct((1, d), F32)),
        grid=(s // ts,),
        in_specs=[pl.BlockSpec((ts, d), row)] * n_dh + [pl.BlockSpec((ts, d), row), pl.BlockSpec((1, d), fixed),
                                                        pl.BlockSpec((ts, d), row)],
        out_specs=(pl.BlockSpec((ts, d), row), pl.BlockSpec((ts, d), row), pl.BlockSpec((1, d), fixed)),
        compiler_params=_params("arbitrary"),
    )(*dhs, x, g, dx_next)


def _pool_counts(t0, rows, cols, window):
    t = t0 + lax.broadcasted_iota(jnp.int32, (rows, cols), 0)
    return jnp.minimum(t + 1, window).astype(F32)


def _proj_pool_fwd(h, w, name, ts=1024, tc=512):
    s, dm = h.shape
    e = w.shape[1]
    ng = len(POOL_WINDOWS)
    gdim = e // ng
    ts, tc = _pick(s, ts), _pick(gdim, tc)
    cpg = gdim // tc
    hb = ts // POOL_HALO

    def body(h_ref, halo_ref, w_ref, d_ref):
        i, grp = pl.program_id(0), pl.program_id(1)
        cur = jnp.dot(h_ref[...], w_ref[...], preferred_element_type=F32)
        halo = jnp.dot(halo_ref[...], w_ref[...], preferred_element_type=F32)
        ext = jnp.concatenate([jnp.where(i > 0, halo, 0.0), cur], axis=0)
        for gi, window in enumerate(POOL_WINDOWS):
            @pl.when(grp == gi)
            def _(window=window):
                acc = ext
                k = 1
                while k < window:
                    acc = acc + pltpu.roll(acc, k, 0)
                    k *= 2
                pooled = acc[POOL_HALO:, :] / _pool_counts(i * ts, ts, tc, window)
                d_ref[...] = (pooled - cur).astype(BF16)

    return pl.pallas_call(
        body, name=name,
        out_shape=jax.ShapeDtypeStruct((s, e), BF16),
        grid=(s // ts, ng, cpg),
        in_specs=[pl.BlockSpec((ts, dm), lambda i, g, j: (i, 0)),
                  pl.BlockSpec((POOL_HALO, dm), lambda i, g, j: (jnp.maximum(i * hb - 1, 0), 0)),
                  pl.BlockSpec((dm, tc), lambda i, g, j: (0, g * cpg + j))],
        out_specs=pl.BlockSpec((ts, tc), lambda i, g, j: (i, g * cpg + j)),
        compiler_params=_params("parallel", "parallel", "parallel"),
    )(h, h, w)


def _dz_fused(dxb, w_t, tiles, vecs, n_out, epilogue, name, tm, tn, with_col_sum=False, rows=()):
    s, dm = dxb.shape
    e = w_t.shape[1]
    tm, tn = _pick(s, tm), _pick(e, tn)
    n_t, n_v, n_r = len(tiles), len(vecs), len(rows)

    def body(*refs):
        a_ref, b_ref = refs[:2]
        tile_refs, vec_refs = refs[2:2 + n_t], refs[2 + n_t:2 + n_t + n_v]
        row_refs = refs[2 + n_t + n_v:2 + n_t + n_v + n_r]
        out_refs = refs[2 + n_t + n_v + n_r:]
        i = pl.program_id(1)
        dz = jnp.dot(a_ref[...], b_ref[...], preferred_element_type=F32)
        extra = ([r[...] for r in row_refs], pl.program_id(0) * tn) if n_r else ()
        res = epilogue(dz, [t[...] for t in tile_refs], [v[...] for v in vec_refs], *extra)
        for o_ref, val in zip(out_refs[:n_out], res[:n_out]):
            o_ref[...] = val.astype(o_ref.dtype)
        if with_col_sum:
            sum_ref = out_refs[n_out]

            @pl.when(i == 0)
            def _():
                sum_ref[...] = jnp.zeros_like(sum_ref)

            sum_ref[...] += jnp.sum(res[n_out], axis=0, keepdims=True)

    blk = lambda j, i: (i, j)
    vec = lambda j, i: (0, j)
    out_shape = [jax.ShapeDtypeStruct((s, e), BF16)] * n_out
    out_specs = [pl.BlockSpec((tm, tn), blk)] * n_out
    if with_col_sum:
        out_shape.append(jax.ShapeDtypeStruct((1, e), F32))
        out_specs.append(pl.BlockSpec((1, tn), vec))
    return pl.pallas_call(
        body, name=name,
        out_shape=tuple(out_shape),
        grid=(e // tn, s // tm),
        in_specs=[pl.BlockSpec((tm, dm), lambda j, i: (i, 0)), pl.BlockSpec((dm, tn), lambda j, i: (0, j))]
        + [pl.BlockSpec((tm, tn), blk)] * n_t + [pl.BlockSpec((1, tn), vec)] * n_v
        + [pl.BlockSpec((tm, r.shape[1]), lambda j, i: (i, 0)) for r in rows],
        out_specs=tuple(out_specs),
        compiler_params=_params("parallel", "arbitrary"),
    )(dxb, w_t, *tiles, *vecs, *rows)


def _group_pool_bwd(dyr, w_t, name, ts=1024, tc=512):
    s, e = dyr.shape
    ng = len(POOL_WINDOWS)
    gdim = e // ng
    ts, tc = _pick(s, ts), _pick(gdim, tc)
    cpg = gdim // tc
    hb = ts // POOL_HALO
    n_halo = s // POOL_HALO
    nst = s // ts

    def body(dy_ref, halo_ref, w_ref, du_ref):
        i, grp = pl.program_id(0), pl.program_id(1)
        cur = jnp.dot(dy_ref[...], w_ref[...], preferred_element_type=F32)
        halo = jnp.dot(halo_ref[...], w_ref[...], preferred_element_type=F32)
        ext = jnp.concatenate([cur, jnp.where(i < nst - 1, halo, 0.0)], axis=0)
        rows = ts + POOL_HALO
        for gi, window in enumerate(POOL_WINDOWS):
            @pl.when(grp == gi)
            def _(window=window):
                acc = ext / _pool_counts(i * ts, rows, tc, window)
                k = 1
                while k < window:
                    acc = acc + pltpu.roll(acc, rows - k, 0)
                    k *= 2
                du_ref[...] = (acc[:ts, :] - cur).astype(BF16)

    return pl.pallas_call(
        body, name=name,
        out_shape=jax.ShapeDtypeStruct((s, e), BF16),
        grid=(nst, ng, cpg),
        in_specs=[pl.BlockSpec((ts, gdim), lambda i, g, j: (i, g)),
                  pl.BlockSpec((POOL_HALO, gdim), lambda i, g, j: (jnp.minimum((i + 1) * hb, n_halo - 1), g)),
                  pl.BlockSpec((None, gdim, tc), lambda i, g, j: (g, 0, j))],
        out_specs=pl.BlockSpec((ts, tc), lambda i, g, j: (i, g * cpg + j)),
        compiler_params=_params("parallel", "parallel", "parallel"),
    )(dyr, dyr, w_t)


def _a_group_fwd(d, w, scale, gate, name, tm=1024):
    s, e = d.shape
    ng, g, _ = w.shape
    tm = _pick(s, tm)

    def body(d_ref, w_ref, s_ref, gate_ref, yr_ref, z_ref):
        yr = jnp.dot(d_ref[...], w_ref[...], preferred_element_type=F32)
        yr_ref[...] = yr.astype(yr_ref.dtype)
        gt = gate_ref[...].astype(F32)
        z_ref[...] = ((yr * s_ref[...]) * (gt * _sigmoid(gt))).astype(BF16)

    blk = lambda i, j: (i, j)
    return pl.pallas_call(
        body, name=name,
        out_shape=(jax.ShapeDtypeStruct((s, e), BF16), jax.ShapeDtypeStruct((s, e), BF16)),
        grid=(s // tm, ng),
        in_specs=[pl.BlockSpec((tm, g), blk), pl.BlockSpec((None, g, g), lambda i, j: (j, 0, 0)),
                  pl.BlockSpec((1, g), lambda i, j: (0, j)), pl.BlockSpec((tm, g), blk)],
        out_specs=(pl.BlockSpec((tm, g), blk), pl.BlockSpec((tm, g), blk)),
        compiler_params=_params("parallel", "parallel"),
    )(d, w, scale, gate)


def _silu_and_slope(gt):
    sg = _sigmoid(gt)
    return gt * sg, sg * (1.0 + gt * (1.0 - sg))


def _a_gate_epilogue(dz, tiles, vecs):
    yr, gt = tiles[0].astype(F32), tiles[1].astype(F32)
    sc = vecs[0]
    silu, slope = _silu_and_slope(gt)
    dy = dz * silu
    return dz * (yr * sc) * slope, dy * sc, dy * yr


def _gate_epilogue(dz, tiles, vecs):
    y, gt = tiles[0].astype(F32), tiles[1].astype(F32)
    silu, slope = _silu_and_slope(gt)
    return dz * y * slope, dz * silu


def _merge_gate_epilogue(dz, tiles, vecs, lses, col0):
    y, gt = tiles[0].astype(F32), tiles[1].astype(F32)
    silu, slope = _silu_and_slope(gt)
    dy = dz * silu
    w0, w1, w2 = _merge_weights_expanded(lses, col0, dz.shape[1])
    return dz * y * slope, w0 * dy, w1 * dy, w2 * dy


def _gate_fwd(y, gate, name, ts=512, tc=512):
    s, e = y.shape
    ts, tc = _pick(s, ts), _pick(e, tc)

    def body(y_ref, gate_ref, z_ref):
        gt = gate_ref[...].astype(F32)
        z_ref[...] = (y_ref[...].astype(F32) * (gt * _sigmoid(gt))).astype(BF16)

    blk = lambda i, j: (i, j)
    return pl.pallas_call(
        body, name=name,
        out_shape=jax.ShapeDtypeStruct((s, e), BF16),
        grid=(s // ts, e // tc),
        in_specs=[pl.BlockSpec((ts, tc), blk)] * 2,
        out_specs=pl.BlockSpec((ts, tc), blk),
        compiler_params=_params("parallel", "parallel"),
    )(y, gate)


def _merge_weights(l0, l1, l2):
    m = jnp.maximum(jnp.maximum(l0, l1), l2)
    e0, e1, e2 = jnp.exp(l0 - m), jnp.exp(l1 - m), jnp.exp(l2 - m)
    inv = 1.0 / (e0 + e1 + e2)
    return e0 * inv, e1 * inv, e2 * inv


def _expand_heads(w, col0, width):
    n_heads = w.shape[1]
    head_of_lane = (col0 + lax.broadcasted_iota(jnp.int32, (n_heads, width), 1)) // HEAD_DIM
    pick = jnp.where(head_of_lane == lax.broadcasted_iota(jnp.int32, (n_heads, width), 0), 1.0, 0.0).astype(BF16)
    high = w.astype(BF16)
    rest = (w - high.astype(F32)).astype(BF16)
    return (jnp.dot(high, pick, preferred_element_type=F32) + jnp.dot(rest, pick, preferred_element_type=F32))


def _merge_weights_expanded(lses, col0, width):
    return [_expand_heads(w, col0, width) for w in _merge_weights(*lses)]


def _merge_gate_fwd(outs, lses, gate, name, ts=512, tc=512):
    s, e = gate.shape
    n_heads = lses[0].shape[1]
    ts, tc = _pick(s, ts), _pick(e, tc)

    def body(o0, o1, o2, l0, l1, l2, gate_ref, y_ref, z_ref):
        w0, w1, w2 = _merge_weights_expanded([l0[...], l1[...], l2[...]], pl.program_id(1) * tc, tc)
        y = w0 * o0[...].astype(F32) + w1 * o1[...].astype(F32) + w2 * o2[...].astype(F32)
        y_ref[...] = y.astype(y_ref.dtype)
        gt = gate_ref[...].astype(F32)
        z_ref[...] = (y * (gt * _sigmoid(gt))).astype(BF16)

    blk = lambda i, j: (i, j)
    per_head = pl.BlockSpec((ts, n_heads), lambda i, j: (i, 0))
    return pl.pallas_call(
        body, name=name,
        out_shape=(jax.ShapeDtypeStruct((s, e), BF16), jax.ShapeDtypeStruct((s, e), BF16)),
        grid=(s // ts, e // tc),
        in_specs=[pl.BlockSpec((ts, tc), blk)] * 3 + [per_head] * 3 + [pl.BlockSpec((ts, tc), blk)],
        out_specs=(pl.BlockSpec((ts, tc), blk), pl.BlockSpec((ts, tc), blk)),
        compiler_params=_params("parallel", "parallel"),
    )(*outs, *lses, gate)


def _band(max_dist, width):
    row = lax.broadcasted_iota(jnp.int32, (2 * BLOCK, width), 0) & (BLOCK - 1)
    col = lax.broadcasted_iota(jnp.int32, (2 * BLOCK, width), 1)
    low = row if max_dist == BLOCK else row + 1
    return jnp.logical_and(col >= low, col <= row + BLOCK), col >= BLOCK


def _fill_bias(bias_ref, max_dist):
    @pl.when(jnp.logical_and(pl.program_id(0) == 0, jnp.logical_and(pl.program_id(1) == 0, pl.program_id(2) == 0)))
    def _():
        band, own = _band(max_dist, 2 * BLOCK)
        bias_ref[0] = jnp.where(band, 0.0, NEG)
        bias_ref[1] = jnp.where(jnp.logical_and(band, own), 0.0, NEG)


def _aligned(v):
    return v if isinstance(v, int) else pl.multiple_of(v, BLOCK)


def _stack_heads(x, lo):
    return jnp.concatenate([jnp.where(lo, x, 0.0), jnp.where(lo, 0.0, x)], axis=0).astype(BF16)


def _unstack_heads(x2, lo):
    return jnp.where(lo, x2[:BLOCK], x2[BLOCK:])


def _head_col(x, hm):
    return jnp.max(jnp.where(hm, x, NEG), axis=1, keepdims=True)


def _dot_nt(a, b):
    return lax.dot_general(a, b, (((1,), (1,)), ((), ())), preferred_element_type=F32)


def _dot_tn(a, b):
    return lax.dot_general(a, b, (((0,), (0,)), ((), ())), preferred_element_type=F32)


def _stream_view(a, dil):
    s, w = a.shape
    return a.reshape(s // (BLOCK * dil), dil, BLOCK, w)


def _fill_window(dst, halo_ref, cur_ref, n):
    dst[0:BLOCK, :] = halo_ref[0]
    for jc in range(n):
        dst[(jc + 1) * BLOCK:(jc + 2) * BLOCK, :] = cur_ref[jc]


def _attn_fwd(q, k, v, sinks, max_dist, rep, dil, out_dtype, name, tq=2048, per_head_lse=False):
    assert max_dist in (BLOCK - 1, BLOCK)
    s, w = q.shape
    l = s // dil
    n_pairs = w // LANES
    n_heads = 2 * n_pairs
    tq = _pick(l, tq)
    n = tq // BLOCK
    has_sink = sinks is not None
    scale = HEAD_DIM ** -0.5

    def body(*refs):
        if has_sink:
            sink_ref, refs = refs[0], refs[1:]
        q_ref, kc_ref, kh_ref, vc_ref, vh_ref, o_ref, lse_ref = refs[:7]
        refs = refs[7:]
        if per_head_lse:
            lseh_ref, refs = refs[0], refs[1:]
        kx, vx, bias_ref = refs
        i, p = pl.program_id(0), pl.program_id(2)
        _fill_window(kx, kh_ref, kc_ref, n)
        _fill_window(vx, vh_ref, vc_ref, n)
        if per_head_lse:
            @pl.when(p == 0)
            def _():
                lseh_ref[...] = jnp.zeros_like(lseh_ref)
            head_lane = lax.broadcasted_iota(jnp.int32, (BLOCK, n_heads), 1)
        lo = lax.broadcasted_iota(jnp.int32, (BLOCK, LANES), 1) < HEAD_DIM
        _fill_bias(bias_ref, max_dist)
        top = lax.broadcasted_iota(jnp.int32, (2 * BLOCK, 1), 0) < BLOCK

        per_step = 2 if n % 2 == 0 else 1

        def scores(j):
            r0 = _aligned(j * BLOCK)
            q2 = _stack_heads(q_ref[j].astype(F32) * scale, lo)
            first = jnp.logical_and(i == 0, j == 0).astype(jnp.int32)
            return _dot_nt(q2, kx[pl.ds(r0, 2 * BLOCK), :]) + bias_ref[first]

        def step(jj, carry):
            nxt = tuple(scores((jj + 1) * per_step + t) for t in range(per_step))
            for t in range(per_step):
                finish(jj * per_step + t, carry[t])
            return nxt

        def finish(j, s2):
            r0 = _aligned(j * BLOCK)
            vw = vx[pl.ds(r0, 2 * BLOCK), :]
            outs, lses = [], []
            for hh in range(2):
                s1 = s2[hh * BLOCK:(hh + 1) * BLOCK, :]
                m = jnp.max(s1, axis=1, keepdims=True)
                if has_sink:
                    sk = sink_ref[2 * p + hh]
                    m = jnp.maximum(m, sk)
                pr = jnp.exp(s1 - m)
                den = jnp.sum(pr, axis=1, keepdims=True)
                if has_sink:
                    den = den + jnp.exp(sk - m)
                outs.append(jnp.dot(pr.astype(BF16), vw, preferred_element_type=F32) * (1.0 / den))
                lses.append(m + jnp.log(den))
            o_ref[j] = jnp.where(lo, outs[0], outs[1]).astype(o_ref.dtype)
            lse_ref[j] = jnp.where(lo, lses[0], lses[1])
            if per_head_lse:
                lseh_ref[j] = jnp.where(head_lane == 2 * p, lses[0],
                                        jnp.where(head_lane == 2 * p + 1, lses[1], lseh_ref[j]))

        trips = n // per_step
        last = lax.fori_loop(0, trips - 1, step, tuple(scores(t) for t in range(per_step)))
        for t in range(per_step):
            finish((trips - 1) * per_step + t, last[t])

    cur = lambda i, r, p: (i, r, 0, p)
    kv_cur = lambda i, r, p: (i, r, 0, p // rep)
    kv_halo = lambda i, r, p: (jnp.maximum(i * n - 1, 0), r, 0, p // rep)
    big, small = (n, None, BLOCK, LANES), (1, None, BLOCK, LANES)
    in_specs = [pl.BlockSpec(big, cur), pl.BlockSpec(big, kv_cur), pl.BlockSpec(small, kv_halo),
                pl.BlockSpec(big, kv_cur), pl.BlockSpec(small, kv_halo)]
    q4, k4, v4 = _stream_view(q, dil), _stream_view(k, dil), _stream_view(v, dil)
    args = [q4, k4, k4, v4, v4]
    if has_sink:
        in_specs = [pl.BlockSpec(memory_space=pltpu.SMEM)] + in_specs
        args = [sinks] + args
    out_shape = [jax.ShapeDtypeStruct(q4.shape, out_dtype), jax.ShapeDtypeStruct(q4.shape, F32)]
    out_specs = [pl.BlockSpec(big, cur), pl.BlockSpec(big, cur)]
    if per_head_lse:
        out_shape.append(jax.ShapeDtypeStruct(q4.shape[:3] + (n_heads,), F32))
        out_specs.append(pl.BlockSpec((n, None, BLOCK, n_heads), lambda i, r, p: (i, r, 0, 0)))
    outs = pl.pallas_call(
        body, name=name,
        out_shape=tuple(out_shape),
        grid=(l // tq, dil, n_pairs),
        in_specs=in_specs,
        out_specs=tuple(out_specs),
        scratch_shapes=[pltpu.VMEM((tq + BLOCK, LANES), BF16), pltpu.VMEM((tq + BLOCK, LANES), BF16),
                        pltpu.VMEM((2, 2 * BLOCK, 2 * BLOCK), F32)],
        compiler_params=_params("arbitrary", "arbitrary", "arbitrary"),
    )(*args)
    res = [outs[0].reshape(s, w), outs[1].reshape(s, w)]
    if per_head_lse:
        res.append(outs[2].reshape(s, n_heads))
    return res


def _attn_bwd(q, k, v, do, y, lse, sinks, max_dist, rep, dil, name, tq=2048):
    s, w = q.shape
    l = s // dil
    n_pairs = w // LANES
    tq = _pick(l, tq)
    n = tq // BLOCK
    n_blk = l // BLOCK
    n_sb = l // tq
    has_sink = sinks is not None
    scale = HEAD_DIM ** -0.5
    kv_dtype = BF16
    ext = tq + BLOCK

    def body(*refs):
        if has_sink:
            sink_ref, refs = refs[0], refs[1:]
        (q_ref, qn_ref, kc_ref, kh_ref, vc_ref, vh_ref, do_ref, don_ref, y_ref, yn_ref,
         lse_ref, lsen_ref) = refs[:12]
        refs = refs[12:]
        dq_ref, dk_ref, dv_ref = refs[:3]
        refs = refs[3:]
        if has_sink:
            dsink_ref, refs = refs[0], refs[1:]
        kx, vx, bias_ref = refs[:3]
        if rep > 1:
            dk_acc, dv_acc = refs[3:]
        i, p = pl.program_id(0), pl.program_id(2)
        _fill_bias(bias_ref, max_dist)
        own_rows = (q_ref, do_ref, y_ref, lse_ref)
        next_rows = (qn_ref, don_ref, yn_ref, lsen_ref)
        _fill_window(kx, kh_ref, kc_ref, n)
        _fill_window(vx, vh_ref, vc_ref, n)
        if rep > 1:
            @pl.when(p % rep == 0)
            def _():
                dk_acc[...] = jnp.zeros_like(dk_acc)
                dv_acc[...] = jnp.zeros_like(dv_acc)
        lo = lax.broadcasted_iota(jnp.int32, (BLOCK, LANES), 1) < HEAD_DIM
        hi = jnp.logical_not(lo)
        top = lax.broadcasted_iota(jnp.int32, (2 * BLOCK, 1), 0) < BLOCK

        def rows_of(j):
            if isinstance(j, int) and j == n:
                return next_rows, 0
            return own_rows, j

        def front(j, width):
            (qr, dor, _, _), jb = rows_of(j)
            r0 = _aligned(j * BLOCK)
            first = jnp.logical_and(i == 0, j == 0).astype(jnp.int32)
            q2 = _stack_heads(qr[jb].astype(F32) * scale, lo)
            do2 = _stack_heads(dor[jb].astype(F32), lo)
            s2 = _dot_nt(q2, kx[pl.ds(r0, width), :]) + bias_ref[first, :, pl.ds(0, width)]
            return s2, _dot_nt(do2, vx[pl.ds(r0, width), :])

        row_lo = lax.broadcasted_iota(jnp.int32, (LANES, BLOCK), 0) < HEAD_DIM

        def stack_t(x):
            xt = x.T
            return jnp.concatenate([jnp.where(row_lo, xt, 0.0), jnp.where(row_lo, 0.0, xt)], axis=1).astype(BF16)

        def emit(jk, dk_t, dv_t):
            dk_blk, dv_blk = dk_t.T, dv_t.T
            if rep == 1:
                dk_ref[jk] = dk_blk.astype(dk_ref.dtype)
                dv_ref[jk] = dv_blk.astype(dv_ref.dtype)
            else:
                rows = pl.ds(_aligned(jk * BLOCK), BLOCK)
                dk_acc[rows, :] += dk_blk
                dv_acc[rows, :] += dv_blk

        def back(j, width, q_valid, s2, dp2, state):
            sink_acc, carry_k, carry_v = state
            (qr, dor, yr, lser), jb = rows_of(j)
            r0 = _aligned(j * BLOCK)
            qf, dof = qr[jb].astype(F32) * scale, dor[jb].astype(F32)
            yb, lseb = yr[jb].astype(F32), lser[jb]
            prod = dof * yb
            delta = jnp.concatenate([jnp.sum(jnp.where(lo, prod, 0.0), axis=1, keepdims=True),
                                     jnp.sum(jnp.where(lo, 0.0, prod), axis=1, keepdims=True)], axis=0)
            lse2 = jnp.concatenate([_head_col(lseb, lo), _head_col(lseb, hi)], axis=0)
            pr = jnp.exp(s2 - lse2)
            if q_valid is not True:
                pr = jnp.where(q_valid, pr, 0.0)
            ds = pr * (dp2 - delta)
            dk_t = jnp.dot(stack_t(qf), ds.astype(BF16), preferred_element_type=F32)
            dv_t = jnp.dot(stack_t(dof), pr.astype(BF16), preferred_element_type=F32)
            done_k, done_v = carry_k + dk_t[:, :BLOCK], carry_v + dv_t[:, :BLOCK]
            if isinstance(j, int):
                if j > 0:
                    emit(j - 1, done_k, done_v)
            elif rep == 1:
                emit(jnp.maximum(j - 1, 0), done_k, done_v)
            else:
                keep = j > 0
                emit(jnp.maximum(j - 1, 0), jnp.where(keep, done_k, 0.0), jnp.where(keep, done_v, 0.0))
            if width == 2 * BLOCK:
                dq2 = jnp.dot(ds.astype(BF16), kx[pl.ds(r0, width), :], preferred_element_type=F32) * scale
                dq_ref[jb] = _unstack_heads(dq2, lo).astype(dq_ref.dtype)
                carry_k, carry_v = dk_t[:, BLOCK:], dv_t[:, BLOCK:]
            if has_sink:
                sk = jnp.where(top, sink_ref[2 * p], sink_ref[2 * p + 1])
                sink_acc = sink_acc - jnp.exp(sk - lse2) * delta
            return sink_acc, carry_k, carry_v

        per_step = 2 if n % 2 == 0 else 1

        def step(jj, state):
            fronts = [front(jj * per_step + t, 2 * BLOCK) for t in range(per_step)]
            for t in range(per_step):
                state = back(jj * per_step + t, 2 * BLOCK, True, *fronts[t], state)
            return state

        zero_blk = jnp.zeros((LANES, BLOCK), F32)
        state = (jnp.zeros((2 * BLOCK, 1), F32), zero_blk, zero_blk)
        if n // per_step <= 2:
            for jj in range(n // per_step):
                state = step(jj, state)
        else:
            state = lax.fori_loop(0, n // per_step, step, state)
        sink_acc = state[0]
        if n_sb > 1:
            back(n, BLOCK, i < n_sb - 1, *front(n, BLOCK), state)
        else:
            emit(n - 1, state[1], state[2])

        if rep > 1:
            @pl.when(p % rep == rep - 1)
            def _():
                for jc in range(n):
                    rows = slice(jc * BLOCK, (jc + 1) * BLOCK)
                    dk_ref[jc] = dk_acc[rows, :].astype(dk_ref.dtype)
                    dv_ref[jc] = dv_acc[rows, :].astype(dv_ref.dtype)
        if has_sink:
            rowi = lax.broadcasted_iota(jnp.int32, (8, LANES), 0)
            s0 = jnp.sum(sink_acc[:BLOCK], axis=0, keepdims=True)
            s1 = jnp.sum(sink_acc[BLOCK:], axis=0, keepdims=True)
            dsink_ref[...] = jnp.where(rowi == 0, s0, jnp.where(rowi == 1, s1, 0.0))

    cur = lambda i, r, p: (i, r, 0, p)
    nxt = lambda i, r, p: (jnp.minimum((i + 1) * n, n_blk - 1), r, 0, p)
    kv_cur = lambda i, r, p: (i, r, 0, p // rep)
    kv_halo = lambda i, r, p: (jnp.maximum(i * n - 1, 0), r, 0, p // rep)
    big, small = (n, None, BLOCK, LANES), (1, None, BLOCK, LANES)
    in_specs = [pl.BlockSpec(big, cur), pl.BlockSpec(small, nxt),
                pl.BlockSpec(big, kv_cur), pl.BlockSpec(small, kv_halo),
                pl.BlockSpec(big, kv_cur), pl.BlockSpec(small, kv_halo),
                pl.BlockSpec(big, cur), pl.BlockSpec(small, nxt),
                pl.BlockSpec(big, cur), pl.BlockSpec(small, nxt),
                pl.BlockSpec(big, cur), pl.BlockSpec(small, nxt)]
    q4, k4, v4, do4, y4, lse4 = [_stream_view(a, dil) for a in (q, k, v, do, y, lse)]
    args = [q4, q4, k4, k4, v4, v4, do4, do4, y4, y4, lse4, lse4]
    out_shape = [jax.ShapeDtypeStruct(q4.shape, BF16),
                 jax.ShapeDtypeStruct(k4.shape, kv_dtype), jax.ShapeDtypeStruct(v4.shape, kv_dtype)]
    out_specs = [pl.BlockSpec(big, cur), pl.BlockSpec(big, kv_cur), pl.BlockSpec(big, kv_cur)]
    if has_sink:
        in_specs = [pl.BlockSpec(memory_space=pltpu.SMEM)] + in_specs
        args = [sinks] + args
        out_shape.append(jax.ShapeDtypeStruct((n_sb, dil, n_pairs, 8, LANES), F32))
        out_specs.append(pl.BlockSpec((None, None, None, 8, LANES), lambda i, r, p: (i, r, p, 0, 0)))
    outs = pl.pallas_call(
        body, name=name,
        out_shape=tuple(out_shape),
        grid=(n_sb, dil, n_pairs),
        in_specs=in_specs,
        out_specs=tuple(out_specs),
        scratch_shapes=[pltpu.VMEM((ext, LANES), BF16), pltpu.VMEM((ext, LANES), BF16),
                        pltpu.VMEM((2, 2 * BLOCK, 2 * BLOCK), F32)]
        + ([pltpu.VMEM((tq, LANES), F32), pltpu.VMEM((tq, LANES), F32)] if rep > 1 else []),
        compiler_params=_params("arbitrary", "arbitrary", "arbitrary"),
    )(*args)
    grads =[outs[0].reshape(s, w), outs[1].reshape(k.shape), outs[2].reshape(v.shape)]
    if has_sink:
        grads.append(outs[3].sum(axis=(0, 1))[:, 0:2, 0].reshape(1, 2 * n_pairs))
    return grads


def _sum_slots(recv, name, ts=256):
    nd, r, c = recv.shape
    ts = _pick(r, ts, 8)

    def body(r_ref, o_ref):
        acc = r_ref[0].astype(F32)
        for dev in range(1, nd):
            acc = acc + r_ref[dev].astype(F32)
        o_ref[...] = acc

    return pl.pallas_call(
        body, name=name,
        out_shape=jax.ShapeDtypeStruct((r, c), F32),
        grid=(r // ts,),
        in_specs=[pl.BlockSpec((nd, ts, c), lambda i: (0, i, 0))],
        out_specs=pl.BlockSpec((ts, c), lambda i: (i, 0)),
        compiler_params=_params("parallel"),
    )(recv)


def _adamw_math(w, g, m, v):
    c1 = 1.0 - ADAM_B1 ** ADAM_STEP
    c2 = 1.0 - ADAM_B2 ** ADAM_STEP
    m_ = ADAM_B1 * m + (1.0 - ADAM_B1) * g
    v_ = ADAM_B2 * v + (1.0 - ADAM_B2) * (g * g)
    return -ADAM_LR * ((m_ / c1) / (jnp.sqrt(v_ / c2) + ADAM_EPS) + ADAM_WD * w), m_, v_


def _row_tile(r, c, budget=1 << 18):
    return _pick(r, max(8, min(256, budget // c // 8 * 8)), 8)


def _adamw(w, g, m, v, name):
    r, c = w.shape
    ts = _row_tile(r, c)

    def body(w_ref, g_ref, m_ref, v_ref, d_ref, mo_ref, vo_ref):
        d_ref[...], mo_ref[...], vo_ref[...] = _adamw_math(w_ref[...], g_ref[...], m_ref[...], v_ref[...])

    blk = pl.BlockSpec((ts, c), lambda i: (i, 0))
    return pl.pallas_call(
        body, name=name,
        out_shape=tuple([jax.ShapeDtypeStruct((r, c), F32)] * 3),
        grid=(r // ts,),
        in_specs=[blk] * 4,
        out_specs=(blk, blk, blk),
        compiler_params=_params("parallel"),
    )(w, g, m, v)


def _adamw_slots(w, slots, m, v, name):
    r, c = w.shape
    nd = slots.shape[0]
    ts = _row_tile(r, c)

    def body(w_ref, s_ref, m_ref, v_ref, g_ref, d_ref, mo_ref, vo_ref):
        g = s_ref[0].astype(F32)
        for slot in range(1, nd):
            g = g + s_ref[slot].astype(F32)
        g_ref[...] = g
        d_ref[...], mo_ref[...], vo_ref[...] = _adamw_math(w_ref[...], g, m_ref[...], v_ref[...])

    blk = pl.BlockSpec((ts, c), lambda i: (i, 0))
    return pl.pallas_call(
        body, name=name,
        out_shape=tuple([jax.ShapeDtypeStruct((r, c), F32)] * 4),
        grid=(r // ts,),
        in_specs=[blk, pl.BlockSpec((nd, ts, c), lambda i: (0, i, 0)), blk, blk],
        out_specs=(blk, blk, blk, blk),
        compiler_params=_params("parallel"),
    )(w, slots, m, v)


def _rows(a):
    flat = a.reshape(-1)
    pad = (-flat.shape[0]) % PACK_W
    if pad:
        flat = jnp.concatenate([flat, jnp.zeros((pad,), flat.dtype)])
    return flat.reshape(-1, PACK_W)


def _pad_rows(a, mult):
    pad = (-a.shape[-2]) % mult
    if pad:
        widths = [(0, 0)] * (a.ndim - 2) + [(0, pad), (0, 0)]
        a = jnp.pad(a, widths)
    return a


def _to_global(stack, axis):
    moved = jnp.moveaxis(stack, 0, axis)
    shp = list(moved.shape)
    shp[axis:axis + 2] = [shp[axis] * shp[axis + 1]]
    return moved.reshape(shp)


def _to_stack(full, axis):
    shp = list(full.shape)
    shp[axis:axis + 1] = [N_DEV, shp[axis] // N_DEV]
    return jnp.moveaxis(full.reshape(shp), axis, 0)


_BIG = (("w_out", 1), ("a_w_in", 2), ("a_w_group", 2), ("b_w_in", 2), ("c_w_in", 2))


def _dup_heads(wk, n_kv):
    d = wk.shape[0]
    return jnp.tile(wk.reshape(d, n_kv, 1, HEAD_DIM), (1, 1, 2, 1)).reshape(d, n_kv * LANES)


def _fold_heads(dwk, n_kv):
    d = dwk.shape[0]
    folded = dwk.astype(F32).reshape(d, n_kv, 2, HEAD_DIM).sum(axis=2)
    return folded.reshape(d, n_kv * HEAD_DIM).astype(dwk.dtype)


def _perm(a, dil):
    if dil == 1:
        return a
    s, w = a.shape
    return a.reshape(s // (BLOCK * dil), BLOCK, dil, w).transpose(0, 2, 1, 3).reshape(s, w)


def _unperm(a, dil):
    if dil == 1:
        return a
    s, w = a.shape
    return a.reshape(s // (BLOCK * dil), dil, BLOCK, w).transpose(0, 2, 1, 3).reshape(s, w)


def kernel(x, norm_g, final_g, w_out, a_w_in, a_w_group, a_scale, b_w_in, b_sinks, c_w_in, loss_target, m_norm_g, m_final_g, m_w_out, m_a_w_in, m_a_w_group, m_a_scale, m_b_w_in, m_b_sinks, m_c_w_in, v_norm_g, v_final_g, v_w_out, v_a_w_in, v_a_w_group, v_a_scale, v_b_w_in, v_b_sinks, v_c_w_in):
    local = dict(w_out=w_out, a_w_in=a_w_in, a_w_group=a_w_group, b_w_in=b_w_in, c_w_in=c_w_in)
    mom_m = dict(w_out=m_w_out, a_w_in=m_a_w_in, a_w_group=m_a_w_group, b_w_in=m_b_w_in, c_w_in=m_c_w_in)
    mom_v = dict(w_out=v_w_out, a_w_in=v_a_w_in, a_w_group=v_a_w_group, b_w_in=v_b_w_in, c_w_in=v_c_w_in)
    s, d = x.shape[1], x.shape[2]
    depth = norm_g.shape[0]
    e = w_out.shape[1] * N_DEV
    n_heads = e // HEAD_DIM
    n_kv = n_heads // Q_PER_KV
    kv_w = n_kv * HEAD_DIM
    rep = Q_PER_KV // 2
    n_groups = len(POOL_WINDOWS)
    me = 4 * lax.axis_index("x") + 2 * lax.axis_index("y") + lax.axis_index("c")

    flat = {n: local[n].reshape(-1, local[n].shape[-1]) for n, _ in _BIG}
    spack = _pad_rows(_rows(a_scale), 8)
    *walls, sall = _gather([flat[n].astype(BF16) for n, _ in _BIG] + [spack], [True] * len(_BIG) + [False],
                           "gather_weights")
    full = {}
    for k, (name, axis) in enumerate(_BIG):
        full[name] = _to_global(walls[k].reshape((N_DEV,) + local[name].shape), axis)
    scale_full = _to_global(sall.reshape(N_DEV, -1)[:, :a_scale.size].reshape((N_DEV,) + a_scale.shape), 1)

    wout_t = jnp.swapaxes(full["w_out"], 1, 2)
    wa = full["a_w_in"]
    wa_t = jnp.swapaxes(wa, 1, 2)
    wg = full["a_w_group"]
    wg_t = jnp.swapaxes(wg, 2, 3)
    wb = full["b_w_in"][0]
    wb_ext = jnp.concatenate([wb[:, :e], _dup_heads(wb[:, e:e + kv_w], n_kv),
                              _dup_heads(wb[:, e + kv_w:e + 2 * kv_w], n_kv), wb[:, e + 2 * kv_w:]], axis=1)
    wb_ext_t = wb_ext.T
    kd_w = n_kv * LANES
    wc = full["c_w_in"][0]
    wc_t = wc.T

    xs, hs, zs, saved = [x.reshape(s, d)], [], [], []
    hs.append(_rmsnorm_fwd(xs[0], norm_g[0:1], "norm0"))
    loss_vec = dfinal = dx = dxb = None
    for i in range(depth):
        kind, j = i % 3, i // 3
        h = hs[i]
        tag = f"l{i}"
        if kind == 0:
            dpool = _proj_pool_fwd(h, wa[j][:, :e], tag + "_in_pool")
            gate = _matmul(h, wa[j][:, e:], BF16, tag + "_in_gate")
            yr, z = _a_group_fwd(dpool, wg[j], scale_full[j:j + 1], gate, tag + "_group")
            saved.append(dict(dpool=dpool, yr=yr, gate=gate))
        elif kind == 1:
            q = _matmul(h, wb_ext[:, :e], BF16, tag + "_in_q")
            kd = _matmul(h, wb_ext[:, e:e + kd_w], BF16, tag + "_in_k")
            vd = _matmul(h, wb_ext[:, e + kd_w:e + 2 * kd_w], BF16, tag + "_in_v")
            gate = _matmul(h, wb_ext[:, e + 2 * kd_w:], BF16, tag + "_in_gate")
            sinks = b_sinks[j]
            y, lse = _attn_fwd(q, kd, vd, sinks, SWA_MAX_DIST, rep, 1, BF16, tag + "_attn")
            z = _gate_fwd(y, gate, tag + "_gate")
            saved.append(dict(q=q, kd=kd, vd=vd, gate=gate, y=y, lse=lse, sinks=sinks))
        else:
            qkv, outs, lses, lses_tok, h_perm = [], [], [], [], []
            for gi, (window, dil) in enumerate(DILATED_PAIRS):
                hp = _perm(h, dil)
                trio = [_matmul(hp, wc[:, (3 * gi + t) * e:(3 * gi + t + 1) * e], BF16,
                                f"{tag}_in_{'qkv'[t]}{gi}") for t in range(3)]
                o, lse, lse_heads = _attn_fwd(trio[0], trio[1], trio[2], None, window // dil, 1, dil, BF16,
                                              f"{tag}_attn{gi}", per_head_lse=True)
                qkv.append(trio)
                h_perm.append(hp)
                outs.append(_unperm(o, dil))
                lses.append(lse)
                lses_tok.append(_unperm(lse_heads, dil))
            gate = _matmul(h, wc[:, 9 * e:], BF16, tag + "_in_gate")
            y, z = _merge_gate_fwd(outs, lses_tok, gate, tag + "_merge")
            saved.append(dict(qkv=qkv, lses=lses, lses_tok=lses_tok, gate=gate, y=y, h_perm=h_perm))
        zs.append(z)
        if i + 1 < depth:
            x_new, h_new = _outproj_norm(z, full["w_out"][i], xs[i], norm_g[i + 1:i + 2], tag + "_out")
            xs.append(x_new)
            hs.append(h_new)
        else:
            dx, dxb, dfinal, loss_vec = _outproj_loss(z, full["w_out"][i], xs[i], final_g.reshape(1, d),
                                                      loss_target.reshape(s, d), tag + "_out_loss")

    g_full = {"w_out": [None] * depth, "a_w_in": [None] * wa.shape[0], "a_w_group": [None] * wa.shape[0]}
    d_norm = [None] * depth
    d_scale = [None] * wa.shape[0]
    d_sinks = None
    for i in reversed(range(depth)):
        kind, j = i % 3, i // 3
        tag = f"b{i}"
        sv = saved[i]
        g_full["w_out"][i] = _matmul_tn(zs[i], dxb, tag + "_dwout", out_dtype=BF16)
        if kind == 0:
            dgate, dyr, dsc = _dz_fused(dxb, wout_t[i], [sv["yr"], sv["gate"]], [scale_full[j:j + 1]], 2,
                                        _a_gate_epilogue, tag + "_dz_gate", 1024, 1024, with_col_sum=True)
            d_scale[j] = dsc
            du = _group_pool_bwd(dyr, wg_t[j], tag + "_dd_pool")
            g_full["a_w_group"][j] = _grouped_weight_grad(sv["dpool"], dyr, n_groups, tag + "_dwg")
            parts = [du, dgate]
            g_full["a_w_in"][j] = jnp.concatenate(
                [_matmul_tn(hs[i], part, f"{tag}_dwin{t}", out_dtype=BF16) for t, part in enumerate(parts)], axis=1)
            dhs = [_matmul_cat(parts, wa_t[j], F32, tag + "_dh")]
        elif kind == 1:
            dgate, do = _dz_fused(dxb, wout_t[i], [sv["y"], sv["gate"]], [], 2, _gate_epilogue,
                                  tag + "_dz_gate", 1024, 1024)
            dq, dkd, dvd, d_sinks = _attn_bwd(sv["q"], sv["kd"], sv["vd"], do, sv["y"], sv["lse"], sv["sinks"],
                                              SWA_MAX_DIST, rep, 1, tag + "_attn")
            parts = [dq, dkd, dvd, dgate]
            dws = [_matmul_tn(hs[i], part, f"{tag}_dwin{t}", out_dtype=BF16) for t, part in enumerate(parts)]
            g_full["b_w_in"] = jnp.concatenate(
                [dws[0], _fold_heads(dws[1], n_kv), _fold_heads(dws[2], n_kv), dws[3]], axis=1)[None]
            dhs = [_matmul_cat(parts, wb_ext_t, F32, tag + "_dh")]
        else:
            dgate, *dos = _dz_fused(dxb, wout_t[i], [sv["y"], sv["gate"]], [], 4, _merge_gate_epilogue,
                                    tag + "_dz_merge", 1024, 512, rows=sv["lses_tok"])
            y_bf = sv["y"]
            dws, dhs = [], []
            for gi, (window, dil) in enumerate(DILATED_PAIRS):
                qv, kv, vv = sv["qkv"][gi]
                grads = _attn_bwd(qv, kv, vv, _perm(dos[gi], dil), _perm(y_bf, dil), sv["lses"][gi], None,
                                  window // dil, 1, dil, f"{tag}_attn{gi}")
                dws += [_matmul_tn(sv["h_perm"][gi], part, f"{tag}_dwin{gi}{'qkv'[t]}", out_dtype=BF16)
                        for t, part in enumerate(grads)]
                dhs.append(_unperm(_matmul_cat(grads, wc_t[3 * gi * e:3 * (gi + 1) * e], F32, f"{tag}_dh{gi}"),
                                   dil))
            dws.append(_matmul_tn(hs[i], dgate, tag + "_dwin_gate", out_dtype=BF16))
            dhs.append(_matmul(dgate, wc_t[9 * e:], F32, tag + "_dh_gate"))
            g_full["c_w_in"] = jnp.concatenate(dws, axis=1)[None]
        dx, dxb, d_norm[i] = _rmsnorm_bwd(dhs, xs[i], norm_g[i:i + 1], dx, tag + "_norm")
    grad_x = dx.reshape(x.shape)
    for name in ("w_out", "a_w_in", "a_w_group"):
        g_full[name] = jnp.stack(g_full[name], axis=0)

    stacks = [_to_stack(g_full[n], axis).astype(BF16).reshape((N_DEV,) + flat[n].shape) for n, axis in _BIG]
    loss_local = (0.5 / d) * jnp.sum(loss_vec)
    small = [jnp.concatenate(d_norm, axis=0), dfinal, d_sinks, jnp.concatenate(d_scale, axis=0),
             loss_local.reshape(1, 1)]
    small_rows = [_rows(a) for a in small]
    small_offs = [sum(r.shape[0] for r in small_rows[:k]) for k in range(len(small_rows) + 1)]
    small_pack = _pad_rows(jnp.concatenate(small_rows, axis=0), 8)
    core = lax.axis_index("c").astype(jnp.int32).reshape(1)
    from_sibling = _sibling_exchange(stacks, "exchange_sibling")
    chip_sums = [_pair_sum(stacks[k], from_sibling[k], core, "sum_pair_" + n) for k, (n, _) in enumerate(_BIG)]
    grecv, srecv = _chip_exchange(chip_sums, small_pack, "exchange_chips")
    ssum = _sum_slots(srecv, "sum_small")

    def small_part(k, like):
        return ssum[small_offs[k]:small_offs[k + 1]].reshape(-1)[:like.size].reshape(like.shape)

    g_norm = small_part(0, norm_g)
    g_final = small_part(1, final_g)
    g_sinks = small_part(2, b_sinks)
    g_scale_full = small_part(3, scale_full)
    loss = ssum[small_offs[4], 0]
    g_scale = lax.dynamic_slice_in_dim(g_scale_full, me * a_scale.shape[1], a_scale.shape[1], axis=1)

    small_w = [("norm_g", norm_g, m_norm_g, v_norm_g, g_norm), ("final_g", final_g, m_final_g, v_final_g, g_final),
               ("a_scale", a_scale, m_a_scale, v_a_scale, g_scale), ("b_sinks", b_sinks, m_b_sinks, v_b_sinks, g_sinks)]
    tail = lambda idx: _pad_rows(jnp.concatenate([_rows(t[idx]) for t in small_w], axis=0), 8)
    tail_sizes = [_rows(t[1]).shape[0] for t in small_w]
    tail_offs = [sum(tail_sizes[:k]) for k in range(len(tail_sizes) + 1)]
    g_tail = tail(4)
    tails = (g_tail,) + _adamw(tail(1), g_tail, tail(2), tail(3), "adamw_small")
    grads, deltas, new_m, new_v = {}, {}, {}, {}
    for k, (name, w_, _, _, _) in enumerate(small_w):
        for out, packed in zip((grads, deltas, new_m, new_v), tails):
            out[name] = packed[tail_offs[k]:tail_offs[k + 1]].reshape(-1)[:w_.size].reshape(w_.shape)
    for k, (name, _) in enumerate(_BIG):
        shape2d = flat[name].shape
        res = _adamw_slots(flat[name], grecv[k], mom_m[name].reshape(shape2d), mom_v[name].reshape(shape2d),
                           "adamw_" + name)
        for out, val in zip((grads, deltas, new_m, new_v), res):
            out[name] = val.reshape(local[name].shape)

    order = ("norm_g", "final_g", "w_out", "a_w_in", "a_w_group", "a_scale", "b_w_in", "b_sinks", "c_w_in")
    return (loss, grad_x, *[grads[n] for n in order], *[deltas[n] for n in order],
            *[new_m[n] for n in order], *[new_v[n] for n in order])
```
